```python
import math
import jax, jax.numpy as jnp
from jax import lax
import numpy as np


D_MODEL = 2048
BATCH = 4
SEQ = 4096
DEPTH = 1

N_META = 16
POOL_WINDOWS = (2, 4, 8, 16)
POOL_GROUP = 256
POOL_WIDTH = POOL_GROUP * len(POOL_WINDOWS)
MLA_HEADS = 16
Q_LORA = 512
KV_LORA = 512
QK_NOPE = 128
QK_ROPE = 64
V_DIM = 128
QK_DIM = QK_NOPE + QK_ROPE
MLA_WIDTH = MLA_HEADS * V_DIM
ROPE_THETA = 10000.0
SOFTMAX_SCALE = QK_DIM ** -0.5
D_FF = 5632
Q_BLOCK = 128
EPS = 1e-6
SPLITS = (POOL_WIDTH,
          POOL_WIDTH + Q_LORA,
          POOL_WIDTH + Q_LORA + KV_LORA,
          POOL_WIDTH + Q_LORA + KV_LORA + QK_ROPE,
          POOL_WIDTH + Q_LORA + KV_LORA + QK_ROPE + D_MODEL)
IN_COLS = POOL_WIDTH + Q_LORA + KV_LORA + QK_ROPE + 2 * D_MODEL

kernel_name = 'hybrid_pool_mla_macaron_block'


def _rmsnorm(x, gain):
    x32 = x.astype(jnp.float32)
    y = x32 * lax.rsqrt(jnp.mean(x32 * x32, axis=-1, keepdims=True) + EPS)
    return (y * gain.astype(jnp.float32)).astype(x.dtype)


def _swiglu(h, w_gu, w_down):
    g, u = jnp.split(h @ w_gu, 2, axis=-1)
    return (jax.nn.silu(g) * u) @ w_down


def _rope_tables(L, dtype):
    pos = jnp.arange(L, dtype=jnp.float32)
    inv = ROPE_THETA ** (-jnp.arange(0, QK_ROPE, 2, dtype=jnp.float32) / QK_ROPE)
    ang = pos[:, None] * inv[None, :]
    ang = jnp.concatenate([ang, ang], axis=-1)
    return jnp.cos(ang).astype(dtype), jnp.sin(ang).astype(dtype)


def _rotate(x, cos, sin):
    x1, x2 = jnp.split(x, 2, axis=-1)
    return x * cos + jnp.concatenate([-x2, x1], axis=-1) * sin


def _multiscale_pool(u, pool_w, pool_scale):
    B, L, _ = u.shape
    u32 = u.astype(jnp.float32)
    cs = jnp.concatenate([jnp.zeros_like(u32[:, :1]), jnp.cumsum(u32, axis=1)], axis=1)
    hi = jnp.arange(1, L + 1)
    outs = []
    for g, w in enumerate(POOL_WINDOWS):
        csg = cs[..., g * POOL_GROUP:(g + 1) * POOL_GROUP]
        lo = jnp.maximum(hi - w, 0)
        cnt = (hi - lo).astype(jnp.float32)[None, :, None]
        mean = (csg[:, hi] - csg[:, lo]) / cnt
        outs.append(mean - u32[..., g * POOL_GROUP:(g + 1) * POOL_GROUP])
    d = jnp.stack(outs, axis=2).astype(u.dtype)
    y = jnp.einsum('blgc,gcd->blgd', d, pool_w).reshape(B, L, POOL_WIDTH)
    return y * pool_scale


def _attend_block(q_blk, q_pos, k, v, k_pos):
    s = jnp.einsum('bqhd,bkhd->bhqk', q_blk, k, preferred_element_type=jnp.float32) * SOFTMAX_SCALE
    mask = k_pos[None, :] <= q_pos[:, None]
    s = jnp.where(mask[None, None], s, jnp.float32(-1e30))
    p = jax.nn.softmax(s, axis=-1).astype(v.dtype)
    return jnp.einsum('bhqk,bkhd->bqhd', p, v)


def _mla(c_q, c_kv, k_rope, q_a_norm, w_q_b, kv_a_norm, w_kv_b, cos, sin):
    B, L, _ = c_q.shape
    q = (_rmsnorm(c_q, q_a_norm) @ w_q_b).reshape(B, L, MLA_HEADS, QK_DIM)
    q_nope, q_pe = jnp.split(q, [QK_NOPE], axis=-1)
    q_pe = _rotate(q_pe, cos[:, None, :], sin[:, None, :])
    kv = (_rmsnorm(c_kv, kv_a_norm) @ w_kv_b).reshape(B, L, MLA_HEADS, QK_NOPE + V_DIM)
    k_nope, v = jnp.split(kv, [QK_NOPE], axis=-1)
    k_pe = _rotate(k_rope, cos, sin)
    q = jnp.concatenate([q_nope, q_pe], axis=-1)
    k = jnp.concatenate([k_nope, jnp.broadcast_to(k_pe[:, :, None, :], (B, L, MLA_HEADS, QK_ROPE))], axis=-1)
    pos = jnp.arange(L)
    o_meta = _attend_block(q[:, :N_META], pos[:N_META], k[:, :N_META], v[:, :N_META], pos[:N_META])
    n_blk = (L - N_META) // Q_BLOCK
    q_real = q[:, N_META:].reshape(B, n_blk, Q_BLOCK, MLA_HEADS, QK_DIM).transpose(1, 0, 2, 3, 4)
    pos_real = pos[N_META:].reshape(n_blk, Q_BLOCK)
    o_real = lax.map(lambda a: _attend_block(a[0], a[1], k, v, pos), (q_real, pos_real))
    o_real = o_real.transpose(1, 0, 2, 3, 4).reshape(B, L - N_META, MLA_HEADS, V_DIM)
    o = jnp.concatenate([o_meta, o_real], axis=1)
    return o.reshape(B, L, MLA_WIDTH)


def _hybrid_mixer(h, w_in, pool_w, pool_scale, w_pool_o, q_a_norm, w_q_b, kv_a_norm, w_kv_b,
                  w_mla_o, w_out, cos, sin):
    z = h @ w_in
    u_pool, c_q, c_kv, k_rope, g_pool, g_mla = jnp.split(z, SPLITS, axis=-1)
    y_pool = _multiscale_pool(u_pool, pool_w, pool_scale) @ w_pool_o
    y_mla = _mla(c_q, c_kv, k_rope, q_a_norm, w_q_b, kv_a_norm, w_kv_b, cos, sin) @ w_mla_o
    y = jax.nn.sigmoid(g_pool) * y_pool + jax.nn.sigmoid(g_mla) * y_mla
    return y @ w_out


def setup_inputs(seed: int = 0) -> dict:
    key = jax.random.key(seed)
    ks = jax.random.split(key, 32)

    def dense(k, shape, fan_in):
        return jax.random.normal(k, shape, jnp.float32) * (fan_in ** -0.5)

    def gain(k, shape):
        return 1.0 + 0.1 * jax.random.normal(k, shape, jnp.float32)

    return {
        'x': jax.random.normal(ks[0], (BATCH, SEQ, D_MODEL), jnp.float32),
        'meta_tokens': jax.random.normal(ks[1], (N_META, D_MODEL), jnp.float32),
        'norm_ffn1_pre': gain(ks[2], (DEPTH, D_MODEL)),
        'norm_ffn1_post': gain(ks[3], (DEPTH, D_MODEL)),
        'ffn1_w_gu': dense(ks[4], (DEPTH, D_MODEL, 2 * D_FF), D_MODEL),
        'ffn1_w_down': dense(ks[5], (DEPTH, D_FF, D_MODEL), D_FF),
        'norm_mix_pre': gain(ks[6], (DEPTH, D_MODEL)),
        'norm_mix_post': gain(ks[7], (DEPTH, D_MODEL)),
        'w_in': dense(ks[8], (DEPTH, D_MODEL, IN_COLS), D_MODEL),
        'pool_w': dense(ks[9], (DEPTH, len(POOL_WINDOWS), POOL_GROUP, POOL_GROUP), POOL_GROUP),
        'pool_scale': gain(ks[10], (DEPTH, POOL_WIDTH)),
        'w_pool_o': dense(ks[11], (DEPTH, POOL_WIDTH, D_MODEL), POOL_WIDTH),
        'q_a_norm': gain(ks[12], (DEPTH, Q_LORA)),
        'w_q_b': dense(ks[13], (DEPTH, Q_LORA, MLA_HEADS * QK_DIM), Q_LORA),
        'kv_a_norm': gain(ks[14], (DEPTH, KV_LORA)),
        'w_kv_b': dense(ks[15], (DEPTH, KV_LORA, MLA_HEADS * (QK_NOPE + V_DIM)), KV_LORA),
        'w_mla_o': dense(ks[16], (DEPTH, MLA_WIDTH, D_MODEL), MLA_WIDTH),
        'w_out': dense(ks[17], (DEPTH, D_MODEL, D_MODEL), D_MODEL),
        'norm_ffn2_pre': gain(ks[18], (DEPTH, D_MODEL)),
        'norm_ffn2_post': gain(ks[19], (DEPTH, D_MODEL)),
        'ffn2_w_gu': dense(ks[20], (DEPTH, D_MODEL, 2 * D_FF), D_MODEL),
        'ffn2_w_down': dense(ks[21], (DEPTH, D_FF, D_MODEL), D_FF),
    }


def reference(x, meta_tokens, norm_ffn1_pre, norm_ffn1_post, ffn1_w_gu, ffn1_w_down,
              norm_mix_pre, norm_mix_post, w_in, pool_w, pool_scale, w_pool_o,
              q_a_norm, w_q_b, kv_a_norm, w_kv_b, w_mla_o, w_out,
              norm_ffn2_pre, norm_ffn2_post, ffn2_w_gu, ffn2_w_down):
    B = x.shape[0]
    meta = jnp.broadcast_to(meta_tokens.astype(x.dtype)[None], (B, N_META, D_MODEL))
    h = jnp.concatenate([meta, x], axis=1)
    L = h.shape[1]
    cos, sin = _rope_tables(L, h.dtype)
    for i in range(DEPTH):
        h = h + 0.5 * _rmsnorm(_swiglu(_rmsnorm(h, norm_ffn1_pre[i]), ffn1_w_gu[i], ffn1_w_down[i]),
                               norm_ffn1_post[i])
        m = _hybrid_mixer(_rmsnorm(h, norm_mix_pre[i]), w_in[i], pool_w[i], pool_scale[i], w_pool_o[i],
                          q_a_norm[i], w_q_b[i], kv_a_norm[i], w_kv_b[i], w_mla_o[i], w_out[i], cos, sin)
        h = h + _rmsnorm(m, norm_mix_post[i])
        h = h + 0.5 * _rmsnorm(_swiglu(_rmsnorm(h, norm_ffn2_pre[i]), ffn2_w_gu[i], ffn2_w_down[i]),
                               norm_ffn2_post[i])
    return h[:, N_META:]
```

```python
import functools
import math

import jax
import jax.numpy as jnp
from jax import lax
from jax.experimental import pallas as pl
from jax.experimental.pallas import tpu as pltpu

F32 = jnp.float32
BF16 = jnp.bfloat16

N_META = 16
POOL_WINDOWS = (2, 4, 8, 16)
POOL_GROUP = 256
POOL_WIDTH = POOL_GROUP * len(POOL_WINDOWS)
MLA_HEADS = 16
Q_LORA = 512
KV_LORA = 512
QK_NOPE = 128
QK_ROPE = 64
V_DIM = 128
QK_DIM = QK_NOPE + QK_ROPE
ROPE_THETA = 10000.0
EPS = 1e-6
LANES = 128
HEAD_PAD = 2 * LANES
Q_SCALE = (QK_DIM ** -0.5) * math.log2(math.e)
MASK_VALUE = -1e30

VMEM_LIMIT = 56 * 1024 * 1024


def _rms(x):
    return x * lax.rsqrt(jnp.mean(x * x, axis=-1, keepdims=True) + EPS)


def _rope(x, cos, sin):
    rot = pltpu.roll(x, 32, 1) - pltpu.roll(x, 96, 1)
    return x * cos + rot * sin


def _params(sem):
    return pltpu.CompilerParams(dimension_semantics=sem, vmem_limit_bytes=VMEM_LIMIT)


def _ffn_kernel(x_ref, gpre_ref, gpost_ref, wg_ref, wu_ref, wd_ref, o_ref, xn_ref, acc_ref):
    f = pl.program_id(1)

    @pl.when(f == 0)
    def _():
        xn_ref[...] = (_rms(x_ref[...]) * gpre_ref[...]).astype(BF16)
        acc_ref[...] = jnp.zeros_like(acc_ref)

    xn = xn_ref[...]
    g = jnp.dot(xn, wg_ref[...], preferred_element_type=F32)
    u = jnp.dot(xn, wu_ref[...], preferred_element_type=F32)
    a = ((g * jax.nn.sigmoid(g)) * u).astype(BF16)
    acc_ref[...] += jnp.dot(a, wd_ref[...], preferred_element_type=F32)

    @pl.when(f == pl.num_programs(1) - 1)
    def _():
        o_ref[...] = x_ref[...] + 0.5 * (_rms(acc_ref[...]) * gpost_ref[...])


def _ffn(x, g_pre, g_post, w_gu, w_down, *, tm, tf):
    rows, d = x.shape
    d_ff = w_down.shape[0]
    nf = d_ff // tf
    return pl.pallas_call(
        _ffn_kernel,
        grid=(rows // tm, nf),
        in_specs=[
            pl.BlockSpec((tm, d), lambda i, f: (i, 0)),
            pl.BlockSpec((1, d), lambda i, f: (0, 0)),
            pl.BlockSpec((1, d), lambda i, f: (0, 0)),
            pl.BlockSpec((d, tf), lambda i, f: (0, f)),
            pl.BlockSpec((d, tf), lambda i, f: (0, f + nf)),
            pl.BlockSpec((tf, d), lambda i, f: (f, 0)),
        ],
        out_specs=pl.BlockSpec((tm, d), lambda i, f: (i, 0)),
        out_shape=jax.ShapeDtypeStruct((rows, d), F32),
        scratch_shapes=[pltpu.VMEM((tm, d), BF16), pltpu.VMEM((tm, d), F32)],
        compiler_params=_params(("parallel", "arbitrary")),
        name="ffn",
    )(x, g_pre, g_post, w_gu, w_gu, w_down)


def _inproj_kernel(h_ref, gmix_ref, w_ref, gq_ref, gkv_ref, cos_ref, sin_ref,
                   u_ref, cq_ref, ckv_ref, kpe_ref):
    hn = (_rms(h_ref[...]) * gmix_ref[...]).astype(BF16)
    c0, c1, c2, c3 = POOL_WIDTH, POOL_WIDTH + Q_LORA, POOL_WIDTH + Q_LORA + KV_LORA, w_ref.shape[1]
    u_ref[...] = jnp.dot(hn, w_ref[:, :c0], preferred_element_type=F32)
    cq = jnp.dot(hn, w_ref[:, c0:c1], preferred_element_type=F32)
    cq_ref[...] = (_rms(cq) * gq_ref[...]).astype(BF16)
    ckv = jnp.dot(hn, w_ref[:, c1:c2], preferred_element_type=F32)
    ckv_ref[...] = (_rms(ckv) * gkv_ref[...]).astype(BF16)
    kr = jnp.dot(hn, w_ref[:, c2:c3], preferred_element_type=F32)
    kpe_ref[...] = _rope(kr, cos_ref[...], sin_ref[...]).astype(BF16)


def _inproj(h, g_mix, w_b, g_q, g_kv, cos, sin, *, tm):
    rows, d = h.shape
    n_pos = cos.shape[0] // tm
    wcols = w_b.shape[1]
    row = lambda i: (i, 0)
    const = lambda i: (0, 0)
    pos = lambda i: (i % n_pos, 0)
    return pl.pallas_call(
        _inproj_kernel,
        grid=(rows // tm,),
        in_specs=[
            pl.BlockSpec((tm, d), row),
            pl.BlockSpec((1, d), const),
            pl.BlockSpec((d, wcols), const),
            pl.BlockSpec((1, Q_LORA), const),
            pl.BlockSpec((1, KV_LORA), const),
            pl.BlockSpec((tm, LANES), pos),
            pl.BlockSpec((tm, LANES), pos),
        ],
        out_specs=[
            pl.BlockSpec((tm, POOL_WIDTH), row),
            pl.BlockSpec((tm, Q_LORA), row),
            pl.BlockSpec((tm, KV_LORA), row),
            pl.BlockSpec((tm, LANES), row),
        ],
        out_shape=[
            jax.ShapeDtypeStruct((rows, POOL_WIDTH), F32),
            jax.ShapeDtypeStruct((rows, Q_LORA), BF16),
            jax.ShapeDtypeStruct((rows, KV_LORA), BF16),
            jax.ShapeDtypeStruct((rows, LANES), BF16),
        ],
        compiler_params=_params(("parallel",)),
        name="inproj",
    )(h, g_mix, w_b, g_q, g_kv, cos, sin)


def _qkv_kernel(cq_ref, ckv_ref, wq_ref, wk_ref, wvt_ref, cos_ref, sin_ref,
                q_ref, k_ref, vt_ref):
    cq = cq_ref[...]
    ckv = ckv_ref[...]
    cos = cos_ref[...] * Q_SCALE
    sin = sin_ref[...] * Q_SCALE
    for h in range(MLA_HEADS):
        lo = h * HEAD_PAD
        qh = jnp.dot(cq, wq_ref[:, lo:lo + HEAD_PAD], preferred_element_type=F32)
        q_ref[:, lo:lo + LANES] = (qh[:, :LANES] * Q_SCALE).astype(BF16)
        q_ref[:, lo + LANES:lo + HEAD_PAD] = _rope(qh[:, LANES:], cos, sin).astype(BF16)
    k_ref[...] = jnp.dot(ckv, wk_ref[...], preferred_element_type=F32).astype(BF16)
    vt = lax.dot_general(wvt_ref[...], ckv, (((1,), (1,)), ((), ())),
                         preferred_element_type=F32)
    tk = vt_ref.shape[-1]
    for c in range(vt_ref.shape[1]):
        vt_ref[0, c] = vt[:, c * tk:(c + 1) * tk].astype(BF16)


def _qkv(cq, ckv, w_q, w_k, w_vt, cos, sin, *, tm, tk, seq):
    rows = cq.shape[0]
    n_pos = seq // tm
    nh = MLA_HEADS
    row = lambda i: (i, 0)
    const = lambda i: (0, 0)
    pos = lambda i: (i % n_pos, 0)
    return pl.pallas_call(
        _qkv_kernel,
        grid=(rows // tm,),
        in_specs=[
            pl.BlockSpec((tm, Q_LORA), row),
            pl.BlockSpec((tm, KV_LORA), row),
            pl.BlockSpec((Q_LORA, nh * HEAD_PAD), const),
            pl.BlockSpec((KV_LORA, nh * QK_NOPE), const),
            pl.BlockSpec((nh * V_DIM, KV_LORA), const),
            pl.BlockSpec((tm, LANES), pos),
            pl.BlockSpec((tm, LANES), pos),
        ],
        out_specs=[
            pl.BlockSpec((tm, nh * HEAD_PAD), row),
            pl.BlockSpec((tm, nh * QK_NOPE), row),
            pl.BlockSpec((1, tm // tk, nh * V_DIM, tk), lambda i: (i // n_pos, i % n_pos, 0, 0)),
        ],
        out_shape=[
            jax.ShapeDtypeStruct((rows, nh * HEAD_PAD), BF16),
            jax.ShapeDtypeStruct((rows, nh * QK_NOPE), BF16),
            jax.ShapeDtypeStruct((rows // seq, seq // tk, nh * V_DIM, tk), BF16),
        ],
        compiler_params=_params(("parallel",)),
        name="qkv",
    )(cq, ckv, w_q, w_k, w_vt, cos, sin)


def _pool_kernel(u_ref, prev_ref, meta_ref, w_ref, scale_ref, o_ref, buf_ref):
    tp = u_ref.shape[1]
    first = pl.program_id(1) == 0
    halo = jnp.where(first, meta_ref[...], prev_ref[0])
    for g, win in enumerate(POOL_WINDOWS):
        cols = slice(g * POOL_GROUP, (g + 1) * POOL_GROUP)
        buf_ref[:N_META, :] = halo[:, cols]
        buf_ref[N_META:, :] = u_ref[0, :, cols]
        cur = buf_ref[N_META:, :]
        tot = cur
        for j in range(1, win):
            tot = tot + buf_ref[N_META - j:N_META - j + tp, :]
        dlt = (tot * (1.0 / win) - cur).astype(BF16)
        y = jnp.dot(dlt, w_ref[g], preferred_element_type=F32)
        o_ref[0, :, cols] = (y * scale_ref[:, cols]).astype(BF16)


def _pool(u, u_meta, pool_w, pool_scale, *, tp):
    b, s, c = u.shape
    hb = tp // N_META
    return pl.pallas_call(
        _pool_kernel,
        grid=(b, s // tp),
        in_specs=[
            pl.BlockSpec((1, tp, c), lambda bi, i: (bi, i, 0)),
            pl.BlockSpec((1, N_META, c), lambda bi, i: (bi, jnp.maximum(i * hb - 1, 0), 0)),
            pl.BlockSpec((N_META, c), lambda bi, i: (0, 0)),
            pl.BlockSpec((len(POOL_WINDOWS), POOL_GROUP, POOL_GROUP), lambda bi, i: (0, 0, 0)),
            pl.BlockSpec((1, c), lambda bi, i: (0, 0)),
        ],
        out_specs=pl.BlockSpec((1, tp, c), lambda bi, i: (bi, i, 0)),
        out_shape=jax.ShapeDtypeStruct((b, s, c), BF16),
        scratch_shapes=[pltpu.VMEM((tp + N_META, POOL_GROUP), F32)],
        compiler_params=_params(("parallel", "arbitrary")),
        name="pool",
    )(u, u, u_meta, pool_w, pool_scale)


def _attn_kernel(q_ref, kn_ref, kp_ref, vt_ref, knm_ref, kpm_ref, vtm_ref, o_ref, acc_ref):
    tq = q_ref.shape[1]
    tk = vt_ref.shape[-1]
    i = pl.program_id(2)
    q = q_ref[0]
    nt = (((1,), (1,)), ((), ()))

    km = jnp.concatenate([knm_ref[...], kpm_ref[...]], axis=1)
    s = lax.dot_general(km, q, nt, preferred_element_type=F32)
    m0 = jnp.max(s, axis=0, keepdims=True)
    p = jnp.exp2(s - m0)
    l0 = jnp.sum(p, axis=0, keepdims=True)
    acc_ref[...] = jnp.dot(vtm_ref[...], p.astype(BF16), preferred_element_type=F32)

    def step(j, m, l, masked):
        start = pl.multiple_of(j * tk, tk)
        k = jnp.concatenate([kn_ref[0, pl.ds(start, tk), :], kp_ref[0, pl.ds(start, tk), :]],
                            axis=1)
        s = lax.dot_general(k, q, nt, preferred_element_type=F32)
        if masked:
            kpos = lax.broadcasted_iota(jnp.int32, s.shape, 0)
            qpos = lax.broadcasted_iota(jnp.int32, s.shape, 1)
            s = jnp.where(kpos <= qpos, s, MASK_VALUE)
        m_new = jnp.maximum(m, jnp.max(s, axis=0, keepdims=True))
        alpha = jnp.exp2(m - m_new)
        p = jnp.exp2(s - m_new)
        l_new = alpha * l + jnp.sum(p, axis=0, keepdims=True)
        pv = jnp.dot(vt_ref[0, j], p.astype(BF16), preferred_element_type=F32)
        acc_ref[...] = alpha * acc_ref[...] + pv
        return m_new, l_new

    m, l = lax.fori_loop(0, i, lambda j, c: step(j, c[0], c[1], False), (m0, l0))
    m, l = step(i, m, l, True)
    o_ref[0] = (acc_ref[...] / l).T.astype(BF16)


def _attn(q, k_nope, k_pe, vt, knm, kpm, vtm, *, tq):
    b, s, _ = q.shape
    tk = vt.shape[-1]
    assert tq == tk
    nh = MLA_HEADS
    return pl.pallas_call(
        _attn_kernel,
        grid=(b, nh, s // tq),
        in_specs=[
            pl.BlockSpec((1, tq, HEAD_PAD), lambda bi, h, i: (bi, i, h)),
            pl.BlockSpec((1, s, QK_NOPE), lambda bi, h, i: (bi, 0, h)),
            pl.BlockSpec((1, s, LANES), lambda bi, h, i: (bi, 0, 0)),
            pl.BlockSpec((1, s // tk, V_DIM, tk), lambda bi, h, i: (bi, 0, h, 0)),
            pl.BlockSpec((N_META, QK_NOPE), lambda bi, h, i: (0, h)),
            pl.BlockSpec((N_META, LANES), lambda bi, h, i: (0, 0)),
            pl.BlockSpec((V_DIM, N_META), lambda bi, h, i: (h, 0)),
        ],
        out_specs=pl.BlockSpec((1, tq, V_DIM), lambda bi, h, i: (bi, i, h)),
        out_shape=jax.ShapeDtypeStruct((b, s, nh * V_DIM), BF16),
        scratch_shapes=[pltpu.VMEM((V_DIM, tq), F32)],
        compiler_params=_params(("parallel", "parallel", "arbitrary")),
        name="attn",
    )(q, k_nope, k_pe, vt, knm, kpm, vtm)


def _mixout_kernel(h_ref, gmix_ref, gpost_ref, pool_ref, attn_ref,
                   wgp_ref, wgm_ref, wpo_ref, wmo_ref, wout_ref, o_ref, hn_ref, acc_ref):
    c = pl.program_id(1)

    @pl.when(c == 0)
    def _():
        hn_ref[...] = (_rms(h_ref[...]) * gmix_ref[...]).astype(BF16)
        acc_ref[...] = jnp.zeros_like(acc_ref)

    hn = hn_ref[...]
    g_pool = jnp.dot(hn, wgp_ref[...], preferred_element_type=F32)
    g_mla = jnp.dot(hn, wgm_ref[...], preferred_element_type=F32)
    y_pool = jnp.dot(pool_ref[...], wpo_ref[...], preferred_element_type=F32)
    y_mla = jnp.dot(attn_ref[...], wmo_ref[...], preferred_element_type=F32)
    y = (jax.nn.sigmoid(g_pool) * y_pool + jax.nn.sigmoid(g_mla) * y_mla).astype(BF16)
    acc_ref[...] += jnp.dot(y, wout_ref[...], preferred_element_type=F32)

    @pl.when(c == pl.num_programs(1) - 1)
    def _():
        o_ref[...] = h_ref[...] + _rms(acc_ref[...]) * gpost_ref[...]


def _mixout(h, g_mix, g_post, pool, attn, w_gp, w_gm, w_po, w_mo, w_out, *, tm, tn):
    rows, d = h.shape
    row = lambda i, c: (i, 0)
    const = lambda i, c: (0, 0)
    col = lambda i, c: (0, c)
    return pl.pallas_call(
        _mixout_kernel,
        grid=(rows // tm, d // tn),
        in_specs=[
            pl.BlockSpec((tm, d), row),
            pl.BlockSpec((1, d), const),
            pl.BlockSpec((1, d), const),
            pl.BlockSpec((tm, pool.shape[1]), row),
            pl.BlockSpec((tm, attn.shape[1]), row),
            pl.BlockSpec((d, tn), col),
            pl.BlockSpec((d, tn), col),
            pl.BlockSpec((pool.shape[1], tn), col),
            pl.BlockSpec((attn.shape[1], tn), col),
            pl.BlockSpec((tn, d), lambda i, c: (c, 0)),
        ],
        out_specs=pl.BlockSpec((tm, d), row),
        out_shape=jax.ShapeDtypeStruct((rows, d), F32),
        scratch_shapes=[pltpu.VMEM((tm, d), BF16), pltpu.VMEM((tm, d), F32)],
        compiler_params=_params(("parallel", "arbitrary")),
        name="mixout",
    )(h, g_mix, g_post, pool, attn, w_gp, w_gm, w_po, w_mo, w_out)


def _rope_tables(n_pos):
    pos = jnp.arange(n_pos, dtype=F32)
    inv = ROPE_THETA ** (-jnp.arange(0, QK_ROPE, 2, dtype=F32) / QK_ROPE)
    ang = pos[:, None] * inv[None, :]
    ang = jnp.concatenate([ang, ang], axis=-1)
    pad = jnp.zeros((n_pos, LANES - QK_ROPE), F32)
    return (jnp.concatenate([jnp.cos(ang), pad], axis=-1),
            jnp.concatenate([jnp.sin(ang), pad], axis=-1))


def kernel(x, meta_tokens, norm_ffn1_pre, norm_ffn1_post, ffn1_w_gu, ffn1_w_down, norm_mix_pre, norm_mix_post, w_in, pool_w, pool_scale, w_pool_o, q_a_norm, w_q_b, kv_a_norm, w_kv_b, w_mla_o, w_out, norm_ffn2_pre, norm_ffn2_post, ffn2_w_gu, ffn2_w_down):
    bsz, seq, d = x.shape
    depth = w_in.shape[0]
    nh = MLA_HEADS
    tm, tf, tn, tq, tp = 512, 512, 512, 512, 512

    cos, sin = _rope_tables(N_META + seq)
    cos_m, sin_m, cos_r, sin_r = cos[:N_META], sin[:N_META], cos[N_META:], sin[N_META:]

    h = x.reshape(bsz * seq, d)
    hm = meta_tokens.astype(x.dtype)
    row = lambda v: v.reshape(1, -1)

    for i in range(depth):
        w_gu1, w_dn1 = ffn1_w_gu[i].astype(BF16), ffn1_w_down[i].astype(BF16)
        w_gu2, w_dn2 = ffn2_w_gu[i].astype(BF16), ffn2_w_down[i].astype(BF16)
        n_lat = POOL_WIDTH + Q_LORA + KV_LORA + QK_ROPE
        w_b = jnp.concatenate([w_in[i][:, :n_lat], jnp.zeros((d, LANES - QK_ROPE), F32)],
                              axis=1).astype(BF16)
        w_gp = w_in[i][:, n_lat:n_lat + d].astype(BF16)
        w_gm = w_in[i][:, n_lat + d:].astype(BF16)
        w_q = jnp.pad(w_q_b[i].reshape(Q_LORA, nh, QK_DIM),
                      ((0, 0), (0, 0), (0, HEAD_PAD - QK_DIM))).reshape(Q_LORA, nh * HEAD_PAD)
        w_q = w_q.astype(BF16)
        w_kv = w_kv_b[i].reshape(KV_LORA, nh, QK_NOPE + V_DIM)
        w_k = w_kv[:, :, :QK_NOPE].reshape(KV_LORA, nh * QK_NOPE).astype(BF16)
        w_vt = w_kv[:, :, QK_NOPE:].reshape(KV_LORA, nh * V_DIM).T.astype(BF16)
        w_po, w_mo, w_o = w_pool_o[i].astype(BF16), w_mla_o[i].astype(BF16), w_out[i].astype(BF16)
        p_w = pool_w[i].astype(BF16)

        h1 = _ffn(h, row(norm_ffn1_pre[i]), row(norm_ffn1_post[i]), w_gu1, w_dn1, tm=tm, tf=tf)
        h1m = _ffn(hm, row(norm_ffn1_pre[i]), row(norm_ffn1_post[i]), w_gu1, w_dn1,
                   tm=N_META, tf=tf)

        u, cq, ckv, kpe = _inproj(h1, row(norm_mix_pre[i]), w_b, row(q_a_norm[i]),
                                  row(kv_a_norm[i]), cos_r, sin_r, tm=tm)
        um, cqm, ckvm, kpem = _inproj(h1m, row(norm_mix_pre[i]), w_b, row(q_a_norm[i]),
                                      row(kv_a_norm[i]), cos_m, sin_m, tm=N_META)
        q, kn, vt = _qkv(cq, ckv, w_q, w_k, w_vt, cos_r, sin_r, tm=tm, tk=tq, seq=seq)
        _, knm, vtm = _qkv(cqm, ckvm, w_q, w_k, w_vt, cos_m, sin_m,
                           tm=N_META, tk=N_META, seq=N_META)

        pooled = _pool(u.reshape(bsz, seq, POOL_WIDTH), um, p_w, row(pool_scale[i]), tp=tp)
        attn = _attn(q.reshape(bsz, seq, nh * HEAD_PAD), kn.reshape(bsz, seq, nh * QK_NOPE),
                     kpe.reshape(bsz, seq, LANES), vt, knm, kpem,
                     vtm.reshape(nh * V_DIM, N_META), tq=tq)

        h2 = _mixout(h1, row(norm_mix_pre[i]), row(norm_mix_post[i]),
                     pooled.reshape(bsz * seq, POOL_WIDTH), attn.reshape(bsz * seq, nh * V_DIM),
                     w_gp, w_gm, w_po, w_mo, w_o, tm=tm, tn=tn)

        h = _ffn(h2, row(norm_ffn2_pre[i]), row(norm_ffn2_post[i]), w_gu2, w_dn2, tm=tm, tf=tf)
        if i + 1 < depth:
            raise NotImplementedError("only DEPTH == 1 is supported")

    return h.reshape(bsz, seq, d)
```

```python
import functools
import math

import jax
import jax.numpy as jnp
from jax import lax
from jax.experimental import pallas as pl
from jax.experimental.pallas import tpu as pltpu

F32 = jnp.float32
BF16 = jnp.bfloat16

N_META = 16
POOL_WINDOWS = (2, 4, 8, 16)
POOL_GROUP = 256
POOL_WIDTH = POOL_GROUP * len(POOL_WINDOWS)
MLA_HEADS = 16
Q_LORA = 512
KV_LORA = 512
QK_NOPE = 128
QK_ROPE = 64
V_DIM = 128
QK_DIM = QK_NOPE + QK_ROPE
ROPE_THETA = 10000.0
EPS = 1e-6
LANES = 128
HEAD_PAD = 2 * LANES
Q_SCALE = (QK_DIM ** -0.5) * math.log2(math.e)
MASK_VALUE = -1e30

VMEM_LIMIT = 56 * 1024 * 1024


def _rms(x):
    return x * lax.rsqrt(jnp.mean(x * x, axis=-1, keepdims=True) + EPS)


def _rope(x, cos, sin):
    rot = pltpu.roll(x, 32, 1) - pltpu.roll(x, 96, 1)
    return x * cos + rot * sin


def _params(sem):
    return pltpu.CompilerParams(dimension_semantics=sem, vmem_limit_bytes=VMEM_LIMIT)


def _ffn_kernel(x_ref, gpre_ref, gpost_ref, wg_ref, wu_ref, wd_ref, o_ref, xn_ref, acc_ref):
    f = pl.program_id(1)

    @pl.when(f == 0)
    def _():
        xn_ref[...] = (_rms(x_ref[...]) * gpre_ref[...]).astype(BF16)
        acc_ref[...] = jnp.zeros_like(acc_ref)

    xn = xn_ref[...]
    g = jnp.dot(xn, wg_ref[...], preferred_element_type=F32)
    u = jnp.dot(xn, wu_ref[...], preferred_element_type=F32)
    a = ((g * jax.nn.sigmoid(g)) * u).astype(BF16)
    acc_ref[...] += jnp.dot(a, wd_ref[...], preferred_element_type=F32)

    @pl.when(f == pl.num_programs(1) - 1)
    def _():
        o_ref[...] = x_ref[...] + 0.5 * (_rms(acc_ref[...]) * gpost_ref[...])


def _ffn(x, g_pre, g_post, w_gu, w_down, *, tm, tf):
    rows, d = x.shape
    d_ff = w_down.shape[0]
    nf = d_ff // tf
    return pl.pallas_call(
        _ffn_kernel,
        grid=(rows // tm, nf),
        in_specs=[
            pl.BlockSpec((tm, d), lambda i, f: (i, 0)),
            pl.BlockSpec((1, d), lambda i, f: (0, 0)),
            pl.BlockSpec((1, d), lambda i, f: (0, 0)),
            pl.BlockSpec((d, tf), lambda i, f: (0, f)),
            pl.BlockSpec((d, tf), lambda i, f: (0, f + nf)),
            pl.BlockSpec((tf, d), lambda i, f: (f, 0)),
        ],
        out_specs=pl.BlockSpec((tm, d), lambda i, f: (i, 0)),
        out_shape=jax.ShapeDtypeStruct((rows, d), F32),
        scratch_shapes=[pltpu.VMEM((tm, d), BF16), pltpu.VMEM((tm, d), F32)],
        compiler_params=_params(("parallel", "arbitrary")),
        name="ffn",
    )(x, g_pre, g_post, w_gu, w_gu, w_down)


def _inproj_kernel(h_ref, gmix_ref, w_ref, gq_ref, gkv_ref, cos_ref, sin_ref,
                   u_ref, cq_ref, ckv_ref, kpe_ref):
    hn = (_rms(h_ref[...]) * gmix_ref[...]).astype(BF16)
    c0, c1, c2, c3 = POOL_WIDTH, POOL_WIDTH + Q_LORA, POOL_WIDTH + Q_LORA + KV_LORA, w_ref.shape[1]
    u_ref[...] = jnp.dot(hn, w_ref[:, :c0], preferred_element_type=F32)
    cq = jnp.dot(hn, w_ref[:, c0:c1], preferred_element_type=F32)
    cq_ref[...] = (_rms(cq) * gq_ref[...]).astype(BF16)
    ckv = jnp.dot(hn, w_ref[:, c1:c2], preferred_element_type=F32)
    ckv_ref[...] = (_rms(ckv) * gkv_ref[...]).astype(BF16)
    kr = jnp.dot(hn, w_ref[:, c2:c3], preferred_element_type=F32)
    kpe_ref[...] = _rope(kr, cos_ref[...], sin_ref[...]).astype(BF16)


def _inproj(h, g_mix, w_b, g_q, g_kv, cos, sin, *, tm):
    rows, d = h.shape
    n_pos = cos.shape[0] // tm
    wcols = w_b.shape[1]
    row = lambda i: (i, 0)
    const = lambda i: (0, 0)
    pos = lambda i: (i % n_pos, 0)
    return pl.pallas_call(
        _inproj_kernel,
        grid=(rows // tm,),
        in_specs=[
            pl.BlockSpec((tm, d), row),
            pl.BlockSpec((1, d), const),
            pl.BlockSpec((d, wcols), const),
            pl.BlockSpec((1, Q_LORA), const),
            pl.BlockSpec((1, KV_LORA), const),
            pl.BlockSpec((tm, LANES), pos),
            pl.BlockSpec((tm, LANES), pos),
        ],
        out_specs=[
            pl.BlockSpec((tm, POOL_WIDTH), row),
            pl.BlockSpec((tm, Q_LORA), row),
            pl.BlockSpec((tm, KV_LORA), row),
            pl.BlockSpec((tm, LANES), row),
        ],
        out_shape=[
            jax.ShapeDtypeStruct((rows, POOL_WIDTH), F32),
            jax.ShapeDtypeStruct((rows, Q_LORA), BF16),
            jax.ShapeDtypeStruct((rows, KV_LORA), BF16),
            jax.ShapeDtypeStruct((rows, LANES), BF16),
        ],
        compiler_params=_params(("parallel",)),
        name="inproj",
    )(h, g_mix, w_b, g_q, g_kv, cos, sin)


def _qkv_kernel(cq_ref, ckv_ref, wq_ref, wk_ref, wvt_ref, cos_ref, sin_ref,
                q_ref, k_ref, vt_ref):
    cq = cq_ref[...]
    ckv = ckv_ref[...]
    cos = cos_ref[...] * Q_SCALE
    sin = sin_ref[...] * Q_SCALE
    for h in range(MLA_HEADS):
        lo = h * HEAD_PAD
        qh = jnp.dot(cq, wq_ref[:, lo:lo + HEAD_PAD], preferred_element_type=F32)
        q_ref[:, lo:lo + LANES] = (qh[:, :LANES] * Q_SCALE).astype(BF16)
        q_ref[:, lo + LANES:lo + HEAD_PAD] = _rope(qh[:, LANES:], cos, sin).astype(BF16)
    k_ref[...] = jnp.dot(ckv, wk_ref[...], preferred_element_type=F32).astype(BF16)
    vt = lax.dot_general(wvt_ref[...], ckv, (((1,), (1,)), ((), ())),
                         preferred_element_type=F32)
    tk = vt_ref.shape[-1]
    for c in range(vt_ref.shape[1]):
        vt_ref[0, c] = vt[:, c * tk:(c + 1) * tk].astype(BF16)


def _qkv(cq, ckv, w_q, w_k, w_vt, cos, sin, *, tm, tk, seq):
    rows = cq.shape[0]
    n_pos = seq // tm
    nh = MLA_HEADS
    row = lambda i: (i, 0)
    const = lambda i: (0, 0)
    pos = lambda i: (i % n_pos, 0)
    return pl.pallas_call(
        _qkv_kernel,
        grid=(rows // tm,),
        in_specs=[
            pl.BlockSpec((tm, Q_LORA), row),
            pl.BlockSpec((tm, KV_LORA), row),
            pl.BlockSpec((Q_LORA, nh * HEAD_PAD), const),
            pl.BlockSpec((KV_LORA, nh * QK_NOPE), const),
            pl.BlockSpec((nh * V_DIM, KV_LORA), const),
            pl.BlockSpec((tm, LANES), pos),
            pl.BlockSpec((tm, LANES), pos),
        ],
        out_specs=[
            pl.BlockSpec((tm, nh * HEAD_PAD), row),
            pl.BlockSpec((tm, nh * QK_NOPE), row),
            pl.BlockSpec((1, tm // tk, nh * V_DIM, tk), lambda i: (i // n_pos, i % n_pos, 0, 0)),
        ],
        out_shape=[
            jax.ShapeDtypeStruct((rows, nh * HEAD_PAD), BF16),
            jax.ShapeDtypeStruct((rows, nh * QK_NOPE), BF16),
            jax.ShapeDtypeStruct((rows // seq, seq // tk, nh * V_DIM, tk), BF16),
        ],
        compiler_params=_params(("parallel",)),
        name="qkv",
    )(cq, ckv, w_q, w_k, w_vt, cos, sin)


def _pool_kernel(u_ref, prev_ref, meta_ref, w_ref, scale_ref, o_ref, buf_ref):
    tp = u_ref.shape[1]
    first = pl.program_id(1) == 0
    halo = jnp.where(first, meta_ref[...], prev_ref[0])
    for g, win in enumerate(POOL_WINDOWS):
        cols = slice(g * POOL_GROUP, (g + 1) * POOL_GROUP)
        buf_ref[:N_META, :] = halo[:, cols]
        buf_ref[N_META:, :] = u_ref[0, :, cols]
        cur = buf_ref[N_META:, :]
        tot = cur
        for j in range(1, win):
            tot = tot + buf_ref[N_META - j:N_META - j + tp, :]
        dlt = (tot * (1.0 / win) - cur).astype(BF16)
        y = jnp.dot(dlt, w_ref[g], preferred_element_type=F32)
        o_ref[0, :, cols] = (y * scale_ref[:, cols]).astype(BF16)


def _pool(u, u_meta, pool_w, pool_scale, *, tp):
    b, s, c = u.shape
    hb = tp // N_META
    return pl.pallas_call(
        _pool_kernel,
        grid=(b, s // tp),
        in_specs=[
            pl.BlockSpec((1, tp, c), lambda bi, i: (bi, i, 0)),
            pl.BlockSpec((1, N_META, c), lambda bi, i: (bi, jnp.maximum(i * hb - 1, 0), 0)),
            pl.BlockSpec((N_META, c), lambda bi, i: (0, 0)),
            pl.BlockSpec((len(POOL_WINDOWS), POOL_GROUP, POOL_GROUP), lambda bi, i: (0, 0, 0)),
            pl.BlockSpec((1, c), lambda bi, i: (0, 0)),
        ],
        out_specs=pl.BlockSpec((1, tp, c), lambda bi, i: (bi, i, 0)),
        out_shape=jax.ShapeDtypeStruct((b, s, c), BF16),
        scratch_shapes=[pltpu.VMEM((tp + N_META, POOL_GROUP), F32)],
        compiler_params=_params(("parallel", "arbitrary")),
        name="pool",
    )(u, u, u_meta, pool_w, pool_scale)


def _attn_kernel(q_ref, kn_ref, kp_ref, vt_ref, knm_ref, kpm_ref, vtm_ref, o_ref, acc_ref, s_ref):
    tk = vt_ref.shape[-1]
    hp = acc_ref.shape[0]
    i = pl.program_id(2)
    nt = (((1,), (1,)), ((), ()))

    def q_of(h):
        return q_ref[0, :, h * HEAD_PAD:(h + 1) * HEAD_PAD]

    def scores(h, j, masked):
        start = pl.multiple_of(j * tk, tk)
        k = jnp.concatenate([kn_ref[0, pl.ds(start, tk), h * QK_NOPE:(h + 1) * QK_NOPE],
                             kp_ref[0, pl.ds(start, tk), :]], axis=1)
        s = lax.dot_general(k, q_of(h), nt, preferred_element_type=F32)
        if masked:
            kpos = lax.broadcasted_iota(jnp.int32, s.shape, 0)
            qpos = lax.broadcasted_iota(jnp.int32, s.shape, 1)
            s = jnp.where(kpos <= qpos, s, MASK_VALUE)
        s_ref[h] = s
        return jnp.max(s, axis=0, keepdims=True)

    def absorb(h, j, m, l, bmax):
        m_new = jnp.maximum(m, bmax)
        alpha = jnp.exp2(m - m_new)
        p = jnp.exp2(s_ref[h] - m_new)
        pv = jnp.dot(vt_ref[0, j, h * V_DIM:(h + 1) * V_DIM, :], p.astype(BF16),
                     preferred_element_type=F32)
        acc_ref[h] = alpha * acc_ref[h] + pv
        return m_new, alpha * l + jnp.sum(p, axis=0, keepdims=True)

    kpm = kpm_ref[...]
    state = []
    for h in range(hp):
        km = jnp.concatenate([knm_ref[:, h * QK_NOPE:(h + 1) * QK_NOPE], kpm], axis=1)
        s = lax.dot_general(km, q_of(h), nt, preferred_element_type=F32)
        m0 = jnp.max(s, axis=0, keepdims=True)
        p = jnp.exp2(s - m0)
        acc_ref[h] = jnp.dot(vtm_ref[h * V_DIM:(h + 1) * V_DIM, :], p.astype(BF16),
                             preferred_element_type=F32)
        state.extend([m0, jnp.sum(p, axis=0, keepdims=True), scores(h, i, True)])

    def body(j, state):
        prev = jnp.where(j == 0, i, j - 1)
        out = []
        for h in range(hp):
            m, l, bmax = state[3 * h:3 * h + 3]
            m, l = absorb(h, prev, m, l, bmax)
            out.extend([m, l, scores(h, j, False)])
        return tuple(out)

    state = lax.fori_loop(0, i, body, tuple(state))
    last = jnp.where(i == 0, i, i - 1)
    for h in range(hp):
        m, l, bmax = state[3 * h:3 * h + 3]
        m, l = absorb(h, last, m, l, bmax)
        o_ref[0, :, h * V_DIM:(h + 1) * V_DIM] = (acc_ref[h] / l).T.astype(BF16)


def _attn(q, k_nope, k_pe, vt, knm, kpm, vtm, *, tq, hp):
    b, s, _ = q.shape
    tk = vt.shape[-1]
    assert tq == tk
    nh = MLA_HEADS
    return pl.pallas_call(
        _attn_kernel,
        grid=(b, nh // hp, s // tq),
        in_specs=[
            pl.BlockSpec((1, tq, hp * HEAD_PAD), lambda bi, h, i: (bi, i, h)),
            pl.BlockSpec((1, s, hp * QK_NOPE), lambda bi, h, i: (bi, 0, h)),
            pl.BlockSpec((1, s, LANES), lambda bi, h, i: (bi, 0, 0)),
            pl.BlockSpec((1, s // tk, hp * V_DIM, tk), lambda bi, h, i: (bi, 0, h, 0)),
            pl.BlockSpec((N_META, hp * QK_NOPE), lambda bi, h, i: (0, h)),
            pl.BlockSpec((N_META, LANES), lambda bi, h, i: (0, 0)),
            pl.BlockSpec((hp * V_DIM, N_META), lambda bi, h, i: (h, 0)),
        ],
        out_specs=pl.BlockSpec((1, tq, hp * V_DIM), lambda bi, h, i: (bi, i, h)),
        out_shape=jax.ShapeDtypeStruct((b, s, nh * V_DIM), BF16),
        scratch_shapes=[pltpu.VMEM((hp, V_DIM, tq), F32), pltpu.VMEM((hp, tk, tq), F32)],
        compiler_params=_params(("parallel", "parallel", "arbitrary")),
        name="attn",
    )(q, k_nope, k_pe, vt, knm, kpm, vtm)


def _mixout_kernel(h_ref, gmix_ref, gpost_ref, pool_ref, attn_ref,
                   wgp_ref, wgm_ref, wpo_ref, wmo_ref, wout_ref, o_ref, hn_ref, acc_ref):
    c = pl.program_id(1)

    @pl.when(c == 0)
    def _():
        hn_ref[...] = (_rms(h_ref[...]) * gmix_ref[...]).astype(BF16)
        acc_ref[...] = jnp.zeros_like(acc_ref)

    hn = hn_ref[...]
    g_pool = jnp.dot(hn, wgp_ref[...], preferred_element_type=F32)
    g_mla = jnp.dot(hn, wgm_ref[...], preferred_element_type=F32)
    y_pool = jnp.dot(pool_ref[...], wpo_ref[...], preferred_element_type=F32)
    y_mla = jnp.dot(attn_ref[...], wmo_ref[...], preferred_element_type=F32)
    y = (jax.nn.sigmoid(g_pool) * y_pool + jax.nn.sigmoid(g_mla) * y_mla).astype(BF16)
    acc_ref[...] += jnp.dot(y, wout_ref[...], preferred_element_type=F32)

    @pl.when(c == pl.num_programs(1) - 1)
    def _():
        o_ref[...] = h_ref[...] + _rms(acc_ref[...]) * gpost_ref[...]


def _mixout(h, g_mix, g_post, pool, attn, w_gp, w_gm, w_po, w_mo, w_out, *, tm, tn):
    rows, d = h.shape
    row = lambda i, c: (i, 0)
    const = lambda i, c: (0, 0)
    col = lambda i, c: (0, c)
    return pl.pallas_call(
        _mixout_kernel,
        grid=(rows // tm, d // tn),
        in_specs=[
            pl.BlockSpec((tm, d), row),
            pl.BlockSpec((1, d), const),
            pl.BlockSpec((1, d), const),
            pl.BlockSpec((tm, pool.shape[1]), row),
            pl.BlockSpec((tm, attn.shape[1]), row),
            pl.BlockSpec((d, tn), col),
            pl.BlockSpec((d, tn), col),
            pl.BlockSpec((pool.shape[1], tn), col),
            pl.BlockSpec((attn.shape[1], tn), col),
            pl.BlockSpec((tn, d), lambda i, c: (c, 0)),
        ],
        out_specs=pl.BlockSpec((tm, d), row),
        out_shape=jax.ShapeDtypeStruct((rows, d), F32),
        scratch_shapes=[pltpu.VMEM((tm, d), BF16), pltpu.VMEM((tm, d), F32)],
        compiler_params=_params(("parallel", "arbitrary")),
        name="mixout",
    )(h, g_mix, g_post, pool, attn, w_gp, w_gm, w_po, w_mo, w_out)


def _rope_tables(n_pos):
    pos = jnp.arange(n_pos, dtype=F32)
    inv = ROPE_THETA ** (-jnp.arange(0, QK_ROPE, 2, dtype=F32) / QK_ROPE)
    ang = pos[:, None] * inv[None, :]
    ang = jnp.concatenate([ang, ang], axis=-1)
    pad = jnp.zeros((n_pos, LANES - QK_ROPE), F32)
    return (jnp.concatenate([jnp.cos(ang), pad], axis=-1),
            jnp.concatenate([jnp.sin(ang), pad], axis=-1))


def kernel(x, meta_tokens, norm_ffn1_pre, norm_ffn1_post, ffn1_w_gu, ffn1_w_down, norm_mix_pre, norm_mix_post, w_in, pool_w, pool_scale, w_pool_o, q_a_norm, w_q_b, kv_a_norm, w_kv_b, w_mla_o, w_out, norm_ffn2_pre, norm_ffn2_post, ffn2_w_gu, ffn2_w_down):
    bsz, seq, d = x.shape
    depth = w_in.shape[0]
    nh = MLA_HEADS
    tm, tf, tn, tq, tp = 512, 512, 512, 512, 512

    cos, sin = _rope_tables(N_META + seq)
    cos_m, sin_m, cos_r, sin_r = cos[:N_META], sin[:N_META], cos[N_META:], sin[N_META:]

    h = x.reshape(bsz * seq, d)
    hm = meta_tokens.astype(x.dtype)
    row = lambda v: v.reshape(1, -1)

    for i in range(depth):
        w_gu1, w_dn1 = ffn1_w_gu[i].astype(BF16), ffn1_w_down[i].astype(BF16)
        w_gu2, w_dn2 = ffn2_w_gu[i].astype(BF16), ffn2_w_down[i].astype(BF16)
        n_lat = POOL_WIDTH + Q_LORA + KV_LORA + QK_ROPE
        w_b = jnp.concatenate([w_in[i][:, :n_lat], jnp.zeros((d, LANES - QK_ROPE), F32)],
                              axis=1).astype(BF16)
        w_gp = w_in[i][:, n_lat:n_lat + d].astype(BF16)
        w_gm = w_in[i][:, n_lat + d:].astype(BF16)
        w_q = jnp.pad(w_q_b[i].reshape(Q_LORA, nh, QK_DIM),
                      ((0, 0), (0, 0), (0, HEAD_PAD - QK_DIM))).reshape(Q_LORA, nh * HEAD_PAD)
        w_q = w_q.astype(BF16)
        w_kv = w_kv_b[i].reshape(KV_LORA, nh, QK_NOPE + V_DIM)
        w_k = w_kv[:, :, :QK_NOPE].reshape(KV_LORA, nh * QK_NOPE).astype(BF16)
        w_vt = w_kv[:, :, QK_NOPE:].reshape(KV_LORA, nh * V_DIM).T.astype(BF16)
        w_po, w_mo, w_o = w_pool_o[i].astype(BF16), w_mla_o[i].astype(BF16), w_out[i].astype(BF16)
        p_w = pool_w[i].astype(BF16)

        h1 = _ffn(h, row(norm_ffn1_pre[i]), row(norm_ffn1_post[i]), w_gu1, w_dn1, tm=tm, tf=tf)
        h1m = _ffn(hm, row(norm_ffn1_pre[i]), row(norm_ffn1_post[i]), w_gu1, w_dn1,
                   tm=N_META, tf=tf)

        u, cq, ckv, kpe = _inproj(h1, row(norm_mix_pre[i]), w_b, row(q_a_norm[i]),
                                  row(kv_a_norm[i]), cos_r, sin_r, tm=tm)
        um, cqm, ckvm, kpem = _inproj(h1m, row(norm_mix_pre[i]), w_b, row(q_a_norm[i]),
                                      row(kv_a_norm[i]), cos_m, sin_m, tm=N_META)
        q, kn, vt = _qkv(cq, ckv, w_q, w_k, w_vt, cos_r, sin_r, tm=tm, tk=tq, seq=seq)
        _, knm, vtm = _qkv(cqm, ckvm, w_q, w_k, w_vt, cos_m, sin_m,
                           tm=N_META, tk=N_META, seq=N_META)

        pooled = _pool(u.reshape(bsz, seq, POOL_WIDTH), um, p_w, row(pool_scale[i]), tp=tp)
        attn = _attn(q.reshape(bsz, seq, nh * HEAD_PAD), kn.reshape(bsz, seq, nh * QK_NOPE),
                     kpe.reshape(bsz, seq, LANES), vt, knm, kpem,
                     vtm.reshape(nh * V_DIM, N_META), tq=tq, hp=4)

        h2 = _mixout(h1, row(norm_mix_pre[i]), row(norm_mix_post[i]),
                     pooled.reshape(bsz * seq, POOL_WIDTH), attn.reshape(bsz * seq, nh * V_DIM),
                     w_gp, w_gm, w_po, w_mo, w_o, tm=tm, tn=tn)

        h = _ffn(h2, row(norm_ffn2_pre[i]), row(norm_ffn2_post[i]), w_gu2, w_dn2, tm=tm, tf=tf)
        if i + 1 < depth:
            raise NotImplementedError("only DEPTH == 1 is supported")

    return h.reshape(bsz, seq, d)
```

```python
import functools
import math

import jax
import jax.numpy as jnp
from jax import lax
from jax.experimental import pallas as pl
from jax.experimental.pallas import tpu as pltpu

F32 = jnp.float32
BF16 = jnp.bfloat16

N_META = 16
POOL_WINDOWS = (2, 4, 8, 16)
POOL_GROUP = 256
POOL_WIDTH = POOL_GROUP * len(POOL_WINDOWS)
MLA_HEADS = 16
Q_LORA = 512
KV_LORA = 512
QK_NOPE = 128
QK_ROPE = 64
V_DIM = 128
QK_DIM = QK_NOPE + QK_ROPE
ROPE_THETA = 10000.0
EPS = 1e-6
LANES = 128
HEAD_PAD = 2 * LANES
Q_SCALE = (QK_DIM ** -0.5) * math.log2(math.e)
MASK_VALUE = -1e30

VMEM_LIMIT = 56 * 1024 * 1024
FFN_VMEM_LIMIT = 62 * 1024 * 1024


def _rms(x):
    return x * lax.rsqrt(jnp.mean(x * x, axis=-1, keepdims=True) + EPS)


def _rope(x, cos, sin):
    rot = pltpu.roll(x, 32, 1) - pltpu.roll(x, 96, 1)
    return x * cos + rot * sin


def _params(sem, vmem_limit=VMEM_LIMIT):
    return pltpu.CompilerParams(dimension_semantics=sem, vmem_limit_bytes=vmem_limit)


def _ffn_kernel(x_ref, gpre_ref, gpost_ref, wg_ref, wu_ref, wd_ref, o_ref, xn_ref):
    f = pl.program_id(1)

    @pl.when(f == 0)
    def _():
        xn_ref[...] = (_rms(x_ref[...]) * gpre_ref[...]).astype(BF16)
        o_ref[...] = jnp.zeros_like(o_ref)

    xn = xn_ref[...]
    g = jnp.dot(xn, wg_ref[...], preferred_element_type=F32)
    u = jnp.dot(xn, wu_ref[...], preferred_element_type=F32)
    a = ((g * jax.nn.sigmoid(g)) * u).astype(BF16)
    o_ref[...] += jnp.dot(a, wd_ref[...], preferred_element_type=F32)

    @pl.when(f == pl.num_programs(1) - 1)
    def _():
        o_ref[...] = x_ref[...] + 0.5 * (_rms(o_ref[...]) * gpost_ref[...])


def _ffn(x, g_pre, g_post, w_gu, w_down, *, tm, tf):
    rows, d = x.shape
    d_ff = w_down.shape[0]
    nf = d_ff // tf
    return pl.pallas_call(
        _ffn_kernel,
        grid=(rows // tm, nf),
        in_specs=[
            pl.BlockSpec((tm, d), lambda i, f: (i, 0)),
            pl.BlockSpec((1, d), lambda i, f: (0, 0)),
            pl.BlockSpec((1, d), lambda i, f: (0, 0)),
            pl.BlockSpec((d, tf), lambda i, f: (0, f)),
            pl.BlockSpec((d, tf), lambda i, f: (0, f + nf)),
            pl.BlockSpec((tf, d), lambda i, f: (f, 0)),
        ],
        out_specs=pl.BlockSpec((tm, d), lambda i, f: (i, 0)),
        out_shape=jax.ShapeDtypeStruct((rows, d), F32),
        scratch_shapes=[pltpu.VMEM((tm, d), BF16)],
        compiler_params=_params(("parallel", "arbitrary"), FFN_VMEM_LIMIT),
        name="ffn",
    )(x, g_pre, g_post, w_gu, w_gu, w_down)


def _inproj_kernel(h_ref, gmix_ref, w_ref, gq_ref, gkv_ref, cos_ref, sin_ref,
                   u_ref, cq_ref, ckv_ref, kpe_ref):
    hn = (_rms(h_ref[...]) * gmix_ref[...]).astype(BF16)
    c0, c1, c2, c3 = POOL_WIDTH, POOL_WIDTH + Q_LORA, POOL_WIDTH + Q_LORA + KV_LORA, w_ref.shape[1]
    u_ref[...] = jnp.dot(hn, w_ref[:, :c0], preferred_element_type=F32)
    cq = jnp.dot(hn, w_ref[:, c0:c1], preferred_element_type=F32)
    cq_ref[...] = (_rms(cq) * gq_ref[...]).astype(BF16)
    ckv = jnp.dot(hn, w_ref[:, c1:c2], preferred_element_type=F32)
    ckv_ref[...] = (_rms(ckv) * gkv_ref[...]).astype(BF16)
    kr = jnp.dot(hn, w_ref[:, c2:c3], preferred_element_type=F32)
    kpe_ref[...] = _rope(kr, cos_ref[...], sin_ref[...]).astype(BF16)


def _inproj(h, g_mix, w_b, g_q, g_kv, cos, sin, *, tm):
    rows, d = h.shape
    n_pos = cos.shape[0] // tm
    wcols = w_b.shape[1]
    row = lambda i: (i, 0)
    const = lambda i: (0, 0)
    pos = lambda i: (i % n_pos, 0)
    return pl.pallas_call(
        _inproj_kernel,
        grid=(rows // tm,),
        in_specs=[
            pl.BlockSpec((tm, d), row),
            pl.BlockSpec((1, d), const),
            pl.BlockSpec((d, wcols), const),
            pl.BlockSpec((1, Q_LORA), const),
            pl.BlockSpec((1, KV_LORA), const),
            pl.BlockSpec((tm, LANES), pos),
            pl.BlockSpec((tm, LANES), pos),
        ],
        out_specs=[
            pl.BlockSpec((tm, POOL_WIDTH), row),
            pl.BlockSpec((tm, Q_LORA), row),
            pl.BlockSpec((tm, KV_LORA), row),
            pl.BlockSpec((tm, LANES), row),
        ],
        out_shape=[
            jax.ShapeDtypeStruct((rows, POOL_WIDTH), F32),
            jax.ShapeDtypeStruct((rows, Q_LORA), BF16),
            jax.ShapeDtypeStruct((rows, KV_LORA), BF16),
            jax.ShapeDtypeStruct((rows, LANES), BF16),
        ],
        compiler_params=_params(("parallel",)),
        name="inproj",
    )(h, g_mix, w_b, g_q, g_kv, cos, sin)


def _qkv_kernel(cq_ref, ckv_ref, wq_ref, wk_ref, wvt_ref, cos_ref, sin_ref,
                q_ref, k_ref, vt_ref):
    cq = cq_ref[...]
    ckv = ckv_ref[...]
    cos = cos_ref[...] * Q_SCALE
    sin = sin_ref[...] * Q_SCALE
    for h in range(MLA_HEADS):
        lo = h * HEAD_PAD
        qh = jnp.dot(cq, wq_ref[:, lo:lo + HEAD_PAD], preferred_element_type=F32)
        q_ref[:, lo:lo + LANES] = (qh[:, :LANES] * Q_SCALE).astype(BF16)
        q_ref[:, lo + LANES:lo + HEAD_PAD] = _rope(qh[:, LANES:], cos, sin).astype(BF16)
    k_ref[...] = jnp.dot(ckv, wk_ref[...], preferred_element_type=F32).astype(BF16)
    vt = lax.dot_general(wvt_ref[...], ckv, (((1,), (1,)), ((), ())),
                         preferred_element_type=F32)
    tk = vt_ref.shape[-1]
    for c in range(vt_ref.shape[1]):
        vt_ref[0, c] = vt[:, c * tk:(c + 1) * tk].astype(BF16)


def _qkv(cq, ckv, w_q, w_k, w_vt, cos, sin, *, tm, tk, seq):
    rows = cq.shape[0]
    n_pos = seq // tm
    nh = MLA_HEADS
    row = lambda i: (i, 0)
    const = lambda i: (0, 0)
    pos = lambda i: (i % n_pos, 0)
    return pl.pallas_call(
        _qkv_kernel,
        grid=(rows // tm,),
        in_specs=[
            pl.BlockSpec((tm, Q_LORA), row),
            pl.BlockSpec((tm, KV_LORA), row),
            pl.BlockSpec((Q_LORA, nh * HEAD_PAD), const),
            pl.BlockSpec((KV_LORA, nh * QK_NOPE), const),
            pl.BlockSpec((nh * V_DIM, KV_LORA), const),
            pl.BlockSpec((tm, LANES), pos),
            pl.BlockSpec((tm, LANES), pos),
        ],
        out_specs=[
            pl.BlockSpec((tm, nh * HEAD_PAD), row),
            pl.BlockSpec((tm, nh * QK_NOPE), row),
            pl.BlockSpec((1, tm // tk, nh * V_DIM, tk), lambda i: (i // n_pos, i % n_pos, 0, 0)),
        ],
        out_shape=[
            jax.ShapeDtypeStruct((rows, nh * HEAD_PAD), BF16),
            jax.ShapeDtypeStruct((rows, nh * QK_NOPE), BF16),
            jax.ShapeDtypeStruct((rows // seq, seq // tk, nh * V_DIM, tk), BF16),
        ],
        compiler_params=_params(("parallel",)),
        name="qkv",
    )(cq, ckv, w_q, w_k, w_vt, cos, sin)


def _pool_kernel(u_ref, prev_ref, meta_ref, w_ref, scale_ref, o_ref, buf_ref):
    tp = u_ref.shape[1]
    first = pl.program_id(1) == 0
    halo = jnp.where(first, meta_ref[...], prev_ref[0])
    for g, win in enumerate(POOL_WINDOWS):
        cols = slice(g * POOL_GROUP, (g + 1) * POOL_GROUP)
        buf_ref[:N_META, :] = halo[:, cols]
        buf_ref[N_META:, :] = u_ref[0, :, cols]
        cur = buf_ref[N_META:, :]
        tot = cur
        for j in range(1, win):
            tot = tot + buf_ref[N_META - j:N_META - j + tp, :]
        dlt = (tot * (1.0 / win) - cur).astype(BF16)
        y = jnp.dot(dlt, w_ref[g], preferred_element_type=F32)
        o_ref[0, :, cols] = (y * scale_ref[:, cols]).astype(BF16)


def _pool(u, u_meta, pool_w, pool_scale, *, tp):
    b, s, c = u.shape
    hb = tp // N_META
    return pl.pallas_call(
        _pool_kernel,
        grid=(b, s // tp),
        in_specs=[
            pl.BlockSpec((1, tp, c), lambda bi, i: (bi, i, 0)),
            pl.BlockSpec((1, N_META, c), lambda bi, i: (bi, jnp.maximum(i * hb - 1, 0), 0)),
            pl.BlockSpec((N_META, c), lambda bi, i: (0, 0)),
            pl.BlockSpec((len(POOL_WINDOWS), POOL_GROUP, POOL_GROUP), lambda bi, i: (0, 0, 0)),
            pl.BlockSpec((1, c), lambda bi, i: (0, 0)),
        ],
        out_specs=pl.BlockSpec((1, tp, c), lambda bi, i: (bi, i, 0)),
        out_shape=jax.ShapeDtypeStruct((b, s, c), BF16),
        scratch_shapes=[pltpu.VMEM((tp + N_META, POOL_GROUP), F32)],
        compiler_params=_params(("parallel", "arbitrary")),
        name="pool",
    )(u, u, u_meta, pool_w, pool_scale)


def _attn_kernel(q_ref, kn_ref, kp_ref, vt_ref, knm_ref, kpm_ref, vtm_ref, o_ref, acc_ref, s_ref):
    tk = vt_ref.shape[-1]
    hp = acc_ref.shape[0]
    i = pl.program_id(2)
    nt = (((1,), (1,)), ((), ()))

    def q_of(h):
        return q_ref[0, :, h * HEAD_PAD:(h + 1) * HEAD_PAD]

    def scores(h, j, masked):
        start = pl.multiple_of(j * tk, tk)
        k = jnp.concatenate([kn_ref[0, pl.ds(start, tk), h * QK_NOPE:(h + 1) * QK_NOPE],
                             kp_ref[0, pl.ds(start, tk), :]], axis=1)
        s = lax.dot_general(k, q_of(h), nt, preferred_element_type=F32)
        if masked:
            kpos = lax.broadcasted_iota(jnp.int32, s.shape, 0)
            qpos = lax.broadcasted_iota(jnp.int32, s.shape, 1)
            s = jnp.where(kpos <= qpos, s, MASK_VALUE)
        s_ref[h] = s
        return jnp.max(s, axis=0, keepdims=True)

    def absorb(h, j, m, l, bmax):
        m_new = jnp.maximum(m, bmax)
        alpha = jnp.exp2(m - m_new)
        p = jnp.exp2(s_ref[h] - m_new)
        pv = jnp.dot(vt_ref[0, j, h * V_DIM:(h + 1) * V_DIM, :], p.astype(BF16),
                     preferred_element_type=F32)
        acc_ref[h] = alpha * acc_ref[h] + pv
        return m_new, alpha * l + jnp.sum(p, axis=0, keepdims=True)

    kpm = kpm_ref[...]
    state = []
    for h in range(hp):
        km = jnp.concatenate([knm_ref[:, h * QK_NOPE:(h + 1) * QK_NOPE], kpm], axis=1)
        s = lax.dot_general(km, q_of(h), nt, preferred_element_type=F32)
        m0 = jnp.max(s, axis=0, keepdims=True)
        p = jnp.exp2(s - m0)
        acc_ref[h] = jnp.dot(vtm_ref[h * V_DIM:(h + 1) * V_DIM, :], p.astype(BF16),
                             preferred_element_type=F32)
        state.extend([m0, jnp.sum(p, axis=0, keepdims=True), scores(h, i, True)])

    def body(j, state):
        prev = jnp.where(j == 0, i, j - 1)
        out = []
        for h in range(hp):
            m, l, bmax = state[3 * h:3 * h + 3]
            m, l = absorb(h, prev, m, l, bmax)
            out.extend([m, l, scores(h, j, False)])
        return tuple(out)

    state = lax.fori_loop(0, i, body, tuple(state))
    last = jnp.where(i == 0, i, i - 1)
    for h in range(hp):
        m, l, bmax = state[3 * h:3 * h + 3]
        m, l = absorb(h, last, m, l, bmax)
        o_ref[0, :, h * V_DIM:(h + 1) * V_DIM] = (acc_ref[h] / l).T.astype(BF16)


def _attn(q, k_nope, k_pe, vt, knm, kpm, vtm, *, tq, hp):
    b, s, _ = q.shape
    tk = vt.shape[-1]
    assert tq == tk
    nh = MLA_HEADS
    return pl.pallas_call(
        _attn_kernel,
        grid=(b, nh // hp, s // tq),
        in_specs=[
            pl.BlockSpec((1, tq, hp * HEAD_PAD), lambda bi, h, i: (bi, i, h)),
            pl.BlockSpec((1, s, hp * QK_NOPE), lambda bi, h, i: (bi, 0, h)),
            pl.BlockSpec((1, s, LANES), lambda bi, h, i: (bi, 0, 0)),
            pl.BlockSpec((1, s // tk, hp * V_DIM, tk), lambda bi, h, i: (bi, 0, h, 0)),
            pl.BlockSpec((N_META, hp * QK_NOPE), lambda bi, h, i: (0, h)),
            pl.BlockSpec((N_META, LANES), lambda bi, h, i: (0, 0)),
            pl.BlockSpec((hp * V_DIM, N_META), lambda bi, h, i: (h, 0)),
        ],
        out_specs=pl.BlockSpec((1, tq, hp * V_DIM), lambda bi, h, i: (bi, i, h)),
        out_shape=jax.ShapeDtypeStruct((b, s, nh * V_DIM), BF16),
        scratch_shapes=[pltpu.VMEM((hp, V_DIM, tq), F32), pltpu.VMEM((hp, tk, tq), F32)],
        compiler_params=_params(("parallel", "parallel", "arbitrary")),
        name="attn",
    )(q, k_nope, k_pe, vt, knm, kpm, vtm)


def _mixout_kernel(h_ref, gmix_ref, gpost_ref, pool_ref, attn_ref,
                   wgp_ref, wgm_ref, wpo_ref, wmo_ref, wout_ref, o_ref, hn_ref, acc_ref):
    c = pl.program_id(1)

    @pl.when(c == 0)
    def _():
        hn_ref[...] = (_rms(h_ref[...]) * gmix_ref[...]).astype(BF16)
        acc_ref[...] = jnp.zeros_like(acc_ref)

    hn = hn_ref[...]
    g_pool = jnp.dot(hn, wgp_ref[...], preferred_element_type=F32)
    g_mla = jnp.dot(hn, wgm_ref[...], preferred_element_type=F32)
    y_pool = jnp.dot(pool_ref[...], wpo_ref[...], preferred_element_type=F32)
    y_mla = jnp.dot(attn_ref[...], wmo_ref[...], preferred_element_type=F32)
    y = (jax.nn.sigmoid(g_pool) * y_pool + jax.nn.sigmoid(g_mla) * y_mla).astype(BF16)
    acc_ref[...] += jnp.dot(y, wout_ref[...], preferred_element_type=F32)

    @pl.when(c == pl.num_programs(1) - 1)
    def _():
        o_ref[...] = h_ref[...] + _rms(acc_ref[...]) * gpost_ref[...]


def _mixout(h, g_mix, g_post, pool, attn, w_gp, w_gm, w_po, w_mo, w_out, *, tm, tn):
    rows, d = h.shape
    row = lambda i, c: (i, 0)
    const = lambda i, c: (0, 0)
    col = lambda i, c: (0, c)
    return pl.pallas_call(
        _mixout_kernel,
        grid=(rows // tm, d // tn),
        in_specs=[
            pl.BlockSpec((tm, d), row),
            pl.BlockSpec((1, d), const),
            pl.BlockSpec((1, d), const),
            pl.BlockSpec((tm, pool.shape[1]), row),
            pl.BlockSpec((tm, attn.shape[1]), row),
            pl.BlockSpec((d, tn), col),
            pl.BlockSpec((d, tn), col),
            pl.BlockSpec((pool.shape[1], tn), col),
            pl.BlockSpec((attn.shape[1], tn), col),
            pl.BlockSpec((tn, d), lambda i, c: (c, 0)),
        ],
        out_specs=pl.BlockSpec((tm, d), row),
        out_shape=jax.ShapeDtypeStruct((rows, d), F32),
        scratch_shapes=[pltpu.VMEM((tm, d), BF16), pltpu.VMEM((tm, d), F32)],
        compiler_params=_params(("parallel", "arbitrary")),
        name="mixout",
    )(h, g_mix, g_post, pool, attn, w_gp, w_gm, w_po, w_mo, w_out)


def _rope_tables(n_pos):
    pos = jnp.arange(n_pos, dtype=F32)
    inv = ROPE_THETA ** (-jnp.arange(0, QK_ROPE, 2, dtype=F32) / QK_ROPE)
    ang = pos[:, None] * inv[None, :]
    ang = jnp.concatenate([ang, ang], axis=-1)
    pad = jnp.zeros((n_pos, LANES - QK_ROPE), F32)
    return (jnp.concatenate([jnp.cos(ang), pad], axis=-1),
            jnp.concatenate([jnp.sin(ang), pad], axis=-1))


def kernel(x, meta_tokens, norm_ffn1_pre, norm_ffn1_post, ffn1_w_gu, ffn1_w_down, norm_mix_pre, norm_mix_post, w_in, pool_w, pool_scale, w_pool_o, q_a_norm, w_q_b, kv_a_norm, w_kv_b, w_mla_o, w_out, norm_ffn2_pre, norm_ffn2_post, ffn2_w_gu, ffn2_w_down):
    bsz, seq, d = x.shape
    depth = w_in.shape[0]
    nh = MLA_HEADS
    tm, tf, tn, tq, tp = 512, 512, 512, 512, 512
    tm_ffn = 1024

    cos, sin = _rope_tables(N_META + seq)
    cos_m, sin_m, cos_r, sin_r = cos[:N_META], sin[:N_META], cos[N_META:], sin[N_META:]

    h = x.reshape(bsz * seq, d)
    hm = meta_tokens.astype(x.dtype)
    row = lambda v: v.reshape(1, -1)

    for i in range(depth):
        w_gu1, w_dn1 = ffn1_w_gu[i].astype(BF16), ffn1_w_down[i].astype(BF16)
        w_gu2, w_dn2 = ffn2_w_gu[i].astype(BF16), ffn2_w_down[i].astype(BF16)
        n_lat = POOL_WIDTH + Q_LORA + KV_LORA + QK_ROPE
        w_b = jnp.concatenate([w_in[i][:, :n_lat], jnp.zeros((d, LANES - QK_ROPE), F32)],
                              axis=1).astype(BF16)
        w_gp = w_in[i][:, n_lat:n_lat + d].astype(BF16)
        w_gm = w_in[i][:, n_lat + d:].astype(BF16)
        w_q = jnp.pad(w_q_b[i].reshape(Q_LORA, nh, QK_DIM),
                      ((0, 0), (0, 0), (0, HEAD_PAD - QK_DIM))).reshape(Q_LORA, nh * HEAD_PAD)
        w_q = w_q.astype(BF16)
        w_kv = w_kv_b[i].reshape(KV_LORA, nh, QK_NOPE + V_DIM)
        w_k = w_kv[:, :, :QK_NOPE].reshape(KV_LORA, nh * QK_NOPE).astype(BF16)
        w_vt = w_kv[:, :, QK_NOPE:].reshape(KV_LORA, nh * V_DIM).T.astype(BF16)
        w_po, w_mo, w_o = w_pool_o[i].astype(BF16), w_mla_o[i].astype(BF16), w_out[i].astype(BF16)
        p_w = pool_w[i].astype(BF16)

        h1 = _ffn(h, row(norm_ffn1_pre[i]), row(norm_ffn1_post[i]), w_gu1, w_dn1,
                  tm=tm_ffn, tf=tf)
        h1m = _ffn(hm, row(norm_ffn1_pre[i]), row(norm_ffn1_post[i]), w_gu1, w_dn1,
                   tm=N_META, tf=tf)

        u, cq, ckv, kpe = _inproj(h1, row(norm_mix_pre[i]), w_b, row(q_a_norm[i]),
                                  row(kv_a_norm[i]), cos_r, sin_r, tm=tm)
        um, cqm, ckvm, kpem = _inproj(h1m, row(norm_mix_pre[i]), w_b, row(q_a_norm[i]),
                                      row(kv_a_norm[i]), cos_m, sin_m, tm=N_META)
        q, kn, vt = _qkv(cq, ckv, w_q, w_k, w_vt, cos_r, sin_r, tm=tm, tk=tq, seq=seq)
        _, knm, vtm = _qkv(cqm, ckvm, w_q, w_k, w_vt, cos_m, sin_m,
                           tm=N_META, tk=N_META, seq=N_META)

        pooled = _pool(u.reshape(bsz, seq, POOL_WIDTH), um, p_w, row(pool_scale[i]), tp=tp)
        attn = _attn(q.reshape(bsz, seq, nh * HEAD_PAD), kn.reshape(bsz, seq, nh * QK_NOPE),
                     kpe.reshape(bsz, seq, LANES), vt, knm, kpem,
                     vtm.reshape(nh * V_DIM, N_META), tq=tq, hp=4)

        h2 = _mixout(h1, row(norm_mix_pre[i]), row(norm_mix_post[i]),
                     pooled.reshape(bsz * seq, POOL_WIDTH), attn.reshape(bsz * seq, nh * V_DIM),
                     w_gp, w_gm, w_po, w_mo, w_o, tm=tm, tn=tn)

        h = _ffn(h2, row(norm_ffn2_pre[i]), row(norm_ffn2_post[i]), w_gu2, w_dn2,
                 tm=tm_ffn, tf=tf)
        if i + 1 < depth:
            raise NotImplementedError("only DEPTH == 1 is supported")

    return h.reshape(bsz, seq, d)
```

```python
import functools
import math

import jax
import jax.numpy as jnp
from jax import lax
from jax.experimental import pallas as pl
from jax.experimental.pallas import tpu as pltpu

F32 = jnp.float32
BF16 = jnp.bfloat16

N_META = 16
POOL_WINDOWS = (2, 4, 8, 16)
POOL_GROUP = 256
POOL_WIDTH = POOL_GROUP * len(POOL_WINDOWS)
MLA_HEADS = 16
Q_LORA = 512
KV_LORA = 512
QK_NOPE = 128
QK_ROPE = 64
V_DIM = 128
QK_DIM = QK_NOPE + QK_ROPE
ROPE_THETA = 10000.0
EPS = 1e-6
LANES = 128
HEAD_PAD = 2 * LANES
Q_SCALE = (QK_DIM ** -0.5) * math.log2(math.e)
MASK_VALUE = -1e30

VMEM_LIMIT = 56 * 1024 * 1024
FFN_VMEM_LIMIT = 62 * 1024 * 1024


def _rms(x):
    return x * lax.rsqrt(jnp.mean(x * x, axis=-1, keepdims=True) + EPS)


def _rope(x, cos, sin):
    rot = pltpu.roll(x, 32, 1) - pltpu.roll(x, 96, 1)
    return x * cos + rot * sin


def _params(sem, vmem_limit=VMEM_LIMIT):
    return pltpu.CompilerParams(dimension_semantics=sem, vmem_limit_bytes=vmem_limit)


def _ffn_kernel(x_ref, gpre_ref, gpost_ref, wg_ref, wu_ref, wd_ref, o_ref, xn_ref):
    f = pl.program_id(1)

    @pl.when(f == 0)
    def _():
        xn_ref[...] = (_rms(x_ref[...]) * gpre_ref[...]).astype(BF16)
        o_ref[...] = jnp.zeros_like(o_ref)

    xn = xn_ref[...]
    g = jnp.dot(xn, wg_ref[...], preferred_element_type=F32)
    u = jnp.dot(xn, wu_ref[...], preferred_element_type=F32)
    a = ((g * jax.nn.sigmoid(g)) * u).astype(BF16)
    o_ref[...] += jnp.dot(a, wd_ref[...], preferred_element_type=F32)

    @pl.when(f == pl.num_programs(1) - 1)
    def _():
        o_ref[...] = x_ref[...] + 0.5 * (_rms(o_ref[...]) * gpost_ref[...])


def _ffn(x, g_pre, g_post, w_gu, w_down, *, tm, tf):
    rows, d = x.shape
    d_ff = w_down.shape[0]
    nf = d_ff // tf
    return pl.pallas_call(
        _ffn_kernel,
        grid=(rows // tm, nf),
        in_specs=[
            pl.BlockSpec((tm, d), lambda i, f: (i, 0)),
            pl.BlockSpec((1, d), lambda i, f: (0, 0)),
            pl.BlockSpec((1, d), lambda i, f: (0, 0)),
            pl.BlockSpec((d, tf), lambda i, f: (0, f)),
            pl.BlockSpec((d, tf), lambda i, f: (0, f + nf)),
            pl.BlockSpec((tf, d), lambda i, f: (f, 0)),
        ],
        out_specs=pl.BlockSpec((tm, d), lambda i, f: (i, 0)),
        out_shape=jax.ShapeDtypeStruct((rows, d), F32),
        scratch_shapes=[pltpu.VMEM((tm, d), BF16)],
        compiler_params=_params(("parallel", "arbitrary"), FFN_VMEM_LIMIT),
        name="ffn",
    )(x, g_pre, g_post, w_gu, w_gu, w_down)


def _inproj_kernel(h_ref, gmix_ref, w_ref, gq_ref, gkv_ref, cos_ref, sin_ref,
                   u_ref, cq_ref, ckv_ref, kpe_ref):
    hn = (_rms(h_ref[...]) * gmix_ref[...]).astype(BF16)
    c0, c1, c2, c3 = POOL_WIDTH, POOL_WIDTH + Q_LORA, POOL_WIDTH + Q_LORA + KV_LORA, w_ref.shape[1]
    u_ref[...] = jnp.dot(hn, w_ref[:, :c0], preferred_element_type=F32)
    cq = jnp.dot(hn, w_ref[:, c0:c1], preferred_element_type=F32)
    cq_ref[...] = (_rms(cq) * gq_ref[...]).astype(BF16)
    ckv = jnp.dot(hn, w_ref[:, c1:c2], preferred_element_type=F32)
    ckv_ref[...] = (_rms(ckv) * gkv_ref[...]).astype(BF16)
    kr = jnp.dot(hn, w_ref[:, c2:c3], preferred_element_type=F32)
    kpe_ref[...] = _rope(kr, cos_ref[...], sin_ref[...]).astype(BF16)


def _inproj(h, g_mix, w_b, g_q, g_kv, cos, sin, *, tm):
    rows, d = h.shape
    n_pos = cos.shape[0] // tm
    wcols = w_b.shape[1]
    row = lambda i: (i, 0)
    const = lambda i: (0, 0)
    pos = lambda i: (i % n_pos, 0)
    return pl.pallas_call(
        _inproj_kernel,
        grid=(rows // tm,),
        in_specs=[
            pl.BlockSpec((tm, d), row),
            pl.BlockSpec((1, d), const),
            pl.BlockSpec((d, wcols), const),
            pl.BlockSpec((1, Q_LORA), const),
            pl.BlockSpec((1, KV_LORA), const),
            pl.BlockSpec((tm, LANES), pos),
            pl.BlockSpec((tm, LANES), pos),
        ],
        out_specs=[
            pl.BlockSpec((tm, POOL_WIDTH), row),
            pl.BlockSpec((tm, Q_LORA), row),
            pl.BlockSpec((tm, KV_LORA), row),
            pl.BlockSpec((tm, LANES), row),
        ],
        out_shape=[
            jax.ShapeDtypeStruct((rows, POOL_WIDTH), F32),
            jax.ShapeDtypeStruct((rows, Q_LORA), BF16),
            jax.ShapeDtypeStruct((rows, KV_LORA), BF16),
            jax.ShapeDtypeStruct((rows, LANES), BF16),
        ],
        compiler_params=_params(("parallel",)),
        name="inproj",
    )(h, g_mix, w_b, g_q, g_kv, cos, sin)


def _qkv_kernel(cq_ref, ckv_ref, wq_ref, wk_ref, wvt_ref, cos_ref, sin_ref,
                q_ref, k_ref, vt_ref):
    cq = cq_ref[...]
    ckv = ckv_ref[...]
    cos = cos_ref[...] * Q_SCALE
    sin = sin_ref[...] * Q_SCALE
    for h in range(MLA_HEADS):
        lo = h * HEAD_PAD
        qh = jnp.dot(cq, wq_ref[:, lo:lo + HEAD_PAD], preferred_element_type=F32)
        q_ref[:, lo:lo + LANES] = (qh[:, :LANES] * Q_SCALE).astype(BF16)
        q_ref[:, lo + LANES:lo + HEAD_PAD] = _rope(qh[:, LANES:], cos, sin).astype(BF16)
    k_ref[...] = jnp.dot(ckv, wk_ref[...], preferred_element_type=F32).astype(BF16)
    vt = lax.dot_general(wvt_ref[...], ckv, (((1,), (1,)), ((), ())),
                         preferred_element_type=F32)
    tk = vt_ref.shape[-1]
    for c in range(vt_ref.shape[1]):
        vt_ref[0, c] = vt[:, c * tk:(c + 1) * tk].astype(BF16)


def _qkv(cq, ckv, w_q, w_k, w_vt, cos, sin, *, tm, tk, seq):
    rows = cq.shape[0]
    n_pos = seq // tm
    nh = MLA_HEADS
    row = lambda i: (i, 0)
    const = lambda i: (0, 0)
    pos = lambda i: (i % n_pos, 0)
    return pl.pallas_call(
        _qkv_kernel,
        grid=(rows // tm,),
        in_specs=[
            pl.BlockSpec((tm, Q_LORA), row),
            pl.BlockSpec((tm, KV_LORA), row),
            pl.BlockSpec((Q_LORA, nh * HEAD_PAD), const),
            pl.BlockSpec((KV_LORA, nh * QK_NOPE), const),
            pl.BlockSpec((nh * V_DIM, KV_LORA), const),
            pl.BlockSpec((tm, LANES), pos),
            pl.BlockSpec((tm, LANES), pos),
        ],
        out_specs=[
            pl.BlockSpec((tm, nh * HEAD_PAD), row),
            pl.BlockSpec((tm, nh * QK_NOPE), row),
            pl.BlockSpec((1, tm // tk, nh * V_DIM, tk), lambda i: (i // n_pos, i % n_pos, 0, 0)),
        ],
        out_shape=[
            jax.ShapeDtypeStruct((rows, nh * HEAD_PAD), BF16),
            jax.ShapeDtypeStruct((rows, nh * QK_NOPE), BF16),
            jax.ShapeDtypeStruct((rows // seq, seq // tk, nh * V_DIM, tk), BF16),
        ],
        compiler_params=_params(("parallel",)),
        name="qkv",
    )(cq, ckv, w_q, w_k, w_vt, cos, sin)


def _pool_kernel(u_ref, prev_ref, meta_ref, w_ref, scale_ref, o_ref, buf_ref):
    tp = u_ref.shape[1]
    first = pl.program_id(1) == 0
    halo = jnp.where(first, meta_ref[...], prev_ref[0])
    for g, win in enumerate(POOL_WINDOWS):
        cols = slice(g * POOL_GROUP, (g + 1) * POOL_GROUP)
        buf_ref[:N_META, :] = halo[:, cols]
        buf_ref[N_META:, :] = u_ref[0, :, cols]
        cur = buf_ref[N_META:, :]
        tot = cur
        for j in range(1, win):
            tot = tot + buf_ref[N_META - j:N_META - j + tp, :]
        dlt = (tot * (1.0 / win) - cur).astype(BF16)
        y = jnp.dot(dlt, w_ref[g], preferred_element_type=F32)
        o_ref[0, :, cols] = (y * scale_ref[:, cols]).astype(BF16)


def _pool(u, u_meta, pool_w, pool_scale, *, tp):
    b, s, c = u.shape
    hb = tp // N_META
    return pl.pallas_call(
        _pool_kernel,
        grid=(b, s // tp),
        in_specs=[
            pl.BlockSpec((1, tp, c), lambda bi, i: (bi, i, 0)),
            pl.BlockSpec((1, N_META, c), lambda bi, i: (bi, jnp.maximum(i * hb - 1, 0), 0)),
            pl.BlockSpec((N_META, c), lambda bi, i: (0, 0)),
            pl.BlockSpec((len(POOL_WINDOWS), POOL_GROUP, POOL_GROUP), lambda bi, i: (0, 0, 0)),
            pl.BlockSpec((1, c), lambda bi, i: (0, 0)),
        ],
        out_specs=pl.BlockSpec((1, tp, c), lambda bi, i: (bi, i, 0)),
        out_shape=jax.ShapeDtypeStruct((b, s, c), BF16),
        scratch_shapes=[pltpu.VMEM((tp + N_META, POOL_GROUP), F32)],
        compiler_params=_params(("parallel", "arbitrary")),
        name="pool",
    )(u, u, u_meta, pool_w, pool_scale)


def _attn_kernel(qi_ref, kj_ref, new_ref, q_ref, kn_ref, kp_ref, vt_ref, knm_ref, kpm_ref,
                 vtm_ref, o_ref, acc_ref, s_ref, st_ref):
    tk = vt_ref.shape[-1]
    tq = tk
    hp = acc_ref.shape[0]
    n_pairs = qi_ref.shape[0]
    nt = (((1,), (1,)), ((), ()))

    def q_of(h, qi):
        return q_ref[0, pl.ds(pl.multiple_of(qi * tq, tq), tq), h * HEAD_PAD:(h + 1) * HEAD_PAD]

    def init_tile(qi):
        kpm = kpm_ref[...]
        for h in range(hp):
            km = jnp.concatenate([knm_ref[:, h * QK_NOPE:(h + 1) * QK_NOPE], kpm], axis=1)
            s = lax.dot_general(km, q_of(h, qi), nt, preferred_element_type=F32)
            m0 = jnp.max(s, axis=0, keepdims=True)
            p = jnp.exp2(s - m0)
            acc_ref[h] = jnp.dot(vtm_ref[h * V_DIM:(h + 1) * V_DIM, :], p.astype(BF16),
                                 preferred_element_type=F32)
            st_ref[h, 0] = m0
            st_ref[h, 1] = jnp.sum(p, axis=0, keepdims=True)

    def scores(t, masked):
        qi = qi_ref[t]
        start = pl.multiple_of(kj_ref[t] * tk, tk)
        kp = kp_ref[0, pl.ds(start, tk), :]
        for h in range(hp):
            k = jnp.concatenate(
                [kn_ref[0, pl.ds(start, tk), h * QK_NOPE:(h + 1) * QK_NOPE], kp], axis=1)
            s = lax.dot_general(k, q_of(h, qi), nt, preferred_element_type=F32)
            if masked:
                kpos = lax.broadcasted_iota(jnp.int32, s.shape, 0)
                qpos = lax.broadcasted_iota(jnp.int32, s.shape, 1)
                s = jnp.where(kpos <= qpos, s, MASK_VALUE)
            s_ref[h] = s
            st_ref[h, 2] = jnp.max(s, axis=0, keepdims=True)

    def absorb(t):
        kj = kj_ref[t]
        for h in range(hp):
            m, l = st_ref[h, 0], st_ref[h, 1]
            m_new = jnp.maximum(m, st_ref[h, 2])
            alpha = jnp.exp2(m - m_new)
            p = jnp.exp2(s_ref[h] - m_new)
            pv = jnp.dot(vt_ref[0, kj, h * V_DIM:(h + 1) * V_DIM, :], p.astype(BF16),
                         preferred_element_type=F32)
            acc_ref[h] = alpha * acc_ref[h] + pv
            st_ref[h, 0] = m_new
            st_ref[h, 1] = alpha * l + jnp.sum(p, axis=0, keepdims=True)

    def finalize(qi):
        rows = pl.ds(pl.multiple_of(qi * tq, tq), tq)
        for h in range(hp):
            o_ref[0, rows, h * V_DIM:(h + 1) * V_DIM] = (
                acc_ref[h] / st_ref[h, 1]).T.astype(BF16)

    init_tile(qi_ref[0])
    scores(0, True)

    def body(t, carry):
        @pl.when(new_ref[t] == 1)
        def _():
            absorb(t - 1)
            scores(t, True)
            finalize(qi_ref[t - 1])
            init_tile(qi_ref[t])

        @pl.when(new_ref[t] == 0)
        def _():
            absorb(t - 1)
            scores(t, False)

        return carry

    lax.fori_loop(1, n_pairs, body, 0)
    absorb(n_pairs - 1)
    finalize(qi_ref[n_pairs - 1])


def _attn(q, k_nope, k_pe, vt, knm, kpm, vtm, *, hp):
    b, s, _ = q.shape
    tk = vt.shape[-1]
    nq = s // tk
    nh = MLA_HEADS
    qi, kj, new = [], [], []
    for i in range(nq):
        for n, j in enumerate([i] + list(range(i))):
            qi.append(i)
            kj.append(j)
            new.append(int(n == 0))
    tables = [jnp.asarray(v, jnp.int32) for v in (qi, kj, new)]
    grid_spec = pltpu.PrefetchScalarGridSpec(
        num_scalar_prefetch=len(tables),
        grid=(b, nh // hp),
        in_specs=[
            pl.BlockSpec((1, s, hp * HEAD_PAD), lambda bi, h, *_: (bi, 0, h)),
            pl.BlockSpec((1, s, hp * QK_NOPE), lambda bi, h, *_: (bi, 0, h)),
            pl.BlockSpec((1, s, LANES), lambda bi, h, *_: (bi, 0, 0)),
            pl.BlockSpec((1, nq, hp * V_DIM, tk), lambda bi, h, *_: (bi, 0, h, 0)),
            pl.BlockSpec((N_META, hp * QK_NOPE), lambda bi, h, *_: (0, h)),
            pl.BlockSpec((N_META, LANES), lambda bi, h, *_: (0, 0)),
            pl.BlockSpec((hp * V_DIM, N_META), lambda bi, h, *_: (h, 0)),
        ],
        out_specs=pl.BlockSpec((1, s, hp * V_DIM), lambda bi, h, *_: (bi, 0, h)),
        scratch_shapes=[pltpu.VMEM((hp, V_DIM, tk), F32), pltpu.VMEM((hp, tk, tk), F32),
                        pltpu.VMEM((hp, 3, 1, tk), F32)],
    )
    return pl.pallas_call(
        _attn_kernel,
        grid_spec=grid_spec,
        out_shape=jax.ShapeDtypeStruct((b, s, nh * V_DIM), BF16),
        compiler_params=_params(("parallel", "parallel")),
        name="attn",
    )(*tables, q, k_nope, k_pe, vt, knm, kpm, vtm)


def _mixout_kernel(h_ref, gmix_ref, gpost_ref, pool_ref, attn_ref,
                   wgp_ref, wgm_ref, wpo_ref, wmo_ref, wout_ref, o_ref, hn_ref, acc_ref):
    c = pl.program_id(1)

    @pl.when(c == 0)
    def _():
        hn_ref[...] = (_rms(h_ref[...]) * gmix_ref[...]).astype(BF16)
        acc_ref[...] = jnp.zeros_like(acc_ref)

    hn = hn_ref[...]
    g_pool = jnp.dot(hn, wgp_ref[...], preferred_element_type=F32)
    g_mla = jnp.dot(hn, wgm_ref[...], preferred_element_type=F32)
    y_pool = jnp.dot(pool_ref[...], wpo_ref[...], preferred_element_type=F32)
    y_mla = jnp.dot(attn_ref[...], wmo_ref[...], preferred_element_type=F32)
    y = (jax.nn.sigmoid(g_pool) * y_pool + jax.nn.sigmoid(g_mla) * y_mla).astype(BF16)
    acc_ref[...] += jnp.dot(y, wout_ref[...], preferred_element_type=F32)

    @pl.when(c == pl.num_programs(1) - 1)
    def _():
        o_ref[...] = h_ref[...] + _rms(acc_ref[...]) * gpost_ref[...]


def _mixout(h, g_mix, g_post, pool, attn, w_gp, w_gm, w_po, w_mo, w_out, *, tm, tn):
    rows, d = h.shape
    row = lambda i, c: (i, 0)
    const = lambda i, c: (0, 0)
    col = lambda i, c: (0, c)
    return pl.pallas_call(
        _mixout_kernel,
        grid=(rows // tm, d // tn),
        in_specs=[
            pl.BlockSpec((tm, d), row),
            pl.BlockSpec((1, d), const),
            pl.BlockSpec((1, d), const),
            pl.BlockSpec((tm, pool.shape[1]), row),
            pl.BlockSpec((tm, attn.shape[1]), row),
            pl.BlockSpec((d, tn), col),
            pl.BlockSpec((d, tn), col),
            pl.BlockSpec((pool.shape[1], tn), col),
            pl.BlockSpec((attn.shape[1], tn), col),
            pl.BlockSpec((tn, d), lambda i, c: (c, 0)),
        ],
        out_specs=pl.BlockSpec((tm, d), row),
        out_shape=jax.ShapeDtypeStruct((rows, d), F32),
        scratch_shapes=[pltpu.VMEM((tm, d), BF16), pltpu.VMEM((tm, d), F32)],
        compiler_params=_params(("parallel", "arbitrary")),
        name="mixout",
    )(h, g_mix, g_post, pool, attn, w_gp, w_gm, w_po, w_mo, w_out)


def _rope_tables(n_pos):
    pos = jnp.arange(n_pos, dtype=F32)
    inv = ROPE_THETA ** (-jnp.arange(0, QK_ROPE, 2, dtype=F32) / QK_ROPE)
    ang = pos[:, None] * inv[None, :]
    ang = jnp.concatenate([ang, ang], axis=-1)
    pad = jnp.zeros((n_pos, LANES - QK_ROPE), F32)
    return (jnp.concatenate([jnp.cos(ang), pad], axis=-1),
            jnp.concatenate([jnp.sin(ang), pad], axis=-1))


def kernel(x, meta_tokens, norm_ffn1_pre, norm_ffn1_post, ffn1_w_gu, ffn1_w_down, norm_mix_pre, norm_mix_post, w_in, pool_w, pool_scale, w_pool_o, q_a_norm, w_q_b, kv_a_norm, w_kv_b, w_mla_o, w_out, norm_ffn2_pre, norm_ffn2_post, ffn2_w_gu, ffn2_w_down):
    bsz, seq, d = x.shape
    depth = w_in.shape[0]
    nh = MLA_HEADS
    tm, tf, tn, tq, tp = 512, 512, 512, 512, 512
    tm_ffn = 1024

    cos, sin = _rope_tables(N_META + seq)
    cos_m, sin_m, cos_r, sin_r = cos[:N_META], sin[:N_META], cos[N_META:], sin[N_META:]

    h = x.reshape(bsz * seq, d)
    hm = meta_tokens.astype(x.dtype)
    row = lambda v: v.reshape(1, -1)

    for i in range(depth):
        w_gu1, w_dn1 = ffn1_w_gu[i].astype(BF16), ffn1_w_down[i].astype(BF16)
        w_gu2, w_dn2 = ffn2_w_gu[i].astype(BF16), ffn2_w_down[i].astype(BF16)
        n_lat = POOL_WIDTH + Q_LORA + KV_LORA + QK_ROPE
        w_b = jnp.concatenate([w_in[i][:, :n_lat], jnp.zeros((d, LANES - QK_ROPE), F32)],
                              axis=1).astype(BF16)
        w_gp = w_in[i][:, n_lat:n_lat + d].astype(BF16)
        w_gm = w_in[i][:, n_lat + d:].astype(BF16)
        w_q = jnp.pad(w_q_b[i].reshape(Q_LORA, nh, QK_DIM),
                      ((0, 0), (0, 0), (0, HEAD_PAD - QK_DIM))).reshape(Q_LORA, nh * HEAD_PAD)
        w_q = w_q.astype(BF16)
        w_kv = w_kv_b[i].reshape(KV_LORA, nh, QK_NOPE + V_DIM)
        w_k = w_kv[:, :, :QK_NOPE].reshape(KV_LORA, nh * QK_NOPE).astype(BF16)
        w_vt = w_kv[:, :, QK_NOPE:].reshape(KV_LORA, nh * V_DIM).T.astype(BF16)
        w_po, w_mo, w_o = w_pool_o[i].astype(BF16), w_mla_o[i].astype(BF16), w_out[i].astype(BF16)
        p_w = pool_w[i].astype(BF16)

        h1 = _ffn(h, row(norm_ffn1_pre[i]), row(norm_ffn1_post[i]), w_gu1, w_dn1,
                  tm=tm_ffn, tf=tf)
        h1m = _ffn(hm, row(norm_ffn1_pre[i]), row(norm_ffn1_post[i]), w_gu1, w_dn1,
                   tm=N_META, tf=tf)

        u, cq, ckv, kpe = _inproj(h1, row(norm_mix_pre[i]), w_b, row(q_a_norm[i]),
                                  row(kv_a_norm[i]), cos_r, sin_r, tm=tm)
        um, cqm, ckvm, kpem = _inproj(h1m, row(norm_mix_pre[i]), w_b, row(q_a_norm[i]),
                                      row(kv_a_norm[i]), cos_m, sin_m, tm=N_META)
        q, kn, vt = _qkv(cq, ckv, w_q, w_k, w_vt, cos_r, sin_r, tm=tm, tk=tq, seq=seq)
        _, knm, vtm = _qkv(cqm, ckvm, w_q, w_k, w_vt, cos_m, sin_m,
                           tm=N_META, tk=N_META, seq=N_META)

        pooled = _pool(u.reshape(bsz, seq, POOL_WIDTH), um, p_w, row(pool_scale[i]), tp=tp)
        attn = _attn(q.reshape(bsz, seq, nh * HEAD_PAD), kn.reshape(bsz, seq, nh * QK_NOPE),
                     kpe.reshape(bsz, seq, LANES), vt, knm, kpem,
                     vtm.reshape(nh * V_DIM, N_META), hp=4)

        h2 = _mixout(h1, row(norm_mix_pre[i]), row(norm_mix_post[i]),
                     pooled.reshape(bsz * seq, POOL_WIDTH), attn.reshape(bsz * seq, nh * V_DIM),
                     w_gp, w_gm, w_po, w_mo, w_o, tm=tm, tn=tn)

        h = _ffn(h2, row(norm_ffn2_pre[i]), row(norm_ffn2_post[i]), w_gu2, w_dn2,
                 tm=tm_ffn, tf=tf)
        if i + 1 < depth:
            raise NotImplementedError("only DEPTH == 1 is supported")

    return h.reshape(bsz, seq, d)
```

```python
import functools
import math

import jax
import jax.numpy as jnp
from jax import lax
from jax.experimental import pallas as pl
from jax.experimental.pallas import tpu as pltpu

F32 = jnp.float32
BF16 = jnp.bfloat16

N_META = 16
POOL_WINDOWS = (2, 4, 8, 16)
POOL_GROUP = 256
POOL_WIDTH = POOL_GROUP * len(POOL_WINDOWS)
MLA_HEADS = 16
Q_LORA = 512
KV_LORA = 512
QK_NOPE = 128
QK_ROPE = 64
V_DIM = 128
QK_DIM = QK_NOPE + QK_ROPE
ROPE_THETA = 10000.0
EPS = 1e-6
LANES = 128
HEAD_PAD = 2 * LANES
Q_SCALE = (QK_DIM ** -0.5) * math.log2(math.e)
MASK_VALUE = -1e30

VMEM_LIMIT = 56 * 1024 * 1024
FFN_VMEM_LIMIT = 62 * 1024 * 1024


def _rms(x):
    return x * lax.rsqrt(jnp.mean(x * x, axis=-1, keepdims=True) + EPS)


def _rope(x, cos, sin):
    rot = pltpu.roll(x, 32, 1) - pltpu.roll(x, 96, 1)
    return x * cos + rot * sin


def _params(sem, vmem_limit=VMEM_LIMIT):
    return pltpu.CompilerParams(dimension_semantics=sem, vmem_limit_bytes=vmem_limit)


def _ffn_kernel(x_ref, gpre_ref, gpost_ref, wg_ref, wu_ref, wd_ref, o_ref, xn_ref):
    f = pl.program_id(1)

    @pl.when(f == 0)
    def _():
        xn_ref[...] = (_rms(x_ref[...]) * gpre_ref[...]).astype(BF16)
        o_ref[...] = jnp.zeros_like(o_ref)

    xn = xn_ref[...]
    g = jnp.dot(xn, wg_ref[...], preferred_element_type=F32)
    u = jnp.dot(xn, wu_ref[...], preferred_element_type=F32)
    a = ((g * jax.nn.sigmoid(g)) * u).astype(BF16)
    o_ref[...] += jnp.dot(a, wd_ref[...], preferred_element_type=F32)

    @pl.when(f == pl.num_programs(1) - 1)
    def _():
        o_ref[...] = x_ref[...] + 0.5 * (_rms(o_ref[...]) * gpost_ref[...])


def _ffn(x, g_pre, g_post, w_gu, w_down, *, tm, tf):
    rows, d = x.shape
    d_ff = w_down.shape[0]
    nf = d_ff // tf
    return pl.pallas_call(
        _ffn_kernel,
        grid=(rows // tm, nf),
        in_specs=[
            pl.BlockSpec((tm, d), lambda i, f: (i, 0)),
            pl.BlockSpec((1, d), lambda i, f: (0, 0)),
            pl.BlockSpec((1, d), lambda i, f: (0, 0)),
            pl.BlockSpec((d, tf), lambda i, f: (0, f)),
            pl.BlockSpec((d, tf), lambda i, f: (0, f + nf)),
            pl.BlockSpec((tf, d), lambda i, f: (f, 0)),
        ],
        out_specs=pl.BlockSpec((tm, d), lambda i, f: (i, 0)),
        out_shape=jax.ShapeDtypeStruct((rows, d), F32),
        scratch_shapes=[pltpu.VMEM((tm, d), BF16)],
        compiler_params=_params(("parallel", "arbitrary"), FFN_VMEM_LIMIT),
        name="ffn",
    )(x, g_pre, g_post, w_gu, w_gu, w_down)


def _inproj_kernel(h_ref, gmix_ref, w_ref, gq_ref, gkv_ref, cos_ref, sin_ref,
                   u_ref, cq_ref, ckv_ref, kpe_ref):
    hn = (_rms(h_ref[...]) * gmix_ref[...]).astype(BF16)
    c0, c1, c2, c3 = POOL_WIDTH, POOL_WIDTH + Q_LORA, POOL_WIDTH + Q_LORA + KV_LORA, w_ref.shape[1]
    u_ref[...] = jnp.dot(hn, w_ref[:, :c0], preferred_element_type=F32)
    cq = jnp.dot(hn, w_ref[:, c0:c1], preferred_element_type=F32)
    cq_ref[...] = (_rms(cq) * gq_ref[...]).astype(BF16)
    ckv = jnp.dot(hn, w_ref[:, c1:c2], preferred_element_type=F32)
    ckv_ref[...] = (_rms(ckv) * gkv_ref[...]).astype(BF16)
    kr = jnp.dot(hn, w_ref[:, c2:c3], preferred_element_type=F32)
    kpe_ref[...] = _rope(kr, cos_ref[...], sin_ref[...]).astype(BF16)


def _inproj(h, g_mix, w_b, g_q, g_kv, cos, sin, *, tm):
    rows, d = h.shape
    n_pos = cos.shape[0] // tm
    wcols = w_b.shape[1]
    row = lambda i: (i, 0)
    const = lambda i: (0, 0)
    pos = lambda i: (i % n_pos, 0)
    return pl.pallas_call(
        _inproj_kernel,
        grid=(rows // tm,),
        in_specs=[
            pl.BlockSpec((tm, d), row),
            pl.BlockSpec((1, d), const),
            pl.BlockSpec((d, wcols), const),
            pl.BlockSpec((1, Q_LORA), const),
            pl.BlockSpec((1, KV_LORA), const),
            pl.BlockSpec((tm, LANES), pos),
            pl.BlockSpec((tm, LANES), pos),
        ],
        out_specs=[
            pl.BlockSpec((tm, POOL_WIDTH), row),
            pl.BlockSpec((tm, Q_LORA), row),
            pl.BlockSpec((tm, KV_LORA), row),
            pl.BlockSpec((tm, LANES), row),
        ],
        out_shape=[
            jax.ShapeDtypeStruct((rows, POOL_WIDTH), F32),
            jax.ShapeDtypeStruct((rows, Q_LORA), BF16),
            jax.ShapeDtypeStruct((rows, KV_LORA), BF16),
            jax.ShapeDtypeStruct((rows, LANES), BF16),
        ],
        compiler_params=_params(("parallel",)),
        name="inproj",
    )(h, g_mix, w_b, g_q, g_kv, cos, sin)


def _qkv_kernel(cq_ref, ckv_ref, wq_ref, wk_ref, wvt_ref, cos_ref, sin_ref,
                q_ref, k_ref, vt_ref):
    cq = cq_ref[...]
    ckv = ckv_ref[...]
    cos = cos_ref[...] * Q_SCALE
    sin = sin_ref[...] * Q_SCALE
    for h in range(MLA_HEADS):
        lo = h * HEAD_PAD
        qh = jnp.dot(cq, wq_ref[:, lo:lo + HEAD_PAD], preferred_element_type=F32)
        q_ref[:, lo:lo + LANES] = (qh[:, :LANES] * Q_SCALE).astype(BF16)
        q_ref[:, lo + LANES:lo + HEAD_PAD] = _rope(qh[:, LANES:], cos, sin).astype(BF16)
    k_ref[...] = jnp.dot(ckv, wk_ref[...], preferred_element_type=F32).astype(BF16)
    vt = lax.dot_general(wvt_ref[...], ckv, (((1,), (1,)), ((), ())),
                         preferred_element_type=F32)
    tk = vt_ref.shape[-1]
    for c in range(vt_ref.shape[1]):
        vt_ref[0, c] = vt[:, c * tk:(c + 1) * tk].astype(BF16)


def _qkv(cq, ckv, w_q, w_k, w_vt, cos, sin, *, tm, tk, seq):
    rows = cq.shape[0]
    n_pos = seq // tm
    nh = MLA_HEADS
    row = lambda i: (i, 0)
    const = lambda i: (0, 0)
    pos = lambda i: (i % n_pos, 0)
    return pl.pallas_call(
        _qkv_kernel,
        grid=(rows // tm,),
        in_specs=[
            pl.BlockSpec((tm, Q_LORA), row),
            pl.BlockSpec((tm, KV_LORA), row),
            pl.BlockSpec((Q_LORA, nh * HEAD_PAD), const),
            pl.BlockSpec((KV_LORA, nh * QK_NOPE), const),
            pl.BlockSpec((nh * V_DIM, KV_LORA), const),
            pl.BlockSpec((tm, LANES), pos),
            pl.BlockSpec((tm, LANES), pos),
        ],
        out_specs=[
            pl.BlockSpec((tm, nh * HEAD_PAD), row),
            pl.BlockSpec((tm, nh * QK_NOPE), row),
            pl.BlockSpec((1, tm // tk, nh * V_DIM, tk), lambda i: (i // n_pos, i % n_pos, 0, 0)),
        ],
        out_shape=[
            jax.ShapeDtypeStruct((rows, nh * HEAD_PAD), BF16),
            jax.ShapeDtypeStruct((rows, nh * QK_NOPE), BF16),
            jax.ShapeDtypeStruct((rows // seq, seq // tk, nh * V_DIM, tk), BF16),
        ],
        compiler_params=_params(("parallel",)),
        name="qkv",
    )(cq, ckv, w_q, w_k, w_vt, cos, sin)


def _pool_kernel(u_ref, prev_ref, meta_ref, w_ref, scale_ref, o_ref, buf_ref):
    tp = u_ref.shape[1]
    first = pl.program_id(1) == 0
    halo = jnp.where(first, meta_ref[...], prev_ref[0])
    for g, win in enumerate(POOL_WINDOWS):
        cols = slice(g * POOL_GROUP, (g + 1) * POOL_GROUP)
        buf_ref[:N_META, :] = halo[:, cols]
        buf_ref[N_META:, :] = u_ref[0, :, cols]
        cur = buf_ref[N_META:, :]
        tot = cur
        for j in range(1, win):
            tot = tot + buf_ref[N_META - j:N_META - j + tp, :]
        dlt = (tot * (1.0 / win) - cur).astype(BF16)
        y = jnp.dot(dlt, w_ref[g], preferred_element_type=F32)
        o_ref[0, :, cols] = (y * scale_ref[:, cols]).astype(BF16)


def _pool(u, u_meta, pool_w, pool_scale, *, tp):
    b, s, c = u.shape
    hb = tp // N_META
    return pl.pallas_call(
        _pool_kernel,
        grid=(b, s // tp),
        in_specs=[
            pl.BlockSpec((1, tp, c), lambda bi, i: (bi, i, 0)),
            pl.BlockSpec((1, N_META, c), lambda bi, i: (bi, jnp.maximum(i * hb - 1, 0), 0)),
            pl.BlockSpec((N_META, c), lambda bi, i: (0, 0)),
            pl.BlockSpec((len(POOL_WINDOWS), POOL_GROUP, POOL_GROUP), lambda bi, i: (0, 0, 0)),
            pl.BlockSpec((1, c), lambda bi, i: (0, 0)),
        ],
        out_specs=pl.BlockSpec((1, tp, c), lambda bi, i: (bi, i, 0)),
        out_shape=jax.ShapeDtypeStruct((b, s, c), BF16),
        scratch_shapes=[pltpu.VMEM((tp + N_META, POOL_GROUP), F32)],
        compiler_params=_params(("parallel", "arbitrary")),
        name="pool",
    )(u, u, u_meta, pool_w, pool_scale)


def _attn_kernel(qi_ref, kj_ref, new_ref, q_ref, kn_ref, kp_ref, vt_ref, knm_ref, kpm_ref,
                 vtm_ref, o_ref, acc_ref, sa_ref, sb_ref, st_ref):
    tk = vt_ref.shape[-1]
    tq = tk
    hp = acc_ref.shape[0]
    n_pairs = qi_ref.shape[0]
    nt = (((1,), (1,)), ((), ()))
    bufs = ((sa_ref, 2), (sb_ref, 3))

    def q_of(h, qi):
        return q_ref[0, pl.ds(pl.multiple_of(qi * tq, tq), tq), h * HEAD_PAD:(h + 1) * HEAD_PAD]

    def init_tile(qi, heads):
        kpm = kpm_ref[...]
        for h in heads:
            km = jnp.concatenate([knm_ref[:, h * QK_NOPE:(h + 1) * QK_NOPE], kpm], axis=1)
            s = lax.dot_general(km, q_of(h, qi), nt, preferred_element_type=F32)
            m0 = jnp.max(s, axis=0, keepdims=True)
            p = jnp.exp2(s - m0)
            acc_ref[h] = jnp.dot(vtm_ref[h * V_DIM:(h + 1) * V_DIM, :], p.astype(BF16),
                                 preferred_element_type=F32)
            st_ref[h, 0] = m0
            st_ref[h, 1] = jnp.sum(p, axis=0, keepdims=True)

    def scores(t, dst, masked, heads):
        s_ref, row = dst
        qi = qi_ref[t]
        start = pl.multiple_of(kj_ref[t] * tk, tk)
        kp = kp_ref[0, pl.ds(start, tk), :]
        for h in heads:
            k = jnp.concatenate(
                [kn_ref[0, pl.ds(start, tk), h * QK_NOPE:(h + 1) * QK_NOPE], kp], axis=1)
            s = lax.dot_general(k, q_of(h, qi), nt, preferred_element_type=F32)
            if masked:
                kpos = lax.broadcasted_iota(jnp.int32, s.shape, 0)
                qpos = lax.broadcasted_iota(jnp.int32, s.shape, 1)
                s = jnp.where(kpos <= qpos, s, MASK_VALUE)
            s_ref[h] = s
            st_ref[h, row] = jnp.max(s, axis=0, keepdims=True)

    def absorb(t, src, heads):
        s_ref, row = src
        kj = kj_ref[t]
        for h in heads:
            m, l = st_ref[h, 0], st_ref[h, 1]
            m_new = jnp.maximum(m, st_ref[h, row])
            alpha = jnp.exp2(m - m_new)
            p = jnp.exp2(s_ref[h] - m_new)
            pv = jnp.dot(vt_ref[0, kj, h * V_DIM:(h + 1) * V_DIM, :], p.astype(BF16),
                         preferred_element_type=F32)
            acc_ref[h] = alpha * acc_ref[h] + pv
            st_ref[h, 0] = m_new
            st_ref[h, 1] = alpha * l + jnp.sum(p, axis=0, keepdims=True)

    def finalize(qi, heads):
        rows = pl.ds(pl.multiple_of(qi * tq, tq), tq)
        for h in heads:
            o_ref[0, rows, h * V_DIM:(h + 1) * V_DIM] = (
                acc_ref[h] / st_ref[h, 1]).T.astype(BF16)

    def stage(t, src, dst):
        @pl.when(new_ref[t] == 1)
        def _():
            scores(t, dst, True, [0])
            for h in range(hp):
                absorb(t - 1, src, [h])
                if h + 1 < hp:
                    scores(t, dst, True, [h + 1])
                finalize(qi_ref[t - 1], [h])
                init_tile(qi_ref[t], [h])

        @pl.when(new_ref[t] == 0)
        def _():
            for h in range(hp):
                scores(t, dst, False, [h])
                absorb(t - 1, src, [h])

    heads = list(range(hp))
    init_tile(qi_ref[0], heads)
    scores(0, bufs[0], True, heads)

    def body(r, carry):
        stage(2 * r + 1, bufs[0], bufs[1])
        stage(2 * r + 2, bufs[1], bufs[0])
        return carry

    lax.fori_loop(0, (n_pairs - 1) // 2, body, 0)
    last = bufs[0]
    if (n_pairs - 1) % 2:
        stage(n_pairs - 1, bufs[0], bufs[1])
        last = bufs[1]
    absorb(n_pairs - 1, last, heads)
    finalize(qi_ref[n_pairs - 1], heads)


def _attn(q, k_nope, k_pe, vt, knm, kpm, vtm, *, hp):
    b, s, _ = q.shape
    tk = vt.shape[-1]
    nq = s // tk
    nh = MLA_HEADS
    qi, kj, new = [], [], []
    for i in range(nq):
        for n, j in enumerate([i] + list(range(i))):
            qi.append(i)
            kj.append(j)
            new.append(int(n == 0))
    tables = [jnp.asarray(v, jnp.int32) for v in (qi, kj, new)]
    grid_spec = pltpu.PrefetchScalarGridSpec(
        num_scalar_prefetch=len(tables),
        grid=(b, nh // hp),
        in_specs=[
            pl.BlockSpec((1, s, hp * HEAD_PAD), lambda bi, h, *_: (bi, 0, h)),
            pl.BlockSpec((1, s, hp * QK_NOPE), lambda bi, h, *_: (bi, 0, h)),
            pl.BlockSpec((1, s, LANES), lambda bi, h, *_: (bi, 0, 0)),
            pl.BlockSpec((1, nq, hp * V_DIM, tk), lambda bi, h, *_: (bi, 0, h, 0)),
            pl.BlockSpec((N_META, hp * QK_NOPE), lambda bi, h, *_: (0, h)),
            pl.BlockSpec((N_META, LANES), lambda bi, h, *_: (0, 0)),
            pl.BlockSpec((hp * V_DIM, N_META), lambda bi, h, *_: (h, 0)),
        ],
        out_specs=pl.BlockSpec((1, s, hp * V_DIM), lambda bi, h, *_: (bi, 0, h)),
        scratch_shapes=[pltpu.VMEM((hp, V_DIM, tk), F32), pltpu.VMEM((hp, tk, tk), F32),
                        pltpu.VMEM((hp, tk, tk), F32), pltpu.VMEM((hp, 4, 1, tk), F32)],
    )
    return pl.pallas_call(
        _attn_kernel,
        grid_spec=grid_spec,
        out_shape=jax.ShapeDtypeStruct((b, s, nh * V_DIM), BF16),
        compiler_params=_params(("parallel", "parallel")),
        name="attn",
    )(*tables, q, k_nope, k_pe, vt, knm, kpm, vtm)


def _mixout_kernel(h_ref, gmix_ref, gpost_ref, pool_ref, attn_ref,
                   wgp_ref, wgm_ref, wpo_ref, wmo_ref, wout_ref, o_ref, hn_ref, acc_ref):
    c = pl.program_id(1)

    @pl.when(c == 0)
    def _():
        hn_ref[...] = (_rms(h_ref[...]) * gmix_ref[...]).astype(BF16)
        acc_ref[...] = jnp.zeros_like(acc_ref)

    hn = hn_ref[...]
    g_pool = jnp.dot(hn, wgp_ref[...], preferred_element_type=F32)
    g_mla = jnp.dot(hn, wgm_ref[...], preferred_element_type=F32)
    y_pool = jnp.dot(pool_ref[...], wpo_ref[...], preferred_element_type=F32)
    y_mla = jnp.dot(attn_ref[...], wmo_ref[...], preferred_element_type=F32)
    y = (jax.nn.sigmoid(g_pool) * y_pool + jax.nn.sigmoid(g_mla) * y_mla).astype(BF16)
    acc_ref[...] += jnp.dot(y, wout_ref[...], preferred_element_type=F32)

    @pl.when(c == pl.num_programs(1) - 1)
    def _():
        o_ref[...] = h_ref[...] + _rms(acc_ref[...]) * gpost_ref[...]


def _mixout(h, g_mix, g_post, pool, attn, w_gp, w_gm, w_po, w_mo, w_out, *, tm, tn):
    rows, d = h.shape
    row = lambda i, c: (i, 0)
    const = lambda i, c: (0, 0)
    col = lambda i, c: (0, c)
    return pl.pallas_call(
        _mixout_kernel,
        grid=(rows // tm, d // tn),
        in_specs=[
            pl.BlockSpec((tm, d), row),
            pl.BlockSpec((1, d), const),
            pl.BlockSpec((1, d), const),
            pl.BlockSpec((tm, pool.shape[1]), row),
            pl.BlockSpec((tm, attn.shape[1]), row),
            pl.BlockSpec((d, tn), col),
            pl.BlockSpec((d, tn), col),
            pl.BlockSpec((pool.shape[1], tn), col),
            pl.BlockSpec((attn.shape[1], tn), col),
            pl.BlockSpec((tn, d), lambda i, c: (c, 0)),
        ],
        out_specs=pl.BlockSpec((tm, d), row),
        out_shape=jax.ShapeDtypeStruct((rows, d), F32),
        scratch_shapes=[pltpu.VMEM((tm, d), BF16), pltpu.VMEM((tm, d), F32)],
        compiler_params=_params(("parallel", "arbitrary")),
        name="mixout",
    )(h, g_mix, g_post, pool, attn, w_gp, w_gm, w_po, w_mo, w_out)


def _rope_tables(n_pos):
    pos = jnp.arange(n_pos, dtype=F32)
    inv = ROPE_THETA ** (-jnp.arange(0, QK_ROPE, 2, dtype=F32) / QK_ROPE)
    ang = pos[:, None] * inv[None, :]
    ang = jnp.concatenate([ang, ang], axis=-1)
    pad = jnp.zeros((n_pos, LANES - QK_ROPE), F32)
    return (jnp.concatenate([jnp.cos(ang), pad], axis=-1),
            jnp.concatenate([jnp.sin(ang), pad], axis=-1))


def kernel(x, meta_tokens, norm_ffn1_pre, norm_ffn1_post, ffn1_w_gu, ffn1_w_down, norm_mix_pre, norm_mix_post, w_in, pool_w, pool_scale, w_pool_o, q_a_norm, w_q_b, kv_a_norm, w_kv_b, w_mla_o, w_out, norm_ffn2_pre, norm_ffn2_post, ffn2_w_gu, ffn2_w_down):
    bsz, seq, d = x.shape
    depth = w_in.shape[0]
    nh = MLA_HEADS
    tm, tf, tn, tq, tp = 512, 512, 512, 512, 512
    tm_ffn = 1024

    cos, sin = _rope_tables(N_META + seq)
    cos_m, sin_m, cos_r, sin_r = cos[:N_META], sin[:N_META], cos[N_META:], sin[N_META:]

    h = x.reshape(bsz * seq, d)
    hm = meta_tokens.astype(x.dtype)
    row = lambda v: v.reshape(1, -1)

    for i in range(depth):
        w_gu1, w_dn1 = ffn1_w_gu[i].astype(BF16), ffn1_w_down[i].astype(BF16)
        w_gu2, w_dn2 = ffn2_w_gu[i].astype(BF16), ffn2_w_down[i].astype(BF16)
        n_lat = POOL_WIDTH + Q_LORA + KV_LORA + QK_ROPE
        w_b = jnp.concatenate([w_in[i][:, :n_lat], jnp.zeros((d, LANES - QK_ROPE), F32)],
                              axis=1).astype(BF16)
        w_gp = w_in[i][:, n_lat:n_lat + d].astype(BF16)
        w_gm = w_in[i][:, n_lat + d:].astype(BF16)
        w_q = jnp.pad(w_q_b[i].reshape(Q_LORA, nh, QK_DIM),
                      ((0, 0), (0, 0), (0, HEAD_PAD - QK_DIM))).reshape(Q_LORA, nh * HEAD_PAD)
        w_q = w_q.astype(BF16)
        w_kv = w_kv_b[i].reshape(KV_LORA, nh, QK_NOPE + V_DIM)
        w_k = w_kv[:, :, :QK_NOPE].reshape(KV_LORA, nh * QK_NOPE).astype(BF16)
        w_vt = w_kv[:, :, QK_NOPE:].reshape(KV_LORA, nh * V_DIM).T.astype(BF16)
        w_po, w_mo, w_o = w_pool_o[i].astype(BF16), w_mla_o[i].astype(BF16), w_out[i].astype(BF16)
        p_w = pool_w[i].astype(BF16)

        h1 = _ffn(h, row(norm_ffn1_pre[i]), row(norm_ffn1_post[i]), w_gu1, w_dn1,
                  tm=tm_ffn, tf=tf)
        h1m = _ffn(hm, row(norm_ffn1_pre[i]), row(norm_ffn1_post[i]), w_gu1, w_dn1,
                   tm=N_META, tf=tf)

        u, cq, ckv, kpe = _inproj(h1, row(norm_mix_pre[i]), w_b, row(q_a_norm[i]),
                                  row(kv_a_norm[i]), cos_r, sin_r, tm=tm)
        um, cqm, ckvm, kpem = _inproj(h1m, row(norm_mix_pre[i]), w_b, row(q_a_norm[i]),
                                      row(kv_a_norm[i]), cos_m, sin_m, tm=N_META)
        q, kn, vt = _qkv(cq, ckv, w_q, w_k, w_vt, cos_r, sin_r, tm=tm, tk=tq, seq=seq)
        _, knm, vtm = _qkv(cqm, ckvm, w_q, w_k, w_vt, cos_m, sin_m,
                           tm=N_META, tk=N_META, seq=N_META)

        pooled = _pool(u.reshape(bsz, seq, POOL_WIDTH), um, p_w, row(pool_scale[i]), tp=tp)
        attn = _attn(q.reshape(bsz, seq, nh * HEAD_PAD), kn.reshape(bsz, seq, nh * QK_NOPE),
                     kpe.reshape(bsz, seq, LANES), vt, knm, kpem,
                     vtm.reshape(nh * V_DIM, N_META), hp=4)

        h2 = _mixout(h1, row(norm_mix_pre[i]), row(norm_mix_post[i]),
                     pooled.reshape(bsz * seq, POOL_WIDTH), attn.reshape(bsz * seq, nh * V_DIM),
                     w_gp, w_gm, w_po, w_mo, w_o, tm=tm, tn=tn)

        h = _ffn(h2, row(norm_ffn2_pre[i]), row(norm_ffn2_post[i]), w_gu2, w_dn2,
                 tm=tm_ffn, tf=tf)
        if i + 1 < depth:
            raise NotImplementedError("only DEPTH == 1 is supported")

    return h.reshape(bsz, seq, d)
```

```python
import functools
import math

import jax
import jax.numpy as jnp
from jax import lax
from jax.experimental import pallas as pl
from jax.experimental.pallas import tpu as pltpu

F32 = jnp.float32
BF16 = jnp.bfloat16

N_META = 16
POOL_WINDOWS = (2, 4, 8, 16)
POOL_GROUP = 256
POOL_WIDTH = POOL_GROUP * len(POOL_WINDOWS)
MLA_HEADS = 16
Q_LORA = 512
KV_LORA = 512
QK_NOPE = 128
QK_ROPE = 64
V_DIM = 128
QK_DIM = QK_NOPE + QK_ROPE
ROPE_THETA = 10000.0
EPS = 1e-6
LANES = 128
HEAD_PAD = 2 * LANES
Q_SCALE = (QK_DIM ** -0.5) * math.log2(math.e)
MASK_VALUE = -1e30

VMEM_LIMIT = 56 * 1024 * 1024
FFN_VMEM_LIMIT = 62 * 1024 * 1024
ROW_CHUNK = 256


def _rms(x):
    return x * lax.rsqrt(jnp.mean(x * x, axis=-1, keepdims=True) + EPS)


def _rope(x, cos, sin):
    rot = pltpu.roll(x, 32, 1) - pltpu.roll(x, 96, 1)
    return x * cos + rot * sin


def _params(sem, vmem_limit=VMEM_LIMIT):
    return pltpu.CompilerParams(dimension_semantics=sem, vmem_limit_bytes=vmem_limit)


def _row_chunks(rows, chunk):
    chunk = min(chunk, rows)
    return [slice(r, r + chunk) for r in range(0, rows, chunk)]


def _ffn_kernel(x_ref, gpre_ref, gpost_ref, wg_ref, wu_ref, wd_ref, o_ref, xn_ref):
    f = pl.program_id(1)
    last = pl.num_programs(1) - 1
    chunks = _row_chunks(x_ref.shape[0], ROW_CHUNK)

    def mlp(xn):
        g = jnp.dot(xn, wg_ref[...], preferred_element_type=F32)
        u = jnp.dot(xn, wu_ref[...], preferred_element_type=F32)
        a = ((g * jax.nn.sigmoid(g)) * u).astype(BF16)
        return jnp.dot(a, wd_ref[...], preferred_element_type=F32)

    @pl.when(f == 0)
    def _():
        for rows in chunks:
            xn = (_rms(x_ref[rows, :]) * gpre_ref[...]).astype(BF16)
            xn_ref[rows, :] = xn
            o_ref[rows, :] = mlp(xn)

    @pl.when(jnp.logical_and(f > 0, f < last))
    def _():
        o_ref[...] += mlp(xn_ref[...])

    @pl.when(f == last)
    def _():
        for rows in chunks:
            y = o_ref[rows, :] + mlp(xn_ref[rows, :])
            o_ref[rows, :] = x_ref[rows, :] + 0.5 * (_rms(y) * gpost_ref[...])


def _ffn(x, g_pre, g_post, w_gu, w_down, *, tm, tf):
    rows, d = x.shape
    d_ff = w_down.shape[0]
    nf = d_ff // tf
    return pl.pallas_call(
        _ffn_kernel,
        grid=(rows // tm, nf),
        in_specs=[
            pl.BlockSpec((tm, d), lambda i, f: (i, 0)),
            pl.BlockSpec((1, d), lambda i, f: (0, 0)),
            pl.BlockSpec((1, d), lambda i, f: (0, 0)),
            pl.BlockSpec((d, tf), lambda i, f: (0, f)),
            pl.BlockSpec((d, tf), lambda i, f: (0, f + nf)),
            pl.BlockSpec((tf, d), lambda i, f: (f, 0)),
        ],
        out_specs=pl.BlockSpec((tm, d), lambda i, f: (i, 0)),
        out_shape=jax.ShapeDtypeStruct((rows, d), F32),
        scratch_shapes=[pltpu.VMEM((tm, d), BF16)],
        compiler_params=_params(("parallel", "arbitrary"), FFN_VMEM_LIMIT),
        name="ffn",
    )(x, g_pre, g_post, w_gu, w_gu, w_down)


def _inproj_kernel(h_ref, gmix_ref, w_ref, gq_ref, gkv_ref, cos_ref, sin_ref,
                   u_ref, cq_ref, ckv_ref, kpe_ref):
    c0, c1, c2, c3 = POOL_WIDTH, POOL_WIDTH + Q_LORA, POOL_WIDTH + Q_LORA + KV_LORA, w_ref.shape[1]
    for rows in _row_chunks(h_ref.shape[0], ROW_CHUNK):
        hn = (_rms(h_ref[rows, :]) * gmix_ref[...]).astype(BF16)
        u_ref[rows, :] = jnp.dot(hn, w_ref[:, :c0], preferred_element_type=F32)
        cq = jnp.dot(hn, w_ref[:, c0:c1], preferred_element_type=F32)
        cq_ref[rows, :] = (_rms(cq) * gq_ref[...]).astype(BF16)
        ckv = jnp.dot(hn, w_ref[:, c1:c2], preferred_element_type=F32)
        ckv_ref[rows, :] = (_rms(ckv) * gkv_ref[...]).astype(BF16)
        kr = jnp.dot(hn, w_ref[:, c2:c3], preferred_element_type=F32)
        kpe_ref[rows, :] = _rope(kr, cos_ref[rows, :], sin_ref[rows, :]).astype(BF16)


def _inproj(h, g_mix, w_b, g_q, g_kv, cos, sin, *, tm):
    rows, d = h.shape
    n_pos = cos.shape[0] // tm
    wcols = w_b.shape[1]
    row = lambda i: (i, 0)
    const = lambda i: (0, 0)
    pos = lambda i: (i % n_pos, 0)
    return pl.pallas_call(
        _inproj_kernel,
        grid=(rows // tm,),
        in_specs=[
            pl.BlockSpec((tm, d), row),
            pl.BlockSpec((1, d), const),
            pl.BlockSpec((d, wcols), const),
            pl.BlockSpec((1, Q_LORA), const),
            pl.BlockSpec((1, KV_LORA), const),
            pl.BlockSpec((tm, LANES), pos),
            pl.BlockSpec((tm, LANES), pos),
        ],
        out_specs=[
            pl.BlockSpec((tm, POOL_WIDTH), row),
            pl.BlockSpec((tm, Q_LORA), row),
            pl.BlockSpec((tm, KV_LORA), row),
            pl.BlockSpec((tm, LANES), row),
        ],
        out_shape=[
            jax.ShapeDtypeStruct((rows, POOL_WIDTH), F32),
            jax.ShapeDtypeStruct((rows, Q_LORA), BF16),
            jax.ShapeDtypeStruct((rows, KV_LORA), BF16),
            jax.ShapeDtypeStruct((rows, LANES), BF16),
        ],
        compiler_params=_params(("parallel",)),
        name="inproj",
    )(h, g_mix, w_b, g_q, g_kv, cos, sin)


def _qkv_kernel(cq_ref, ckv_ref, wq_ref, wk_ref, wvt_ref, cos_ref, sin_ref,
                q_ref, k_ref, vt_ref):
    cq = cq_ref[...]
    ckv = ckv_ref[...]
    cos = cos_ref[...] * Q_SCALE
    sin = sin_ref[...] * Q_SCALE
    for h in range(MLA_HEADS):
        lo = h * HEAD_PAD
        qh = jnp.dot(cq, wq_ref[:, lo:lo + HEAD_PAD], preferred_element_type=F32)
        q_ref[:, lo:lo + LANES] = (qh[:, :LANES] * Q_SCALE).astype(BF16)
        q_ref[:, lo + LANES:lo + HEAD_PAD] = _rope(qh[:, LANES:], cos, sin).astype(BF16)
    k_ref[...] = jnp.dot(ckv, wk_ref[...], preferred_element_type=F32).astype(BF16)
    vt = lax.dot_general(wvt_ref[...], ckv, (((1,), (1,)), ((), ())),
                         preferred_element_type=F32)
    tk = vt_ref.shape[-1]
    for c in range(vt_ref.shape[1]):
        vt_ref[0, c] = vt[:, c * tk:(c + 1) * tk].astype(BF16)


def _qkv(cq, ckv, w_q, w_k, w_vt, cos, sin, *, tm, tk, seq):
    rows = cq.shape[0]
    n_pos = seq // tm
    nh = MLA_HEADS
    row = lambda i: (i, 0)
    const = lambda i: (0, 0)
    pos = lambda i: (i % n_pos, 0)
    return pl.pallas_call(
        _qkv_kernel,
        grid=(rows // tm,),
        in_specs=[
            pl.BlockSpec((tm, Q_LORA), row),
            pl.BlockSpec((tm, KV_LORA), row),
            pl.BlockSpec((Q_LORA, nh * HEAD_PAD), const),
            pl.BlockSpec((KV_LORA, nh * QK_NOPE), const),
            pl.BlockSpec((nh * V_DIM, KV_LORA), const),
            pl.BlockSpec((tm, LANES), pos),
            pl.BlockSpec((tm, LANES), pos),
        ],
        out_specs=[
            pl.BlockSpec((tm, nh * HEAD_PAD), row),
            pl.BlockSpec((tm, nh * QK_NOPE), row),
            pl.BlockSpec((1, tm // tk, nh * V_DIM, tk), lambda i: (i // n_pos, i % n_pos, 0, 0)),
        ],
        out_shape=[
            jax.ShapeDtypeStruct((rows, nh * HEAD_PAD), BF16),
            jax.ShapeDtypeStruct((rows, nh * QK_NOPE), BF16),
            jax.ShapeDtypeStruct((rows // seq, seq // tk, nh * V_DIM, tk), BF16),
        ],
        compiler_params=_params(("parallel",)),
        name="qkv",
    )(cq, ckv, w_q, w_k, w_vt, cos, sin)


def _pool_kernel(u_ref, prev_ref, meta_ref, w_ref, scale_ref, o_ref, buf_ref):
    tp = u_ref.shape[1]
    first = pl.program_id(1) == 0
    halo = jnp.where(first, meta_ref[...], prev_ref[0])
    for g, win in enumerate(POOL_WINDOWS):
        cols = slice(g * POOL_GROUP, (g + 1) * POOL_GROUP)
        buf_ref[:N_META, :] = halo[:, cols]
        buf_ref[N_META:, :] = u_ref[0, :, cols]
        cur = buf_ref[N_META:, :]
        tot = cur
        for j in range(1, win):
            tot = tot + buf_ref[N_META - j:N_META - j + tp, :]
        dlt = (tot * (1.0 / win) - cur).astype(BF16)
        y = jnp.dot(dlt, w_ref[g], preferred_element_type=F32)
        o_ref[0, :, cols] = (y * scale_ref[:, cols]).astype(BF16)


def _pool(u, u_meta, pool_w, pool_scale, *, tp):
    b, s, c = u.shape
    hb = tp // N_META
    return pl.pallas_call(
        _pool_kernel,
        grid=(b, s // tp),
        in_specs=[
            pl.BlockSpec((1, tp, c), lambda bi, i: (bi, i, 0)),
            pl.BlockSpec((1, N_META, c), lambda bi, i: (bi, jnp.maximum(i * hb - 1, 0), 0)),
            pl.BlockSpec((N_META, c), lambda bi, i: (0, 0)),
            pl.BlockSpec((len(POOL_WINDOWS), POOL_GROUP, POOL_GROUP), lambda bi, i: (0, 0, 0)),
            pl.BlockSpec((1, c), lambda bi, i: (0, 0)),
        ],
        out_specs=pl.BlockSpec((1, tp, c), lambda bi, i: (bi, i, 0)),
        out_shape=jax.ShapeDtypeStruct((b, s, c), BF16),
        scratch_shapes=[pltpu.VMEM((tp + N_META, POOL_GROUP), F32)],
        compiler_params=_params(("parallel", "arbitrary")),
        name="pool",
    )(u, u, u_meta, pool_w, pool_scale)


def _attn_kernel(qi_ref, kj_ref, new_ref, q_ref, kn_ref, kp_ref, vt_ref, knm_ref, kpm_ref,
                 vtm_ref, o_ref, acc_ref, sa_ref, sb_ref, st_ref):
    tk = vt_ref.shape[-1]
    tq = tk
    hp = acc_ref.shape[0]
    n_pairs = qi_ref.shape[0]
    nt = (((1,), (1,)), ((), ()))
    bufs = ((sa_ref, 2), (sb_ref, 3))

    def q_of(h, qi):
        return q_ref[0, pl.ds(pl.multiple_of(qi * tq, tq), tq), h * HEAD_PAD:(h + 1) * HEAD_PAD]

    def init_tile(qi, heads):
        kpm = kpm_ref[...]
        for h in heads:
            km = jnp.concatenate([knm_ref[:, h * QK_NOPE:(h + 1) * QK_NOPE], kpm], axis=1)
            s = lax.dot_general(km, q_of(h, qi), nt, preferred_element_type=F32)
            m0 = jnp.max(s, axis=0, keepdims=True)
            p = jnp.exp2(s - m0)
            acc_ref[h] = jnp.dot(vtm_ref[h * V_DIM:(h + 1) * V_DIM, :], p.astype(BF16),
                                 preferred_element_type=F32)
            st_ref[h, 0] = m0
            st_ref[h, 1] = jnp.sum(p, axis=0, keepdims=True)

    def scores(t, dst, masked, heads):
        s_ref, row = dst
        qi = qi_ref[t]
        start = pl.multiple_of(kj_ref[t] * tk, tk)
        kp = kp_ref[0, pl.ds(start, tk), :]
        for h in heads:
            k = jnp.concatenate(
                [kn_ref[0, pl.ds(start, tk), h * QK_NOPE:(h + 1) * QK_NOPE], kp], axis=1)
            s = lax.dot_general(k, q_of(h, qi), nt, preferred_element_type=F32)
            if masked:
                kpos = lax.broadcasted_iota(jnp.int32, s.shape, 0)
                qpos = lax.broadcasted_iota(jnp.int32, s.shape, 1)
                s = jnp.where(kpos <= qpos, s, MASK_VALUE)
            s_ref[h] = s
            st_ref[h, row] = jnp.max(s, axis=0, keepdims=True)

    def absorb(t, src, heads):
        s_ref, row = src
        kj = kj_ref[t]
        for h in heads:
            m, l = st_ref[h, 0], st_ref[h, 1]
            m_new = jnp.maximum(m, st_ref[h, row])
            alpha = jnp.exp2(m - m_new)
            p = jnp.exp2(s_ref[h] - m_new)
            pv = jnp.dot(vt_ref[0, kj, h * V_DIM:(h + 1) * V_DIM, :], p.astype(BF16),
                         preferred_element_type=F32)
            acc_ref[h] = alpha * acc_ref[h] + pv
            st_ref[h, 0] = m_new
            st_ref[h, 1] = alpha * l + jnp.sum(p, axis=0, keepdims=True)

    def finalize(qi, heads):
        rows = pl.ds(pl.multiple_of(qi * tq, tq), tq)
        for h in heads:
            o_ref[0, rows, h * V_DIM:(h + 1) * V_DIM] = (
                acc_ref[h] / st_ref[h, 1]).T.astype(BF16)

    def stage(t, src, dst):
        @pl.when(new_ref[t] == 1)
        def _():
            scores(t, dst, True, [0])
            for h in range(hp):
                absorb(t - 1, src, [h])
                if h + 1 < hp:
                    scores(t, dst, True, [h + 1])
                finalize(qi_ref[t - 1], [h])
                init_tile(qi_ref[t], [h])

        @pl.when(new_ref[t] == 0)
        def _():
            for h in range(hp):
                scores(t, dst, False, [h])
                absorb(t - 1, src, [h])

    heads = list(range(hp))
    init_tile(qi_ref[0], heads)
    scores(0, bufs[0], True, heads)

    def body(r, carry):
        stage(2 * r + 1, bufs[0], bufs[1])
        stage(2 * r + 2, bufs[1], bufs[0])
        return carry

    lax.fori_loop(0, (n_pairs - 1) // 2, body, 0)
    last = bufs[0]
    if (n_pairs - 1) % 2:
        stage(n_pairs - 1, bufs[0], bufs[1])
        last = bufs[1]
    absorb(n_pairs - 1, last, heads)
    finalize(qi_ref[n_pairs - 1], heads)


def _attn(q, k_nope, k_pe, vt, knm, kpm, vtm, *, hp):
    b, s, _ = q.shape
    tk = vt.shape[-1]
    nq = s // tk
    nh = MLA_HEADS
    qi, kj, new = [], [], []
    for i in range(nq):
        for n, j in enumerate([i] + list(range(i))):
            qi.append(i)
            kj.append(j)
            new.append(int(n == 0))
    tables = [jnp.asarray(v, jnp.int32) for v in (qi, kj, new)]
    grid_spec = pltpu.PrefetchScalarGridSpec(
        num_scalar_prefetch=len(tables),
        grid=(b, nh // hp),
        in_specs=[
            pl.BlockSpec((1, s, hp * HEAD_PAD), lambda bi, h, *_: (bi, 0, h)),
            pl.BlockSpec((1, s, hp * QK_NOPE), lambda bi, h, *_: (bi, 0, h)),
            pl.BlockSpec((1, s, LANES), lambda bi, h, *_: (bi, 0, 0)),
            pl.BlockSpec((1, nq, hp * V_DIM, tk), lambda bi, h, *_: (bi, 0, h, 0)),
            pl.BlockSpec((N_META, hp * QK_NOPE), lambda bi, h, *_: (0, h)),
            pl.BlockSpec((N_META, LANES), lambda bi, h, *_: (0, 0)),
            pl.BlockSpec((hp * V_DIM, N_META), lambda bi, h, *_: (h, 0)),
        ],
        out_specs=pl.BlockSpec((1, s, hp * V_DIM), lambda bi, h, *_: (bi, 0, h)),
        scratch_shapes=[pltpu.VMEM((hp, V_DIM, tk), F32), pltpu.VMEM((hp, tk, tk), F32),
                        pltpu.VMEM((hp, tk, tk), F32), pltpu.VMEM((hp, 4, 1, tk), F32)],
    )
    return pl.pallas_call(
        _attn_kernel,
        grid_spec=grid_spec,
        out_shape=jax.ShapeDtypeStruct((b, s, nh * V_DIM), BF16),
        compiler_params=_params(("parallel", "parallel")),
        name="attn",
    )(*tables, q, k_nope, k_pe, vt, knm, kpm, vtm)


def _mixout_kernel(h_ref, gmix_ref, gpost_ref, pool_ref, attn_ref,
                   wgp_ref, wgm_ref, wpo_ref, wmo_ref, wout_ref, o_ref, hn_ref):
    c = pl.program_id(1)
    last = pl.num_programs(1) - 1
    chunks = _row_chunks(h_ref.shape[0], ROW_CHUNK)

    def mix(hn, rows):
        g_pool = jnp.dot(hn, wgp_ref[...], preferred_element_type=F32)
        g_mla = jnp.dot(hn, wgm_ref[...], preferred_element_type=F32)
        y_pool = jnp.dot(pool_ref[rows, :], wpo_ref[...], preferred_element_type=F32)
        y_mla = jnp.dot(attn_ref[rows, :], wmo_ref[...], preferred_element_type=F32)
        y = (jax.nn.sigmoid(g_pool) * y_pool + jax.nn.sigmoid(g_mla) * y_mla).astype(BF16)
        return jnp.dot(y, wout_ref[...], preferred_element_type=F32)

    @pl.when(c == 0)
    def _():
        for rows in chunks:
            hn = (_rms(h_ref[rows, :]) * gmix_ref[...]).astype(BF16)
            hn_ref[rows, :] = hn
            o_ref[rows, :] = mix(hn, rows)

    @pl.when(jnp.logical_and(c > 0, c < last))
    def _():
        o_ref[...] += mix(hn_ref[...], slice(None))

    @pl.when(c == last)
    def _():
        for rows in chunks:
            y = o_ref[rows, :] + mix(hn_ref[rows, :], rows)
            o_ref[rows, :] = h_ref[rows, :] + _rms(y) * gpost_ref[...]


def _mixout(h, g_mix, g_post, pool, attn, w_gp, w_gm, w_po, w_mo, w_out, *, tm, tn):
    rows, d = h.shape
    row = lambda i, c: (i, 0)
    const = lambda i, c: (0, 0)
    col = lambda i, c: (0, c)
    return pl.pallas_call(
        _mixout_kernel,
        grid=(rows // tm, d // tn),
        in_specs=[
            pl.BlockSpec((tm, d), row),
            pl.BlockSpec((1, d), const),
            pl.BlockSpec((1, d), const),
            pl.BlockSpec((tm, pool.shape[1]), row),
            pl.BlockSpec((tm, attn.shape[1]), row),
            pl.BlockSpec((d, tn), col),
            pl.BlockSpec((d, tn), col),
            pl.BlockSpec((pool.shape[1], tn), col),
            pl.BlockSpec((attn.shape[1], tn), col),
            pl.BlockSpec((tn, d), lambda i, c: (c, 0)),
        ],
        out_specs=pl.BlockSpec((tm, d), row),
        out_shape=jax.ShapeDtypeStruct((rows, d), F32),
        scratch_shapes=[pltpu.VMEM((tm, d), BF16)],
        compiler_params=_params(("parallel", "arbitrary")),
        name="mixout",
    )(h, g_mix, g_post, pool, attn, w_gp, w_gm, w_po, w_mo, w_out)


def _rope_tables(n_pos):
    pos = jnp.arange(n_pos, dtype=F32)
    inv = ROPE_THETA ** (-jnp.arange(0, QK_ROPE, 2, dtype=F32) / QK_ROPE)
    ang = pos[:, None] * inv[None, :]
    ang = jnp.concatenate([ang, ang], axis=-1)
    pad = jnp.zeros((n_pos, LANES - QK_ROPE), F32)
    return (jnp.concatenate([jnp.cos(ang), pad], axis=-1),
            jnp.concatenate([jnp.sin(ang), pad], axis=-1))


def kernel(x, meta_tokens, norm_ffn1_pre, norm_ffn1_post, ffn1_w_gu, ffn1_w_down, norm_mix_pre, norm_mix_post, w_in, pool_w, pool_scale, w_pool_o, q_a_norm, w_q_b, kv_a_norm, w_kv_b, w_mla_o, w_out, norm_ffn2_pre, norm_ffn2_post, ffn2_w_gu, ffn2_w_down):
    bsz, seq, d = x.shape
    depth = w_in.shape[0]
    nh = MLA_HEADS
    tm, tf, tn, tq, tp = 512, 512, 512, 512, 512
    tm_ffn = 1024

    cos, sin = _rope_tables(N_META + seq)
    cos_m, sin_m, cos_r, sin_r = cos[:N_META], sin[:N_META], cos[N_META:], sin[N_META:]

    h = x.reshape(bsz * seq, d)
    hm = meta_tokens.astype(x.dtype)
    row = lambda v: v.reshape(1, -1)

    for i in range(depth):
        w_gu1, w_dn1 = ffn1_w_gu[i].astype(BF16), ffn1_w_down[i].astype(BF16)
        w_gu2, w_dn2 = ffn2_w_gu[i].astype(BF16), ffn2_w_down[i].astype(BF16)
        n_lat = POOL_WIDTH + Q_LORA + KV_LORA + QK_ROPE
        w_b = jnp.concatenate([w_in[i][:, :n_lat], jnp.zeros((d, LANES - QK_ROPE), F32)],
                              axis=1).astype(BF16)
        w_gp = w_in[i][:, n_lat:n_lat + d].astype(BF16)
        w_gm = w_in[i][:, n_lat + d:].astype(BF16)
        w_q = jnp.pad(w_q_b[i].reshape(Q_LORA, nh, QK_DIM),
                      ((0, 0), (0, 0), (0, HEAD_PAD - QK_DIM))).reshape(Q_LORA, nh * HEAD_PAD)
        w_q = w_q.astype(BF16)
        w_kv = w_kv_b[i].reshape(KV_LORA, nh, QK_NOPE + V_DIM)
        w_k = w_kv[:, :, :QK_NOPE].reshape(KV_LORA, nh * QK_NOPE).astype(BF16)
        w_vt = w_kv[:, :, QK_NOPE:].reshape(KV_LORA, nh * V_DIM).T.astype(BF16)
        w_po, w_mo, w_o = w_pool_o[i].astype(BF16), w_mla_o[i].astype(BF16), w_out[i].astype(BF16)
        p_w = pool_w[i].astype(BF16)

        h1 = _ffn(h, row(norm_ffn1_pre[i]), row(norm_ffn1_post[i]), w_gu1, w_dn1,
                  tm=tm_ffn, tf=tf)
        h1m = _ffn(hm, row(norm_ffn1_pre[i]), row(norm_ffn1_post[i]), w_gu1, w_dn1,
                   tm=N_META, tf=tf)

        u, cq, ckv, kpe = _inproj(h1, row(norm_mix_pre[i]), w_b, row(q_a_norm[i]),
                                  row(kv_a_norm[i]), cos_r, sin_r, tm=tm)
        um, cqm, ckvm, kpem = _inproj(h1m, row(norm_mix_pre[i]), w_b, row(q_a_norm[i]),
                                      row(kv_a_norm[i]), cos_m, sin_m, tm=N_META)
        q, kn, vt = _qkv(cq, ckv, w_q, w_k, w_vt, cos_r, sin_r, tm=tm, tk=tq, seq=seq)
        _, knm, vtm = _qkv(cqm, ckvm, w_q, w_k, w_vt, cos_m, sin_m,
                           tm=N_META, tk=N_META, seq=N_META)

        pooled = _pool(u.reshape(bsz, seq, POOL_WIDTH), um, p_w, row(pool_scale[i]), tp=tp)
        attn = _attn(q.reshape(bsz, seq, nh * HEAD_PAD), kn.reshape(bsz, seq, nh * QK_NOPE),
                     kpe.reshape(bsz, seq, LANES), vt, knm, kpem,
                     vtm.reshape(nh * V_DIM, N_META), hp=4)

        h2 = _mixout(h1, row(norm_mix_pre[i]), row(norm_mix_post[i]),
                     pooled.reshape(bsz * seq, POOL_WIDTH), attn.reshape(bsz * seq, nh * V_DIM),
                     w_gp, w_gm, w_po, w_mo, w_o, tm=tm, tn=tn)

        h = _ffn(h2, row(norm_ffn2_pre[i]), row(norm_ffn2_post[i]), w_gu2, w_dn2,
                 tm=tm_ffn, tf=tf)
        if i + 1 < depth:
            raise NotImplementedError("only DEPTH == 1 is supported")

    return h.reshape(bsz, seq, d)
```

```python
import functools
import math

import jax
import jax.numpy as jnp
from jax import lax
from jax.experimental import pallas as pl
from jax.experimental.pallas import tpu as pltpu

F32 = jnp.float32
BF16 = jnp.bfloat16

N_META = 16
POOL_WINDOWS = (2, 4, 8, 16)
POOL_GROUP = 256
POOL_WIDTH = POOL_GROUP * len(POOL_WINDOWS)
MLA_HEADS = 16
Q_LORA = 512
KV_LORA = 512
QK_NOPE = 128
QK_ROPE = 64
V_DIM = 128
QK_DIM = QK_NOPE + QK_ROPE
ROPE_THETA = 10000.0
EPS = 1e-6
LANES = 128
HEAD_PAD = 2 * LANES
Q_SCALE = (QK_DIM ** -0.5) * math.log2(math.e)
MASK_VALUE = -1e30

VMEM_LIMIT = 56 * 1024 * 1024
FFN_VMEM_LIMIT = 62 * 1024 * 1024
ROW_CHUNK = 256


def _rms(x):
    return x * lax.rsqrt(jnp.mean(x * x, axis=-1, keepdims=True) + EPS)


def _rope(x, cos, sin):
    rot = pltpu.roll(x, 32, 1) - pltpu.roll(x, 96, 1)
    return x * cos + rot * sin


def _params(sem, vmem_limit=VMEM_LIMIT):
    return pltpu.CompilerParams(dimension_semantics=sem, vmem_limit_bytes=vmem_limit)


def _row_chunks(rows, chunk):
    chunk = min(chunk, rows)
    return [slice(r, r + chunk) for r in range(0, rows, chunk)]


def _ffn_kernel(x_ref, gpre_ref, gpost_ref, wg_ref, wu_ref, wd_ref, o_ref, xn_ref):
    f = pl.program_id(1)
    last = pl.num_programs(1) - 1
    chunks = _row_chunks(x_ref.shape[0], ROW_CHUNK)

    def mlp(xn):
        g = jnp.dot(xn, wg_ref[...], preferred_element_type=F32)
        u = jnp.dot(xn, wu_ref[...], preferred_element_type=F32)
        a = ((g * jax.nn.sigmoid(g)) * u).astype(BF16)
        return jnp.dot(a, wd_ref[...], preferred_element_type=F32)

    @pl.when(f == 0)
    def _():
        for rows in chunks:
            xn = (_rms(x_ref[rows, :]) * gpre_ref[...]).astype(BF16)
            xn_ref[rows, :] = xn
            o_ref[rows, :] = mlp(xn)

    @pl.when(jnp.logical_and(f > 0, f < last))
    def _():
        o_ref[...] += mlp(xn_ref[...])

    @pl.when(f == last)
    def _():
        for rows in chunks:
            y = o_ref[rows, :] + mlp(xn_ref[rows, :])
            o_ref[rows, :] = x_ref[rows, :] + 0.5 * (_rms(y) * gpost_ref[...])


def _ffn(x, g_pre, g_post, w_gu, w_down, *, tm, tf):
    rows, d = x.shape
    d_ff = w_down.shape[0]
    nf = d_ff // tf
    return pl.pallas_call(
        _ffn_kernel,
        grid=(rows // tm, nf),
        in_specs=[
            pl.BlockSpec((tm, d), lambda i, f: (i, 0)),
            pl.BlockSpec((1, d), lambda i, f: (0, 0)),
            pl.BlockSpec((1, d), lambda i, f: (0, 0)),
            pl.BlockSpec((d, tf), lambda i, f: (0, f)),
            pl.BlockSpec((d, tf), lambda i, f: (0, f + nf)),
            pl.BlockSpec((tf, d), lambda i, f: (f, 0)),
        ],
        out_specs=pl.BlockSpec((tm, d), lambda i, f: (i, 0)),
        out_shape=jax.ShapeDtypeStruct((rows, d), F32),
        scratch_shapes=[pltpu.VMEM((tm, d), BF16)],
        compiler_params=_params(("parallel", "arbitrary"), FFN_VMEM_LIMIT),
        name="ffn",
    )(x, g_pre, g_post, w_gu, w_gu, w_down)


def _inproj_kernel(h_ref, gmix_ref, w_ref, gq_ref, gkv_ref, cos_ref, sin_ref,
                   u_ref, cq_ref, ckv_ref, kpe_ref):
    c0, c1, c2, c3 = POOL_WIDTH, POOL_WIDTH + Q_LORA, POOL_WIDTH + Q_LORA + KV_LORA, w_ref.shape[1]
    for rows in _row_chunks(h_ref.shape[0], ROW_CHUNK):
        hn = (_rms(h_ref[rows, :]) * gmix_ref[...]).astype(BF16)
        u_ref[rows, :] = jnp.dot(hn, w_ref[:, :c0], preferred_element_type=F32)
        cq = jnp.dot(hn, w_ref[:, c0:c1], preferred_element_type=F32)
        cq_ref[rows, :] = (_rms(cq) * gq_ref[...]).astype(BF16)
        ckv = jnp.dot(hn, w_ref[:, c1:c2], preferred_element_type=F32)
        ckv_ref[rows, :] = (_rms(ckv) * gkv_ref[...]).astype(BF16)
        kr = jnp.dot(hn, w_ref[:, c2:c3], preferred_element_type=F32)
        kpe_ref[rows, :] = _rope(kr, cos_ref[rows, :], sin_ref[rows, :]).astype(BF16)


def _inproj(h, g_mix, w_b, g_q, g_kv, cos, sin, *, tm):
    rows, d = h.shape
    n_pos = cos.shape[0] // tm
    wcols = w_b.shape[1]
    row = lambda i: (i, 0)
    const = lambda i: (0, 0)
    pos = lambda i: (i % n_pos, 0)
    return pl.pallas_call(
        _inproj_kernel,
        grid=(rows // tm,),
        in_specs=[
            pl.BlockSpec((tm, d), row),
            pl.BlockSpec((1, d), const),
            pl.BlockSpec((d, wcols), const),
            pl.BlockSpec((1, Q_LORA), const),
            pl.BlockSpec((1, KV_LORA), const),
            pl.BlockSpec((tm, LANES), pos),
            pl.BlockSpec((tm, LANES), pos),
        ],
        out_specs=[
            pl.BlockSpec((tm, POOL_WIDTH), row),
            pl.BlockSpec((tm, Q_LORA), row),
            pl.BlockSpec((tm, KV_LORA), row),
            pl.BlockSpec((tm, LANES), row),
        ],
        out_shape=[
            jax.ShapeDtypeStruct((rows, POOL_WIDTH), F32),
            jax.ShapeDtypeStruct((rows, Q_LORA), BF16),
            jax.ShapeDtypeStruct((rows, KV_LORA), BF16),
            jax.ShapeDtypeStruct((rows, LANES), BF16),
        ],
        compiler_params=_params(("parallel",)),
        name="inproj",
    )(h, g_mix, w_b, g_q, g_kv, cos, sin)


def _qkv_kernel(cq_ref, ckv_ref, wq_ref, wk_ref, wvt_ref, cos_ref, sin_ref,
                q_ref, k_ref, vt_ref):
    cq = cq_ref[...]
    ckv = ckv_ref[...]
    cos = cos_ref[...] * Q_SCALE
    sin = sin_ref[...] * Q_SCALE
    for h in range(MLA_HEADS):
        lo = h * HEAD_PAD
        qh = jnp.dot(cq, wq_ref[:, lo:lo + HEAD_PAD], preferred_element_type=F32)
        q_ref[:, lo:lo + LANES] = (qh[:, :LANES] * Q_SCALE).astype(BF16)
        q_ref[:, lo + LANES:lo + HEAD_PAD] = _rope(qh[:, LANES:], cos, sin).astype(BF16)
    k_ref[...] = jnp.dot(ckv, wk_ref[...], preferred_element_type=F32).astype(BF16)
    vt = lax.dot_general(wvt_ref[...], ckv, (((1,), (1,)), ((), ())),
                         preferred_element_type=F32)
    tk = vt_ref.shape[-1]
    for c in range(vt_ref.shape[1]):
        vt_ref[0, c] = vt[:, c * tk:(c + 1) * tk].astype(BF16)


def _qkv(cq, ckv, w_q, w_k, w_vt, cos, sin, *, tm, tk, seq):
    rows = cq.shape[0]
    n_pos = seq // tm
    nh = MLA_HEADS
    row = lambda i: (i, 0)
    const = lambda i: (0, 0)
    pos = lambda i: (i % n_pos, 0)
    return pl.pallas_call(
        _qkv_kernel,
        grid=(rows // tm,),
        in_specs=[
            pl.BlockSpec((tm, Q_LORA), row),
            pl.BlockSpec((tm, KV_LORA), row),
            pl.BlockSpec((Q_LORA, nh * HEAD_PAD), const),
            pl.BlockSpec((KV_LORA, nh * QK_NOPE), const),
            pl.BlockSpec((nh * V_DIM, KV_LORA), const),
            pl.BlockSpec((tm, LANES), pos),
            pl.BlockSpec((tm, LANES), pos),
        ],
        out_specs=[
            pl.BlockSpec((tm, nh * HEAD_PAD), row),
            pl.BlockSpec((tm, nh * QK_NOPE), row),
            pl.BlockSpec((1, tm // tk, nh * V_DIM, tk), lambda i: (i // n_pos, i % n_pos, 0, 0)),
        ],
        out_shape=[
            jax.ShapeDtypeStruct((rows, nh * HEAD_PAD), BF16),
            jax.ShapeDtypeStruct((rows, nh * QK_NOPE), BF16),
            jax.ShapeDtypeStruct((rows // seq, seq // tk, nh * V_DIM, tk), BF16),
        ],
        compiler_params=_params(("parallel",)),
        name="qkv",
    )(cq, ckv, w_q, w_k, w_vt, cos, sin)


def _pool_kernel(u_ref, prev_ref, meta_ref, w_ref, scale_ref, o_ref, buf_ref):
    first = pl.program_id(1) == 0
    halo = jnp.where(first, meta_ref[...], prev_ref[0])
    for g, win in enumerate(POOL_WINDOWS):
        cols = slice(g * POOL_GROUP, (g + 1) * POOL_GROUP)
        buf_ref[:N_META, :] = halo[:, cols]
        buf_ref[N_META:, :] = u_ref[0, :, cols]
        tot = buf_ref[...]
        shift = 1
        while shift < win:
            tot = tot + pltpu.roll(tot, shift, 0)
            shift *= 2
        cur = buf_ref[N_META:, :]
        dlt = (tot[N_META:, :] * (1.0 / win) - cur).astype(BF16)
        y = jnp.dot(dlt, w_ref[g], preferred_element_type=F32)
        o_ref[0, :, cols] = (y * scale_ref[:, cols]).astype(BF16)


def _pool(u, u_meta, pool_w, pool_scale, *, tp):
    b, s, c = u.shape
    hb = tp // N_META
    return pl.pallas_call(
        _pool_kernel,
        grid=(b, s // tp),
        in_specs=[
            pl.BlockSpec((1, tp, c), lambda bi, i: (bi, i, 0)),
            pl.BlockSpec((1, N_META, c), lambda bi, i: (bi, jnp.maximum(i * hb - 1, 0), 0)),
            pl.BlockSpec((N_META, c), lambda bi, i: (0, 0)),
            pl.BlockSpec((len(POOL_WINDOWS), POOL_GROUP, POOL_GROUP), lambda bi, i: (0, 0, 0)),
            pl.BlockSpec((1, c), lambda bi, i: (0, 0)),
        ],
        out_specs=pl.BlockSpec((1, tp, c), lambda bi, i: (bi, i, 0)),
        out_shape=jax.ShapeDtypeStruct((b, s, c), BF16),
        scratch_shapes=[pltpu.VMEM((tp + N_META, POOL_GROUP), F32)],
        compiler_params=_params(("parallel", "arbitrary")),
        name="pool",
    )(u, u, u_meta, pool_w, pool_scale)


def _attn_kernel(qi_ref, kj_ref, new_ref, q_ref, kn_ref, kp_ref, vt_ref, knm_ref, kpm_ref,
                 vtm_ref, o_ref, acc_ref, sa_ref, sb_ref, st_ref):
    tk = vt_ref.shape[-1]
    tq = tk
    hp = acc_ref.shape[0]
    n_pairs = qi_ref.shape[0]
    nt = (((1,), (1,)), ((), ()))
    bufs = ((sa_ref, 2), (sb_ref, 3))

    def q_of(h, qi):
        return q_ref[0, pl.ds(pl.multiple_of(qi * tq, tq), tq), h * HEAD_PAD:(h + 1) * HEAD_PAD]

    def init_tile(qi, heads):
        kpm = kpm_ref[...]
        for h in heads:
            km = jnp.concatenate([knm_ref[:, h * QK_NOPE:(h + 1) * QK_NOPE], kpm], axis=1)
            s = lax.dot_general(km, q_of(h, qi), nt, preferred_element_type=F32)
            m0 = jnp.max(s, axis=0, keepdims=True)
            p = jnp.exp2(s - m0)
            acc_ref[h] = jnp.dot(vtm_ref[h * V_DIM:(h + 1) * V_DIM, :], p.astype(BF16),
                                 preferred_element_type=F32)
            st_ref[h, 0] = m0
            st_ref[h, 1] = jnp.sum(p, axis=0, keepdims=True)

    def scores(t, dst, masked, heads):
        s_ref, row = dst
        qi = qi_ref[t]
        start = pl.multiple_of(kj_ref[t] * tk, tk)
        kp = kp_ref[0, pl.ds(start, tk), :]
        for h in heads:
            k = jnp.concatenate(
                [kn_ref[0, pl.ds(start, tk), h * QK_NOPE:(h + 1) * QK_NOPE], kp], axis=1)
            s = lax.dot_general(k, q_of(h, qi), nt, preferred_element_type=F32)
            if masked:
                kpos = lax.broadcasted_iota(jnp.int32, s.shape, 0)
                qpos = lax.broadcasted_iota(jnp.int32, s.shape, 1)
                s = jnp.where(kpos <= qpos, s, MASK_VALUE)
            s_ref[h] = s
            st_ref[h, row] = jnp.max(s, axis=0, keepdims=True)

    def absorb(t, src, heads):
        s_ref, row = src
        kj = kj_ref[t]
        for h in heads:
            m, l = st_ref[h, 0], st_ref[h, 1]
            m_new = jnp.maximum(m, st_ref[h, row])
            alpha = jnp.exp2(m - m_new)
            p = jnp.exp2(s_ref[h] - m_new)
            pv = jnp.dot(vt_ref[0, kj, h * V_DIM:(h + 1) * V_DIM, :], p.astype(BF16),
                         preferred_element_type=F32)
            acc_ref[h] = alpha * acc_ref[h] + pv
            st_ref[h, 0] = m_new
            st_ref[h, 1] = alpha * l + jnp.sum(p, axis=0, keepdims=True)

    def finalize(qi, heads):
        rows = pl.ds(pl.multiple_of(qi * tq, tq), tq)
        for h in heads:
            o_ref[0, rows, h * V_DIM:(h + 1) * V_DIM] = (
                acc_ref[h] / st_ref[h, 1]).T.astype(BF16)

    def plain(t, src, dst):
        for h in range(hp):
            scores(t, dst, False, [h])
            absorb(t - 1, src, [h])

    def fresh(t, src, dst):
        scores(t, dst, True, [0])
        for h in range(hp):
            absorb(t - 1, src, [h])
            if h + 1 < hp:
                scores(t, dst, True, [h + 1])
            finalize(qi_ref[t - 1], [h])
            init_tile(qi_ref[t], [h])

    def stage(t, src, dst):
        pl.when(new_ref[t] == 1)(lambda: fresh(t, src, dst))
        pl.when(new_ref[t] == 0)(lambda: plain(t, src, dst))

    heads = list(range(hp))
    init_tile(qi_ref[0], heads)
    scores(0, bufs[0], True, heads)

    def body(r, carry):
        t = 2 * r + 1
        a, b = bufs

        @pl.when(new_ref[t] == 1)
        def _():
            fresh(t, a, b)
            plain(t + 1, b, a)

        @pl.when(new_ref[t + 1] == 1)
        def _():
            plain(t, a, b)
            fresh(t + 1, b, a)

        @pl.when(new_ref[t] + new_ref[t + 1] == 0)
        def _():
            plain(t, a, b)
            plain(t + 1, b, a)

        return carry

    lax.fori_loop(0, (n_pairs - 1) // 2, body, 0)
    last = bufs[0]
    if (n_pairs - 1) % 2:
        stage(n_pairs - 1, bufs[0], bufs[1])
        last = bufs[1]
    absorb(n_pairs - 1, last, heads)
    finalize(qi_ref[n_pairs - 1], heads)


def _attn(q, k_nope, k_pe, vt, knm, kpm, vtm, *, hp):
    b, s, _ = q.shape
    tk = vt.shape[-1]
    nq = s // tk
    nh = MLA_HEADS
    qi, kj, new = [], [], []
    for i in range(nq):
        for n, j in enumerate([i] + list(range(i))):
            qi.append(i)
            kj.append(j)
            new.append(int(n == 0))
    assert not any(new[t] and new[t + 1] for t in range(1, len(new) - 1, 2))
    tables = [jnp.asarray(v, jnp.int32) for v in (qi, kj, new)]
    grid_spec = pltpu.PrefetchScalarGridSpec(
        num_scalar_prefetch=len(tables),
        grid=(b, nh // hp),
        in_specs=[
            pl.BlockSpec((1, s, hp * HEAD_PAD), lambda bi, h, *_: (bi, 0, h)),
            pl.BlockSpec((1, s, hp * QK_NOPE), lambda bi, h, *_: (bi, 0, h)),
            pl.BlockSpec((1, s, LANES), lambda bi, h, *_: (bi, 0, 0)),
            pl.BlockSpec((1, nq, hp * V_DIM, tk), lambda bi, h, *_: (bi, 0, h, 0)),
            pl.BlockSpec((N_META, hp * QK_NOPE), lambda bi, h, *_: (0, h)),
            pl.BlockSpec((N_META, LANES), lambda bi, h, *_: (0, 0)),
            pl.BlockSpec((hp * V_DIM, N_META), lambda bi, h, *_: (h, 0)),
        ],
        out_specs=pl.BlockSpec((1, s, hp * V_DIM), lambda bi, h, *_: (bi, 0, h)),
        scratch_shapes=[pltpu.VMEM((hp, V_DIM, tk), F32), pltpu.VMEM((hp, tk, tk), F32),
                        pltpu.VMEM((hp, tk, tk), F32), pltpu.VMEM((hp, 4, 1, tk), F32)],
    )
    return pl.pallas_call(
        _attn_kernel,
        grid_spec=grid_spec,
        out_shape=jax.ShapeDtypeStruct((b, s, nh * V_DIM), BF16),
        compiler_params=_params(("parallel", "parallel")),
        name="attn",
    )(*tables, q, k_nope, k_pe, vt, knm, kpm, vtm)


def _mixout_kernel(h_ref, gmix_ref, gpost_ref, pool_ref, attn_ref,
                   wgp_ref, wgm_ref, wpo_ref, wmo_ref, wout_ref, o_ref, hn_ref):
    c = pl.program_id(1)
    last = pl.num_programs(1) - 1
    chunks = _row_chunks(h_ref.shape[0], ROW_CHUNK)

    def mix(hn, rows):
        g_pool = jnp.dot(hn, wgp_ref[...], preferred_element_type=F32)
        g_mla = jnp.dot(hn, wgm_ref[...], preferred_element_type=F32)
        y_pool = jnp.dot(pool_ref[rows, :], wpo_ref[...], preferred_element_type=F32)
        y_mla = jnp.dot(attn_ref[rows, :], wmo_ref[...], preferred_element_type=F32)
        y = (jax.nn.sigmoid(g_pool) * y_pool + jax.nn.sigmoid(g_mla) * y_mla).astype(BF16)
        return jnp.dot(y, wout_ref[...], preferred_element_type=F32)

    @pl.when(c == 0)
    def _():
        for rows in chunks:
            hn = (_rms(h_ref[rows, :]) * gmix_ref[...]).astype(BF16)
            hn_ref[rows, :] = hn
            o_ref[rows, :] = mix(hn, rows)

    @pl.when(jnp.logical_and(c > 0, c < last))
    def _():
        o_ref[...] += mix(hn_ref[...], slice(None))

    @pl.when(c == last)
    def _():
        for rows in chunks:
            y = o_ref[rows, :] + mix(hn_ref[rows, :], rows)
            o_ref[rows, :] = h_ref[rows, :] + _rms(y) * gpost_ref[...]


def _mixout(h, g_mix, g_post, pool, attn, w_gp, w_gm, w_po, w_mo, w_out, *, tm, tn):
    rows, d = h.shape
    row = lambda i, c: (i, 0)
    const = lambda i, c: (0, 0)
    col = lambda i, c: (0, c)
    return pl.pallas_call(
        _mixout_kernel,
        grid=(rows // tm, d // tn),
        in_specs=[
            pl.BlockSpec((tm, d), row),
            pl.BlockSpec((1, d), const),
            pl.BlockSpec((1, d), const),
            pl.BlockSpec((tm, pool.shape[1]), row),
            pl.BlockSpec((tm, attn.shape[1]), row),
            pl.BlockSpec((d, tn), col),
            pl.BlockSpec((d, tn), col),
            pl.BlockSpec((pool.shape[1], tn), col),
            pl.BlockSpec((attn.shape[1], tn), col),
            pl.BlockSpec((tn, d), lambda i, c: (c, 0)),
        ],
        out_specs=pl.BlockSpec((tm, d), row),
        out_shape=jax.ShapeDtypeStruct((rows, d), F32),
        scratch_shapes=[pltpu.VMEM((tm, d), BF16)],
        compiler_params=_params(("parallel", "arbitrary")),
        name="mixout",
    )(h, g_mix, g_post, pool, attn, w_gp, w_gm, w_po, w_mo, w_out)


def _rope_tables(n_pos):
    pos = jnp.arange(n_pos, dtype=F32)
    inv = ROPE_THETA ** (-jnp.arange(0, QK_ROPE, 2, dtype=F32) / QK_ROPE)
    ang = pos[:, None] * inv[None, :]
    ang = jnp.concatenate([ang, ang], axis=-1)
    pad = jnp.zeros((n_pos, LANES - QK_ROPE), F32)
    return (jnp.concatenate([jnp.cos(ang), pad], axis=-1),
            jnp.concatenate([jnp.sin(ang), pad], axis=-1))


def kernel(x, meta_tokens, norm_ffn1_pre, norm_ffn1_post, ffn1_w_gu, ffn1_w_down, norm_mix_pre, norm_mix_post, w_in, pool_w, pool_scale, w_pool_o, q_a_norm, w_q_b, kv_a_norm, w_kv_b, w_mla_o, w_out, norm_ffn2_pre, norm_ffn2_post, ffn2_w_gu, ffn2_w_down):
    bsz, seq, d = x.shape
    depth = w_in.shape[0]
    nh = MLA_HEADS
    tm, tf, tn, tq, tp = 512, 512, 512, 512, 512
    tm_ffn = 1024

    cos, sin = _rope_tables(N_META + seq)
    cos_m, sin_m, cos_r, sin_r = cos[:N_META], sin[:N_META], cos[N_META:], sin[N_META:]

    h = x.reshape(bsz * seq, d)
    hm = meta_tokens.astype(x.dtype)
    row = lambda v: v.reshape(1, -1)

    for i in range(depth):
        w_gu1, w_dn1 = ffn1_w_gu[i].astype(BF16), ffn1_w_down[i].astype(BF16)
        w_gu2, w_dn2 = ffn2_w_gu[i].astype(BF16), ffn2_w_down[i].astype(BF16)
        n_lat = POOL_WIDTH + Q_LORA + KV_LORA + QK_ROPE
        w_b = jnp.concatenate([w_in[i][:, :n_lat], jnp.zeros((d, LANES - QK_ROPE), F32)],
                              axis=1).astype(BF16)
        w_gp = w_in[i][:, n_lat:n_lat + d].astype(BF16)
        w_gm = w_in[i][:, n_lat + d:].astype(BF16)
        w_q = jnp.pad(w_q_b[i].reshape(Q_LORA, nh, QK_DIM),
                      ((0, 0), (0, 0), (0, HEAD_PAD - QK_DIM))).reshape(Q_LORA, nh * HEAD_PAD)
        w_q = w_q.astype(BF16)
        w_kv = w_kv_b[i].reshape(KV_LORA, nh, QK_NOPE + V_DIM)
        w_k = w_kv[:, :, :QK_NOPE].reshape(KV_LORA, nh * QK_NOPE).astype(BF16)
        w_vt = w_kv[:, :, QK_NOPE:].reshape(KV_LORA, nh * V_DIM).T.astype(BF16)
        w_po, w_mo, w_o = w_pool_o[i].astype(BF16), w_mla_o[i].astype(BF16), w_out[i].astype(BF16)
        p_w = pool_w[i].astype(BF16)

        h1 = _ffn(h, row(norm_ffn1_pre[i]), row(norm_ffn1_post[i]), w_gu1, w_dn1,
                  tm=tm_ffn, tf=tf)
        h1m = _ffn(hm, row(norm_ffn1_pre[i]), row(norm_ffn1_post[i]), w_gu1, w_dn1,
                   tm=N_META, tf=tf)

        u, cq, ckv, kpe = _inproj(h1, row(norm_mix_pre[i]), w_b, row(q_a_norm[i]),
                                  row(kv_a_norm[i]), cos_r, sin_r, tm=tm)
        um, cqm, ckvm, kpem = _inproj(h1m, row(norm_mix_pre[i]), w_b, row(q_a_norm[i]),
                                      row(kv_a_norm[i]), cos_m, sin_m, tm=N_META)
        q, kn, vt = _qkv(cq, ckv, w_q, w_k, w_vt, cos_r, sin_r, tm=tm, tk=tq, seq=seq)
        _, knm, vtm = _qkv(cqm, ckvm, w_q, w_k, w_vt, cos_m, sin_m,
                           tm=N_META, tk=N_META, seq=N_META)

        pooled = _pool(u.reshape(bsz, seq, POOL_WIDTH), um, p_w, row(pool_scale[i]), tp=tp)
        attn = _attn(q.reshape(bsz, seq, nh * HEAD_PAD), kn.reshape(bsz, seq, nh * QK_NOPE),
                     kpe.reshape(bsz, seq, LANES), vt, knm, kpem,
                     vtm.reshape(nh * V_DIM, N_META), hp=4)

        h2 = _mixout(h1, row(norm_mix_pre[i]), row(norm_mix_post[i]),
                     pooled.reshape(bsz * seq, POOL_WIDTH), attn.reshape(bsz * seq, nh * V_DIM),
                     w_gp, w_gm, w_po, w_mo, w_o, tm=tm, tn=tn)

        h = _ffn(h2, row(norm_ffn2_pre[i]), row(norm_ffn2_post[i]), w_gu2, w_dn2,
                 tm=tm_ffn, tf=tf)
        if i + 1 < depth:
            raise NotImplementedError("only DEPTH == 1 is supported")

    return h.reshape(bsz, seq, d)
```

```python
import functools
import math

import jax
import jax.numpy as jnp
from jax import lax
from jax.experimental import pallas as pl
from jax.experimental.pallas import tpu as pltpu

F32 = jnp.float32
BF16 = jnp.bfloat16

N_META = 16
POOL_WINDOWS = (2, 4, 8, 16)
POOL_GROUP = 256
POOL_WIDTH = POOL_GROUP * len(POOL_WINDOWS)
MLA_HEADS = 16
Q_LORA = 512
KV_LORA = 512
QK_NOPE = 128
QK_ROPE = 64
V_DIM = 128
QK_DIM = QK_NOPE + QK_ROPE
ROPE_THETA = 10000.0
EPS = 1e-6
LANES = 128
BF16_ROWS = 16
HEAD_PAD = 2 * LANES
Q_SCALE = (QK_DIM ** -0.5) * math.log2(math.e)
MASK_VALUE = -1e30

VMEM_LIMIT = 56 * 1024 * 1024
FFN_VMEM_LIMIT = 62 * 1024 * 1024
ROW_CHUNK = 256


def _rms(x):
    return x * lax.rsqrt(jnp.mean(x * x, axis=-1, keepdims=True) + EPS)


def _rope(x, cos, sin):
    rot = pltpu.roll(x, 32, 1) - pltpu.roll(x, 96, 1)
    return x * cos + rot * sin


def _params(sem, vmem_limit=VMEM_LIMIT):
    return pltpu.CompilerParams(dimension_semantics=sem, vmem_limit_bytes=vmem_limit)


def _row_chunks(rows, chunk):
    chunk = min(chunk, rows)
    return [slice(r, r + chunk) for r in range(0, rows, chunk)]


def _ffn_kernel(x_ref, gpre_ref, gpost_ref, wg_ref, wu_ref, wd_ref, o_ref, xn_ref):
    f = pl.program_id(1)
    last = pl.num_programs(1) - 1
    chunks = _row_chunks(x_ref.shape[0], ROW_CHUNK)

    def mlp(xn):
        g = jnp.dot(xn, wg_ref[...], preferred_element_type=F32)
        u = jnp.dot(xn, wu_ref[...], preferred_element_type=F32)
        a = ((g * jax.nn.sigmoid(g)) * u).astype(BF16)
        return jnp.dot(a, wd_ref[...], preferred_element_type=F32)

    @pl.when(f == 0)
    def _():
        for rows in chunks:
            xn = (_rms(x_ref[rows, :]) * gpre_ref[...]).astype(BF16)
            xn_ref[rows, :] = xn
            o_ref[rows, :] = mlp(xn)

    @pl.when(jnp.logical_and(f > 0, f < last))
    def _():
        o_ref[...] += mlp(xn_ref[...])

    @pl.when(f == last)
    def _():
        for rows in chunks:
            y = o_ref[rows, :] + mlp(xn_ref[rows, :])
            o_ref[rows, :] = x_ref[rows, :] + 0.5 * (_rms(y) * gpost_ref[...])


def _ffn(x, g_pre, g_post, w_gu, w_down, *, tm, tf):
    rows, d = x.shape
    d_ff = w_down.shape[0]
    nf = d_ff // tf
    return pl.pallas_call(
        _ffn_kernel,
        grid=(rows // tm, nf),
        in_specs=[
            pl.BlockSpec((tm, d), lambda i, f: (i, 0)),
            pl.BlockSpec((1, d), lambda i, f: (0, 0)),
            pl.BlockSpec((1, d), lambda i, f: (0, 0)),
            pl.BlockSpec((d, tf), lambda i, f: (0, f)),
            pl.BlockSpec((d, tf), lambda i, f: (0, f + nf)),
            pl.BlockSpec((tf, d), lambda i, f: (f, 0)),
        ],
        out_specs=pl.BlockSpec((tm, d), lambda i, f: (i, 0)),
        out_shape=jax.ShapeDtypeStruct((rows, d), F32),
        scratch_shapes=[pltpu.VMEM((tm, d), BF16)],
        compiler_params=_params(("parallel", "arbitrary"), FFN_VMEM_LIMIT),
        name="ffn",
    )(x, g_pre, g_post, w_gu, w_gu, w_down)


def _inproj_kernel(h_ref, gmix_ref, w_ref, gq_ref, gkv_ref, cos_ref, sin_ref,
                   u_ref, cq_ref, ckv_ref, kpe_ref):
    c0, c1, c2, c3 = POOL_WIDTH, POOL_WIDTH + Q_LORA, POOL_WIDTH + Q_LORA + KV_LORA, w_ref.shape[1]
    for rows in _row_chunks(h_ref.shape[0], ROW_CHUNK):
        hn = (_rms(h_ref[rows, :]) * gmix_ref[...]).astype(BF16)
        u_ref[rows, :] = jnp.dot(hn, w_ref[:, :c0], preferred_element_type=F32)
        cq = jnp.dot(hn, w_ref[:, c0:c1], preferred_element_type=F32)
        cq_ref[rows, :] = (_rms(cq) * gq_ref[...]).astype(BF16)
        ckv = jnp.dot(hn, w_ref[:, c1:c2], preferred_element_type=F32)
        ckv_ref[rows, :] = (_rms(ckv) * gkv_ref[...]).astype(BF16)
        kr = jnp.dot(hn, w_ref[:, c2:c3], preferred_element_type=F32)
        kpe_ref[rows, :] = _rope(kr, cos_ref[rows, :], sin_ref[rows, :]).astype(BF16)


def _inproj(h, g_mix, w_b, g_q, g_kv, cos, sin, *, tm):
    rows, d = h.shape
    n_pos = cos.shape[0] // tm
    wcols = w_b.shape[1]
    row = lambda i: (i, 0)
    const = lambda i: (0, 0)
    pos = lambda i: (i % n_pos, 0)
    return pl.pallas_call(
        _inproj_kernel,
        grid=(rows // tm,),
        in_specs=[
            pl.BlockSpec((tm, d), row),
            pl.BlockSpec((1, d), const),
            pl.BlockSpec((d, wcols), const),
            pl.BlockSpec((1, Q_LORA), const),
            pl.BlockSpec((1, KV_LORA), const),
            pl.BlockSpec((tm, LANES), pos),
            pl.BlockSpec((tm, LANES), pos),
        ],
        out_specs=[
            pl.BlockSpec((tm, POOL_WIDTH), row),
            pl.BlockSpec((tm, Q_LORA), row),
            pl.BlockSpec((tm, KV_LORA), row),
            pl.BlockSpec((tm, LANES), row),
        ],
        out_shape=[
            jax.ShapeDtypeStruct((rows, POOL_WIDTH), F32),
            jax.ShapeDtypeStruct((rows, Q_LORA), BF16),
            jax.ShapeDtypeStruct((rows, KV_LORA), BF16),
            jax.ShapeDtypeStruct((rows, LANES), BF16),
        ],
        compiler_params=_params(("parallel",)),
        name="inproj",
    )(h, g_mix, w_b, g_q, g_kv, cos, sin)


def _qkv_kernel(cq_ref, ckv_ref, wq_ref, wk_ref, wvt_ref, cos_ref, sin_ref,
                q_ref, k_ref, vt_ref):
    cq = cq_ref[...]
    ckv = ckv_ref[...]
    cos = cos_ref[...] * Q_SCALE
    sin = sin_ref[...] * Q_SCALE
    for h in range(MLA_HEADS):
        lo = h * HEAD_PAD
        qh = jnp.dot(cq, wq_ref[:, lo:lo + HEAD_PAD], preferred_element_type=F32)
        q_ref[:, lo:lo + LANES] = (qh[:, :LANES] * Q_SCALE).astype(BF16)
        q_ref[:, lo + LANES:lo + HEAD_PAD] = _rope(qh[:, LANES:], cos, sin).astype(BF16)
    k_ref[...] = jnp.dot(ckv, wk_ref[...], preferred_element_type=F32).astype(BF16)
    vt = lax.dot_general(wvt_ref[...], ckv, (((1,), (1,)), ((), ())),
                         preferred_element_type=F32)
    tk = vt_ref.shape[-1]
    for c in range(vt_ref.shape[1]):
        vt_ref[0, c] = vt[:, c * tk:(c + 1) * tk].astype(BF16)


def _qkv(cq, ckv, w_q, w_k, w_vt, cos, sin, *, tm, tk, seq):
    rows = cq.shape[0]
    n_pos = seq // tm
    nh = MLA_HEADS
    row = lambda i: (i, 0)
    const = lambda i: (0, 0)
    pos = lambda i: (i % n_pos, 0)
    return pl.pallas_call(
        _qkv_kernel,
        grid=(rows // tm,),
        in_specs=[
            pl.BlockSpec((tm, Q_LORA), row),
            pl.BlockSpec((tm, KV_LORA), row),
            pl.BlockSpec((Q_LORA, nh * HEAD_PAD), const),
            pl.BlockSpec((KV_LORA, nh * QK_NOPE), const),
            pl.BlockSpec((nh * V_DIM, KV_LORA), const),
            pl.BlockSpec((tm, LANES), pos),
            pl.BlockSpec((tm, LANES), pos),
        ],
        out_specs=[
            pl.BlockSpec((tm, nh * HEAD_PAD), row),
            pl.BlockSpec((tm, nh * QK_NOPE), row),
            pl.BlockSpec((1, tm // tk, nh * V_DIM, tk), lambda i: (i // n_pos, i % n_pos, 0, 0)),
        ],
        out_shape=[
            jax.ShapeDtypeStruct((rows, nh * HEAD_PAD), BF16),
            jax.ShapeDtypeStruct((rows, nh * QK_NOPE), BF16),
            jax.ShapeDtypeStruct((rows // seq, seq // tk, nh * V_DIM, tk), BF16),
        ],
        compiler_params=_params(("parallel",)),
        name="qkv",
    )(cq, ckv, w_q, w_k, w_vt, cos, sin)


def _pool_kernel(u_ref, prev_ref, meta_ref, w_ref, scale_ref, o_ref, buf_ref):
    first = pl.program_id(1) == 0
    halo = jnp.where(first, meta_ref[...], prev_ref[0])
    for g, win in enumerate(POOL_WINDOWS):
        cols = slice(g * POOL_GROUP, (g + 1) * POOL_GROUP)
        buf_ref[:N_META, :] = halo[:, cols]
        buf_ref[N_META:, :] = u_ref[0, :, cols]
        tot = buf_ref[...]
        shift = 1
        while shift < win:
            tot = tot + pltpu.roll(tot, shift, 0)
            shift *= 2
        cur = buf_ref[N_META:, :]
        dlt = (tot[N_META:, :] * (1.0 / win) - cur).astype(BF16)
        y = jnp.dot(dlt, w_ref[g], preferred_element_type=F32)
        o_ref[0, :, cols] = (y * scale_ref[:, cols]).astype(BF16)


def _pool(u, u_meta, pool_w, pool_scale, *, tp):
    b, s, c = u.shape
    hb = tp // N_META
    return pl.pallas_call(
        _pool_kernel,
        grid=(b, s // tp),
        in_specs=[
            pl.BlockSpec((1, tp, c), lambda bi, i: (bi, i, 0)),
            pl.BlockSpec((1, N_META, c), lambda bi, i: (bi, jnp.maximum(i * hb - 1, 0), 0)),
            pl.BlockSpec((N_META, c), lambda bi, i: (0, 0)),
            pl.BlockSpec((len(POOL_WINDOWS), POOL_GROUP, POOL_GROUP), lambda bi, i: (0, 0, 0)),
            pl.BlockSpec((1, c), lambda bi, i: (0, 0)),
        ],
        out_specs=pl.BlockSpec((1, tp, c), lambda bi, i: (bi, i, 0)),
        out_shape=jax.ShapeDtypeStruct((b, s, c), BF16),
        scratch_shapes=[pltpu.VMEM((tp + N_META, POOL_GROUP), F32)],
        compiler_params=_params(("parallel", "arbitrary")),
        name="pool",
    )(u, u, u_meta, pool_w, pool_scale)


def _attn_kernel(qi_ref, kj_ref, new_ref, q_ref, kn_ref, kp_ref, vt_ref, knm_ref, kpm_ref,
                 vtm_ref, o_ref, acc_ref, sa_ref, sb_ref, st_ref):
    tk = vt_ref.shape[-1]
    tq = tk
    hp = acc_ref.shape[0]
    n_pairs = qi_ref.shape[0]
    nt = (((1,), (1,)), ((), ()))
    bufs = ((sa_ref, 2), (sb_ref, 3))

    def q_of(h, qi):
        return q_ref[0, pl.ds(pl.multiple_of(qi * tq, tq), tq), h * HEAD_PAD:(h + 1) * HEAD_PAD]

    def init_tile(qi, heads):
        kpm = kpm_ref[...]
        for h in heads:
            km = jnp.concatenate([knm_ref[:, h * QK_NOPE:(h + 1) * QK_NOPE], kpm], axis=1)
            s = lax.dot_general(km, q_of(h, qi), nt, preferred_element_type=F32)
            m0 = jnp.max(s, axis=0, keepdims=True)
            p = jnp.exp2(s - m0)
            acc_ref[h] = jnp.dot(vtm_ref[h * V_DIM:(h + 1) * V_DIM, :], p.astype(BF16),
                                 preferred_element_type=F32)
            st_ref[h, 0] = m0
            st_ref[h, 1] = jnp.sum(p, axis=0, keepdims=True)

    def scores(t, dst, masked, heads):
        s_ref, row = dst
        qi = qi_ref[t]
        start = pl.multiple_of(kj_ref[t] * tk, tk)
        kp = kp_ref[0, pl.ds(start, tk), :]
        for h in heads:
            k = jnp.concatenate(
                [kn_ref[0, pl.ds(start, tk), h * QK_NOPE:(h + 1) * QK_NOPE], kp], axis=1)
            s = lax.dot_general(k, q_of(h, qi), nt, preferred_element_type=F32)
            if masked:
                kpos = lax.broadcasted_iota(jnp.int32, s.shape, 0)
                qpos = lax.broadcasted_iota(jnp.int32, s.shape, 1)
                s = jnp.where(kpos <= qpos, s, MASK_VALUE)
            s_ref[h] = s
            st_ref[h, row] = jnp.max(s, axis=0, keepdims=True)

    def absorb(t, src, heads):
        s_ref, row = src
        kj = kj_ref[t]
        ones = jnp.ones((BF16_ROWS, tk), BF16)
        for h in heads:
            m, l = st_ref[h, 0], st_ref[h, 1]
            m_new = jnp.maximum(m, st_ref[h, row])
            alpha = jnp.exp2(m - m_new)
            p = jnp.exp2(s_ref[h] - m_new)
            vt1 = jnp.concatenate([vt_ref[0, kj, h * V_DIM:(h + 1) * V_DIM, :], ones], axis=0)
            pv = jnp.dot(vt1, p.astype(BF16), preferred_element_type=F32)
            acc_ref[h] = alpha * acc_ref[h] + pv[:V_DIM]
            st_ref[h, 0] = m_new
            st_ref[h, 1] = alpha * l + pv[V_DIM:V_DIM + 1]

    def finalize(qi, heads):
        rows = pl.ds(pl.multiple_of(qi * tq, tq), tq)
        for h in heads:
            o_ref[0, rows, h * V_DIM:(h + 1) * V_DIM] = (
                acc_ref[h] / st_ref[h, 1]).T.astype(BF16)

    def plain(t, src, dst):
        for h in range(hp):
            scores(t, dst, False, [h])
            absorb(t - 1, src, [h])

    def fresh(t, src, dst):
        scores(t, dst, True, [0])
        for h in range(hp):
            absorb(t - 1, src, [h])
            if h + 1 < hp:
                scores(t, dst, True, [h + 1])
            finalize(qi_ref[t - 1], [h])
            init_tile(qi_ref[t], [h])

    def stage(t, src, dst):
        pl.when(new_ref[t] == 1)(lambda: fresh(t, src, dst))
        pl.when(new_ref[t] == 0)(lambda: plain(t, src, dst))

    heads = list(range(hp))
    init_tile(qi_ref[0], heads)
    scores(0, bufs[0], True, heads)

    def body(r, carry):
        t = 2 * r + 1
        a, b = bufs

        @pl.when(new_ref[t] == 1)
        def _():
            fresh(t, a, b)
            plain(t + 1, b, a)

        @pl.when(new_ref[t + 1] == 1)
        def _():
            plain(t, a, b)
            fresh(t + 1, b, a)

        @pl.when(new_ref[t] + new_ref[t + 1] == 0)
        def _():
            plain(t, a, b)
            plain(t + 1, b, a)

        return carry

    lax.fori_loop(0, (n_pairs - 1) // 2, body, 0)
    last = bufs[0]
    if (n_pairs - 1) % 2:
        stage(n_pairs - 1, bufs[0], bufs[1])
        last = bufs[1]
    absorb(n_pairs - 1, last, heads)
    finalize(qi_ref[n_pairs - 1], heads)


def _attn(q, k_nope, k_pe, vt, knm, kpm, vtm, *, hp):
    b, s, _ = q.shape
    tk = vt.shape[-1]
    nq = s // tk
    nh = MLA_HEADS
    qi, kj, new = [], [], []
    for i in range(nq):
        for n, j in enumerate([i] + list(range(i))):
            qi.append(i)
            kj.append(j)
            new.append(int(n == 0))
    assert not any(new[t] and new[t + 1] for t in range(1, len(new) - 1, 2))
    tables = [jnp.asarray(v, jnp.int32) for v in (qi, kj, new)]
    grid_spec = pltpu.PrefetchScalarGridSpec(
        num_scalar_prefetch=len(tables),
        grid=(b, nh // hp),
        in_specs=[
            pl.BlockSpec((1, s, hp * HEAD_PAD), lambda bi, h, *_: (bi, 0, h)),
            pl.BlockSpec((1, s, hp * QK_NOPE), lambda bi, h, *_: (bi, 0, h)),
            pl.BlockSpec((1, s, LANES), lambda bi, h, *_: (bi, 0, 0)),
            pl.BlockSpec((1, nq, hp * V_DIM, tk), lambda bi, h, *_: (bi, 0, h, 0)),
            pl.BlockSpec((N_META, hp * QK_NOPE), lambda bi, h, *_: (0, h)),
            pl.BlockSpec((N_META, LANES), lambda bi, h, *_: (0, 0)),
            pl.BlockSpec((hp * V_DIM, N_META), lambda bi, h, *_: (h, 0)),
        ],
        out_specs=pl.BlockSpec((1, s, hp * V_DIM), lambda bi, h, *_: (bi, 0, h)),
        scratch_shapes=[pltpu.VMEM((hp, V_DIM, tk), F32), pltpu.VMEM((hp, tk, tk), F32),
                        pltpu.VMEM((hp, tk, tk), F32), pltpu.VMEM((hp, 4, 1, tk), F32)],
    )
    return pl.pallas_call(
        _attn_kernel,
        grid_spec=grid_spec,
        out_shape=jax.ShapeDtypeStruct((b, s, nh * V_DIM), BF16),
        compiler_params=_params(("parallel", "parallel")),
        name="attn",
    )(*tables, q, k_nope, k_pe, vt, knm, kpm, vtm)


def _mixout_kernel(h_ref, gmix_ref, gpost_ref, pool_ref, attn_ref,
                   wgp_ref, wgm_ref, wpo_ref, wmo_ref, wout_ref, o_ref, hn_ref):
    c = pl.program_id(1)
    last = pl.num_programs(1) - 1
    chunks = _row_chunks(h_ref.shape[0], ROW_CHUNK)

    def mix(hn, rows):
        g_pool = jnp.dot(hn, wgp_ref[...], preferred_element_type=F32)
        g_mla = jnp.dot(hn, wgm_ref[...], preferred_element_type=F32)
        y_pool = jnp.dot(pool_ref[rows, :], wpo_ref[...], preferred_element_type=F32)
        y_mla = jnp.dot(attn_ref[rows, :], wmo_ref[...], preferred_element_type=F32)
        y = (jax.nn.sigmoid(g_pool) * y_pool + jax.nn.sigmoid(g_mla) * y_mla).astype(BF16)
        return jnp.dot(y, wout_ref[...], preferred_element_type=F32)

    @pl.when(c == 0)
    def _():
        for rows in chunks:
            hn = (_rms(h_ref[rows, :]) * gmix_ref[...]).astype(BF16)
            hn_ref[rows, :] = hn
            o_ref[rows, :] = mix(hn, rows)

    @pl.when(jnp.logical_and(c > 0, c < last))
    def _():
        o_ref[...] += mix(hn_ref[...], slice(None))

    @pl.when(c == last)
    def _():
        for rows in chunks:
            y = o_ref[rows, :] + mix(hn_ref[rows, :], rows)
            o_ref[rows, :] = h_ref[rows, :] + _rms(y) * gpost_ref[...]


def _mixout(h, g_mix, g_post, pool, attn, w_gp, w_gm, w_po, w_mo, w_out, *, tm, tn):
    rows, d = h.shape
    row = lambda i, c: (i, 0)
    const = lambda i, c: (0, 0)
    col = lambda i, c: (0, c)
    return pl.pallas_call(
        _mixout_kernel,
        grid=(rows // tm, d // tn),
        in_specs=[
            pl.BlockSpec((tm, d), row),
            pl.BlockSpec((1, d), const),
            pl.BlockSpec((1, d), const),
            pl.BlockSpec((tm, pool.shape[1]), row),
            pl.BlockSpec((tm, attn.shape[1]), row),
            pl.BlockSpec((d, tn), col),
            pl.BlockSpec((d, tn), col),
            pl.BlockSpec((pool.shape[1], tn), col),
            pl.BlockSpec((attn.shape[1], tn), col),
            pl.BlockSpec((tn, d), lambda i, c: (c, 0)),
        ],
        out_specs=pl.BlockSpec((tm, d), row),
        out_shape=jax.ShapeDtypeStruct((rows, d), F32),
        scratch_shapes=[pltpu.VMEM((tm, d), BF16)],
        compiler_params=_params(("parallel", "arbitrary")),
        name="mixout",
    )(h, g_mix, g_post, pool, attn, w_gp, w_gm, w_po, w_mo, w_out)


def _rope_tables(n_pos):
    pos = jnp.arange(n_pos, dtype=F32)
    inv = ROPE_THETA ** (-jnp.arange(0, QK_ROPE, 2, dtype=F32) / QK_ROPE)
    ang = pos[:, None] * inv[None, :]
    ang = jnp.concatenate([ang, ang], axis=-1)
    pad = jnp.zeros((n_pos, LANES - QK_ROPE), F32)
    return (jnp.concatenate([jnp.cos(ang), pad], axis=-1),
            jnp.concatenate([jnp.sin(ang), pad], axis=-1))


def kernel(x, meta_tokens, norm_ffn1_pre, norm_ffn1_post, ffn1_w_gu, ffn1_w_down, norm_mix_pre, norm_mix_post, w_in, pool_w, pool_scale, w_pool_o, q_a_norm, w_q_b, kv_a_norm, w_kv_b, w_mla_o, w_out, norm_ffn2_pre, norm_ffn2_post, ffn2_w_gu, ffn2_w_down):
    bsz, seq, d = x.shape
    depth = w_in.shape[0]
    nh = MLA_HEADS
    tm, tf, tn, tq, tp = 512, 512, 512, 512, 512
    tm_ffn = 1024

    cos, sin = _rope_tables(N_META + seq)
    cos_m, sin_m, cos_r, sin_r = cos[:N_META], sin[:N_META], cos[N_META:], sin[N_META:]

    h = x.reshape(bsz * seq, d)
    hm = meta_tokens.astype(x.dtype)
    row = lambda v: v.reshape(1, -1)

    for i in range(depth):
        w_gu1, w_dn1 = ffn1_w_gu[i].astype(BF16), ffn1_w_down[i].astype(BF16)
        w_gu2, w_dn2 = ffn2_w_gu[i].astype(BF16), ffn2_w_down[i].astype(BF16)
        n_lat = POOL_WIDTH + Q_LORA + KV_LORA + QK_ROPE
        w_b = jnp.concatenate([w_in[i][:, :n_lat], jnp.zeros((d, LANES - QK_ROPE), F32)],
                              axis=1).astype(BF16)
        w_gp = w_in[i][:, n_lat:n_lat + d].astype(BF16)
        w_gm = w_in[i][:, n_lat + d:].astype(BF16)
        w_q = jnp.pad(w_q_b[i].reshape(Q_LORA, nh, QK_DIM),
                      ((0, 0), (0, 0), (0, HEAD_PAD - QK_DIM))).reshape(Q_LORA, nh * HEAD_PAD)
        w_q = w_q.astype(BF16)
        w_kv = w_kv_b[i].reshape(KV_LORA, nh, QK_NOPE + V_DIM)
        w_k = w_kv[:, :, :QK_NOPE].reshape(KV_LORA, nh * QK_NOPE).astype(BF16)
        w_vt = w_kv[:, :, QK_NOPE:].reshape(KV_LORA, nh * V_DIM).T.astype(BF16)
        w_po, w_mo, w_o = w_pool_o[i].astype(BF16), w_mla_o[i].astype(BF16), w_out[i].astype(BF16)
        p_w = pool_w[i].astype(BF16)

        h1 = _ffn(h, row(norm_ffn1_pre[i]), row(norm_ffn1_post[i]), w_gu1, w_dn1,
                  tm=tm_ffn, tf=tf)
        h1m = _ffn(hm, row(norm_ffn1_pre[i]), row(norm_ffn1_post[i]), w_gu1, w_dn1,
                   tm=N_META, tf=tf)

        u, cq, ckv, kpe = _inproj(h1, row(norm_mix_pre[i]), w_b, row(q_a_norm[i]),
                                  row(kv_a_norm[i]), cos_r, sin_r, tm=tm)
        um, cqm, ckvm, kpem = _inproj(h1m, row(norm_mix_pre[i]), w_b, row(q_a_norm[i]),
                                      row(kv_a_norm[i]), cos_m, sin_m, tm=N_META)
        q, kn, vt = _qkv(cq, ckv, w_q, w_k, w_vt, cos_r, sin_r, tm=tm, tk=tq, seq=seq)
        _, knm, vtm = _qkv(cqm, ckvm, w_q, w_k, w_vt, cos_m, sin_m,
                           tm=N_META, tk=N_META, seq=N_META)

        pooled = _pool(u.reshape(bsz, seq, POOL_WIDTH), um, p_w, row(pool_scale[i]), tp=tp)
        attn = _attn(q.reshape(bsz, seq, nh * HEAD_PAD), kn.reshape(bsz, seq, nh * QK_NOPE),
                     kpe.reshape(bsz, seq, LANES), vt, knm, kpem,
                     vtm.reshape(nh * V_DIM, N_META), hp=4)

        h2 = _mixout(h1, row(norm_mix_pre[i]), row(norm_mix_post[i]),
                     pooled.reshape(bsz * seq, POOL_WIDTH), attn.reshape(bsz * seq, nh * V_DIM),
                     w_gp, w_gm, w_po, w_mo, w_o, tm=tm, tn=tn)

        h = _ffn(h2, row(norm_ffn2_pre[i]), row(norm_ffn2_post[i]), w_gu2, w_dn2,
                 tm=tm_ffn, tf=tf)
        if i + 1 < depth:
            raise NotImplementedError("only DEPTH == 1 is supported")

    return h.reshape(bsz, seq, d)
```

```python
import functools
import math

import jax
import jax.numpy as jnp
from jax import lax
from jax.experimental import pallas as pl
from jax.experimental.pallas import tpu as pltpu

F32 = jnp.float32
BF16 = jnp.bfloat16

N_META = 16
POOL_WINDOWS = (2, 4, 8, 16)
POOL_GROUP = 256
POOL_WIDTH = POOL_GROUP * len(POOL_WINDOWS)
MLA_HEADS = 16
Q_LORA = 512
KV_LORA = 512
QK_NOPE = 128
QK_ROPE = 64
V_DIM = 128
QK_DIM = QK_NOPE + QK_ROPE
ROPE_THETA = 10000.0
EPS = 1e-6
LANES = 128
BF16_ROWS = 16
HEAD_PAD = 2 * LANES
Q_SCALE = (QK_DIM ** -0.5) * math.log2(math.e)
MASK_VALUE = -1e30

VMEM_LIMIT = 56 * 1024 * 1024
FFN_VMEM_LIMIT = 62 * 1024 * 1024
ROW_CHUNK = 256


def _rms(x):
    return x * lax.rsqrt(jnp.mean(x * x, axis=-1, keepdims=True) + EPS)


def _rope(x, cos, sin):
    rot = pltpu.roll(x, 32, 1) - pltpu.roll(x, 96, 1)
    return x * cos + rot * sin


def _params(sem, vmem_limit=VMEM_LIMIT):
    return pltpu.CompilerParams(dimension_semantics=sem, vmem_limit_bytes=vmem_limit)


def _row_chunks(rows, chunk):
    chunk = min(chunk, rows)
    return [slice(r, r + chunk) for r in range(0, rows, chunk)]


def _ffn_kernel(x_ref, gpre_ref, gpost_ref, wg_ref, wu_ref, wd_ref, o_ref, xn_ref):
    f = pl.program_id(1)
    last = pl.num_programs(1) - 1
    chunks = _row_chunks(x_ref.shape[0], ROW_CHUNK)

    def mlp(xn):
        g = jnp.dot(xn, wg_ref[...], preferred_element_type=F32)
        u = jnp.dot(xn, wu_ref[...], preferred_element_type=F32)
        a = ((g * jax.nn.sigmoid(g)) * u).astype(BF16)
        return jnp.dot(a, wd_ref[...], preferred_element_type=F32)

    @pl.when(f == 0)
    def _():
        for rows in chunks:
            xn = (_rms(x_ref[rows, :]) * gpre_ref[...]).astype(BF16)
            xn_ref[rows, :] = xn
            o_ref[rows, :] = mlp(xn)

    @pl.when(jnp.logical_and(f > 0, f < last))
    def _():
        o_ref[...] += mlp(xn_ref[...])

    @pl.when(f == last)
    def _():
        for rows in chunks:
            y = o_ref[rows, :] + mlp(xn_ref[rows, :])
            o_ref[rows, :] = x_ref[rows, :] + 0.5 * (_rms(y) * gpost_ref[...])


def _ffn(x, g_pre, g_post, w_gu, w_down, *, tm, tf):
    rows, d = x.shape
    d_ff = w_down.shape[0]
    nf = d_ff // tf
    return pl.pallas_call(
        _ffn_kernel,
        grid=(rows // tm, nf),
        in_specs=[
            pl.BlockSpec((tm, d), lambda i, f: (i, 0)),
            pl.BlockSpec((1, d), lambda i, f: (0, 0)),
            pl.BlockSpec((1, d), lambda i, f: (0, 0)),
            pl.BlockSpec((d, tf), lambda i, f: (0, f)),
            pl.BlockSpec((d, tf), lambda i, f: (0, f + nf)),
            pl.BlockSpec((tf, d), lambda i, f: (f, 0)),
        ],
        out_specs=pl.BlockSpec((tm, d), lambda i, f: (i, 0)),
        out_shape=jax.ShapeDtypeStruct((rows, d), F32),
        scratch_shapes=[pltpu.VMEM((tm, d), BF16)],
        compiler_params=_params(("parallel", "arbitrary"), FFN_VMEM_LIMIT),
        name="ffn",
    )(x, g_pre, g_post, w_gu, w_gu, w_down)


def _inproj_kernel(h_ref, gmix_ref, w_ref, gq_ref, gkv_ref, cos_ref, sin_ref,
                   u_ref, cq_ref, ckv_ref, kpe_ref):
    c0, c1, c2, c3 = POOL_WIDTH, POOL_WIDTH + Q_LORA, POOL_WIDTH + Q_LORA + KV_LORA, w_ref.shape[1]
    for rows in _row_chunks(h_ref.shape[0], ROW_CHUNK):
        hn = (_rms(h_ref[rows, :]) * gmix_ref[...]).astype(BF16)
        u_ref[rows, :] = jnp.dot(hn, w_ref[:, :c0], preferred_element_type=F32)
        cq = jnp.dot(hn, w_ref[:, c0:c1], preferred_element_type=F32)
        cq_ref[rows, :] = (_rms(cq) * gq_ref[...]).astype(BF16)
        ckv = jnp.dot(hn, w_ref[:, c1:c2], preferred_element_type=F32)
        ckv_ref[rows, :] = (_rms(ckv) * gkv_ref[...]).astype(BF16)
        kr = jnp.dot(hn, w_ref[:, c2:c3], preferred_element_type=F32)
        kpe_ref[rows, :] = _rope(kr, cos_ref[rows, :], sin_ref[rows, :]).astype(BF16)


def _inproj(h, g_mix, w_b, g_q, g_kv, cos, sin, *, tm):
    rows, d = h.shape
    n_pos = cos.shape[0] // tm
    wcols = w_b.shape[1]
    row = lambda i: (i, 0)
    const = lambda i: (0, 0)
    pos = lambda i: (i % n_pos, 0)
    return pl.pallas_call(
        _inproj_kernel,
        grid=(rows // tm,),
        in_specs=[
            pl.BlockSpec((tm, d), row),
            pl.BlockSpec((1, d), const),
            pl.BlockSpec((d, wcols), const),
            pl.BlockSpec((1, Q_LORA), const),
            pl.BlockSpec((1, KV_LORA), const),
            pl.BlockSpec((tm, LANES), pos),
            pl.BlockSpec((tm, LANES), pos),
        ],
        out_specs=[
            pl.BlockSpec((tm, POOL_WIDTH), row),
            pl.BlockSpec((tm, Q_LORA), row),
            pl.BlockSpec((tm, KV_LORA), row),
            pl.BlockSpec((tm, LANES), row),
        ],
        out_shape=[
            jax.ShapeDtypeStruct((rows, POOL_WIDTH), F32),
            jax.ShapeDtypeStruct((rows, Q_LORA), BF16),
            jax.ShapeDtypeStruct((rows, KV_LORA), BF16),
            jax.ShapeDtypeStruct((rows, LANES), BF16),
        ],
        compiler_params=_params(("parallel",)),
        name="inproj",
    )(h, g_mix, w_b, g_q, g_kv, cos, sin)


def _qkv_kernel(cq_ref, ckv_ref, wq_ref, wk_ref, wvt_ref, cos_ref, sin_ref,
                q_ref, k_ref, vt_ref):
    cq = cq_ref[...]
    ckv = ckv_ref[...]
    cos = cos_ref[...] * Q_SCALE
    sin = sin_ref[...] * Q_SCALE
    for h in range(MLA_HEADS):
        lo = h * HEAD_PAD
        qh = jnp.dot(cq, wq_ref[:, lo:lo + HEAD_PAD], preferred_element_type=F32)
        q_ref[:, lo:lo + LANES] = (qh[:, :LANES] * Q_SCALE).astype(BF16)
        q_ref[:, lo + LANES:lo + HEAD_PAD] = _rope(qh[:, LANES:], cos, sin).astype(BF16)
    k_ref[...] = jnp.dot(ckv, wk_ref[...], preferred_element_type=F32).astype(BF16)
    vt = lax.dot_general(wvt_ref[...], ckv, (((1,), (1,)), ((), ())),
                         preferred_element_type=F32)
    tk = vt_ref.shape[-1]
    for c in range(vt_ref.shape[1]):
        vt_ref[0, c] = vt[:, c * tk:(c + 1) * tk].astype(BF16)


def _qkv(cq, ckv, w_q, w_k, w_vt, cos, sin, *, tm, tk, seq):
    rows = cq.shape[0]
    n_pos = seq // tm
    nh = MLA_HEADS
    row = lambda i: (i, 0)
    const = lambda i: (0, 0)
    pos = lambda i: (i % n_pos, 0)
    return pl.pallas_call(
        _qkv_kernel,
        grid=(rows // tm,),
        in_specs=[
            pl.BlockSpec((tm, Q_LORA), row),
            pl.BlockSpec((tm, KV_LORA), row),
            pl.BlockSpec((Q_LORA, nh * HEAD_PAD), const),
            pl.BlockSpec((KV_LORA, nh * QK_NOPE), const),
            pl.BlockSpec((nh * V_DIM, KV_LORA), const),
            pl.BlockSpec((tm, LANES), pos),
            pl.BlockSpec((tm, LANES), pos),
        ],
        out_specs=[
            pl.BlockSpec((tm, nh * HEAD_PAD), row),
            pl.BlockSpec((tm, nh * QK_NOPE), row),
            pl.BlockSpec((1, tm // tk, nh * V_DIM, tk), lambda i: (i // n_pos, i % n_pos, 0, 0)),
        ],
        out_shape=[
            jax.ShapeDtypeStruct((rows, nh * HEAD_PAD), BF16),
            jax.ShapeDtypeStruct((rows, nh * QK_NOPE), BF16),
            jax.ShapeDtypeStruct((rows // seq, seq // tk, nh * V_DIM, tk), BF16),
        ],
        compiler_params=_params(("parallel",)),
        name="qkv",
    )(cq, ckv, w_q, w_k, w_vt, cos, sin)


def _attn_kernel(qi_ref, kj_ref, new_ref, q_ref, kn_ref, kp_ref, vt_ref, knm_ref, kpm_ref,
                 vtm_ref, o_ref, acc_ref, sa_ref, sb_ref, st_ref):
    tk = vt_ref.shape[-1]
    tq = tk
    hp = acc_ref.shape[0]
    n_pairs = qi_ref.shape[0]
    nt = (((1,), (1,)), ((), ()))
    bufs = ((sa_ref, 2), (sb_ref, 3))

    def q_of(h, qi):
        return q_ref[0, pl.ds(pl.multiple_of(qi * tq, tq), tq), h * HEAD_PAD:(h + 1) * HEAD_PAD]

    def init_tile(qi, heads):
        kpm = kpm_ref[...]
        for h in heads:
            km = jnp.concatenate([knm_ref[:, h * QK_NOPE:(h + 1) * QK_NOPE], kpm], axis=1)
            s = lax.dot_general(km, q_of(h, qi), nt, preferred_element_type=F32)
            m0 = jnp.max(s, axis=0, keepdims=True)
            p = jnp.exp2(s - m0)
            acc_ref[h] = jnp.dot(vtm_ref[h * V_DIM:(h + 1) * V_DIM, :], p.astype(BF16),
                                 preferred_element_type=F32)
            st_ref[h, 0] = m0
            st_ref[h, 1] = jnp.sum(p, axis=0, keepdims=True)

    def scores(t, dst, masked, heads):
        s_ref, row = dst
        qi = qi_ref[t]
        start = pl.multiple_of(kj_ref[t] * tk, tk)
        kp = kp_ref[0, pl.ds(start, tk), :]
        for h in heads:
            k = jnp.concatenate(
                [kn_ref[0, pl.ds(start, tk), h * QK_NOPE:(h + 1) * QK_NOPE], kp], axis=1)
            s = lax.dot_general(k, q_of(h, qi), nt, preferred_element_type=F32)
            if masked:
                kpos = lax.broadcasted_iota(jnp.int32, s.shape, 0)
                qpos = lax.broadcasted_iota(jnp.int32, s.shape, 1)
                s = jnp.where(kpos <= qpos, s, MASK_VALUE)
            s_ref[h] = s
            st_ref[h, row] = jnp.max(s, axis=0, keepdims=True)

    def absorb(t, src, heads):
        s_ref, row = src
        kj = kj_ref[t]
        ones = jnp.ones((BF16_ROWS, tk), BF16)
        for h in heads:
            m, l = st_ref[h, 0], st_ref[h, 1]
            m_new = jnp.maximum(m, st_ref[h, row])
            alpha = jnp.exp2(m - m_new)
            p = jnp.exp2(s_ref[h] - m_new)
            vt1 = jnp.concatenate([vt_ref[0, kj, h * V_DIM:(h + 1) * V_DIM, :], ones], axis=0)
            pv = jnp.dot(vt1, p.astype(BF16), preferred_element_type=F32)
            acc_ref[h] = alpha * acc_ref[h] + pv[:V_DIM]
            st_ref[h, 0] = m_new
            st_ref[h, 1] = alpha * l + pv[V_DIM:V_DIM + 1]

    def finalize(qi, heads):
        rows = pl.ds(pl.multiple_of(qi * tq, tq), tq)
        for h in heads:
            o_ref[0, rows, h * V_DIM:(h + 1) * V_DIM] = (
                acc_ref[h] / st_ref[h, 1]).T.astype(BF16)

    def plain(t, src, dst):
        for h in range(hp):
            scores(t, dst, False, [h])
            absorb(t - 1, src, [h])

    def fresh(t, src, dst):
        scores(t, dst, True, [0])
        for h in range(hp):
            absorb(t - 1, src, [h])
            if h + 1 < hp:
                scores(t, dst, True, [h + 1])
            finalize(qi_ref[t - 1], [h])
            init_tile(qi_ref[t], [h])

    def stage(t, src, dst):
        pl.when(new_ref[t] == 1)(lambda: fresh(t, src, dst))
        pl.when(new_ref[t] == 0)(lambda: plain(t, src, dst))

    heads = list(range(hp))
    init_tile(qi_ref[0], heads)
    scores(0, bufs[0], True, heads)

    def body(r, carry):
        t = 2 * r + 1
        a, b = bufs

        @pl.when(new_ref[t] == 1)
        def _():
            fresh(t, a, b)
            plain(t + 1, b, a)

        @pl.when(new_ref[t + 1] == 1)
        def _():
            plain(t, a, b)
            fresh(t + 1, b, a)

        @pl.when(new_ref[t] + new_ref[t + 1] == 0)
        def _():
            plain(t, a, b)
            plain(t + 1, b, a)

        return carry

    lax.fori_loop(0, (n_pairs - 1) // 2, body, 0)
    last = bufs[0]
    if (n_pairs - 1) % 2:
        stage(n_pairs - 1, bufs[0], bufs[1])
        last = bufs[1]
    absorb(n_pairs - 1, last, heads)
    finalize(qi_ref[n_pairs - 1], heads)


def _attn(q, k_nope, k_pe, vt, knm, kpm, vtm, *, hp):
    b, s, _ = q.shape
    tk = vt.shape[-1]
    nq = s // tk
    nh = MLA_HEADS
    qi, kj, new = [], [], []
    for i in range(nq):
        for n, j in enumerate([i] + list(range(i))):
            qi.append(i)
            kj.append(j)
            new.append(int(n == 0))
    assert not any(new[t] and new[t + 1] for t in range(1, len(new) - 1, 2))
    tables = [jnp.asarray(v, jnp.int32) for v in (qi, kj, new)]
    grid_spec = pltpu.PrefetchScalarGridSpec(
        num_scalar_prefetch=len(tables),
        grid=(b, nh // hp),
        in_specs=[
            pl.BlockSpec((1, s, hp * HEAD_PAD), lambda bi, h, *_: (bi, 0, h)),
            pl.BlockSpec((1, s, hp * QK_NOPE), lambda bi, h, *_: (bi, 0, h)),
            pl.BlockSpec((1, s, LANES), lambda bi, h, *_: (bi, 0, 0)),
            pl.BlockSpec((1, nq, hp * V_DIM, tk), lambda bi, h, *_: (bi, 0, h, 0)),
            pl.BlockSpec((N_META, hp * QK_NOPE), lambda bi, h, *_: (0, h)),
            pl.BlockSpec((N_META, LANES), lambda bi, h, *_: (0, 0)),
            pl.BlockSpec((hp * V_DIM, N_META), lambda bi, h, *_: (h, 0)),
        ],
        out_specs=pl.BlockSpec((1, s, hp * V_DIM), lambda bi, h, *_: (bi, 0, h)),
        scratch_shapes=[pltpu.VMEM((hp, V_DIM, tk), F32), pltpu.VMEM((hp, tk, tk), F32),
                        pltpu.VMEM((hp, tk, tk), F32), pltpu.VMEM((hp, 4, 1, tk), F32)],
    )
    return pl.pallas_call(
        _attn_kernel,
        grid_spec=grid_spec,
        out_shape=jax.ShapeDtypeStruct((b, s, nh * V_DIM), BF16),
        compiler_params=_params(("parallel", "parallel")),
        name="attn",
    )(*tables, q, k_nope, k_pe, vt, knm, kpm, vtm)


def _mixout_kernel(h_ref, gmix_ref, gpost_ref, u_ref, uprev_ref, umeta_ref, pw_ref, pscale_ref,
                   attn_ref, wgp_ref, wgm_ref, wpo_ref, wmo_ref, wout_ref, o_ref,
                   hn_ref, pool_ref, buf_ref, *, tiles_per_seq):
    c = pl.program_id(1)
    last = pl.num_programs(1) - 1
    chunks = _row_chunks(h_ref.shape[0], ROW_CHUNK)

    def pool(rows, halo):
        for g, win in enumerate(POOL_WINDOWS):
            cols = slice(g * POOL_GROUP, (g + 1) * POOL_GROUP)
            buf_ref[:N_META, :] = halo[:, cols]
            buf_ref[N_META:, :] = u_ref[rows, cols]
            tot = buf_ref[...]
            shift = 1
            while shift < win:
                tot = tot + pltpu.roll(tot, shift, 0)
                shift *= 2
            dlt = (tot[N_META:, :] * (1.0 / win) - buf_ref[N_META:, :]).astype(BF16)
            y = jnp.dot(dlt, pw_ref[g], preferred_element_type=F32)
            pool_ref[rows, cols] = (y * pscale_ref[:, cols]).astype(BF16)

    def mix(hn, rows):
        g_pool = jnp.dot(hn, wgp_ref[...], preferred_element_type=F32)
        g_mla = jnp.dot(hn, wgm_ref[...], preferred_element_type=F32)
        y_pool = jnp.dot(pool_ref[rows, :], wpo_ref[...], preferred_element_type=F32)
        y_mla = jnp.dot(attn_ref[rows, :], wmo_ref[...], preferred_element_type=F32)
        y = (jax.nn.sigmoid(g_pool) * y_pool + jax.nn.sigmoid(g_mla) * y_mla).astype(BF16)
        return jnp.dot(y, wout_ref[...], preferred_element_type=F32)

    @pl.when(c == 0)
    def _():
        seq_start = pl.program_id(0) % tiles_per_seq == 0
        for n, rows in enumerate(chunks):
            if n == 0:
                halo = jnp.where(seq_start, umeta_ref[...], uprev_ref[...])
            else:
                halo = u_ref[rows.start - N_META:rows.start, :]
            pool(rows, halo)
            hn = (_rms(h_ref[rows, :]) * gmix_ref[...]).astype(BF16)
            hn_ref[rows, :] = hn
            o_ref[rows, :] = mix(hn, rows)

    @pl.when(jnp.logical_and(c > 0, c < last))
    def _():
        o_ref[...] += mix(hn_ref[...], slice(None))

    @pl.when(c == last)
    def _():
        for rows in chunks:
            y = o_ref[rows, :] + mix(hn_ref[rows, :], rows)
            o_ref[rows, :] = h_ref[rows, :] + _rms(y) * gpost_ref[...]


def _mixout(h, g_mix, g_post, u, u_meta, pool_w, pool_scale, attn,
            w_gp, w_gm, w_po, w_mo, w_out, *, tm, tn, seq):
    rows, d = h.shape
    pw = u.shape[1]
    hb = tm // N_META
    chunk = min(ROW_CHUNK, tm)
    row = lambda i, c: (i, 0)
    const = lambda i, c: (0, 0)
    col = lambda i, c: (0, c)
    return pl.pallas_call(
        functools.partial(_mixout_kernel, tiles_per_seq=seq // tm),
        grid=(rows // tm, d // tn),
        in_specs=[
            pl.BlockSpec((tm, d), row),
            pl.BlockSpec((1, d), const),
            pl.BlockSpec((1, d), const),
            pl.BlockSpec((tm, pw), row),
            pl.BlockSpec((N_META, pw), lambda i, c: (jnp.maximum(i * hb - 1, 0), 0)),
            pl.BlockSpec((N_META, pw), const),
            pl.BlockSpec((len(POOL_WINDOWS), POOL_GROUP, POOL_GROUP), lambda i, c: (0, 0, 0)),
            pl.BlockSpec((1, pw), const),
            pl.BlockSpec((tm, attn.shape[1]), row),
            pl.BlockSpec((d, tn), col),
            pl.BlockSpec((d, tn), col),
            pl.BlockSpec((pw, tn), col),
            pl.BlockSpec((attn.shape[1], tn), col),
            pl.BlockSpec((tn, d), lambda i, c: (c, 0)),
        ],
        out_specs=pl.BlockSpec((tm, d), row),
        out_shape=jax.ShapeDtypeStruct((rows, d), F32),
        scratch_shapes=[pltpu.VMEM((tm, d), BF16), pltpu.VMEM((tm, pw), BF16),
                        pltpu.VMEM((chunk + N_META, POOL_GROUP), F32)],
        compiler_params=_params(("parallel", "arbitrary")),
        name="mixout",
    )(h, g_mix, g_post, u, u, u_meta, pool_w, pool_scale, attn, w_gp, w_gm, w_po, w_mo, w_out)


def _rope_tables(n_pos):
    pos = jnp.arange(n_pos, dtype=F32)
    inv = ROPE_THETA ** (-jnp.arange(0, QK_ROPE, 2, dtype=F32) / QK_ROPE)
    ang = pos[:, None] * inv[None, :]
    ang = jnp.concatenate([ang, ang], axis=-1)
    pad = jnp.zeros((n_pos, LANES - QK_ROPE), F32)
    return (jnp.concatenate([jnp.cos(ang), pad], axis=-1),
            jnp.concatenate([jnp.sin(ang), pad], axis=-1))


def kernel(x, meta_tokens, norm_ffn1_pre, norm_ffn1_post, ffn1_w_gu, ffn1_w_down, norm_mix_pre, norm_mix_post, w_in, pool_w, pool_scale, w_pool_o, q_a_norm, w_q_b, kv_a_norm, w_kv_b, w_mla_o, w_out, norm_ffn2_pre, norm_ffn2_post, ffn2_w_gu, ffn2_w_down):
    bsz, seq, d = x.shape
    depth = w_in.shape[0]
    nh = MLA_HEADS
    tm, tf, tn, tq = 512, 512, 512, 512
    tm_ffn = 1024

    cos, sin = _rope_tables(N_META + seq)
    cos_m, sin_m, cos_r, sin_r = cos[:N_META], sin[:N_META], cos[N_META:], sin[N_META:]

    h = x.reshape(bsz * seq, d)
    hm = meta_tokens.astype(x.dtype)
    row = lambda v: v.reshape(1, -1)

    for i in range(depth):
        w_gu1, w_dn1 = ffn1_w_gu[i].astype(BF16), ffn1_w_down[i].astype(BF16)
        w_gu2, w_dn2 = ffn2_w_gu[i].astype(BF16), ffn2_w_down[i].astype(BF16)
        n_lat = POOL_WIDTH + Q_LORA + KV_LORA + QK_ROPE
        w_b = jnp.concatenate([w_in[i][:, :n_lat], jnp.zeros((d, LANES - QK_ROPE), F32)],
                              axis=1).astype(BF16)
        w_gp = w_in[i][:, n_lat:n_lat + d].astype(BF16)
        w_gm = w_in[i][:, n_lat + d:].astype(BF16)
        w_q = jnp.pad(w_q_b[i].reshape(Q_LORA, nh, QK_DIM),
                      ((0, 0), (0, 0), (0, HEAD_PAD - QK_DIM))).reshape(Q_LORA, nh * HEAD_PAD)
        w_q = w_q.astype(BF16)
        w_kv = w_kv_b[i].reshape(KV_LORA, nh, QK_NOPE + V_DIM)
        w_k = w_kv[:, :, :QK_NOPE].reshape(KV_LORA, nh * QK_NOPE).astype(BF16)
        w_vt = w_kv[:, :, QK_NOPE:].reshape(KV_LORA, nh * V_DIM).T.astype(BF16)
        w_po, w_mo, w_o = w_pool_o[i].astype(BF16), w_mla_o[i].astype(BF16), w_out[i].astype(BF16)
        p_w = pool_w[i].astype(BF16)

        h1 = _ffn(h, row(norm_ffn1_pre[i]), row(norm_ffn1_post[i]), w_gu1, w_dn1,
                  tm=tm_ffn, tf=tf)
        h1m = _ffn(hm, row(norm_ffn1_pre[i]), row(norm_ffn1_post[i]), w_gu1, w_dn1,
                   tm=N_META, tf=tf)

        u, cq, ckv, kpe = _inproj(h1, row(norm_mix_pre[i]), w_b, row(q_a_norm[i]),
                                  row(kv_a_norm[i]), cos_r, sin_r, tm=tm)
        um, cqm, ckvm, kpem = _inproj(h1m, row(norm_mix_pre[i]), w_b, row(q_a_norm[i]),
                                      row(kv_a_norm[i]), cos_m, sin_m, tm=N_META)
        q, kn, vt = _qkv(cq, ckv, w_q, w_k, w_vt, cos_r, sin_r, tm=tm, tk=tq, seq=seq)
        _, knm, vtm = _qkv(cqm, ckvm, w_q, w_k, w_vt, cos_m, sin_m,
                           tm=N_META, tk=N_META, seq=N_META)

        attn = _attn(q.reshape(bsz, seq, nh * HEAD_PAD), kn.reshape(bsz, seq, nh * QK_NOPE),
                     kpe.reshape(bsz, seq, LANES), vt, knm, kpem,
                     vtm.reshape(nh * V_DIM, N_META), hp=4)

        h2 = _mixout(h1, row(norm_mix_pre[i]), row(norm_mix_post[i]), u, um, p_w,
                     row(pool_scale[i]), attn.reshape(bsz * seq, nh * V_DIM),
                     w_gp, w_gm, w_po, w_mo, w_o, tm=tm, tn=tn, seq=seq)

        h = _ffn(h2, row(norm_ffn2_pre[i]), row(norm_ffn2_post[i]), w_gu2, w_dn2,
                 tm=tm_ffn, tf=tf)
        if i + 1 < depth:
            raise NotImplementedError("only DEPTH == 1 is supported")

    return h.reshape(bsz, seq, d)
```

```python
import functools
import math

import jax
import jax.numpy as jnp
from jax import lax
from jax.experimental import pallas as pl
from jax.experimental.pallas import tpu as pltpu

F32 = jnp.float32
BF16 = jnp.bfloat16

N_META = 16
POOL_WINDOWS = (2, 4, 8, 16)
POOL_GROUP = 256
POOL_WIDTH = POOL_GROUP * len(POOL_WINDOWS)
MLA_HEADS = 16
Q_LORA = 512
KV_LORA = 512
QK_NOPE = 128
QK_ROPE = 64
V_DIM = 128
QK_DIM = QK_NOPE + QK_ROPE
ROPE_THETA = 10000.0
EPS = 1e-6
LANES = 128
BF16_ROWS = 16
HEAD_PAD = 2 * LANES
Q_SCALE = (QK_DIM ** -0.5) * math.log2(math.e)
MASK_VALUE = -1e30

VMEM_LIMIT = 56 * 1024 * 1024
FFN_VMEM_LIMIT = 62 * 1024 * 1024
ROW_CHUNK = 256
SCORE_LEAD = 1


def _rms(x):
    return x * lax.rsqrt(jnp.mean(x * x, axis=-1, keepdims=True) + EPS)


def _rope(x, cos, sin):
    rot = pltpu.roll(x, 32, 1) - pltpu.roll(x, 96, 1)
    return x * cos + rot * sin


def _params(sem, vmem_limit=VMEM_LIMIT):
    return pltpu.CompilerParams(dimension_semantics=sem, vmem_limit_bytes=vmem_limit)


def _row_chunks(rows, chunk):
    chunk = min(chunk, rows)
    return [slice(r, r + chunk) for r in range(0, rows, chunk)]


def _ffn_kernel(x_ref, gpre_ref, gpost_ref, wg_ref, wu_ref, wd_ref, o_ref, xn_ref):
    f = pl.program_id(1)
    last = pl.num_programs(1) - 1
    chunks = _row_chunks(x_ref.shape[0], ROW_CHUNK)

    def mlp(xn):
        g = jnp.dot(xn, wg_ref[...], preferred_element_type=F32)
        u = jnp.dot(xn, wu_ref[...], preferred_element_type=F32)
        a = ((g * jax.nn.sigmoid(g)) * u).astype(BF16)
        return jnp.dot(a, wd_ref[...], preferred_element_type=F32)

    @pl.when(f == 0)
    def _():
        for rows in chunks:
            xn = (_rms(x_ref[rows, :]) * gpre_ref[...]).astype(BF16)
            xn_ref[rows, :] = xn
            o_ref[rows, :] = mlp(xn)

    @pl.when(jnp.logical_and(f > 0, f < last))
    def _():
        o_ref[...] += mlp(xn_ref[...])

    @pl.when(f == last)
    def _():
        for rows in chunks:
            y = o_ref[rows, :] + mlp(xn_ref[rows, :])
            o_ref[rows, :] = x_ref[rows, :] + 0.5 * (_rms(y) * gpost_ref[...])


def _ffn(x, g_pre, g_post, w_gu, w_down, *, tm, tf):
    rows, d = x.shape
    d_ff = w_down.shape[0]
    nf = d_ff // tf
    return pl.pallas_call(
        _ffn_kernel,
        grid=(rows // tm, nf),
        in_specs=[
            pl.BlockSpec((tm, d), lambda i, f: (i, 0)),
            pl.BlockSpec((1, d), lambda i, f: (0, 0)),
            pl.BlockSpec((1, d), lambda i, f: (0, 0)),
            pl.BlockSpec((d, tf), lambda i, f: (0, f)),
            pl.BlockSpec((d, tf), lambda i, f: (0, f + nf)),
            pl.BlockSpec((tf, d), lambda i, f: (f, 0)),
        ],
        out_specs=pl.BlockSpec((tm, d), lambda i, f: (i, 0)),
        out_shape=jax.ShapeDtypeStruct((rows, d), F32),
        scratch_shapes=[pltpu.VMEM((tm, d), BF16)],
        compiler_params=_params(("parallel", "arbitrary"), FFN_VMEM_LIMIT),
        name="ffn",
    )(x, g_pre, g_post, w_gu, w_gu, w_down)


def _inproj_kernel(h_ref, gmix_ref, w_ref, gq_ref, gkv_ref, cos_ref, sin_ref,
                   u_ref, cq_ref, ckv_ref, kpe_ref):
    c0, c1, c2, c3 = POOL_WIDTH, POOL_WIDTH + Q_LORA, POOL_WIDTH + Q_LORA + KV_LORA, w_ref.shape[1]
    for rows in _row_chunks(h_ref.shape[0], ROW_CHUNK):
        hn = (_rms(h_ref[rows, :]) * gmix_ref[...]).astype(BF16)
        u_ref[rows, :] = jnp.dot(hn, w_ref[:, :c0], preferred_element_type=F32)
        cq = jnp.dot(hn, w_ref[:, c0:c1], preferred_element_type=F32)
        cq_ref[rows, :] = (_rms(cq) * gq_ref[...]).astype(BF16)
        ckv = jnp.dot(hn, w_ref[:, c1:c2], preferred_element_type=F32)
        ckv_ref[rows, :] = (_rms(ckv) * gkv_ref[...]).astype(BF16)
        kr = jnp.dot(hn, w_ref[:, c2:c3], preferred_element_type=F32)
        kpe_ref[rows, :] = _rope(kr, cos_ref[rows, :], sin_ref[rows, :]).astype(BF16)


def _inproj(h, g_mix, w_b, g_q, g_kv, cos, sin, *, tm):
    rows, d = h.shape
    n_pos = cos.shape[0] // tm
    wcols = w_b.shape[1]
    row = lambda i: (i, 0)
    const = lambda i: (0, 0)
    pos = lambda i: (i % n_pos, 0)
    return pl.pallas_call(
        _inproj_kernel,
        grid=(rows // tm,),
        in_specs=[
            pl.BlockSpec((tm, d), row),
            pl.BlockSpec((1, d), const),
            pl.BlockSpec((d, wcols), const),
            pl.BlockSpec((1, Q_LORA), const),
            pl.BlockSpec((1, KV_LORA), const),
            pl.BlockSpec((tm, LANES), pos),
            pl.BlockSpec((tm, LANES), pos),
        ],
        out_specs=[
            pl.BlockSpec((tm, POOL_WIDTH), row),
            pl.BlockSpec((tm, Q_LORA), row),
            pl.BlockSpec((tm, KV_LORA), row),
            pl.BlockSpec((tm, LANES), row),
        ],
        out_shape=[
            jax.ShapeDtypeStruct((rows, POOL_WIDTH), F32),
            jax.ShapeDtypeStruct((rows, Q_LORA), BF16),
            jax.ShapeDtypeStruct((rows, KV_LORA), BF16),
            jax.ShapeDtypeStruct((rows, LANES), BF16),
        ],
        compiler_params=_params(("parallel",)),
        name="inproj",
    )(h, g_mix, w_b, g_q, g_kv, cos, sin)


def _qkv_kernel(cq_ref, ckv_ref, wqt_ref, wk_ref, wvt_ref, cos_ref, sin_ref,
                qt_ref, k_ref, vt_ref):
    cq = cq_ref[...]
    ckv = ckv_ref[...]
    nt = (((1,), (1,)), ((), ()))
    tk = vt_ref.shape[-1]
    chunks = [slice(c * tk, (c + 1) * tk) for c in range(vt_ref.shape[1])]
    cos = cos_ref[...] * Q_SCALE
    sin = sin_ref[...] * Q_SCALE
    half = QK_ROPE // 2
    qt = lax.dot_general(wqt_ref[...], cq, nt, preferred_element_type=F32)
    zeros = jnp.zeros((HEAD_PAD - QK_DIM, tk), BF16)
    for h in range(MLA_HEADS):
        lo = h * HEAD_PAD
        nope = qt[lo:lo + QK_NOPE] * Q_SCALE
        x1 = qt[lo + QK_NOPE:lo + QK_NOPE + half]
        x2 = qt[lo + QK_NOPE + half:lo + QK_DIM]
        r1 = x1 * cos - x2 * sin
        r2 = x2 * cos + x1 * sin
        for c, cols in enumerate(chunks):
            qt_ref[0, c, lo:lo + QK_NOPE, :] = nope[:, cols].astype(BF16)
            qt_ref[0, c, lo + QK_NOPE:lo + QK_NOPE + half, :] = r1[:, cols].astype(BF16)
            qt_ref[0, c, lo + QK_NOPE + half:lo + QK_DIM, :] = r2[:, cols].astype(BF16)
            qt_ref[0, c, lo + QK_DIM:lo + HEAD_PAD, :] = zeros
    k_ref[...] = jnp.dot(ckv, wk_ref[...], preferred_element_type=F32).astype(BF16)
    vt = lax.dot_general(wvt_ref[...], ckv, nt, preferred_element_type=F32)
    for c, cols in enumerate(chunks):
        vt_ref[0, c] = vt[:, cols].astype(BF16)


def _qkv(cq, ckv, w_qt, w_k, w_vt, cos_t, sin_t, *, tm, tk, seq):
    rows = cq.shape[0]
    n_pos = seq // tm
    nh = MLA_HEADS
    row = lambda i: (i, 0)
    const = lambda i: (0, 0)
    tiled = lambda i: (i // n_pos, i % n_pos, 0, 0)
    return pl.pallas_call(
        _qkv_kernel,
        grid=(rows // tm,),
        in_specs=[
            pl.BlockSpec((tm, Q_LORA), row),
            pl.BlockSpec((tm, KV_LORA), row),
            pl.BlockSpec((nh * HEAD_PAD, Q_LORA), const),
            pl.BlockSpec((KV_LORA, nh * QK_NOPE), const),
            pl.BlockSpec((nh * V_DIM, KV_LORA), const),
            pl.BlockSpec((QK_ROPE // 2, tm), lambda i: (0, i % n_pos)),
            pl.BlockSpec((QK_ROPE // 2, tm), lambda i: (0, i % n_pos)),
        ],
        out_specs=[
            pl.BlockSpec((1, tm // tk, nh * HEAD_PAD, tk), tiled),
            pl.BlockSpec((tm, nh * QK_NOPE), row),
            pl.BlockSpec((1, tm // tk, nh * V_DIM, tk), tiled),
        ],
        out_shape=[
            jax.ShapeDtypeStruct((rows // seq, seq // tk, nh * HEAD_PAD, tk), BF16),
            jax.ShapeDtypeStruct((rows, nh * QK_NOPE), BF16),
            jax.ShapeDtypeStruct((rows // seq, seq // tk, nh * V_DIM, tk), BF16),
        ],
        compiler_params=_params(("parallel",)),
        name="qkv",
    )(cq, ckv, w_qt, w_k, w_vt, cos_t, sin_t)


def _attn_kernel(qi_ref, kj_ref, new_ref, q_ref, kn_ref, kp_ref, vt_ref, knm_ref, kpm_ref,
                 vtm_ref, o_ref, acc_ref, sa_ref, sb_ref, st_ref):
    tk = vt_ref.shape[-1]
    tq = tk
    hp = acc_ref.shape[0]
    n_pairs = qi_ref.shape[0]
    bufs = ((sa_ref, 2), (sb_ref, 3))

    def q_of(h, qi):
        return q_ref[0, qi, h * HEAD_PAD:(h + 1) * HEAD_PAD, :]

    def init_tile(qi, heads):
        kpm = kpm_ref[...]
        for h in heads:
            km = jnp.concatenate([knm_ref[:, h * QK_NOPE:(h + 1) * QK_NOPE], kpm], axis=1)
            s = jnp.dot(km, q_of(h, qi), preferred_element_type=F32)
            m0 = jnp.max(s, axis=0, keepdims=True)
            p = jnp.exp2(s - m0)
            acc_ref[h] = jnp.dot(vtm_ref[h * V_DIM:(h + 1) * V_DIM, :], p.astype(BF16),
                                 preferred_element_type=F32)
            st_ref[h, 0] = m0
            st_ref[h, 1] = jnp.sum(p, axis=0, keepdims=True)

    def scores(t, dst, masked, heads):
        s_ref, row = dst
        qi = qi_ref[t]
        start = pl.multiple_of(kj_ref[t] * tk, tk)
        kp = kp_ref[0, pl.ds(start, tk), :]
        for h in heads:
            k = jnp.concatenate(
                [kn_ref[0, pl.ds(start, tk), h * QK_NOPE:(h + 1) * QK_NOPE], kp], axis=1)
            s = jnp.dot(k, q_of(h, qi), preferred_element_type=F32)
            if masked:
                kpos = lax.broadcasted_iota(jnp.int32, s.shape, 0)
                qpos = lax.broadcasted_iota(jnp.int32, s.shape, 1)
                s = jnp.where(kpos <= qpos, s, MASK_VALUE)
            s_ref[h] = s
            st_ref[h, row] = jnp.max(s, axis=0, keepdims=True)

    def absorb(t, src, heads):
        s_ref, row = src
        kj = kj_ref[t]
        ones = jnp.ones((BF16_ROWS, tk), BF16)
        for h in heads:
            m, l = st_ref[h, 0], st_ref[h, 1]
            m_new = jnp.maximum(m, st_ref[h, row])
            alpha = jnp.exp2(m - m_new)
            p = jnp.exp2(s_ref[h] - m_new)
            vt1 = jnp.concatenate([vt_ref[0, kj, h * V_DIM:(h + 1) * V_DIM, :], ones], axis=0)
            pv = jnp.dot(vt1, p.astype(BF16), preferred_element_type=F32)
            acc_ref[h] = alpha * acc_ref[h] + pv[:V_DIM]
            st_ref[h, 0] = m_new
            st_ref[h, 1] = alpha * l + pv[V_DIM:V_DIM + 1]

    def finalize(qi, heads):
        rows = pl.ds(pl.multiple_of(qi * tq, tq), tq)
        for h in heads:
            o_ref[0, rows, h * V_DIM:(h + 1) * V_DIM] = (
                acc_ref[h] / st_ref[h, 1]).T.astype(BF16)

    def run(stages):
        items = [(st, h) for st in stages for h in range(hp)]

        def put_scores(item):
            (t, _, dst, is_fresh), h = item
            scores(t, dst, is_fresh, [h])

        def put_rest(item):
            (t, src, _, is_fresh), h = item
            absorb(t - 1, src, [h])
            if is_fresh:
                finalize(qi_ref[t - 1], [h])
                init_tile(qi_ref[t], [h])

        for item in items[:SCORE_LEAD + 1]:
            put_scores(item)
        for n, item in enumerate(items):
            put_rest(item)
            if n + SCORE_LEAD + 1 < len(items):
                put_scores(items[n + SCORE_LEAD + 1])

    heads = list(range(hp))
    init_tile(qi_ref[0], heads)
    scores(0, bufs[0], True, heads)

    def body(r, carry):
        t = 2 * r + 1
        a, b = bufs
        for fresh0, fresh1 in ((True, False), (False, True), (False, False)):
            cond = jnp.logical_and(new_ref[t] == int(fresh0), new_ref[t + 1] == int(fresh1))
            pl.when(cond)(functools.partial(
                run, [(t, a, b, fresh0), (t + 1, b, a, fresh1)]))
        return carry

    lax.fori_loop(0, (n_pairs - 1) // 2, body, 0)
    last = bufs[0]
    if (n_pairs - 1) % 2:
        t = n_pairs - 1
        for is_fresh in (True, False):
            pl.when(new_ref[t] == int(is_fresh))(functools.partial(
                run, [(t, bufs[0], bufs[1], is_fresh)]))
        last = bufs[1]
    absorb(n_pairs - 1, last, heads)
    finalize(qi_ref[n_pairs - 1], heads)


def _attn(qt, k_nope, k_pe, vt, knm, kpm, vtm, *, hp):
    b, s, _ = k_nope.shape
    tk = vt.shape[-1]
    nq = s // tk
    nh = MLA_HEADS
    qi, kj, new = [], [], []
    for i in range(nq):
        for n, j in enumerate([i] + list(range(i))):
            qi.append(i)
            kj.append(j)
            new.append(int(n == 0))
    assert not any(new[t] and new[t + 1] for t in range(1, len(new) - 1, 2))
    tables = [jnp.asarray(v, jnp.int32) for v in (qi, kj, new)]
    grid_spec = pltpu.PrefetchScalarGridSpec(
        num_scalar_prefetch=len(tables),
        grid=(b, nh // hp),
        in_specs=[
            pl.BlockSpec((1, nq, hp * HEAD_PAD, tk), lambda bi, h, *_: (bi, 0, h, 0)),
            pl.BlockSpec((1, s, hp * QK_NOPE), lambda bi, h, *_: (bi, 0, h)),
            pl.BlockSpec((1, s, LANES), lambda bi, h, *_: (bi, 0, 0)),
            pl.BlockSpec((1, nq, hp * V_DIM, tk), lambda bi, h, *_: (bi, 0, h, 0)),
            pl.BlockSpec((N_META, hp * QK_NOPE), lambda bi, h, *_: (0, h)),
            pl.BlockSpec((N_META, LANES), lambda bi, h, *_: (0, 0)),
            pl.BlockSpec((hp * V_DIM, N_META), lambda bi, h, *_: (h, 0)),
        ],
        out_specs=pl.BlockSpec((1, s, hp * V_DIM), lambda bi, h, *_: (bi, 0, h)),
        scratch_shapes=[pltpu.VMEM((hp, V_DIM, tk), F32), pltpu.VMEM((hp, tk, tk), F32),
                        pltpu.VMEM((hp, tk, tk), F32), pltpu.VMEM((hp, 4, 1, tk), F32)],
    )
    return pl.pallas_call(
        _attn_kernel,
        grid_spec=grid_spec,
        out_shape=jax.ShapeDtypeStruct((b, s, nh * V_DIM), BF16),
        compiler_params=_params(("parallel", "parallel")),
        name="attn",
    )(*tables, qt, k_nope, k_pe, vt, knm, kpm, vtm)


def _mixout_kernel(h_ref, gmix_ref, gpost_ref, u_ref, uprev_ref, umeta_ref, pw_ref, pscale_ref,
                   attn_ref, wgp_ref, wgm_ref, wpo_ref, wmo_ref, wout_ref, o_ref,
                   hn_ref, pool_ref, buf_ref, *, tiles_per_seq):
    c = pl.program_id(1)
    last = pl.num_programs(1) - 1
    chunks = _row_chunks(h_ref.shape[0], ROW_CHUNK)

    def pool(rows, halo):
        for g, win in enumerate(POOL_WINDOWS):
            cols = slice(g * POOL_GROUP, (g + 1) * POOL_GROUP)
            buf_ref[:N_META, :] = halo[:, cols]
            buf_ref[N_META:, :] = u_ref[rows, cols]
            tot = buf_ref[...]
            shift = 1
            while shift < win:
                tot = tot + pltpu.roll(tot, shift, 0)
                shift *= 2
            dlt = (tot[N_META:, :] * (1.0 / win) - buf_ref[N_META:, :]).astype(BF16)
            y = jnp.dot(dlt, pw_ref[g], preferred_element_type=F32)
            pool_ref[rows, cols] = (y * pscale_ref[:, cols]).astype(BF16)

    def mix(hn, rows):
        g_pool = jnp.dot(hn, wgp_ref[...], preferred_element_type=F32)
        g_mla = jnp.dot(hn, wgm_ref[...], preferred_element_type=F32)
        y_pool = jnp.dot(pool_ref[rows, :], wpo_ref[...], preferred_element_type=F32)
        y_mla = jnp.dot(attn_ref[rows, :], wmo_ref[...], preferred_element_type=F32)
        y = (jax.nn.sigmoid(g_pool) * y_pool + jax.nn.sigmoid(g_mla) * y_mla).astype(BF16)
        return jnp.dot(y, wout_ref[...], preferred_element_type=F32)

    @pl.when(c == 0)
    def _():
        seq_start = pl.program_id(0) % tiles_per_seq == 0
        for n, rows in enumerate(chunks):
            if n == 0:
                halo = jnp.where(seq_start, umeta_ref[...], uprev_ref[...])
            else:
                halo = u_ref[rows.start - N_META:rows.start, :]
            pool(rows, halo)
            hn = (_rms(h_ref[rows, :]) * gmix_ref[...]).astype(BF16)
            hn_ref[rows, :] = hn
            o_ref[rows, :] = mix(hn, rows)

    @pl.when(jnp.logical_and(c > 0, c < last))
    def _():
        o_ref[...] += mix(hn_ref[...], slice(None))

    @pl.when(c == last)
    def _():
        for rows in chunks:
            y = o_ref[rows, :] + mix(hn_ref[rows, :], rows)
            o_ref[rows, :] = h_ref[rows, :] + _rms(y) * gpost_ref[...]


def _mixout(h, g_mix, g_post, u, u_meta, pool_w, pool_scale, attn,
            w_gp, w_gm, w_po, w_mo, w_out, *, tm, tn, seq):
    rows, d = h.shape
    pw = u.shape[1]
    hb = tm // N_META
    chunk = min(ROW_CHUNK, tm)
    row = lambda i, c: (i, 0)
    const = lambda i, c: (0, 0)
    col = lambda i, c: (0, c)
    return pl.pallas_call(
        functools.partial(_mixout_kernel, tiles_per_seq=seq // tm),
        grid=(rows // tm, d // tn),
        in_specs=[
            pl.BlockSpec((tm, d), row),
            pl.BlockSpec((1, d), const),
            pl.BlockSpec((1, d), const),
            pl.BlockSpec((tm, pw), row),
            pl.BlockSpec((N_META, pw), lambda i, c: (jnp.maximum(i * hb - 1, 0), 0)),
            pl.BlockSpec((N_META, pw), const),
            pl.BlockSpec((len(POOL_WINDOWS), POOL_GROUP, POOL_GROUP), lambda i, c: (0, 0, 0)),
            pl.BlockSpec((1, pw), const),
            pl.BlockSpec((tm, attn.shape[1]), row),
            pl.BlockSpec((d, tn), col),
            pl.BlockSpec((d, tn), col),
            pl.BlockSpec((pw, tn), col),
            pl.BlockSpec((attn.shape[1], tn), col),
            pl.BlockSpec((tn, d), lambda i, c: (c, 0)),
        ],
        out_specs=pl.BlockSpec((tm, d), row),
        out_shape=jax.ShapeDtypeStruct((rows, d), F32),
        scratch_shapes=[pltpu.VMEM((tm, d), BF16), pltpu.VMEM((tm, pw), BF16),
                        pltpu.VMEM((chunk + N_META, POOL_GROUP), F32)],
        compiler_params=_params(("parallel", "arbitrary")),
        name="mixout",
    )(h, g_mix, g_post, u, u, u_meta, pool_w, pool_scale, attn, w_gp, w_gm, w_po, w_mo, w_out)


def _rope_tables(n_pos):
    pos = jnp.arange(n_pos, dtype=F32)
    inv = ROPE_THETA ** (-jnp.arange(0, QK_ROPE, 2, dtype=F32) / QK_ROPE)
    ang = pos[:, None] * inv[None, :]
    ang_t = inv[:, None] * pos[None, :]
    ang2 = jnp.concatenate([ang, ang], axis=-1)
    pad = jnp.zeros((n_pos, LANES - QK_ROPE), F32)
    return (jnp.concatenate([jnp.cos(ang2), pad], axis=-1),
            jnp.concatenate([jnp.sin(ang2), pad], axis=-1), jnp.cos(ang_t), jnp.sin(ang_t))


def kernel(x, meta_tokens, norm_ffn1_pre, norm_ffn1_post, ffn1_w_gu, ffn1_w_down, norm_mix_pre, norm_mix_post, w_in, pool_w, pool_scale, w_pool_o, q_a_norm, w_q_b, kv_a_norm, w_kv_b, w_mla_o, w_out, norm_ffn2_pre, norm_ffn2_post, ffn2_w_gu, ffn2_w_down):
    bsz, seq, d = x.shape
    depth = w_in.shape[0]
    nh = MLA_HEADS
    tm, tf, tn, tq = 512, 512, 512, 512
    tm_ffn = 1024

    cos, sin, cos_t, sin_t = _rope_tables(N_META + seq)
    cos_m, sin_m, cos_r, sin_r = cos[:N_META], sin[:N_META], cos[N_META:], sin[N_META:]
    cos_tm, sin_tm, cos_tr, sin_tr = (cos_t[:, :N_META], sin_t[:, :N_META],
                                      cos_t[:, N_META:], sin_t[:, N_META:])

    h = x.reshape(bsz * seq, d)
    hm = meta_tokens.astype(x.dtype)
    row = lambda v: v.reshape(1, -1)

    for i in range(depth):
        w_gu1, w_dn1 = ffn1_w_gu[i].astype(BF16), ffn1_w_down[i].astype(BF16)
        w_gu2, w_dn2 = ffn2_w_gu[i].astype(BF16), ffn2_w_down[i].astype(BF16)
        n_lat = POOL_WIDTH + Q_LORA + KV_LORA + QK_ROPE
        w_b = jnp.concatenate([w_in[i][:, :n_lat], jnp.zeros((d, LANES - QK_ROPE), F32)],
                              axis=1).astype(BF16)
        w_gp = w_in[i][:, n_lat:n_lat + d].astype(BF16)
        w_gm = w_in[i][:, n_lat + d:].astype(BF16)
        w_q = jnp.pad(w_q_b[i].reshape(Q_LORA, nh, QK_DIM),
                      ((0, 0), (0, 0), (0, HEAD_PAD - QK_DIM))).reshape(Q_LORA, nh * HEAD_PAD)
        w_qt = w_q.T.astype(BF16)
        w_kv = w_kv_b[i].reshape(KV_LORA, nh, QK_NOPE + V_DIM)
        w_k = w_kv[:, :, :QK_NOPE].reshape(KV_LORA, nh * QK_NOPE).astype(BF16)
        w_vt = w_kv[:, :, QK_NOPE:].reshape(KV_LORA, nh * V_DIM).T.astype(BF16)
        w_po, w_mo, w_o = w_pool_o[i].astype(BF16), w_mla_o[i].astype(BF16), w_out[i].astype(BF16)
        p_w = pool_w[i].astype(BF16)

        h1 = _ffn(h, row(norm_ffn1_pre[i]), row(norm_ffn1_post[i]), w_gu1, w_dn1,
                  tm=tm_ffn, tf=tf)
        h1m = _ffn(hm, row(norm_ffn1_pre[i]), row(norm_ffn1_post[i]), w_gu1, w_dn1,
                   tm=N_META, tf=tf)

        u, cq, ckv, kpe = _inproj(h1, row(norm_mix_pre[i]), w_b, row(q_a_norm[i]),
                                  row(kv_a_norm[i]), cos_r, sin_r, tm=tm)
        um, cqm, ckvm, kpem = _inproj(h1m, row(norm_mix_pre[i]), w_b, row(q_a_norm[i]),
                                      row(kv_a_norm[i]), cos_m, sin_m, tm=N_META)
        qt, kn, vt = _qkv(cq, ckv, w_qt, w_k, w_vt, cos_tr, sin_tr, tm=tm, tk=tq, seq=seq)
        _, knm, vtm = _qkv(cqm, ckvm, w_qt, w_k, w_vt, cos_tm, sin_tm,
                           tm=N_META, tk=N_META, seq=N_META)

        attn = _attn(qt, kn.reshape(bsz, seq, nh * QK_NOPE),
                     kpe.reshape(bsz, seq, LANES), vt, knm, kpem,
                     vtm.reshape(nh * V_DIM, N_META), hp=4)

        h2 = _mixout(h1, row(norm_mix_pre[i]), row(norm_mix_post[i]), u, um, p_w,
                     row(pool_scale[i]), attn.reshape(bsz * seq, nh * V_DIM),
                     w_gp, w_gm, w_po, w_mo, w_o, tm=tm, tn=tn, seq=seq)

        h = _ffn(h2, row(norm_ffn2_pre[i]), row(norm_ffn2_post[i]), w_gu2, w_dn2,
                 tm=tm_ffn, tf=tf)
        if i + 1 < depth:
            raise NotImplementedError("only DEPTH == 1 is supported")

    return h.reshape(bsz, seq, d)
```

```python
import functools
import math

import jax
import jax.numpy as jnp
from jax import lax
from jax.experimental import pallas as pl
from jax.experimental.pallas import tpu as pltpu

F32 = jnp.float32
BF16 = jnp.bfloat16

N_META = 16
POOL_WINDOWS = (2, 4, 8, 16)
POOL_GROUP = 256
POOL_WIDTH = POOL_GROUP * len(POOL_WINDOWS)
MLA_HEADS = 16
Q_LORA = 512
KV_LORA = 512
QK_NOPE = 128
QK_ROPE = 64
V_DIM = 128
QK_DIM = QK_NOPE + QK_ROPE
ROPE_THETA = 10000.0
EPS = 1e-6
LANES = 128
BF16_ROWS = 16
HEAD_PAD = 2 * LANES
Q_SCALE = (QK_DIM ** -0.5) * math.log2(math.e)
MASK_VALUE = -1e30

VMEM_LIMIT = 56 * 1024 * 1024
FFN_VMEM_LIMIT = 62 * 1024 * 1024
ROW_CHUNK = 256
SCORE_LEAD = 1


def _rms(x):
    return x * lax.rsqrt(jnp.mean(x * x, axis=-1, keepdims=True) + EPS)


def _rope(x, cos, sin):
    rot = pltpu.roll(x, 32, 1) - pltpu.roll(x, 96, 1)
    return x * cos + rot * sin


def _params(sem, vmem_limit=VMEM_LIMIT):
    return pltpu.CompilerParams(dimension_semantics=sem, vmem_limit_bytes=vmem_limit)


def _row_chunks(rows, chunk):
    chunk = min(chunk, rows)
    return [slice(r, r + chunk) for r in range(0, rows, chunk)]


def _ffn_kernel(*refs, n_casts):
    x_ref, gpre_ref, gpost_ref, wg_ref, wu_ref, wd_ref = refs[:6]
    cast_in = refs[6:6 + n_casts]
    o_ref = refs[6 + n_casts]
    cast_out = refs[7 + n_casts:7 + 2 * n_casts]
    xn_ref = refs[7 + 2 * n_casts]
    f = pl.program_id(1)
    last = pl.num_programs(1) - 1
    chunks = _row_chunks(x_ref.shape[0], ROW_CHUNK)

    def mlp(xn):
        g = jnp.dot(xn, wg_ref[...], preferred_element_type=F32)
        u = jnp.dot(xn, wu_ref[...], preferred_element_type=F32)
        a = ((g * jax.nn.sigmoid(g)) * u).astype(BF16)
        return jnp.dot(a, wd_ref[...], preferred_element_type=F32)

    def casts():
        for src, dst in zip(cast_in, cast_out):
            dst[...] = src[...].astype(BF16)

    @pl.when(f == 0)
    def _():
        for n, rows in enumerate(chunks):
            xn = (_rms(x_ref[rows, :]) * gpre_ref[...]).astype(BF16)
            xn_ref[rows, :] = xn
            o_ref[rows, :] = mlp(xn)
            if n == 0:
                casts()

    @pl.when(jnp.logical_and(f > 0, f < last))
    def _():
        o_ref[...] += mlp(xn_ref[...])
        casts()

    @pl.when(f == last)
    def _():
        for n, rows in enumerate(chunks):
            y = o_ref[rows, :] + mlp(xn_ref[rows, :])
            o_ref[rows, :] = x_ref[rows, :] + 0.5 * (_rms(y) * gpost_ref[...])
            if n == 0:
                casts()


def _ffn(x, g_pre, g_post, w_gu, w_down, *, tm, tf, casts=()):
    rows, d = x.shape
    d_ff = w_down.shape[0]
    nf = d_ff // tf
    steps = (rows // tm) * nf
    step_block = lambda i, f: (i * nf + f, 0)
    cast_specs = [pl.BlockSpec((c.shape[0] // steps, c.shape[1]), step_block) for c in casts]
    assert all(c.shape[0] % (steps * BF16_ROWS) == 0 for c in casts)
    return pl.pallas_call(
        functools.partial(_ffn_kernel, n_casts=len(casts)),
        grid=(rows // tm, nf),
        in_specs=[
            pl.BlockSpec((tm, d), lambda i, f: (i, 0)),
            pl.BlockSpec((1, d), lambda i, f: (0, 0)),
            pl.BlockSpec((1, d), lambda i, f: (0, 0)),
            pl.BlockSpec((d, tf), lambda i, f: (0, f)),
            pl.BlockSpec((d, tf), lambda i, f: (0, f + nf)),
            pl.BlockSpec((tf, d), lambda i, f: (f, 0)),
        ] + cast_specs,
        out_specs=[pl.BlockSpec((tm, d), lambda i, f: (i, 0))] + cast_specs,
        out_shape=[jax.ShapeDtypeStruct((rows, d), F32)]
        + [jax.ShapeDtypeStruct(c.shape, BF16) for c in casts],
        scratch_shapes=[pltpu.VMEM((tm, d), BF16)],
        compiler_params=_params(("parallel", "arbitrary"), FFN_VMEM_LIMIT),
        name="ffn",
    )(x, g_pre, g_post, w_gu, w_gu, w_down, *casts)


def _inproj_kernel(*refs, n_casts):
    h_ref, gmix_ref, w_ref, gq_ref, gkv_ref, cos_ref, sin_ref = refs[:7]
    cast_in = refs[7:7 + n_casts]
    u_ref, cq_ref, ckv_ref, kpe_ref = refs[7 + n_casts:11 + n_casts]
    cast_out = refs[11 + n_casts:]
    c0, c1, c2, c3 = POOL_WIDTH, POOL_WIDTH + Q_LORA, POOL_WIDTH + Q_LORA + KV_LORA, w_ref.shape[1]
    for n, rows in enumerate(_row_chunks(h_ref.shape[0], ROW_CHUNK)):
        hn = (_rms(h_ref[rows, :]) * gmix_ref[...]).astype(BF16)
        u_ref[rows, :] = jnp.dot(hn, w_ref[:, :c0], preferred_element_type=F32)
        if n == 0:
            for src, dst in zip(cast_in, cast_out):
                dst[...] = src[...].astype(BF16)
        cq = jnp.dot(hn, w_ref[:, c0:c1], preferred_element_type=F32)
        cq_ref[rows, :] = (_rms(cq) * gq_ref[...]).astype(BF16)
        ckv = jnp.dot(hn, w_ref[:, c1:c2], preferred_element_type=F32)
        ckv_ref[rows, :] = (_rms(ckv) * gkv_ref[...]).astype(BF16)
        kr = jnp.dot(hn, w_ref[:, c2:c3], preferred_element_type=F32)
        kpe_ref[rows, :] = _rope(kr, cos_ref[rows, :], sin_ref[rows, :]).astype(BF16)


def _inproj(h, g_mix, w_b, g_q, g_kv, cos, sin, *, tm, casts=()):
    rows, d = h.shape
    n_pos = cos.shape[0] // tm
    wcols = w_b.shape[1]
    steps = rows // tm
    row = lambda i: (i, 0)
    const = lambda i: (0, 0)
    pos = lambda i: (i % n_pos, 0)
    cast_specs = [pl.BlockSpec((c.shape[0] // steps, c.shape[1]), row) for c in casts]
    assert all(c.shape[0] % (steps * BF16_ROWS) == 0 for c in casts)
    return pl.pallas_call(
        functools.partial(_inproj_kernel, n_casts=len(casts)),
        grid=(steps,),
        in_specs=[
            pl.BlockSpec((tm, d), row),
            pl.BlockSpec((1, d), const),
            pl.BlockSpec((d, wcols), const),
            pl.BlockSpec((1, Q_LORA), const),
            pl.BlockSpec((1, KV_LORA), const),
            pl.BlockSpec((tm, LANES), pos),
            pl.BlockSpec((tm, LANES), pos),
        ] + cast_specs,
        out_specs=[
            pl.BlockSpec((tm, POOL_WIDTH), row),
            pl.BlockSpec((tm, Q_LORA), row),
            pl.BlockSpec((tm, KV_LORA), row),
            pl.BlockSpec((tm, LANES), row),
        ] + cast_specs,
        out_shape=[
            jax.ShapeDtypeStruct((rows, POOL_WIDTH), F32),
            jax.ShapeDtypeStruct((rows, Q_LORA), BF16),
            jax.ShapeDtypeStruct((rows, KV_LORA), BF16),
            jax.ShapeDtypeStruct((rows, LANES), BF16),
        ] + [jax.ShapeDtypeStruct(c.shape, BF16) for c in casts],
        compiler_params=_params(("parallel",)),
        name="inproj",
    )(h, g_mix, w_b, g_q, g_kv, cos, sin, *casts)


def _qkv_kernel(cq_ref, ckv_ref, wqt_ref, wk_ref, wvt_ref, cos_ref, sin_ref,
                qt_ref, k_ref, vt_ref):
    cq = cq_ref[...]
    ckv = ckv_ref[...]
    nt = (((1,), (1,)), ((), ()))
    tk = vt_ref.shape[-1]
    chunks = [slice(c * tk, (c + 1) * tk) for c in range(vt_ref.shape[1])]
    cos = cos_ref[...] * Q_SCALE
    sin = sin_ref[...] * Q_SCALE
    half = QK_ROPE // 2
    qt = lax.dot_general(wqt_ref[...], cq, nt, preferred_element_type=F32)
    zeros = jnp.zeros((HEAD_PAD - QK_DIM, tk), BF16)
    for h in range(MLA_HEADS):
        lo = h * HEAD_PAD
        nope = qt[lo:lo + QK_NOPE] * Q_SCALE
        x1 = qt[lo + QK_NOPE:lo + QK_NOPE + half]
        x2 = qt[lo + QK_NOPE + half:lo + QK_DIM]
        r1 = x1 * cos - x2 * sin
        r2 = x2 * cos + x1 * sin
        for c, cols in enumerate(chunks):
            qt_ref[0, c, lo:lo + QK_NOPE, :] = nope[:, cols].astype(BF16)
            qt_ref[0, c, lo + QK_NOPE:lo + QK_NOPE + half, :] = r1[:, cols].astype(BF16)
            qt_ref[0, c, lo + QK_NOPE + half:lo + QK_DIM, :] = r2[:, cols].astype(BF16)
            qt_ref[0, c, lo + QK_DIM:lo + HEAD_PAD, :] = zeros
    k_ref[...] = jnp.dot(ckv, wk_ref[...], preferred_element_type=F32).astype(BF16)
    vt = lax.dot_general(wvt_ref[...], ckv, nt, preferred_element_type=F32)
    for c, cols in enumerate(chunks):
        vt_ref[0, c] = vt[:, cols].astype(BF16)


def _qkv(cq, ckv, w_qt, w_k, w_vt, cos_t, sin_t, *, tm, tk, seq):
    rows = cq.shape[0]
    n_pos = seq // tm
    nh = MLA_HEADS
    row = lambda i: (i, 0)
    const = lambda i: (0, 0)
    tiled = lambda i: (i // n_pos, i % n_pos, 0, 0)
    return pl.pallas_call(
        _qkv_kernel,
        grid=(rows // tm,),
        in_specs=[
            pl.BlockSpec((tm, Q_LORA), row),
            pl.BlockSpec((tm, KV_LORA), row),
            pl.BlockSpec((nh * HEAD_PAD, Q_LORA), const),
            pl.BlockSpec((KV_LORA, nh * QK_NOPE), const),
            pl.BlockSpec((nh * V_DIM, KV_LORA), const),
            pl.BlockSpec((QK_ROPE // 2, tm), lambda i: (0, i % n_pos)),
            pl.BlockSpec((QK_ROPE // 2, tm), lambda i: (0, i % n_pos)),
        ],
        out_specs=[
            pl.BlockSpec((1, tm // tk, nh * HEAD_PAD, tk), tiled),
            pl.BlockSpec((tm, nh * QK_NOPE), row),
            pl.BlockSpec((1, tm // tk, nh * V_DIM, tk), tiled),
        ],
        out_shape=[
            jax.ShapeDtypeStruct((rows // seq, seq // tk, nh * HEAD_PAD, tk), BF16),
            jax.ShapeDtypeStruct((rows, nh * QK_NOPE), BF16),
            jax.ShapeDtypeStruct((rows // seq, seq // tk, nh * V_DIM, tk), BF16),
        ],
        compiler_params=_params(("parallel",)),
        name="qkv",
    )(cq, ckv, w_qt, w_k, w_vt, cos_t, sin_t)


def _attn_kernel(qi_ref, kj_ref, new_ref, q_ref, kn_ref, kp_ref, vt_ref, knm_ref, kpm_ref,
                 vtm_ref, o_ref, acc_ref, sa_ref, sb_ref, st_ref):
    tk = vt_ref.shape[-1]
    tq = tk
    hp = acc_ref.shape[0]
    n_pairs = qi_ref.shape[0]
    bufs = ((sa_ref, 2), (sb_ref, 3))

    def q_of(h, qi):
        return q_ref[0, qi, h * HEAD_PAD:(h + 1) * HEAD_PAD, :]

    def init_tile(qi, heads):
        kpm = kpm_ref[...]
        for h in heads:
            km = jnp.concatenate([knm_ref[:, h * QK_NOPE:(h + 1) * QK_NOPE], kpm], axis=1)
            s = jnp.dot(km, q_of(h, qi), preferred_element_type=F32)
            m0 = jnp.max(s, axis=0, keepdims=True)
            p = jnp.exp2(s - m0)
            acc_ref[h] = jnp.dot(vtm_ref[h * V_DIM:(h + 1) * V_DIM, :], p.astype(BF16),
                                 preferred_element_type=F32)
            st_ref[h, 0] = m0
            st_ref[h, 1] = jnp.sum(p, axis=0, keepdims=True)

    def scores(t, dst, masked, heads):
        s_ref, row = dst
        qi = qi_ref[t]
        start = pl.multiple_of(kj_ref[t] * tk, tk)
        kp = kp_ref[0, pl.ds(start, tk), :]
        for h in heads:
            k = jnp.concatenate(
                [kn_ref[0, pl.ds(start, tk), h * QK_NOPE:(h + 1) * QK_NOPE], kp], axis=1)
            s = jnp.dot(k, q_of(h, qi), preferred_element_type=F32)
            if masked:
                kpos = lax.broadcasted_iota(jnp.int32, s.shape, 0)
                qpos = lax.broadcasted_iota(jnp.int32, s.shape, 1)
                s = jnp.where(kpos <= qpos, s, MASK_VALUE)
            s_ref[h] = s
            st_ref[h, row] = jnp.max(s, axis=0, keepdims=True)

    def absorb(t, src, heads):
        s_ref, row = src
        kj = kj_ref[t]
        ones = jnp.ones((BF16_ROWS, tk), BF16)
        for h in heads:
            m, l = st_ref[h, 0], st_ref[h, 1]
            m_new = jnp.maximum(m, st_ref[h, row])
            alpha = jnp.exp2(m - m_new)
            p = jnp.exp2(s_ref[h] - m_new)
            vt1 = jnp.concatenate([vt_ref[0, kj, h * V_DIM:(h + 1) * V_DIM, :], ones], axis=0)
            pv = jnp.dot(vt1, p.astype(BF16), preferred_element_type=F32)
            acc_ref[h] = alpha * acc_ref[h] + pv[:V_DIM]
            st_ref[h, 0] = m_new
            st_ref[h, 1] = alpha * l + pv[V_DIM:V_DIM + 1]

    def finalize(qi, heads):
        rows = pl.ds(pl.multiple_of(qi * tq, tq), tq)
        for h in heads:
            o_ref[0, rows, h * V_DIM:(h + 1) * V_DIM] = (
                acc_ref[h] / st_ref[h, 1]).T.astype(BF16)

    def run(stages):
        items = [(st, h) for st in stages for h in range(hp)]

        def put_scores(item):
            (t, _, dst, is_fresh), h = item
            scores(t, dst, is_fresh, [h])

        def put_rest(item):
            (t, src, _, is_fresh), h = item
            absorb(t - 1, src, [h])
            if is_fresh:
                finalize(qi_ref[t - 1], [h])
                init_tile(qi_ref[t], [h])

        for item in items[:SCORE_LEAD + 1]:
            put_scores(item)
        for n, item in enumerate(items):
            put_rest(item)
            if n + SCORE_LEAD + 1 < len(items):
                put_scores(items[n + SCORE_LEAD + 1])

    heads = list(range(hp))
    init_tile(qi_ref[0], heads)
    scores(0, bufs[0], True, heads)

    def body(r, carry):
        t = 2 * r + 1
        a, b = bufs
        for fresh0, fresh1 in ((True, False), (False, True), (False, False)):
            cond = jnp.logical_and(new_ref[t] == int(fresh0), new_ref[t + 1] == int(fresh1))
            pl.when(cond)(functools.partial(
                run, [(t, a, b, fresh0), (t + 1, b, a, fresh1)]))
        return carry

    lax.fori_loop(0, (n_pairs - 1) // 2, body, 0)
    last = bufs[0]
    if (n_pairs - 1) % 2:
        t = n_pairs - 1
        for is_fresh in (True, False):
            pl.when(new_ref[t] == int(is_fresh))(functools.partial(
                run, [(t, bufs[0], bufs[1], is_fresh)]))
        last = bufs[1]
    absorb(n_pairs - 1, last, heads)
    finalize(qi_ref[n_pairs - 1], heads)


def _attn(qt, k_nope, k_pe, vt, knm, kpm, vtm, *, hp):
    b, s, _ = k_nope.shape
    tk = vt.shape[-1]
    nq = s // tk
    nh = MLA_HEADS
    qi, kj, new = [], [], []
    for i in range(nq):
        for n, j in enumerate([i] + list(range(i))):
            qi.append(i)
            kj.append(j)
            new.append(int(n == 0))
    assert not any(new[t] and new[t + 1] for t in range(1, len(new) - 1, 2))
    tables = [jnp.asarray(v, jnp.int32) for v in (qi, kj, new)]
    grid_spec = pltpu.PrefetchScalarGridSpec(
        num_scalar_prefetch=len(tables),
        grid=(b, nh // hp),
        in_specs=[
            pl.BlockSpec((1, nq, hp * HEAD_PAD, tk), lambda bi, h, *_: (bi, 0, h, 0)),
            pl.BlockSpec((1, s, hp * QK_NOPE), lambda bi, h, *_: (bi, 0, h)),
            pl.BlockSpec((1, s, LANES), lambda bi, h, *_: (bi, 0, 0)),
            pl.BlockSpec((1, nq, hp * V_DIM, tk), lambda bi, h, *_: (bi, 0, h, 0)),
            pl.BlockSpec((N_META, hp * QK_NOPE), lambda bi, h, *_: (0, h)),
            pl.BlockSpec((N_META, LANES), lambda bi, h, *_: (0, 0)),
            pl.BlockSpec((hp * V_DIM, N_META), lambda bi, h, *_: (h, 0)),
        ],
        out_specs=pl.BlockSpec((1, s, hp * V_DIM), lambda bi, h, *_: (bi, 0, h)),
        scratch_shapes=[pltpu.VMEM((hp, V_DIM, tk), F32), pltpu.VMEM((hp, tk, tk), F32),
                        pltpu.VMEM((hp, tk, tk), F32), pltpu.VMEM((hp, 4, 1, tk), F32)],
    )
    return pl.pallas_call(
        _attn_kernel,
        grid_spec=grid_spec,
        out_shape=jax.ShapeDtypeStruct((b, s, nh * V_DIM), BF16),
        compiler_params=_params(("parallel", "parallel")),
        name="attn",
    )(*tables, qt, k_nope, k_pe, vt, knm, kpm, vtm)


def _mixout_kernel(h_ref, gmix_ref, gpost_ref, u_ref, uprev_ref, umeta_ref, pw_ref, pscale_ref,
                   attn_ref, wgp_ref, wgm_ref, wpo_ref, wmo_ref, wout_ref, o_ref,
                   hn_ref, pool_ref, buf_ref, *, tiles_per_seq):
    c = pl.program_id(1)
    last = pl.num_programs(1) - 1
    chunks = _row_chunks(h_ref.shape[0], ROW_CHUNK)

    def pool(rows, halo):
        for g, win in enumerate(POOL_WINDOWS):
            cols = slice(g * POOL_GROUP, (g + 1) * POOL_GROUP)
            buf_ref[:N_META, :] = halo[:, cols]
            buf_ref[N_META:, :] = u_ref[rows, cols]
            tot = buf_ref[...]
            shift = 1
            while shift < win:
                tot = tot + pltpu.roll(tot, shift, 0)
                shift *= 2
            dlt = (tot[N_META:, :] * (1.0 / win) - buf_ref[N_META:, :]).astype(BF16)
            y = jnp.dot(dlt, pw_ref[g], preferred_element_type=F32)
            pool_ref[rows, cols] = (y * pscale_ref[:, cols]).astype(BF16)

    def mix(hn, rows):
        g_pool = jnp.dot(hn, wgp_ref[...], preferred_element_type=F32)
        g_mla = jnp.dot(hn, wgm_ref[...], preferred_element_type=F32)
        y_pool = jnp.dot(pool_ref[rows, :], wpo_ref[...], preferred_element_type=F32)
        y_mla = jnp.dot(attn_ref[rows, :], wmo_ref[...], preferred_element_type=F32)
        y = (jax.nn.sigmoid(g_pool) * y_pool + jax.nn.sigmoid(g_mla) * y_mla).astype(BF16)
        return jnp.dot(y, wout_ref[...], preferred_element_type=F32)

    @pl.when(c == 0)
    def _():
        seq_start = pl.program_id(0) % tiles_per_seq == 0
        for n, rows in enumerate(chunks):
            if n == 0:
                halo = jnp.where(seq_start, umeta_ref[...], uprev_ref[...])
            else:
                halo = u_ref[rows.start - N_META:rows.start, :]
            pool(rows, halo)
            hn = (_rms(h_ref[rows, :]) * gmix_ref[...]).astype(BF16)
            hn_ref[rows, :] = hn
            o_ref[rows, :] = mix(hn, rows)

    @pl.when(jnp.logical_and(c > 0, c < last))
    def _():
        o_ref[...] += mix(hn_ref[...], slice(None))

    @pl.when(c == last)
    def _():
        for rows in chunks:
            y = o_ref[rows, :] + mix(hn_ref[rows, :], rows)
            o_ref[rows, :] = h_ref[rows, :] + _rms(y) * gpost_ref[...]


def _mixout(h, g_mix, g_post, u, u_meta, pool_w, pool_scale, attn,
            w_gp, w_gm, w_po, w_mo, w_out, *, tm, tn, seq):
    rows, d = h.shape
    pw = u.shape[1]
    hb = tm // N_META
    chunk = min(ROW_CHUNK, tm)
    row = lambda i, c: (i, 0)
    const = lambda i, c: (0, 0)
    col = lambda i, c: (0, c)
    return pl.pallas_call(
        functools.partial(_mixout_kernel, tiles_per_seq=seq // tm),
        grid=(rows // tm, d // tn),
        in_specs=[
            pl.BlockSpec((tm, d), row),
            pl.BlockSpec((1, d), const),
            pl.BlockSpec((1, d), const),
            pl.BlockSpec((tm, pw), row),
            pl.BlockSpec((N_META, pw), lambda i, c: (jnp.maximum(i * hb - 1, 0), 0)),
            pl.BlockSpec((N_META, pw), const),
            pl.BlockSpec((len(POOL_WINDOWS), POOL_GROUP, POOL_GROUP), lambda i, c: (0, 0, 0)),
            pl.BlockSpec((1, pw), const),
            pl.BlockSpec((tm, attn.shape[1]), row),
            pl.BlockSpec((d, tn), col),
            pl.BlockSpec((d, tn), col),
            pl.BlockSpec((pw, tn), col),
            pl.BlockSpec((attn.shape[1], tn), col),
            pl.BlockSpec((tn, d), lambda i, c: (c, 0)),
        ],
        out_specs=pl.BlockSpec((tm, d), row),
        out_shape=jax.ShapeDtypeStruct((rows, d), F32),
        scratch_shapes=[pltpu.VMEM((tm, d), BF16), pltpu.VMEM((tm, pw), BF16),
                        pltpu.VMEM((chunk + N_META, POOL_GROUP), F32)],
        compiler_params=_params(("parallel", "arbitrary")),
        name="mixout",
    )(h, g_mix, g_post, u, u, u_meta, pool_w, pool_scale, attn, w_gp, w_gm, w_po, w_mo, w_out)


def _rope_tables(n_pos):
    pos = jnp.arange(n_pos, dtype=F32)
    inv = ROPE_THETA ** (-jnp.arange(0, QK_ROPE, 2, dtype=F32) / QK_ROPE)
    ang = pos[:, None] * inv[None, :]
    ang_t = inv[:, None] * pos[None, :]
    ang2 = jnp.concatenate([ang, ang], axis=-1)
    pad = jnp.zeros((n_pos, LANES - QK_ROPE), F32)
    return (jnp.concatenate([jnp.cos(ang2), pad], axis=-1),
            jnp.concatenate([jnp.sin(ang2), pad], axis=-1), jnp.cos(ang_t), jnp.sin(ang_t))


def kernel(x, meta_tokens, norm_ffn1_pre, norm_ffn1_post, ffn1_w_gu, ffn1_w_down, norm_mix_pre, norm_mix_post, w_in, pool_w, pool_scale, w_pool_o, q_a_norm, w_q_b, kv_a_norm, w_kv_b, w_mla_o, w_out, norm_ffn2_pre, norm_ffn2_post, ffn2_w_gu, ffn2_w_down):
    bsz, seq, d = x.shape
    depth = w_in.shape[0]
    nh = MLA_HEADS
    tm, tf, tn, tq = 512, 512, 512, 512
    tm_ffn = 1024

    cos, sin, cos_t, sin_t = _rope_tables(N_META + seq)
    cos_m, sin_m, cos_r, sin_r = cos[:N_META], sin[:N_META], cos[N_META:], sin[N_META:]
    cos_tm, sin_tm, cos_tr, sin_tr = (cos_t[:, :N_META], sin_t[:, :N_META],
                                      cos_t[:, N_META:], sin_t[:, N_META:])

    h = x.reshape(bsz * seq, d)
    hm = meta_tokens.astype(x.dtype)
    row = lambda v: v.reshape(1, -1)

    for i in range(depth):
        w_gu1, w_dn1 = ffn1_w_gu[i].astype(BF16), ffn1_w_down[i].astype(BF16)
        n_lat = POOL_WIDTH + Q_LORA + KV_LORA + QK_ROPE
        w_b = jnp.concatenate([w_in[i][:, :n_lat], jnp.zeros((d, LANES - QK_ROPE), F32)],
                              axis=1).astype(BF16)
        w_gp = w_in[i][:, n_lat:n_lat + d].astype(BF16)
        w_gm = w_in[i][:, n_lat + d:].astype(BF16)
        w_q = jnp.pad(w_q_b[i].reshape(Q_LORA, nh, QK_DIM),
                      ((0, 0), (0, 0), (0, HEAD_PAD - QK_DIM))).reshape(Q_LORA, nh * HEAD_PAD)
        w_qt = w_q.T.astype(BF16)
        w_kv = w_kv_b[i].reshape(KV_LORA, nh, QK_NOPE + V_DIM)
        w_k = w_kv[:, :, :QK_NOPE].reshape(KV_LORA, nh * QK_NOPE).astype(BF16)
        w_vt = w_kv[:, :, QK_NOPE:].reshape(KV_LORA, nh * V_DIM).T.astype(BF16)
        p_w = pool_w[i].astype(BF16)

        d_ff = ffn2_w_down.shape[1]
        h1, w_gu2, w_dn2 = _ffn(h, row(norm_ffn1_pre[i]), row(norm_ffn1_post[i]), w_gu1, w_dn1,
                                tm=tm_ffn, tf=tf,
                                casts=(ffn2_w_gu[i].reshape(2 * d_ff, d), ffn2_w_down[i]))
        w_gu2 = w_gu2.reshape(d, 2 * d_ff)
        h1m, = _ffn(hm, row(norm_ffn1_pre[i]), row(norm_ffn1_post[i]), w_gu1, w_dn1,
                    tm=N_META, tf=tf)

        u, cq, ckv, kpe, w_po, w_mo, w_o = _inproj(
            h1, row(norm_mix_pre[i]), w_b, row(q_a_norm[i]), row(kv_a_norm[i]), cos_r, sin_r,
            tm=tm, casts=(w_pool_o[i], w_mla_o[i], w_out[i]))
        um, cqm, ckvm, kpem = _inproj(h1m, row(norm_mix_pre[i]), w_b, row(q_a_norm[i]),
                                      row(kv_a_norm[i]), cos_m, sin_m, tm=N_META)
        qt, kn, vt = _qkv(cq, ckv, w_qt, w_k, w_vt, cos_tr, sin_tr, tm=tm, tk=tq, seq=seq)
        _, knm, vtm = _qkv(cqm, ckvm, w_qt, w_k, w_vt, cos_tm, sin_tm,
                           tm=N_META, tk=N_META, seq=N_META)

        attn = _attn(qt, kn.reshape(bsz, seq, nh * QK_NOPE),
                     kpe.reshape(bsz, seq, LANES), vt, knm, kpem,
                     vtm.reshape(nh * V_DIM, N_META), hp=4)

        h2 = _mixout(h1, row(norm_mix_pre[i]), row(norm_mix_post[i]), u, um, p_w,
                     row(pool_scale[i]), attn.reshape(bsz * seq, nh * V_DIM),
                     w_gp, w_gm, w_po, w_mo, w_o, tm=tm, tn=tn, seq=seq)

        h, = _ffn(h2, row(norm_ffn2_pre[i]), row(norm_ffn2_post[i]), w_gu2, w_dn2,
                  tm=tm_ffn, tf=tf)
        if i + 1 < depth:
            raise NotImplementedError("only DEPTH == 1 is supported")

    return h.reshape(bsz, seq, d)
```

```python
import functools
import math

import jax
import jax.numpy as jnp
from jax import lax
from jax.experimental import pallas as pl
from jax.experimental.pallas import tpu as pltpu

F32 = jnp.float32
BF16 = jnp.bfloat16

N_META = 16
POOL_WINDOWS = (2, 4, 8, 16)
POOL_GROUP = 256
POOL_WIDTH = POOL_GROUP * len(POOL_WINDOWS)
MLA_HEADS = 16
Q_LORA = 512
KV_LORA = 512
QK_NOPE = 128
QK_ROPE = 64
V_DIM = 128
QK_DIM = QK_NOPE + QK_ROPE
ROPE_THETA = 10000.0
EPS = 1e-6
LANES = 128
BF16_ROWS = 16
HEAD_PAD = 2 * LANES
Q_SCALE = (QK_DIM ** -0.5) * math.log2(math.e)
MASK_VALUE = -1e30

VMEM_LIMIT = 56 * 1024 * 1024
FFN_VMEM_LIMIT = 62 * 1024 * 1024
ROW_CHUNK = 256
SCORE_LEAD = 1


def _rms(x):
    return x * lax.rsqrt(jnp.mean(x * x, axis=-1, keepdims=True) + EPS)


def _rope(x, cos, sin):
    rot = pltpu.roll(x, 32, 1) - pltpu.roll(x, 96, 1)
    return x * cos + rot * sin


def _params(sem, vmem_limit=VMEM_LIMIT):
    return pltpu.CompilerParams(dimension_semantics=sem, vmem_limit_bytes=vmem_limit)


def _row_chunks(rows, chunk):
    chunk = min(chunk, rows)
    return [slice(r, r + chunk) for r in range(0, rows, chunk)]


def _ffn_kernel(*refs, n_casts):
    x_ref, gpre_ref, gpost_ref, wg_ref, wu_ref, wd_ref = refs[:6]
    cast_in = refs[6:6 + n_casts]
    o_ref = refs[6 + n_casts]
    cast_out = refs[7 + n_casts:7 + 2 * n_casts]
    xn_ref = refs[7 + 2 * n_casts]
    f = pl.program_id(1)
    last = pl.num_programs(1) - 1
    chunks = _row_chunks(x_ref.shape[0], ROW_CHUNK)

    def mlp(xn):
        g = jnp.dot(xn, wg_ref[...], preferred_element_type=F32)
        u = jnp.dot(xn, wu_ref[...], preferred_element_type=F32)
        a = ((g * jax.nn.sigmoid(g)) * u).astype(BF16)
        return jnp.dot(a, wd_ref[...], preferred_element_type=F32)

    def casts():
        for src, dst in zip(cast_in, cast_out):
            dst[...] = src[...].astype(BF16)

    @pl.when(f == 0)
    def _():
        for n, rows in enumerate(chunks):
            xn = (_rms(x_ref[rows, :]) * gpre_ref[...]).astype(BF16)
            xn_ref[rows, :] = xn
            o_ref[rows, :] = mlp(xn)
            if n == 0:
                casts()

    @pl.when(jnp.logical_and(f > 0, f < last))
    def _():
        o_ref[...] += mlp(xn_ref[...])
        casts()

    @pl.when(f == last)
    def _():
        for n, rows in enumerate(chunks):
            y = o_ref[rows, :] + mlp(xn_ref[rows, :])
            o_ref[rows, :] = x_ref[rows, :] + 0.5 * (_rms(y) * gpost_ref[...])
            if n == 0:
                casts()


def _ffn(x, g_pre, g_post, w_gu, w_down, *, tm, tf, casts=()):
    rows, d = x.shape
    d_ff = w_down.shape[0]
    nf = d_ff // tf
    ni = rows // tm
    cast_specs = []
    for c in casts:
        r, k = c.shape
        if r % (ni * BF16_ROWS) == 0 and k % (nf * LANES) == 0:
            cast_specs.append(pl.BlockSpec((r // ni, k // nf), lambda i, f: (i, f)))
        else:
            assert r % (ni * nf * BF16_ROWS) == 0
            cast_specs.append(pl.BlockSpec((r // (ni * nf), k), lambda i, f: (i * nf + f, 0)))
    return pl.pallas_call(
        functools.partial(_ffn_kernel, n_casts=len(casts)),
        grid=(rows // tm, nf),
        in_specs=[
            pl.BlockSpec((tm, d), lambda i, f: (i, 0)),
            pl.BlockSpec((1, d), lambda i, f: (0, 0)),
            pl.BlockSpec((1, d), lambda i, f: (0, 0)),
            pl.BlockSpec((d, tf), lambda i, f: (0, f)),
            pl.BlockSpec((d, tf), lambda i, f: (0, f + nf)),
            pl.BlockSpec((tf, d), lambda i, f: (f, 0)),
        ] + cast_specs,
        out_specs=[pl.BlockSpec((tm, d), lambda i, f: (i, 0))] + cast_specs,
        out_shape=[jax.ShapeDtypeStruct((rows, d), F32)]
        + [jax.ShapeDtypeStruct(c.shape, BF16) for c in casts],
        scratch_shapes=[pltpu.VMEM((tm, d), BF16)],
        compiler_params=_params(("parallel", "arbitrary"), FFN_VMEM_LIMIT),
        name="ffn",
    )(x, g_pre, g_post, w_gu, w_gu, w_down, *casts)


def _inproj_kernel(*refs, n_casts):
    h_ref, gmix_ref, w_ref, gq_ref, gkv_ref, cos_ref, sin_ref = refs[:7]
    cast_in = refs[7:7 + n_casts]
    u_ref, cq_ref, ckv_ref, kpe_ref = refs[7 + n_casts:11 + n_casts]
    cast_out = refs[11 + n_casts:]
    c0, c1, c2, c3 = POOL_WIDTH, POOL_WIDTH + Q_LORA, POOL_WIDTH + Q_LORA + KV_LORA, w_ref.shape[1]
    for n, rows in enumerate(_row_chunks(h_ref.shape[0], ROW_CHUNK)):
        hn = (_rms(h_ref[rows, :]) * gmix_ref[...]).astype(BF16)
        u_ref[rows, :] = jnp.dot(hn, w_ref[:, :c0], preferred_element_type=F32)
        if n == 0:
            for src, dst in zip(cast_in, cast_out):
                dst[...] = src[...].astype(BF16)
        cq = jnp.dot(hn, w_ref[:, c0:c1], preferred_element_type=F32)
        cq_ref[rows, :] = (_rms(cq) * gq_ref[...]).astype(BF16)
        ckv = jnp.dot(hn, w_ref[:, c1:c2], preferred_element_type=F32)
        ckv_ref[rows, :] = (_rms(ckv) * gkv_ref[...]).astype(BF16)
        kr = jnp.dot(hn, w_ref[:, c2:c3], preferred_element_type=F32)
        kpe_ref[rows, :] = _rope(kr, cos_ref[rows, :], sin_ref[rows, :]).astype(BF16)


def _inproj(h, g_mix, w_b, g_q, g_kv, cos, sin, *, tm, casts=()):
    rows, d = h.shape
    n_pos = cos.shape[0] // tm
    wcols = w_b.shape[1]
    steps = rows // tm
    row = lambda i: (i, 0)
    const = lambda i: (0, 0)
    pos = lambda i: (i % n_pos, 0)
    cast_specs = [pl.BlockSpec((c.shape[0] // steps, c.shape[1]), row) for c in casts]
    assert all(c.shape[0] % (steps * BF16_ROWS) == 0 for c in casts)
    return pl.pallas_call(
        functools.partial(_inproj_kernel, n_casts=len(casts)),
        grid=(steps,),
        in_specs=[
            pl.BlockSpec((tm, d), row),
            pl.BlockSpec((1, d), const),
            pl.BlockSpec((d, wcols), const),
            pl.BlockSpec((1, Q_LORA), const),
            pl.BlockSpec((1, KV_LORA), const),
            pl.BlockSpec((tm, LANES), pos),
            pl.BlockSpec((tm, LANES), pos),
        ] + cast_specs,
        out_specs=[
            pl.BlockSpec((tm, POOL_WIDTH), row),
            pl.BlockSpec((tm, Q_LORA), row),
            pl.BlockSpec((tm, KV_LORA), row),
            pl.BlockSpec((tm, LANES), row),
        ] + cast_specs,
        out_shape=[
            jax.ShapeDtypeStruct((rows, POOL_WIDTH), F32),
            jax.ShapeDtypeStruct((rows, Q_LORA), BF16),
            jax.ShapeDtypeStruct((rows, KV_LORA), BF16),
            jax.ShapeDtypeStruct((rows, LANES), BF16),
        ] + [jax.ShapeDtypeStruct(c.shape, BF16) for c in casts],
        compiler_params=_params(("parallel",)),
        name="inproj",
    )(h, g_mix, w_b, g_q, g_kv, cos, sin, *casts)


def _qkv_kernel(cq_ref, ckv_ref, wqt_ref, wk_ref, wvt_ref, cos_ref, sin_ref,
                qt_ref, k_ref, vt_ref):
    cq = cq_ref[...]
    ckv = ckv_ref[...]
    nt = (((1,), (1,)), ((), ()))
    tk = vt_ref.shape[-1]
    chunks = [slice(c * tk, (c + 1) * tk) for c in range(vt_ref.shape[1])]
    cos = cos_ref[...] * Q_SCALE
    sin = sin_ref[...] * Q_SCALE
    half = QK_ROPE // 2
    qt = lax.dot_general(wqt_ref[...], cq, nt, preferred_element_type=F32)
    zeros = jnp.zeros((HEAD_PAD - QK_DIM, tk), BF16)
    for h in range(MLA_HEADS):
        lo = h * HEAD_PAD
        nope = qt[lo:lo + QK_NOPE] * Q_SCALE
        x1 = qt[lo + QK_NOPE:lo + QK_NOPE + half]
        x2 = qt[lo + QK_NOPE + half:lo + QK_DIM]
        r1 = x1 * cos - x2 * sin
        r2 = x2 * cos + x1 * sin
        for c, cols in enumerate(chunks):
            qt_ref[0, c, lo:lo + QK_NOPE, :] = nope[:, cols].astype(BF16)
            qt_ref[0, c, lo + QK_NOPE:lo + QK_NOPE + half, :] = r1[:, cols].astype(BF16)
            qt_ref[0, c, lo + QK_NOPE + half:lo + QK_DIM, :] = r2[:, cols].astype(BF16)
            qt_ref[0, c, lo + QK_DIM:lo + HEAD_PAD, :] = zeros
    k_ref[...] = jnp.dot(ckv, wk_ref[...], preferred_element_type=F32).astype(BF16)
    vt = lax.dot_general(wvt_ref[...], ckv, nt, preferred_element_type=F32)
    for c, cols in enumerate(chunks):
        vt_ref[0, c] = vt[:, cols].astype(BF16)


def _qkv(cq, ckv, w_qt, w_k, w_vt, cos_t, sin_t, *, tm, tk, seq):
    rows = cq.shape[0]
    n_pos = seq // tm
    nh = MLA_HEADS
    row = lambda i: (i, 0)
    const = lambda i: (0, 0)
    tiled = lambda i: (i // n_pos, i % n_pos, 0, 0)
    return pl.pallas_call(
        _qkv_kernel,
        grid=(rows // tm,),
        in_specs=[
            pl.BlockSpec((tm, Q_LORA), row),
            pl.BlockSpec((tm, KV_LORA), row),
            pl.BlockSpec((nh * HEAD_PAD, Q_LORA), const),
            pl.BlockSpec((KV_LORA, nh * QK_NOPE), const),
            pl.BlockSpec((nh * V_DIM, KV_LORA), const),
            pl.BlockSpec((QK_ROPE // 2, tm), lambda i: (0, i % n_pos)),
            pl.BlockSpec((QK_ROPE // 2, tm), lambda i: (0, i % n_pos)),
        ],
        out_specs=[
            pl.BlockSpec((1, tm // tk, nh * HEAD_PAD, tk), tiled),
            pl.BlockSpec((tm, nh * QK_NOPE), row),
            pl.BlockSpec((1, tm // tk, nh * V_DIM, tk), tiled),
        ],
        out_shape=[
            jax.ShapeDtypeStruct((rows // seq, seq // tk, nh * HEAD_PAD, tk), BF16),
            jax.ShapeDtypeStruct((rows, nh * QK_NOPE), BF16),
            jax.ShapeDtypeStruct((rows // seq, seq // tk, nh * V_DIM, tk), BF16),
        ],
        compiler_params=_params(("parallel",)),
        name="qkv",
    )(cq, ckv, w_qt, w_k, w_vt, cos_t, sin_t)


def _attn_kernel(qi_ref, kj_ref, new_ref, q_ref, kn_ref, kp_ref, vt_ref, knm_ref, kpm_ref,
                 vtm_ref, o_ref, acc_ref, sa_ref, sb_ref, st_ref):
    tk = vt_ref.shape[-1]
    tq = tk
    hp = acc_ref.shape[0]
    n_pairs = qi_ref.shape[0]
    bufs = ((sa_ref, 2), (sb_ref, 3))

    def q_of(h, qi):
        return q_ref[0, qi, h * HEAD_PAD:(h + 1) * HEAD_PAD, :]

    def init_tile(qi, heads):
        kpm = kpm_ref[...]
        for h in heads:
            km = jnp.concatenate([knm_ref[:, h * QK_NOPE:(h + 1) * QK_NOPE], kpm], axis=1)
            s = jnp.dot(km, q_of(h, qi), preferred_element_type=F32)
            m0 = jnp.max(s, axis=0, keepdims=True)
            p = jnp.exp2(s - m0)
            acc_ref[h] = jnp.dot(vtm_ref[h * V_DIM:(h + 1) * V_DIM, :], p.astype(BF16),
                                 preferred_element_type=F32)
            st_ref[h, 0] = m0
            st_ref[h, 1] = jnp.sum(p, axis=0, keepdims=True)

    def scores(t, dst, masked, heads):
        s_ref, row = dst
        qi = qi_ref[t]
        start = pl.multiple_of(kj_ref[t] * tk, tk)
        kp = kp_ref[0, pl.ds(start, tk), :]
        for h in heads:
            k = jnp.concatenate(
                [kn_ref[0, pl.ds(start, tk), h * QK_NOPE:(h + 1) * QK_NOPE], kp], axis=1)
            s = jnp.dot(k, q_of(h, qi), preferred_element_type=F32)
            if masked:
                kpos = lax.broadcasted_iota(jnp.int32, s.shape, 0)
                qpos = lax.broadcasted_iota(jnp.int32, s.shape, 1)
                s = jnp.where(kpos <= qpos, s, MASK_VALUE)
            s_ref[h] = s
            st_ref[h, row] = jnp.max(s, axis=0, keepdims=True)

    def absorb(t, src, heads):
        s_ref, row = src
        kj = kj_ref[t]
        ones = jnp.ones((BF16_ROWS, tk), BF16)
        for h in heads:
            m, l = st_ref[h, 0], st_ref[h, 1]
            m_new = jnp.maximum(m, st_ref[h, row])
            alpha = jnp.exp2(m - m_new)
            p = jnp.exp2(s_ref[h] - m_new)
            vt1 = jnp.concatenate([vt_ref[0, kj, h * V_DIM:(h + 1) * V_DIM, :], ones], axis=0)
            pv = jnp.dot(vt1, p.astype(BF16), preferred_element_type=F32)
            acc_ref[h] = alpha * acc_ref[h] + pv[:V_DIM]
            st_ref[h, 0] = m_new
            st_ref[h, 1] = alpha * l + pv[V_DIM:V_DIM + 1]

    def finalize(qi, heads):
        rows = pl.ds(pl.multiple_of(qi * tq, tq), tq)
        for h in heads:
            o_ref[0, rows, h * V_DIM:(h + 1) * V_DIM] = (
                acc_ref[h] / st_ref[h, 1]).T.astype(BF16)

    def run(stages):
        items = [(st, h) for st in stages for h in range(hp)]

        def put_scores(item):
            (t, _, dst, is_fresh), h = item
            scores(t, dst, is_fresh, [h])

        def put_rest(item):
            (t, src, _, is_fresh), h = item
            absorb(t - 1, src, [h])
            if is_fresh:
                finalize(qi_ref[t - 1], [h])
                init_tile(qi_ref[t], [h])

        for item in items[:SCORE_LEAD + 1]:
            put_scores(item)
        for n, item in enumerate(items):
            put_rest(item)
            if n + SCORE_LEAD + 1 < len(items):
                put_scores(items[n + SCORE_LEAD + 1])

    heads = list(range(hp))
    init_tile(qi_ref[0], heads)
    scores(0, bufs[0], True, heads)

    def body(r, carry):
        t = 2 * r + 1
        a, b = bufs
        for fresh0, fresh1 in ((True, False), (False, True), (False, False)):
            cond = jnp.logical_and(new_ref[t] == int(fresh0), new_ref[t + 1] == int(fresh1))
            pl.when(cond)(functools.partial(
                run, [(t, a, b, fresh0), (t + 1, b, a, fresh1)]))
        return carry

    lax.fori_loop(0, (n_pairs - 1) // 2, body, 0)
    last = bufs[0]
    if (n_pairs - 1) % 2:
        t = n_pairs - 1
        for is_fresh in (True, False):
            pl.when(new_ref[t] == int(is_fresh))(functools.partial(
                run, [(t, bufs[0], bufs[1], is_fresh)]))
        last = bufs[1]
    absorb(n_pairs - 1, last, heads)
    finalize(qi_ref[n_pairs - 1], heads)


def _attn(qt, k_nope, k_pe, vt, knm, kpm, vtm, *, hp):
    b, s, _ = k_nope.shape
    tk = vt.shape[-1]
    nq = s // tk
    nh = MLA_HEADS
    qi, kj, new = [], [], []
    for i in range(nq):
        for n, j in enumerate([i] + list(range(i))):
            qi.append(i)
            kj.append(j)
            new.append(int(n == 0))
    assert not any(new[t] and new[t + 1] for t in range(1, len(new) - 1, 2))
    tables = [jnp.asarray(v, jnp.int32) for v in (qi, kj, new)]
    grid_spec = pltpu.PrefetchScalarGridSpec(
        num_scalar_prefetch=len(tables),
        grid=(b, nh // hp),
        in_specs=[
            pl.BlockSpec((1, nq, hp * HEAD_PAD, tk), lambda bi, h, *_: (bi, 0, h, 0)),
            pl.BlockSpec((1, s, hp * QK_NOPE), lambda bi, h, *_: (bi, 0, h)),
            pl.BlockSpec((1, s, LANES), lambda bi, h, *_: (bi, 0, 0)),
            pl.BlockSpec((1, nq, hp * V_DIM, tk), lambda bi, h, *_: (bi, 0, h, 0)),
            pl.BlockSpec((N_META, hp * QK_NOPE), lambda bi, h, *_: (0, h)),
            pl.BlockSpec((N_META, LANES), lambda bi, h, *_: (0, 0)),
            pl.BlockSpec((hp * V_DIM, N_META), lambda bi, h, *_: (h, 0)),
        ],
        out_specs=pl.BlockSpec((1, s, hp * V_DIM), lambda bi, h, *_: (bi, 0, h)),
        scratch_shapes=[pltpu.VMEM((hp, V_DIM, tk), F32), pltpu.VMEM((hp, tk, tk), F32),
                        pltpu.VMEM((hp, tk, tk), F32), pltpu.VMEM((hp, 4, 1, tk), F32)],
    )
    return pl.pallas_call(
        _attn_kernel,
        grid_spec=grid_spec,
        out_shape=jax.ShapeDtypeStruct((b, s, nh * V_DIM), BF16),
        compiler_params=_params(("parallel", "parallel")),
        name="attn",
    )(*tables, qt, k_nope, k_pe, vt, knm, kpm, vtm)


def _mixout_kernel(h_ref, gmix_ref, gpost_ref, u_ref, uprev_ref, umeta_ref, pw_ref, pscale_ref,
                   attn_ref, wgp_ref, wgm_ref, wpo_ref, wmo_ref, wout_ref, o_ref,
                   hn_ref, pool_ref, buf_ref, *, tiles_per_seq):
    c = pl.program_id(1)
    last = pl.num_programs(1) - 1
    chunks = _row_chunks(h_ref.shape[0], ROW_CHUNK)

    def pool(rows, halo):
        for g, win in enumerate(POOL_WINDOWS):
            cols = slice(g * POOL_GROUP, (g + 1) * POOL_GROUP)
            buf_ref[:N_META, :] = halo[:, cols]
            buf_ref[N_META:, :] = u_ref[rows, cols]
            tot = buf_ref[...]
            shift = 1
            while shift < win:
                tot = tot + pltpu.roll(tot, shift, 0)
                shift *= 2
            dlt = (tot[N_META:, :] * (1.0 / win) - buf_ref[N_META:, :]).astype(BF16)
            y = jnp.dot(dlt, pw_ref[g], preferred_element_type=F32)
            pool_ref[rows, cols] = (y * pscale_ref[:, cols]).astype(BF16)

    def mix(hn, rows):
        g_pool = jnp.dot(hn, wgp_ref[...], preferred_element_type=F32)
        g_mla = jnp.dot(hn, wgm_ref[...], preferred_element_type=F32)
        y_pool = jnp.dot(pool_ref[rows, :], wpo_ref[...], preferred_element_type=F32)
        y_mla = jnp.dot(attn_ref[rows, :], wmo_ref[...], preferred_element_type=F32)
        y = (jax.nn.sigmoid(g_pool) * y_pool + jax.nn.sigmoid(g_mla) * y_mla).astype(BF16)
        return jnp.dot(y, wout_ref[...], preferred_element_type=F32)

    @pl.when(c == 0)
    def _():
        seq_start = pl.program_id(0) % tiles_per_seq == 0
        for n, rows in enumerate(chunks):
            if n == 0:
                halo = jnp.where(seq_start, umeta_ref[...], uprev_ref[...])
            else:
                halo = u_ref[rows.start - N_META:rows.start, :]
            pool(rows, halo)
            hn = (_rms(h_ref[rows, :]) * gmix_ref[...]).astype(BF16)
            hn_ref[rows, :] = hn
            o_ref[rows, :] = mix(hn, rows)

    @pl.when(jnp.logical_and(c > 0, c < last))
    def _():
        o_ref[...] += mix(hn_ref[...], slice(None))

    @pl.when(c == last)
    def _():
        for rows in chunks:
            y = o_ref[rows, :] + mix(hn_ref[rows, :], rows)
            o_ref[rows, :] = h_ref[rows, :] + _rms(y) * gpost_ref[...]


def _mixout(h, g_mix, g_post, u, u_meta, pool_w, pool_scale, attn,
            w_gp, w_gm, w_po, w_mo, w_out, *, tm, tn, seq):
    rows, d = h.shape
    pw = u.shape[1]
    hb = tm // N_META
    chunk = min(ROW_CHUNK, tm)
    row = lambda i, c: (i, 0)
    const = lambda i, c: (0, 0)
    col = lambda i, c: (0, c)
    return pl.pallas_call(
        functools.partial(_mixout_kernel, tiles_per_seq=seq // tm),
        grid=(rows // tm, d // tn),
        in_specs=[
            pl.BlockSpec((tm, d), row),
            pl.BlockSpec((1, d), const),
            pl.BlockSpec((1, d), const),
            pl.BlockSpec((tm, pw), row),
            pl.BlockSpec((N_META, pw), lambda i, c: (jnp.maximum(i * hb - 1, 0), 0)),
            pl.BlockSpec((N_META, pw), const),
            pl.BlockSpec((len(POOL_WINDOWS), POOL_GROUP, POOL_GROUP), lambda i, c: (0, 0, 0)),
            pl.BlockSpec((1, pw), const),
            pl.BlockSpec((tm, attn.shape[1]), row),
            pl.BlockSpec((d, tn), col),
            pl.BlockSpec((d, tn), col),
            pl.BlockSpec((pw, tn), col),
            pl.BlockSpec((attn.shape[1], tn), col),
            pl.BlockSpec((tn, d), lambda i, c: (c, 0)),
        ],
        out_specs=pl.BlockSpec((tm, d), row),
        out_shape=jax.ShapeDtypeStruct((rows, d), F32),
        scratch_shapes=[pltpu.VMEM((tm, d), BF16), pltpu.VMEM((tm, pw), BF16),
                        pltpu.VMEM((chunk + N_META, POOL_GROUP), F32)],
        compiler_params=_params(("parallel", "arbitrary")),
        name="mixout",
    )(h, g_mix, g_post, u, u, u_meta, pool_w, pool_scale, attn, w_gp, w_gm, w_po, w_mo, w_out)


def _rope_tables(n_pos):
    pos = jnp.arange(n_pos, dtype=F32)
    inv = ROPE_THETA ** (-jnp.arange(0, QK_ROPE, 2, dtype=F32) / QK_ROPE)
    ang = pos[:, None] * inv[None, :]
    ang_t = inv[:, None] * pos[None, :]
    ang2 = jnp.concatenate([ang, ang], axis=-1)
    pad = jnp.zeros((n_pos, LANES - QK_ROPE), F32)
    return (jnp.concatenate([jnp.cos(ang2), pad], axis=-1),
            jnp.concatenate([jnp.sin(ang2), pad], axis=-1), jnp.cos(ang_t), jnp.sin(ang_t))


def kernel(x, meta_tokens, norm_ffn1_pre, norm_ffn1_post, ffn1_w_gu, ffn1_w_down, norm_mix_pre, norm_mix_post, w_in, pool_w, pool_scale, w_pool_o, q_a_norm, w_q_b, kv_a_norm, w_kv_b, w_mla_o, w_out, norm_ffn2_pre, norm_ffn2_post, ffn2_w_gu, ffn2_w_down):
    bsz, seq, d = x.shape
    depth = w_in.shape[0]
    nh = MLA_HEADS
    tm, tf, tn, tq = 512, 512, 512, 512
    tm_ffn = 1024

    cos, sin, cos_t, sin_t = _rope_tables(N_META + seq)
    cos_m, sin_m, cos_r, sin_r = cos[:N_META], sin[:N_META], cos[N_META:], sin[N_META:]
    cos_tm, sin_tm, cos_tr, sin_tr = (cos_t[:, :N_META], sin_t[:, :N_META],
                                      cos_t[:, N_META:], sin_t[:, N_META:])

    h = x.reshape(bsz * seq, d)
    hm = meta_tokens.astype(x.dtype)
    row = lambda v: v.reshape(1, -1)

    for i in range(depth):
        w_gu1, w_dn1 = ffn1_w_gu[i].astype(BF16), ffn1_w_down[i].astype(BF16)
        n_lat = POOL_WIDTH + Q_LORA + KV_LORA + QK_ROPE
        w_b = jnp.concatenate([w_in[i][:, :n_lat], jnp.zeros((d, LANES - QK_ROPE), F32)],
                              axis=1).astype(BF16)
        w_gp = w_in[i][:, n_lat:n_lat + d].astype(BF16)
        w_gm = w_in[i][:, n_lat + d:].astype(BF16)
        w_q = jnp.pad(w_q_b[i].reshape(Q_LORA, nh, QK_DIM),
                      ((0, 0), (0, 0), (0, HEAD_PAD - QK_DIM))).reshape(Q_LORA, nh * HEAD_PAD)
        w_qt = w_q.T.astype(BF16)
        w_kv = w_kv_b[i].reshape(KV_LORA, nh, QK_NOPE + V_DIM)
        w_k = w_kv[:, :, :QK_NOPE].reshape(KV_LORA, nh * QK_NOPE).astype(BF16)
        w_vt = w_kv[:, :, QK_NOPE:].reshape(KV_LORA, nh * V_DIM).T.astype(BF16)
        p_w = pool_w[i].astype(BF16)

        h1, w_gu2, w_dn2 = _ffn(h, row(norm_ffn1_pre[i]), row(norm_ffn1_post[i]), w_gu1, w_dn1,
                                tm=tm_ffn, tf=tf, casts=(ffn2_w_gu[i], ffn2_w_down[i]))
        h1m, = _ffn(hm, row(norm_ffn1_pre[i]), row(norm_ffn1_post[i]), w_gu1, w_dn1,
                    tm=N_META, tf=tf)

        u, cq, ckv, kpe, w_po, w_mo, w_o = _inproj(
            h1, row(norm_mix_pre[i]), w_b, row(q_a_norm[i]), row(kv_a_norm[i]), cos_r, sin_r,
            tm=tm, casts=(w_pool_o[i], w_mla_o[i], w_out[i]))
        um, cqm, ckvm, kpem = _inproj(h1m, row(norm_mix_pre[i]), w_b, row(q_a_norm[i]),
                                      row(kv_a_norm[i]), cos_m, sin_m, tm=N_META)
        qt, kn, vt = _qkv(cq, ckv, w_qt, w_k, w_vt, cos_tr, sin_tr, tm=tm, tk=tq, seq=seq)
        _, knm, vtm = _qkv(cqm, ckvm, w_qt, w_k, w_vt, cos_tm, sin_tm,
                           tm=N_META, tk=N_META, seq=N_META)

        attn = _attn(qt, kn.reshape(bsz, seq, nh * QK_NOPE),
                     kpe.reshape(bsz, seq, LANES), vt, knm, kpem,
                     vtm.reshape(nh * V_DIM, N_META), hp=4)

        h2 = _mixout(h1, row(norm_mix_pre[i]), row(norm_mix_post[i]), u, um, p_w,
                     row(pool_scale[i]), attn.reshape(bsz * seq, nh * V_DIM),
                     w_gp, w_gm, w_po, w_mo, w_o, tm=tm, tn=tn, seq=seq)

        h, = _ffn(h2, row(norm_ffn2_pre[i]), row(norm_ffn2_post[i]), w_gu2, w_dn2,
                  tm=tm_ffn, tf=tf)
        if i + 1 < depth:
            raise NotImplementedError("only DEPTH == 1 is supported")

    return h.reshape(bsz, seq, d)
```

```python
import functools
import math

import jax
import jax.numpy as jnp
from jax import lax
from jax.experimental import pallas as pl
from jax.experimental.pallas import tpu as pltpu

F32 = jnp.float32
BF16 = jnp.bfloat16

N_META = 16
POOL_WINDOWS = (2, 4, 8, 16)
POOL_GROUP = 256
POOL_WIDTH = POOL_GROUP * len(POOL_WINDOWS)
MLA_HEADS = 16
Q_LORA = 512
KV_LORA = 512
QK_NOPE = 128
QK_ROPE = 64
V_DIM = 128
QK_DIM = QK_NOPE + QK_ROPE
ROPE_THETA = 10000.0
EPS = 1e-6
LANES = 128
BF16_ROWS = 16
HEAD_PAD = 2 * LANES
Q_SCALE = (QK_DIM ** -0.5) * math.log2(math.e)
MASK_VALUE = -1e30

VMEM_LIMIT = 56 * 1024 * 1024
FFN_VMEM_LIMIT = 62 * 1024 * 1024
ROW_CHUNK = 256
SCORE_LEAD = 1


def _rms(x):
    return x * lax.rsqrt(jnp.mean(x * x, axis=-1, keepdims=True) + EPS)


def _rope(x, cos, sin):
    rot = pltpu.roll(x, 32, 1) - pltpu.roll(x, 96, 1)
    return x * cos + rot * sin


def _params(sem, vmem_limit=VMEM_LIMIT):
    return pltpu.CompilerParams(dimension_semantics=sem, vmem_limit_bytes=vmem_limit)


def _row_chunks(rows, chunk):
    chunk = min(chunk, rows)
    return [slice(r, r + chunk) for r in range(0, rows, chunk)]


def _ffn_kernel(*refs, n_casts, has_side):
    ns = int(has_side)
    x_ref, gpre_ref, gpost_ref, wg_ref, wu_ref, wd_ref = refs[:6]
    n_in = 6 + ns + n_casts
    cast_in = refs[6 + ns:n_in]
    o_ref = refs[n_in]
    cast_out = refs[n_in + 1 + ns:n_in + 1 + ns + n_casts]
    xn_ref = refs[n_in + 1 + ns + n_casts]
    f = pl.program_id(1)
    last = pl.num_programs(1) - 1
    chunks = _row_chunks(x_ref.shape[0], ROW_CHUNK)

    def mlp(xn):
        g = jnp.dot(xn, wg_ref[...], preferred_element_type=F32)
        u = jnp.dot(xn, wu_ref[...], preferred_element_type=F32)
        a = ((g * jax.nn.sigmoid(g)) * u).astype(BF16)
        return jnp.dot(a, wd_ref[...], preferred_element_type=F32)

    def casts():
        for src, dst in zip(cast_in, cast_out):
            dst[...] = src[...].astype(BF16)

    def side(step):
        if not has_side:
            return
        xs_ref, os_ref, xsn_ref = refs[6], refs[n_in + 1], refs[-1]

        @pl.when(pl.program_id(0) == 0)
        def _():
            if step == "first":
                xsn_ref[...] = (_rms(xs_ref[...]) * gpre_ref[...]).astype(BF16)
                os_ref[...] = mlp(xsn_ref[...])
            elif step == "middle":
                os_ref[...] += mlp(xsn_ref[...])
            else:
                y = os_ref[...] + mlp(xsn_ref[...])
                os_ref[...] = xs_ref[...] + 0.5 * (_rms(y) * gpost_ref[...])

    @pl.when(f == 0)
    def _():
        for n, rows in enumerate(chunks):
            xn = (_rms(x_ref[rows, :]) * gpre_ref[...]).astype(BF16)
            xn_ref[rows, :] = xn
            o_ref[rows, :] = mlp(xn)
            if n == 0:
                casts()
        side("first")

    @pl.when(jnp.logical_and(f > 0, f < last))
    def _():
        o_ref[...] += mlp(xn_ref[...])
        casts()
        side("middle")

    @pl.when(f == last)
    def _():
        for n, rows in enumerate(chunks):
            y = o_ref[rows, :] + mlp(xn_ref[rows, :])
            o_ref[rows, :] = x_ref[rows, :] + 0.5 * (_rms(y) * gpost_ref[...])
            if n == 0:
                casts()
        side("last")


def _ffn(x, g_pre, g_post, w_gu, w_down, *, tm, tf, side=None, casts=()):
    rows, d = x.shape
    d_ff = w_down.shape[0]
    nf = d_ff // tf
    ni = rows // tm
    const = lambda i, f: (0, 0)
    cast_specs = []
    for c in casts:
        r, k = c.shape
        if r % (ni * BF16_ROWS) == 0 and k % (nf * LANES) == 0:
            cast_specs.append(pl.BlockSpec((r // ni, k // nf), lambda i, f: (i, f)))
        else:
            assert r % (ni * nf * BF16_ROWS) == 0
            cast_specs.append(pl.BlockSpec((r // (ni * nf), k), lambda i, f: (i * nf + f, 0)))
    sides = [] if side is None else [side]
    side_specs = [pl.BlockSpec(v.shape, const) for v in sides]
    return pl.pallas_call(
        functools.partial(_ffn_kernel, n_casts=len(casts), has_side=bool(sides)),
        grid=(ni, nf),
        in_specs=[
            pl.BlockSpec((tm, d), lambda i, f: (i, 0)),
            pl.BlockSpec((1, d), const),
            pl.BlockSpec((1, d), const),
            pl.BlockSpec((d, tf), lambda i, f: (0, f)),
            pl.BlockSpec((d, tf), lambda i, f: (0, f + nf)),
            pl.BlockSpec((tf, d), lambda i, f: (f, 0)),
        ] + side_specs + cast_specs,
        out_specs=[pl.BlockSpec((tm, d), lambda i, f: (i, 0))] + side_specs + cast_specs,
        out_shape=[jax.ShapeDtypeStruct((rows, d), F32)]
        + [jax.ShapeDtypeStruct(v.shape, F32) for v in sides]
        + [jax.ShapeDtypeStruct(c.shape, BF16) for c in casts],
        scratch_shapes=[pltpu.VMEM((tm, d), BF16)]
        + [pltpu.VMEM(v.shape, BF16) for v in sides],
        compiler_params=_params(("arbitrary", "arbitrary"), FFN_VMEM_LIMIT),
        name="ffn",
    )(x, g_pre, g_post, w_gu, w_gu, w_down, *sides, *casts)


def _inproj_kernel(*refs, n_casts):
    h_ref, gmix_ref, w_ref, gq_ref, gkv_ref, cos_ref, sin_ref = refs[:7]
    cast_in = refs[7:7 + n_casts]
    u_ref, cq_ref, ckv_ref, kpe_ref = refs[7 + n_casts:11 + n_casts]
    cast_out = refs[11 + n_casts:]
    c0, c1, c2, c3 = POOL_WIDTH, POOL_WIDTH + Q_LORA, POOL_WIDTH + Q_LORA + KV_LORA, w_ref.shape[1]
    for n, rows in enumerate(_row_chunks(h_ref.shape[0], ROW_CHUNK)):
        hn = (_rms(h_ref[rows, :]) * gmix_ref[...]).astype(BF16)
        u_ref[rows, :] = jnp.dot(hn, w_ref[:, :c0], preferred_element_type=F32)
        if n == 0:
            for src, dst in zip(cast_in, cast_out):
                dst[...] = src[...].astype(BF16)
        cq = jnp.dot(hn, w_ref[:, c0:c1], preferred_element_type=F32)
        cq_ref[rows, :] = (_rms(cq) * gq_ref[...]).astype(BF16)
        ckv = jnp.dot(hn, w_ref[:, c1:c2], preferred_element_type=F32)
        ckv_ref[rows, :] = (_rms(ckv) * gkv_ref[...]).astype(BF16)
        kr = jnp.dot(hn, w_ref[:, c2:c3], preferred_element_type=F32)
        kpe_ref[rows, :] = _rope(kr, cos_ref[rows, :], sin_ref[rows, :]).astype(BF16)


def _inproj(h, g_mix, w_b, g_q, g_kv, cos, sin, *, tm, casts=()):
    rows, d = h.shape
    n_pos = cos.shape[0] // tm
    wcols = w_b.shape[1]
    steps = rows // tm
    row = lambda i: (i, 0)
    const = lambda i: (0, 0)
    pos = lambda i: (i % n_pos, 0)
    cast_specs = [pl.BlockSpec((c.shape[0] // steps, c.shape[1]), row) for c in casts]
    assert all(c.shape[0] % (steps * BF16_ROWS) == 0 for c in casts)
    return pl.pallas_call(
        functools.partial(_inproj_kernel, n_casts=len(casts)),
        grid=(steps,),
        in_specs=[
            pl.BlockSpec((tm, d), row),
            pl.BlockSpec((1, d), const),
            pl.BlockSpec((d, wcols), const),
            pl.BlockSpec((1, Q_LORA), const),
            pl.BlockSpec((1, KV_LORA), const),
            pl.BlockSpec((tm, LANES), pos),
            pl.BlockSpec((tm, LANES), pos),
        ] + cast_specs,
        out_specs=[
            pl.BlockSpec((tm, POOL_WIDTH), row),
            pl.BlockSpec((tm, Q_LORA), row),
            pl.BlockSpec((tm, KV_LORA), row),
            pl.BlockSpec((tm, LANES), row),
        ] + cast_specs,
        out_shape=[
            jax.ShapeDtypeStruct((rows, POOL_WIDTH), F32),
            jax.ShapeDtypeStruct((rows, Q_LORA), BF16),
            jax.ShapeDtypeStruct((rows, KV_LORA), BF16),
            jax.ShapeDtypeStruct((rows, LANES), BF16),
        ] + [jax.ShapeDtypeStruct(c.shape, BF16) for c in casts],
        compiler_params=_params(("parallel",)),
        name="inproj",
    )(h, g_mix, w_b, g_q, g_kv, cos, sin, *casts)


def _qkv_kernel(cq_ref, ckv_ref, wqt_ref, wk_ref, wvt_ref, cos_ref, sin_ref,
                qt_ref, k_ref, vt_ref):
    cq = cq_ref[...]
    ckv = ckv_ref[...]
    nt = (((1,), (1,)), ((), ()))
    tk = vt_ref.shape[-1]
    chunks = [slice(c * tk, (c + 1) * tk) for c in range(vt_ref.shape[1])]
    cos = cos_ref[...] * Q_SCALE
    sin = sin_ref[...] * Q_SCALE
    half = QK_ROPE // 2
    qt = lax.dot_general(wqt_ref[...], cq, nt, preferred_element_type=F32)
    zeros = jnp.zeros((HEAD_PAD - QK_DIM, tk), BF16)
    for h in range(MLA_HEADS):
        lo = h * HEAD_PAD
        nope = qt[lo:lo + QK_NOPE] * Q_SCALE
        x1 = qt[lo + QK_NOPE:lo + QK_NOPE + half]
        x2 = qt[lo + QK_NOPE + half:lo + QK_DIM]
        r1 = x1 * cos - x2 * sin
        r2 = x2 * cos + x1 * sin
        for c, cols in enumerate(chunks):
            qt_ref[0, c, lo:lo + QK_NOPE, :] = nope[:, cols].astype(BF16)
            qt_ref[0, c, lo + QK_NOPE:lo + QK_NOPE + half, :] = r1[:, cols].astype(BF16)
            qt_ref[0, c, lo + QK_NOPE + half:lo + QK_DIM, :] = r2[:, cols].astype(BF16)
            qt_ref[0, c, lo + QK_DIM:lo + HEAD_PAD, :] = zeros
    k_ref[...] = jnp.dot(ckv, wk_ref[...], preferred_element_type=F32).astype(BF16)
    vt = lax.dot_general(wvt_ref[...], ckv, nt, preferred_element_type=F32)
    for c, cols in enumerate(chunks):
        vt_ref[0, c] = vt[:, cols].astype(BF16)


def _qkv(cq, ckv, w_qt, w_k, w_vt, cos_t, sin_t, *, tm, tk, seq):
    rows = cq.shape[0]
    n_pos = seq // tm
    nh = MLA_HEADS
    row = lambda i: (i, 0)
    const = lambda i: (0, 0)
    tiled = lambda i: (i // n_pos, i % n_pos, 0, 0)
    return pl.pallas_call(
        _qkv_kernel,
        grid=(rows // tm,),
        in_specs=[
            pl.BlockSpec((tm, Q_LORA), row),
            pl.BlockSpec((tm, KV_LORA), row),
            pl.BlockSpec((nh * HEAD_PAD, Q_LORA), const),
            pl.BlockSpec((KV_LORA, nh * QK_NOPE), const),
            pl.BlockSpec((nh * V_DIM, KV_LORA), const),
            pl.BlockSpec((QK_ROPE // 2, tm), lambda i: (0, i % n_pos)),
            pl.BlockSpec((QK_ROPE // 2, tm), lambda i: (0, i % n_pos)),
        ],
        out_specs=[
            pl.BlockSpec((1, tm // tk, nh * HEAD_PAD, tk), tiled),
            pl.BlockSpec((tm, nh * QK_NOPE), row),
            pl.BlockSpec((1, tm // tk, nh * V_DIM, tk), tiled),
        ],
        out_shape=[
            jax.ShapeDtypeStruct((rows // seq, seq // tk, nh * HEAD_PAD, tk), BF16),
            jax.ShapeDtypeStruct((rows, nh * QK_NOPE), BF16),
            jax.ShapeDtypeStruct((rows // seq, seq // tk, nh * V_DIM, tk), BF16),
        ],
        compiler_params=_params(("parallel",)),
        name="qkv",
    )(cq, ckv, w_qt, w_k, w_vt, cos_t, sin_t)


def _attn_kernel(qi_ref, kj_ref, new_ref, q_ref, kn_ref, kp_ref, vt_ref, knm_ref, kpm_ref,
                 vtm_ref, o_ref, acc_ref, sa_ref, sb_ref, st_ref):
    tk = vt_ref.shape[-1]
    tq = tk
    hp = acc_ref.shape[0]
    n_pairs = qi_ref.shape[0]
    bufs = ((sa_ref, 2), (sb_ref, 3))

    def q_of(h, qi):
        return q_ref[0, qi, h * HEAD_PAD:(h + 1) * HEAD_PAD, :]

    def init_tile(qi, heads):
        kpm = kpm_ref[...]
        for h in heads:
            km = jnp.concatenate([knm_ref[:, h * QK_NOPE:(h + 1) * QK_NOPE], kpm], axis=1)
            s = jnp.dot(km, q_of(h, qi), preferred_element_type=F32)
            m0 = jnp.max(s, axis=0, keepdims=True)
            p = jnp.exp2(s - m0)
            acc_ref[h] = jnp.dot(vtm_ref[h * V_DIM:(h + 1) * V_DIM, :], p.astype(BF16),
                                 preferred_element_type=F32)
            st_ref[h, 0] = m0
            st_ref[h, 1] = jnp.sum(p, axis=0, keepdims=True)

    def scores(t, dst, masked, heads):
        s_ref, row = dst
        qi = qi_ref[t]
        start = pl.multiple_of(kj_ref[t] * tk, tk)
        kp = kp_ref[0, pl.ds(start, tk), :]
        for h in heads:
            k = jnp.concatenate(
                [kn_ref[0, pl.ds(start, tk), h * QK_NOPE:(h + 1) * QK_NOPE], kp], axis=1)
            s = jnp.dot(k, q_of(h, qi), preferred_element_type=F32)
            if masked:
                kpos = lax.broadcasted_iota(jnp.int32, s.shape, 0)
                qpos = lax.broadcasted_iota(jnp.int32, s.shape, 1)
                s = jnp.where(kpos <= qpos, s, MASK_VALUE)
            s_ref[h] = s
            st_ref[h, row] = jnp.max(s, axis=0, keepdims=True)

    def absorb(t, src, heads):
        s_ref, row = src
        kj = kj_ref[t]
        ones = jnp.ones((BF16_ROWS, tk), BF16)
        for h in heads:
            m, l = st_ref[h, 0], st_ref[h, 1]
            m_new = jnp.maximum(m, st_ref[h, row])
            alpha = jnp.exp2(m - m_new)
            p = jnp.exp2(s_ref[h] - m_new)
            vt1 = jnp.concatenate([vt_ref[0, kj, h * V_DIM:(h + 1) * V_DIM, :], ones], axis=0)
            pv = jnp.dot(vt1, p.astype(BF16), preferred_element_type=F32)
            acc_ref[h] = alpha * acc_ref[h] + pv[:V_DIM]
            st_ref[h, 0] = m_new
            st_ref[h, 1] = alpha * l + pv[V_DIM:V_DIM + 1]

    def finalize(qi, heads):
        rows = pl.ds(pl.multiple_of(qi * tq, tq), tq)
        for h in heads:
            o_ref[0, rows, h * V_DIM:(h + 1) * V_DIM] = (
                acc_ref[h] / st_ref[h, 1]).T.astype(BF16)

    def run(stages):
        items = [(st, h) for st in stages for h in range(hp)]

        def put_scores(item):
            (t, _, dst, is_fresh), h = item
            scores(t, dst, is_fresh, [h])

        def put_rest(item):
            (t, src, _, is_fresh), h = item
            absorb(t - 1, src, [h])
            if is_fresh:
                finalize(qi_ref[t - 1], [h])
                init_tile(qi_ref[t], [h])

        for item in items[:SCORE_LEAD + 1]:
            put_scores(item)
        for n, item in enumerate(items):
            put_rest(item)
            if n + SCORE_LEAD + 1 < len(items):
                put_scores(items[n + SCORE_LEAD + 1])

    heads = list(range(hp))
    init_tile(qi_ref[0], heads)
    scores(0, bufs[0], True, heads)

    def body(r, carry):
        t = 2 * r + 1
        a, b = bufs
        for fresh0, fresh1 in ((True, False), (False, True), (False, False)):
            cond = jnp.logical_and(new_ref[t] == int(fresh0), new_ref[t + 1] == int(fresh1))
            pl.when(cond)(functools.partial(
                run, [(t, a, b, fresh0), (t + 1, b, a, fresh1)]))
        return carry

    lax.fori_loop(0, (n_pairs - 1) // 2, body, 0)
    last = bufs[0]
    if (n_pairs - 1) % 2:
        t = n_pairs - 1
        for is_fresh in (True, False):
            pl.when(new_ref[t] == int(is_fresh))(functools.partial(
                run, [(t, bufs[0], bufs[1], is_fresh)]))
        last = bufs[1]
    absorb(n_pairs - 1, last, heads)
    finalize(qi_ref[n_pairs - 1], heads)


def _attn(qt, k_nope, k_pe, vt, knm, kpm, vtm, *, hp):
    b, s, _ = k_nope.shape
    tk = vt.shape[-1]
    nq = s // tk
    nh = MLA_HEADS
    qi, kj, new = [], [], []
    for i in range(nq):
        for n, j in enumerate([i] + list(range(i))):
            qi.append(i)
            kj.append(j)
            new.append(int(n == 0))
    assert not any(new[t] and new[t + 1] for t in range(1, len(new) - 1, 2))
    tables = [jnp.asarray(v, jnp.int32) for v in (qi, kj, new)]
    grid_spec = pltpu.PrefetchScalarGridSpec(
        num_scalar_prefetch=len(tables),
        grid=(b, nh // hp),
        in_specs=[
            pl.BlockSpec((1, nq, hp * HEAD_PAD, tk), lambda bi, h, *_: (bi, 0, h, 0)),
            pl.BlockSpec((1, s, hp * QK_NOPE), lambda bi, h, *_: (bi, 0, h)),
            pl.BlockSpec((1, s, LANES), lambda bi, h, *_: (bi, 0, 0)),
            pl.BlockSpec((1, nq, hp * V_DIM, tk), lambda bi, h, *_: (bi, 0, h, 0)),
            pl.BlockSpec((N_META, hp * QK_NOPE), lambda bi, h, *_: (0, h)),
            pl.BlockSpec((N_META, LANES), lambda bi, h, *_: (0, 0)),
            pl.BlockSpec((hp * V_DIM, N_META), lambda bi, h, *_: (h, 0)),
        ],
        out_specs=pl.BlockSpec((1, s, hp * V_DIM), lambda bi, h, *_: (bi, 0, h)),
        scratch_shapes=[pltpu.VMEM((hp, V_DIM, tk), F32), pltpu.VMEM((hp, tk, tk), F32),
                        pltpu.VMEM((hp, tk, tk), F32), pltpu.VMEM((hp, 4, 1, tk), F32)],
    )
    return pl.pallas_call(
        _attn_kernel,
        grid_spec=grid_spec,
        out_shape=jax.ShapeDtypeStruct((b, s, nh * V_DIM), BF16),
        compiler_params=_params(("parallel", "parallel")),
        name="attn",
    )(*tables, qt, k_nope, k_pe, vt, knm, kpm, vtm)


def _mixout_kernel(h_ref, gmix_ref, gpost_ref, u_ref, uprev_ref, umeta_ref, pw_ref, pscale_ref,
                   attn_ref, wgp_ref, wgm_ref, wpo_ref, wmo_ref, wout_ref, o_ref,
                   hn_ref, pool_ref, buf_ref, *, tiles_per_seq):
    c = pl.program_id(1)
    last = pl.num_programs(1) - 1
    chunks = _row_chunks(h_ref.shape[0], ROW_CHUNK)

    def pool(rows, halo):
        for g, win in enumerate(POOL_WINDOWS):
            cols = slice(g * POOL_GROUP, (g + 1) * POOL_GROUP)
            buf_ref[:N_META, :] = halo[:, cols]
            buf_ref[N_META:, :] = u_ref[rows, cols]
            tot = buf_ref[...]
            shift = 1
            while shift < win:
                tot = tot + pltpu.roll(tot, shift, 0)
                shift *= 2
            dlt = (tot[N_META:, :] * (1.0 / win) - buf_ref[N_META:, :]).astype(BF16)
            y = jnp.dot(dlt, pw_ref[g], preferred_element_type=F32)
            pool_ref[rows, cols] = (y * pscale_ref[:, cols]).astype(BF16)

    def mix(hn, rows):
        g_pool = jnp.dot(hn, wgp_ref[...], preferred_element_type=F32)
        g_mla = jnp.dot(hn, wgm_ref[...], preferred_element_type=F32)
        y_pool = jnp.dot(pool_ref[rows, :], wpo_ref[...], preferred_element_type=F32)
        y_mla = jnp.dot(attn_ref[rows, :], wmo_ref[...], preferred_element_type=F32)
        y = (jax.nn.sigmoid(g_pool) * y_pool + jax.nn.sigmoid(g_mla) * y_mla).astype(BF16)
        return jnp.dot(y, wout_ref[...], preferred_element_type=F32)

    @pl.when(c == 0)
    def _():
        seq_start = pl.program_id(0) % tiles_per_seq == 0
        for n, rows in enumerate(chunks):
            if n == 0:
                halo = jnp.where(seq_start, umeta_ref[...], uprev_ref[...])
            else:
                halo = u_ref[rows.start - N_META:rows.start, :]
            pool(rows, halo)
            hn = (_rms(h_ref[rows, :]) * gmix_ref[...]).astype(BF16)
            hn_ref[rows, :] = hn
            o_ref[rows, :] = mix(hn, rows)

    @pl.when(jnp.logical_and(c > 0, c < last))
    def _():
        o_ref[...] += mix(hn_ref[...], slice(None))

    @pl.when(c == last)
    def _():
        for rows in chunks:
            y = o_ref[rows, :] + mix(hn_ref[rows, :], rows)
            o_ref[rows, :] = h_ref[rows, :] + _rms(y) * gpost_ref[...]


def _mixout(h, g_mix, g_post, u, u_meta, pool_w, pool_scale, attn,
            w_gp, w_gm, w_po, w_mo, w_out, *, tm, tn, seq):
    rows, d = h.shape
    pw = u.shape[1]
    hb = tm // N_META
    chunk = min(ROW_CHUNK, tm)
    row = lambda i, c: (i, 0)
    const = lambda i, c: (0, 0)
    col = lambda i, c: (0, c)
    return pl.pallas_call(
        functools.partial(_mixout_kernel, tiles_per_seq=seq // tm),
        grid=(rows // tm, d // tn),
        in_specs=[
            pl.BlockSpec((tm, d), row),
            pl.BlockSpec((1, d), const),
            pl.BlockSpec((1, d), const),
            pl.BlockSpec((tm, pw), row),
            pl.BlockSpec((N_META, pw), lambda i, c: (jnp.maximum(i * hb - 1, 0), 0)),
            pl.BlockSpec((N_META, pw), const),
            pl.BlockSpec((len(POOL_WINDOWS), POOL_GROUP, POOL_GROUP), lambda i, c: (0, 0, 0)),
            pl.BlockSpec((1, pw), const),
            pl.BlockSpec((tm, attn.shape[1]), row),
            pl.BlockSpec((d, tn), col),
            pl.BlockSpec((d, tn), col),
            pl.BlockSpec((pw, tn), col),
            pl.BlockSpec((attn.shape[1], tn), col),
            pl.BlockSpec((tn, d), lambda i, c: (c, 0)),
        ],
        out_specs=pl.BlockSpec((tm, d), row),
        out_shape=jax.ShapeDtypeStruct((rows, d), F32),
        scratch_shapes=[pltpu.VMEM((tm, d), BF16), pltpu.VMEM((tm, pw), BF16),
                        pltpu.VMEM((chunk + N_META, POOL_GROUP), F32)],
        compiler_params=_params(("parallel", "arbitrary")),
        name="mixout",
    )(h, g_mix, g_post, u, u, u_meta, pool_w, pool_scale, attn, w_gp, w_gm, w_po, w_mo, w_out)


def _rope_tables(n_pos):
    pos = jnp.arange(n_pos, dtype=F32)
    inv = ROPE_THETA ** (-jnp.arange(0, QK_ROPE, 2, dtype=F32) / QK_ROPE)
    ang = pos[:, None] * inv[None, :]
    ang_t = inv[:, None] * pos[None, :]
    ang2 = jnp.concatenate([ang, ang], axis=-1)
    pad = jnp.zeros((n_pos, LANES - QK_ROPE), F32)
    return (jnp.concatenate([jnp.cos(ang2), pad], axis=-1),
            jnp.concatenate([jnp.sin(ang2), pad], axis=-1), jnp.cos(ang_t), jnp.sin(ang_t))


def kernel(x, meta_tokens, norm_ffn1_pre, norm_ffn1_post, ffn1_w_gu, ffn1_w_down, norm_mix_pre, norm_mix_post, w_in, pool_w, pool_scale, w_pool_o, q_a_norm, w_q_b, kv_a_norm, w_kv_b, w_mla_o, w_out, norm_ffn2_pre, norm_ffn2_post, ffn2_w_gu, ffn2_w_down):
    bsz, seq, d = x.shape
    depth = w_in.shape[0]
    nh = MLA_HEADS
    tm, tf, tn, tq = 512, 512, 512, 512
    tm_ffn = 1024

    cos, sin, cos_t, sin_t = _rope_tables(N_META + seq)
    cos_m, sin_m, cos_r, sin_r = cos[:N_META], sin[:N_META], cos[N_META:], sin[N_META:]
    cos_tm, sin_tm, cos_tr, sin_tr = (cos_t[:, :N_META], sin_t[:, :N_META],
                                      cos_t[:, N_META:], sin_t[:, N_META:])

    h = x.reshape(bsz * seq, d)
    hm = meta_tokens.astype(x.dtype)
    row = lambda v: v.reshape(1, -1)

    for i in range(depth):
        w_gu1, w_dn1 = ffn1_w_gu[i].astype(BF16), ffn1_w_down[i].astype(BF16)
        n_lat = POOL_WIDTH + Q_LORA + KV_LORA + QK_ROPE
        w_b = jnp.concatenate([w_in[i][:, :n_lat], jnp.zeros((d, LANES - QK_ROPE), F32)],
                              axis=1).astype(BF16)
        w_gp = w_in[i][:, n_lat:n_lat + d].astype(BF16)
        w_gm = w_in[i][:, n_lat + d:].astype(BF16)
        w_q = jnp.pad(w_q_b[i].reshape(Q_LORA, nh, QK_DIM),
                      ((0, 0), (0, 0), (0, HEAD_PAD - QK_DIM))).reshape(Q_LORA, nh * HEAD_PAD)
        w_qt = w_q.T.astype(BF16)
        w_kv = w_kv_b[i].reshape(KV_LORA, nh, QK_NOPE + V_DIM)
        w_k = w_kv[:, :, :QK_NOPE].reshape(KV_LORA, nh * QK_NOPE).astype(BF16)
        w_vt = w_kv[:, :, QK_NOPE:].reshape(KV_LORA, nh * V_DIM).T.astype(BF16)
        p_w = pool_w[i].astype(BF16)

        h1, h1m, w_gu2, w_dn2 = _ffn(
            h, row(norm_ffn1_pre[i]), row(norm_ffn1_post[i]), w_gu1, w_dn1, tm=tm_ffn, tf=tf,
            side=hm, casts=(ffn2_w_gu[i], ffn2_w_down[i]))

        u, cq, ckv, kpe, w_po, w_mo, w_o = _inproj(
            h1, row(norm_mix_pre[i]), w_b, row(q_a_norm[i]), row(kv_a_norm[i]), cos_r, sin_r,
            tm=tm, casts=(w_pool_o[i], w_mla_o[i], w_out[i]))
        um, cqm, ckvm, kpem = _inproj(h1m, row(norm_mix_pre[i]), w_b, row(q_a_norm[i]),
                                      row(kv_a_norm[i]), cos_m, sin_m, tm=N_META)
        qt, kn, vt = _qkv(cq, ckv, w_qt, w_k, w_vt, cos_tr, sin_tr, tm=tm, tk=tq, seq=seq)
        _, knm, vtm = _qkv(cqm, ckvm, w_qt, w_k, w_vt, cos_tm, sin_tm,
                           tm=N_META, tk=N_META, seq=N_META)

        attn = _attn(qt, kn.reshape(bsz, seq, nh * QK_NOPE),
                     kpe.reshape(bsz, seq, LANES), vt, knm, kpem,
                     vtm.reshape(nh * V_DIM, N_META), hp=4)

        h2 = _mixout(h1, row(norm_mix_pre[i]), row(norm_mix_post[i]), u, um, p_w,
                     row(pool_scale[i]), attn.reshape(bsz * seq, nh * V_DIM),
                     w_gp, w_gm, w_po, w_mo, w_o, tm=tm, tn=tn, seq=seq)

        h, = _ffn(h2, row(norm_ffn2_pre[i]), row(norm_ffn2_post[i]), w_gu2, w_dn2,
                  tm=tm_ffn, tf=tf)
        if i + 1 < depth:
            raise NotImplementedError("only DEPTH == 1 is supported")

    return h.reshape(bsz, seq, d)
```

```python
import functools
import math

import jax
import jax.numpy as jnp
from jax import lax
from jax.experimental import pallas as pl
from jax.experimental.pallas import tpu as pltpu

F32 = jnp.float32
BF16 = jnp.bfloat16

N_META = 16
POOL_WINDOWS = (2, 4, 8, 16)
POOL_GROUP = 256
POOL_WIDTH = POOL_GROUP * len(POOL_WINDOWS)
MLA_HEADS = 16
Q_LORA = 512
KV_LORA = 512
QK_NOPE = 128
QK_ROPE = 64
V_DIM = 128
QK_DIM = QK_NOPE + QK_ROPE
ROPE_THETA = 10000.0
EPS = 1e-6
LANES = 128
BF16_ROWS = 16
HEAD_PAD = 2 * LANES
Q_SCALE = (QK_DIM ** -0.5) * math.log2(math.e)
MASK_VALUE = -1e30

VMEM_LIMIT = 56 * 1024 * 1024
FFN_VMEM_LIMIT = 62 * 1024 * 1024
ROW_CHUNK = 256
SCORE_LEAD = 1


def _rms(x):
    return x * lax.rsqrt(jnp.mean(x * x, axis=-1, keepdims=True) + EPS)


def _rope(x, cos, sin):
    rot = pltpu.roll(x, 32, 1) - pltpu.roll(x, 96, 1)
    return x * cos + rot * sin


def _params(sem, vmem_limit=VMEM_LIMIT):
    return pltpu.CompilerParams(dimension_semantics=sem, vmem_limit_bytes=vmem_limit)


def _row_chunks(rows, chunk):
    chunk = min(chunk, rows)
    return [slice(r, r + chunk) for r in range(0, rows, chunk)]


def _ffn_kernel(*refs, n_casts, has_side):
    ns = int(has_side)
    x_ref, gpre_ref, gpost_ref, wg_ref, wu_ref, wd_ref = refs[:6]
    n_in = 6 + ns + n_casts
    cast_in = refs[6 + ns:n_in]
    o_ref = refs[n_in]
    cast_out = refs[n_in + 1 + ns:n_in + 1 + ns + n_casts]
    xn_ref = refs[n_in + 1 + ns + n_casts]
    f = pl.program_id(1)
    last = pl.num_programs(1) - 1
    chunks = _row_chunks(x_ref.shape[0], ROW_CHUNK)

    def mlp(xn):
        g = jnp.dot(xn, wg_ref[...], preferred_element_type=F32)
        u = jnp.dot(xn, wu_ref[...], preferred_element_type=F32)
        a = ((g * jax.nn.sigmoid(g)) * u).astype(BF16)
        return jnp.dot(a, wd_ref[...], preferred_element_type=F32)

    def casts():
        for src, dst in zip(cast_in, cast_out):
            dst[...] = src[...].astype(BF16)

    def side(step):
        if not has_side:
            return
        xs_ref, os_ref, xsn_ref = refs[6], refs[n_in + 1], refs[-1]

        @pl.when(pl.program_id(0) == 0)
        def _():
            if step == "first":
                xsn_ref[...] = (_rms(xs_ref[...]) * gpre_ref[...]).astype(BF16)
                os_ref[...] = mlp(xsn_ref[...])
            elif step == "middle":
                os_ref[...] += mlp(xsn_ref[...])
            else:
                y = os_ref[...] + mlp(xsn_ref[...])
                os_ref[...] = xs_ref[...] + 0.5 * (_rms(y) * gpost_ref[...])

    @pl.when(f == 0)
    def _():
        for n, rows in enumerate(chunks):
            xn = (_rms(x_ref[rows, :]) * gpre_ref[...]).astype(BF16)
            xn_ref[rows, :] = xn
            o_ref[rows, :] = mlp(xn)
            if n == 0:
                casts()
        side("first")

    @pl.when(jnp.logical_and(f > 0, f < last))
    def _():
        o_ref[...] += mlp(xn_ref[...])
        casts()
        side("middle")

    @pl.when(f == last)
    def _():
        for n, rows in enumerate(chunks):
            y = o_ref[rows, :] + mlp(xn_ref[rows, :])
            o_ref[rows, :] = x_ref[rows, :] + 0.5 * (_rms(y) * gpost_ref[...])
            if n == 0:
                casts()
        side("last")


def _ffn(x, g_pre, g_post, w_gu, w_down, *, tm, tf, side=None, casts=()):
    rows, d = x.shape
    d_ff = w_down.shape[0]
    nf = d_ff // tf
    ni = rows // tm
    const = lambda i, f: (0, 0)
    cast_specs = []
    for c in casts:
        r, k = c.shape
        if r % (ni * BF16_ROWS) == 0 and k % (nf * LANES) == 0:
            cast_specs.append(pl.BlockSpec((r // ni, k // nf), lambda i, f: (i, f)))
        else:
            s = max(v for v in range(1, nf + 1) if r % (ni * v * BF16_ROWS) == 0)
            cast_specs.append(pl.BlockSpec(
                (r // (ni * s), k), lambda i, f, s=s: (i * s + (f * s) // nf, 0)))
    sides = [] if side is None else [side]
    side_specs = [pl.BlockSpec(v.shape, const) for v in sides]
    return pl.pallas_call(
        functools.partial(_ffn_kernel, n_casts=len(casts), has_side=bool(sides)),
        grid=(ni, nf),
        in_specs=[
            pl.BlockSpec((tm, d), lambda i, f: (i, 0)),
            pl.BlockSpec((1, d), const),
            pl.BlockSpec((1, d), const),
            pl.BlockSpec((d, tf), lambda i, f: (0, f)),
            pl.BlockSpec((d, tf), lambda i, f: (0, f + nf)),
            pl.BlockSpec((tf, d), lambda i, f: (f, 0)),
        ] + side_specs + cast_specs,
        out_specs=[pl.BlockSpec((tm, d), lambda i, f: (i, 0))] + side_specs + cast_specs,
        out_shape=[jax.ShapeDtypeStruct((rows, d), F32)]
        + [jax.ShapeDtypeStruct(v.shape, F32) for v in sides]
        + [jax.ShapeDtypeStruct(c.shape, BF16) for c in casts],
        scratch_shapes=[pltpu.VMEM((tm, d), BF16)]
        + [pltpu.VMEM(v.shape, BF16) for v in sides],
        compiler_params=_params(("arbitrary", "arbitrary"), FFN_VMEM_LIMIT),
        name="ffn",
    )(x, g_pre, g_post, w_gu, w_gu, w_down, *sides, *casts)


def _inproj_kernel(*refs, n_casts):
    h_ref, gmix_ref, w_ref, gq_ref, gkv_ref, cos_ref, sin_ref = refs[:7]
    cast_in = refs[7:7 + n_casts]
    u_ref, cq_ref, ckv_ref, kpe_ref = refs[7 + n_casts:11 + n_casts]
    cast_out = refs[11 + n_casts:]
    c0, c1, c2, c3 = POOL_WIDTH, POOL_WIDTH + Q_LORA, POOL_WIDTH + Q_LORA + KV_LORA, w_ref.shape[1]
    for n, rows in enumerate(_row_chunks(h_ref.shape[0], ROW_CHUNK)):
        hn = (_rms(h_ref[rows, :]) * gmix_ref[...]).astype(BF16)
        u_ref[rows, :] = jnp.dot(hn, w_ref[:, :c0], preferred_element_type=F32)
        if n == 0:
            for src, dst in zip(cast_in, cast_out):
                dst[...] = src[...].astype(BF16)
        cq = jnp.dot(hn, w_ref[:, c0:c1], preferred_element_type=F32)
        cq_ref[rows, :] = (_rms(cq) * gq_ref[...]).astype(BF16)
        ckv = jnp.dot(hn, w_ref[:, c1:c2], preferred_element_type=F32)
        ckv_ref[rows, :] = (_rms(ckv) * gkv_ref[...]).astype(BF16)
        kr = jnp.dot(hn, w_ref[:, c2:c3], preferred_element_type=F32)
        kr = jnp.where(lax.broadcasted_iota(jnp.int32, kr.shape, 1) < QK_ROPE, kr, 0.0)
        kpe_ref[rows, :] = _rope(kr, cos_ref[rows, :], sin_ref[rows, :]).astype(BF16)


def _inproj(h, g_mix, w_in, g_q, g_kv, cos, sin, *, tm, casts=()):
    rows, d = h.shape
    n_pos = cos.shape[0] // tm
    wcols = POOL_WIDTH + Q_LORA + KV_LORA + LANES
    steps = rows // tm
    row = lambda i: (i, 0)
    const = lambda i: (0, 0)
    pos = lambda i: (i % n_pos, 0)
    cast_specs = [pl.BlockSpec((c.shape[0] // steps, c.shape[1]), row) for c in casts]
    assert all(c.shape[0] % (steps * BF16_ROWS) == 0 for c in casts)
    return pl.pallas_call(
        functools.partial(_inproj_kernel, n_casts=len(casts)),
        grid=(steps,),
        in_specs=[
            pl.BlockSpec((tm, d), row),
            pl.BlockSpec((1, d), const),
            pl.BlockSpec((d, wcols), const),
            pl.BlockSpec((1, Q_LORA), const),
            pl.BlockSpec((1, KV_LORA), const),
            pl.BlockSpec((tm, LANES), pos),
            pl.BlockSpec((tm, LANES), pos),
        ] + cast_specs,
        out_specs=[
            pl.BlockSpec((tm, POOL_WIDTH), row),
            pl.BlockSpec((tm, Q_LORA), row),
            pl.BlockSpec((tm, KV_LORA), row),
            pl.BlockSpec((tm, LANES), row),
        ] + cast_specs,
        out_shape=[
            jax.ShapeDtypeStruct((rows, POOL_WIDTH), F32),
            jax.ShapeDtypeStruct((rows, Q_LORA), BF16),
            jax.ShapeDtypeStruct((rows, KV_LORA), BF16),
            jax.ShapeDtypeStruct((rows, LANES), BF16),
        ] + [jax.ShapeDtypeStruct(c.shape, BF16) for c in casts],
        compiler_params=_params(("parallel",)),
        name="inproj",
    )(h, g_mix, w_in, g_q, g_kv, cos, sin, *casts)


def _qkv_kernel(cq_ref, ckv_ref, wqt_ref, wk_ref, wvt_ref, cos_ref, sin_ref,
                qt_ref, k_ref, vt_ref):
    cq = cq_ref[...]
    ckv = ckv_ref[...]
    nt = (((1,), (1,)), ((), ()))
    tk = vt_ref.shape[-1]
    chunks = [slice(c * tk, (c + 1) * tk) for c in range(vt_ref.shape[1])]
    cos = cos_ref[...] * Q_SCALE
    sin = sin_ref[...] * Q_SCALE
    half = QK_ROPE // 2
    qt = lax.dot_general(wqt_ref[...], cq, nt, preferred_element_type=F32)
    zeros = jnp.zeros((HEAD_PAD - QK_DIM, tk), BF16)
    for h in range(MLA_HEADS):
        lo = h * HEAD_PAD
        nope = qt[lo:lo + QK_NOPE] * Q_SCALE
        x1 = qt[lo + QK_NOPE:lo + QK_NOPE + half]
        x2 = qt[lo + QK_NOPE + half:lo + QK_DIM]
        r1 = x1 * cos - x2 * sin
        r2 = x2 * cos + x1 * sin
        for c, cols in enumerate(chunks):
            qt_ref[0, c, lo:lo + QK_NOPE, :] = nope[:, cols].astype(BF16)
            qt_ref[0, c, lo + QK_NOPE:lo + QK_NOPE + half, :] = r1[:, cols].astype(BF16)
            qt_ref[0, c, lo + QK_NOPE + half:lo + QK_DIM, :] = r2[:, cols].astype(BF16)
            qt_ref[0, c, lo + QK_DIM:lo + HEAD_PAD, :] = zeros
    k_ref[...] = jnp.dot(ckv, wk_ref[...], preferred_element_type=F32).astype(BF16)
    vt = lax.dot_general(wvt_ref[...], ckv, nt, preferred_element_type=F32)
    for c, cols in enumerate(chunks):
        vt_ref[0, c] = vt[:, cols].astype(BF16)


def _qkv(cq, ckv, w_qt, w_k, w_vt, cos_t, sin_t, *, tm, tk, seq):
    rows = cq.shape[0]
    n_pos = seq // tm
    nh = MLA_HEADS
    row = lambda i: (i, 0)
    const = lambda i: (0, 0)
    tiled = lambda i: (i // n_pos, i % n_pos, 0, 0)
    return pl.pallas_call(
        _qkv_kernel,
        grid=(rows // tm,),
        in_specs=[
            pl.BlockSpec((tm, Q_LORA), row),
            pl.BlockSpec((tm, KV_LORA), row),
            pl.BlockSpec((nh * HEAD_PAD, Q_LORA), const),
            pl.BlockSpec((KV_LORA, nh * QK_NOPE), const),
            pl.BlockSpec((nh * V_DIM, KV_LORA), const),
            pl.BlockSpec((QK_ROPE // 2, tm), lambda i: (0, i % n_pos)),
            pl.BlockSpec((QK_ROPE // 2, tm), lambda i: (0, i % n_pos)),
        ],
        out_specs=[
            pl.BlockSpec((1, tm // tk, nh * HEAD_PAD, tk), tiled),
            pl.BlockSpec((tm, nh * QK_NOPE), row),
            pl.BlockSpec((1, tm // tk, nh * V_DIM, tk), tiled),
        ],
        out_shape=[
            jax.ShapeDtypeStruct((rows // seq, seq // tk, nh * HEAD_PAD, tk), BF16),
            jax.ShapeDtypeStruct((rows, nh * QK_NOPE), BF16),
            jax.ShapeDtypeStruct((rows // seq, seq // tk, nh * V_DIM, tk), BF16),
        ],
        compiler_params=_params(("parallel",)),
        name="qkv",
    )(cq, ckv, w_qt, w_k, w_vt, cos_t, sin_t)


def _attn_kernel(qi_ref, kj_ref, new_ref, q_ref, kn_ref, kp_ref, vt_ref, knm_ref, kpm_ref,
                 vtm_ref, o_ref, acc_ref, sa_ref, sb_ref, st_ref):
    tk = vt_ref.shape[-1]
    tq = tk
    hp = acc_ref.shape[0]
    n_pairs = qi_ref.shape[0]
    bufs = ((sa_ref, 2), (sb_ref, 3))

    def q_of(h, qi):
        return q_ref[0, qi, h * HEAD_PAD:(h + 1) * HEAD_PAD, :]

    def init_tile(qi, heads):
        kpm = kpm_ref[...]
        for h in heads:
            km = jnp.concatenate([knm_ref[:, h * QK_NOPE:(h + 1) * QK_NOPE], kpm], axis=1)
            s = jnp.dot(km, q_of(h, qi), preferred_element_type=F32)
            m0 = jnp.max(s, axis=0, keepdims=True)
            p = jnp.exp2(s - m0)
            acc_ref[h] = jnp.dot(vtm_ref[h * V_DIM:(h + 1) * V_DIM, :], p.astype(BF16),
                                 preferred_element_type=F32)
            st_ref[h, 0] = m0
            st_ref[h, 1] = jnp.sum(p, axis=0, keepdims=True)

    def scores(t, dst, masked, heads):
        s_ref, row = dst
        qi = qi_ref[t]
        start = pl.multiple_of(kj_ref[t] * tk, tk)
        kp = kp_ref[0, pl.ds(start, tk), :]
        for h in heads:
            k = jnp.concatenate(
                [kn_ref[0, pl.ds(start, tk), h * QK_NOPE:(h + 1) * QK_NOPE], kp], axis=1)
            s = jnp.dot(k, q_of(h, qi), preferred_element_type=F32)
            if masked:
                kpos = lax.broadcasted_iota(jnp.int32, s.shape, 0)
                qpos = lax.broadcasted_iota(jnp.int32, s.shape, 1)
                s = jnp.where(kpos <= qpos, s, MASK_VALUE)
            s_ref[h] = s
            st_ref[h, row] = jnp.max(s, axis=0, keepdims=True)

    def absorb(t, src, heads):
        s_ref, row = src
        kj = kj_ref[t]
        ones = jnp.ones((BF16_ROWS, tk), BF16)
        for h in heads:
            m, l = st_ref[h, 0], st_ref[h, 1]
            m_new = jnp.maximum(m, st_ref[h, row])
            alpha = jnp.exp2(m - m_new)
            p = jnp.exp2(s_ref[h] - m_new)
            vt1 = jnp.concatenate([vt_ref[0, kj, h * V_DIM:(h + 1) * V_DIM, :], ones], axis=0)
            pv = jnp.dot(vt1, p.astype(BF16), preferred_element_type=F32)
            acc_ref[h] = alpha * acc_ref[h] + pv[:V_DIM]
            st_ref[h, 0] = m_new
            st_ref[h, 1] = alpha * l + pv[V_DIM:V_DIM + 1]

    def finalize(qi, heads):
        rows = pl.ds(pl.multiple_of(qi * tq, tq), tq)
        for h in heads:
            o_ref[0, rows, h * V_DIM:(h + 1) * V_DIM] = (
                acc_ref[h] / st_ref[h, 1]).T.astype(BF16)

    def run(stages):
        items = [(st, h) for st in stages for h in range(hp)]

        def put_scores(item):
            (t, _, dst, is_fresh), h = item
            scores(t, dst, is_fresh, [h])

        def put_rest(item):
            (t, src, _, is_fresh), h = item
            absorb(t - 1, src, [h])
            if is_fresh:
                finalize(qi_ref[t - 1], [h])
                init_tile(qi_ref[t], [h])

        for item in items[:SCORE_LEAD + 1]:
            put_scores(item)
        for n, item in enumerate(items):
            put_rest(item)
            if n + SCORE_LEAD + 1 < len(items):
                put_scores(items[n + SCORE_LEAD + 1])

    heads = list(range(hp))
    init_tile(qi_ref[0], heads)
    scores(0, bufs[0], True, heads)

    def body(r, carry):
        t = 2 * r + 1
        a, b = bufs
        for fresh0, fresh1 in ((True, False), (False, True), (False, False)):
            cond = jnp.logical_and(new_ref[t] == int(fresh0), new_ref[t + 1] == int(fresh1))
            pl.when(cond)(functools.partial(
                run, [(t, a, b, fresh0), (t + 1, b, a, fresh1)]))
        return carry

    lax.fori_loop(0, (n_pairs - 1) // 2, body, 0)
    last = bufs[0]
    if (n_pairs - 1) % 2:
        t = n_pairs - 1
        for is_fresh in (True, False):
            pl.when(new_ref[t] == int(is_fresh))(functools.partial(
                run, [(t, bufs[0], bufs[1], is_fresh)]))
        last = bufs[1]
    absorb(n_pairs - 1, last, heads)
    finalize(qi_ref[n_pairs - 1], heads)


def _attn(qt, k_nope, k_pe, vt, knm, kpm, vtm, *, hp):
    b, s, _ = k_nope.shape
    tk = vt.shape[-1]
    nq = s // tk
    nh = MLA_HEADS
    qi, kj, new = [], [], []
    for i in range(nq):
        for n, j in enumerate([i] + list(range(i))):
            qi.append(i)
            kj.append(j)
            new.append(int(n == 0))
    assert not any(new[t] and new[t + 1] for t in range(1, len(new) - 1, 2))
    tables = [jnp.asarray(v, jnp.int32) for v in (qi, kj, new)]
    grid_spec = pltpu.PrefetchScalarGridSpec(
        num_scalar_prefetch=len(tables),
        grid=(b, nh // hp),
        in_specs=[
            pl.BlockSpec((1, nq, hp * HEAD_PAD, tk), lambda bi, h, *_: (bi, 0, h, 0)),
            pl.BlockSpec((1, s, hp * QK_NOPE), lambda bi, h, *_: (bi, 0, h)),
            pl.BlockSpec((1, s, LANES), lambda bi, h, *_: (bi, 0, 0)),
            pl.BlockSpec((1, nq, hp * V_DIM, tk), lambda bi, h, *_: (bi, 0, h, 0)),
            pl.BlockSpec((N_META, hp * QK_NOPE), lambda bi, h, *_: (0, h)),
            pl.BlockSpec((N_META, LANES), lambda bi, h, *_: (0, 0)),
            pl.BlockSpec((hp * V_DIM, N_META), lambda bi, h, *_: (h, 0)),
        ],
        out_specs=pl.BlockSpec((1, s, hp * V_DIM), lambda bi, h, *_: (bi, 0, h)),
        scratch_shapes=[pltpu.VMEM((hp, V_DIM, tk), F32), pltpu.VMEM((hp, tk, tk), F32),
                        pltpu.VMEM((hp, tk, tk), F32), pltpu.VMEM((hp, 4, 1, tk), F32)],
    )
    return pl.pallas_call(
        _attn_kernel,
        grid_spec=grid_spec,
        out_shape=jax.ShapeDtypeStruct((b, s, nh * V_DIM), BF16),
        compiler_params=_params(("parallel", "parallel")),
        name="attn",
    )(*tables, qt, k_nope, k_pe, vt, knm, kpm, vtm)


def _mixout_kernel(h_ref, gmix_ref, gpost_ref, u_ref, uprev_ref, umeta_ref, pw_ref, pscale_ref,
                   attn_ref, wgp_ref, wgm_ref, wpo_ref, wmo_ref, wout_ref, o_ref,
                   hn_ref, pool_ref, buf_ref, *, tiles_per_seq):
    c = pl.program_id(1)
    last = pl.num_programs(1) - 1
    chunks = _row_chunks(h_ref.shape[0], ROW_CHUNK)

    def pool(rows, halo):
        for g, win in enumerate(POOL_WINDOWS):
            cols = slice(g * POOL_GROUP, (g + 1) * POOL_GROUP)
            buf_ref[:N_META, :] = halo[:, cols]
            buf_ref[N_META:, :] = u_ref[rows, cols]
            tot = buf_ref[...]
            shift = 1
            while shift < win:
                tot = tot + pltpu.roll(tot, shift, 0)
                shift *= 2
            dlt = (tot[N_META:, :] * (1.0 / win) - buf_ref[N_META:, :]).astype(BF16)
            y = jnp.dot(dlt, pw_ref[g], preferred_element_type=F32)
            pool_ref[rows, cols] = (y * pscale_ref[:, cols]).astype(BF16)

    def mix(hn, rows):
        g_pool = jnp.dot(hn, wgp_ref[...], preferred_element_type=F32)
        g_mla = jnp.dot(hn, wgm_ref[...], preferred_element_type=F32)
        y_pool = jnp.dot(pool_ref[rows, :], wpo_ref[...], preferred_element_type=F32)
        y_mla = jnp.dot(attn_ref[rows, :], wmo_ref[...], preferred_element_type=F32)
        y = (jax.nn.sigmoid(g_pool) * y_pool + jax.nn.sigmoid(g_mla) * y_mla).astype(BF16)
        return jnp.dot(y, wout_ref[...], preferred_element_type=F32)

    @pl.when(c == 0)
    def _():
        seq_start = pl.program_id(0) % tiles_per_seq == 0
        for n, rows in enumerate(chunks):
            if n == 0:
                halo = jnp.where(seq_start, umeta_ref[...], uprev_ref[...])
            else:
                halo = u_ref[rows.start - N_META:rows.start, :]
            pool(rows, halo)
            hn = (_rms(h_ref[rows, :]) * gmix_ref[...]).astype(BF16)
            hn_ref[rows, :] = hn
            o_ref[rows, :] = mix(hn, rows)

    @pl.when(jnp.logical_and(c > 0, c < last))
    def _():
        o_ref[...] += mix(hn_ref[...], slice(None))

    @pl.when(c == last)
    def _():
        for rows in chunks:
            y = o_ref[rows, :] + mix(hn_ref[rows, :], rows)
            o_ref[rows, :] = h_ref[rows, :] + _rms(y) * gpost_ref[...]


def _mixout(h, g_mix, g_post, u, u_meta, pool_w, pool_scale, attn,
            w_gp, w_gm, w_po, w_mo, w_out, *, tm, tn, seq):
    rows, d = h.shape
    pw = u.shape[1]
    hb = tm // N_META
    chunk = min(ROW_CHUNK, tm)
    row = lambda i, c: (i, 0)
    const = lambda i, c: (0, 0)
    col = lambda i, c: (0, c)
    return pl.pallas_call(
        functools.partial(_mixout_kernel, tiles_per_seq=seq // tm),
        grid=(rows // tm, d // tn),
        in_specs=[
            pl.BlockSpec((tm, d), row),
            pl.BlockSpec((1, d), const),
            pl.BlockSpec((1, d), const),
            pl.BlockSpec((tm, pw), row),
            pl.BlockSpec((N_META, pw), lambda i, c: (jnp.maximum(i * hb - 1, 0), 0)),
            pl.BlockSpec((N_META, pw), const),
            pl.BlockSpec((len(POOL_WINDOWS), POOL_GROUP, POOL_GROUP), lambda i, c: (0, 0, 0)),
            pl.BlockSpec((1, pw), const),
            pl.BlockSpec((tm, attn.shape[1]), row),
            pl.BlockSpec((d, tn), col),
            pl.BlockSpec((d, tn), col),
            pl.BlockSpec((pw, tn), col),
            pl.BlockSpec((attn.shape[1], tn), col),
            pl.BlockSpec((tn, d), lambda i, c: (c, 0)),
        ],
        out_specs=pl.BlockSpec((tm, d), row),
        out_shape=jax.ShapeDtypeStruct((rows, d), F32),
        scratch_shapes=[pltpu.VMEM((tm, d), BF16), pltpu.VMEM((tm, pw), BF16),
                        pltpu.VMEM((chunk + N_META, POOL_GROUP), F32)],
        compiler_params=_params(("parallel", "arbitrary")),
        name="mixout",
    )(h, g_mix, g_post, u, u, u_meta, pool_w, pool_scale, attn, w_gp, w_gm, w_po, w_mo, w_out)


def _rope_tables(n_pos):
    pos = jnp.arange(n_pos, dtype=F32)
    inv = ROPE_THETA ** (-jnp.arange(0, QK_ROPE, 2, dtype=F32) / QK_ROPE)
    ang = pos[:, None] * inv[None, :]
    ang_t = inv[:, None] * pos[None, :]
    ang2 = jnp.concatenate([ang, ang], axis=-1)
    pad = jnp.zeros((n_pos, LANES - QK_ROPE), F32)
    return (jnp.concatenate([jnp.cos(ang2), pad], axis=-1),
            jnp.concatenate([jnp.sin(ang2), pad], axis=-1), jnp.cos(ang_t), jnp.sin(ang_t))


def kernel(x, meta_tokens, norm_ffn1_pre, norm_ffn1_post, ffn1_w_gu, ffn1_w_down, norm_mix_pre, norm_mix_post, w_in, pool_w, pool_scale, w_pool_o, q_a_norm, w_q_b, kv_a_norm, w_kv_b, w_mla_o, w_out, norm_ffn2_pre, norm_ffn2_post, ffn2_w_gu, ffn2_w_down):
    bsz, seq, d = x.shape
    depth = w_in.shape[0]
    nh = MLA_HEADS
    tm, tf, tn, tq = 512, 512, 512, 512
    tm_ffn = 1024

    cos, sin, cos_t, sin_t = _rope_tables(N_META + seq)
    cos_m, sin_m, cos_r, sin_r = cos[:N_META], sin[:N_META], cos[N_META:], sin[N_META:]
    cos_tm, sin_tm, cos_tr, sin_tr = (cos_t[:, :N_META], sin_t[:, :N_META],
                                      cos_t[:, N_META:], sin_t[:, N_META:])

    h = x.reshape(bsz * seq, d)
    hm = meta_tokens.astype(x.dtype)
    row = lambda v: v.reshape(1, -1)

    for i in range(depth):
        w_gu1, w_dn1 = ffn1_w_gu[i].astype(BF16), ffn1_w_down[i].astype(BF16)
        w_q = jnp.pad(w_q_b[i].reshape(Q_LORA, nh, QK_DIM),
                      ((0, 0), (0, 0), (0, HEAD_PAD - QK_DIM))).reshape(Q_LORA, nh * HEAD_PAD)
        w_qt = w_q.T.astype(BF16)
        w_kv = w_kv_b[i].reshape(KV_LORA, nh, QK_NOPE + V_DIM)
        w_k = w_kv[:, :, :QK_NOPE].reshape(KV_LORA, nh * QK_NOPE).astype(BF16)
        w_vt = w_kv[:, :, QK_NOPE:].reshape(KV_LORA, nh * V_DIM).T.astype(BF16)
        p_w = pool_w[i].astype(BF16)

        h1, h1m, w_gu2, w_dn2, w_in_bf = _ffn(
            h, row(norm_ffn1_pre[i]), row(norm_ffn1_post[i]), w_gu1, w_dn1, tm=tm_ffn, tf=tf,
            side=hm, casts=(ffn2_w_gu[i], ffn2_w_down[i], w_in[i]))
        n_lat = POOL_WIDTH + Q_LORA + KV_LORA + QK_ROPE
        w_gp, w_gm = w_in_bf[:, n_lat:n_lat + d], w_in_bf[:, n_lat + d:]

        u, cq, ckv, kpe, w_po, w_mo, w_o = _inproj(
            h1, row(norm_mix_pre[i]), w_in_bf, row(q_a_norm[i]), row(kv_a_norm[i]), cos_r, sin_r,
            tm=tm, casts=(w_pool_o[i], w_mla_o[i], w_out[i]))
        um, cqm, ckvm, kpem = _inproj(h1m, row(norm_mix_pre[i]), w_in_bf, row(q_a_norm[i]),
                                      row(kv_a_norm[i]), cos_m, sin_m, tm=N_META)
        qt, kn, vt = _qkv(cq, ckv, w_qt, w_k, w_vt, cos_tr, sin_tr, tm=tm, tk=tq, seq=seq)
        _, knm, vtm = _qkv(cqm, ckvm, w_qt, w_k, w_vt, cos_tm, sin_tm,
                           tm=N_META, tk=N_META, seq=N_META)

        attn = _attn(qt, kn.reshape(bsz, seq, nh * QK_NOPE),
                     kpe.reshape(bsz, seq, LANES), vt, knm, kpem,
                     vtm.reshape(nh * V_DIM, N_META), hp=4)

        h2 = _mixout(h1, row(norm_mix_pre[i]), row(norm_mix_post[i]), u, um, p_w,
                     row(pool_scale[i]), attn.reshape(bsz * seq, nh * V_DIM),
                     w_gp, w_gm, w_po, w_mo, w_o, tm=tm, tn=tn, seq=seq)

        h, = _ffn(h2, row(norm_ffn2_pre[i]), row(norm_ffn2_post[i]), w_gu2, w_dn2,
                  tm=tm_ffn, tf=tf)
        if i + 1 < depth:
            raise NotImplementedError("only DEPTH == 1 is supported")

    return h.reshape(bsz, seq, d)
```

```python
import functools
import math

import jax
import jax.numpy as jnp
from jax import lax
from jax.experimental import pallas as pl
from jax.experimental.pallas import tpu as pltpu

F32 = jnp.float32
BF16 = jnp.bfloat16

N_META = 16
POOL_WINDOWS = (2, 4, 8, 16)
POOL_GROUP = 256
POOL_WIDTH = POOL_GROUP * len(POOL_WINDOWS)
MLA_HEADS = 16
Q_LORA = 512
KV_LORA = 512
QK_NOPE = 128
QK_ROPE = 64
V_DIM = 128
QK_DIM = QK_NOPE + QK_ROPE
ROPE_THETA = 10000.0
EPS = 1e-6
LANES = 128
BF16_ROWS = 16
HEAD_PAD = 2 * LANES
Q_SCALE = (QK_DIM ** -0.5) * math.log2(math.e)
MASK_VALUE = -1e30

VMEM_LIMIT = 56 * 1024 * 1024
FFN_VMEM_LIMIT = 62 * 1024 * 1024
ROW_CHUNK = 256
SCORE_LEAD = 0


def _rms(x):
    return x * lax.rsqrt(jnp.mean(x * x, axis=-1, keepdims=True) + EPS)


def _rope(x, cos, sin):
    rot = pltpu.roll(x, 32, 1) - pltpu.roll(x, 96, 1)
    return x * cos + rot * sin


def _params(sem, vmem_limit=VMEM_LIMIT):
    return pltpu.CompilerParams(dimension_semantics=sem, vmem_limit_bytes=vmem_limit)


def _row_chunks(rows, chunk):
    chunk = min(chunk, rows)
    return [slice(r, r + chunk) for r in range(0, rows, chunk)]


def _ffn_kernel(*refs, n_casts, has_side):
    ns = int(has_side)
    x_ref, gpre_ref, gpost_ref, wg_ref, wu_ref, wd_ref = refs[:6]
    n_in = 6 + ns + n_casts
    cast_in = refs[6 + ns:n_in]
    o_ref = refs[n_in]
    cast_out = refs[n_in + 1 + ns:n_in + 1 + ns + n_casts]
    xn_ref = refs[n_in + 1 + ns + n_casts]
    f = pl.program_id(1)
    last = pl.num_programs(1) - 1
    chunks = _row_chunks(x_ref.shape[0], ROW_CHUNK)

    def mlp(xn):
        g = jnp.dot(xn, wg_ref[...], preferred_element_type=F32)
        u = jnp.dot(xn, wu_ref[...], preferred_element_type=F32)
        a = ((g * jax.nn.sigmoid(g)) * u).astype(BF16)
        return jnp.dot(a, wd_ref[...], preferred_element_type=F32)

    def casts():
        for src, dst in zip(cast_in, cast_out):
            dst[...] = src[...].astype(BF16)

    def side(step):
        if not has_side:
            return
        xs_ref, os_ref, xsn_ref = refs[6], refs[n_in + 1], refs[-1]

        @pl.when(pl.program_id(0) == 0)
        def _():
            if step == "first":
                xsn_ref[...] = (_rms(xs_ref[...]) * gpre_ref[...]).astype(BF16)
                os_ref[...] = mlp(xsn_ref[...])
            elif step == "middle":
                os_ref[...] += mlp(xsn_ref[...])
            else:
                y = os_ref[...] + mlp(xsn_ref[...])
                os_ref[...] = xs_ref[...] + 0.5 * (_rms(y) * gpost_ref[...])

    @pl.when(f == 0)
    def _():
        for n, rows in enumerate(chunks):
            xn = (_rms(x_ref[rows, :]) * gpre_ref[...]).astype(BF16)
            xn_ref[rows, :] = xn
            o_ref[rows, :] = mlp(xn)
            if n == 0:
                casts()
        side("first")

    @pl.when(jnp.logical_and(f > 0, f < last))
    def _():
        o_ref[...] += mlp(xn_ref[...])
        casts()
        side("middle")

    @pl.when(f == last)
    def _():
        for n, rows in enumerate(chunks):
            y = o_ref[rows, :] + mlp(xn_ref[rows, :])
            o_ref[rows, :] = x_ref[rows, :] + 0.5 * (_rms(y) * gpost_ref[...])
            if n == 0:
                casts()
        side("last")


def _ffn(x, g_pre, g_post, w_gu, w_down, *, tm, tf, side=None, casts=()):
    rows, d = x.shape
    d_ff = w_down.shape[0]
    nf = d_ff // tf
    ni = rows // tm
    const = lambda i, f: (0, 0)
    cast_specs = []
    for c in casts:
        r, k = c.shape
        if r % (ni * BF16_ROWS) == 0 and k % (nf * LANES) == 0:
            cast_specs.append(pl.BlockSpec((r // ni, k // nf), lambda i, f: (i, f)))
        else:
            s = max(v for v in range(1, nf + 1) if r % (ni * v * BF16_ROWS) == 0)
            cast_specs.append(pl.BlockSpec(
                (r // (ni * s), k), lambda i, f, s=s: (i * s + (f * s) // nf, 0)))
    sides = [] if side is None else [side]
    side_specs = [pl.BlockSpec(v.shape, const) for v in sides]
    return pl.pallas_call(
        functools.partial(_ffn_kernel, n_casts=len(casts), has_side=bool(sides)),
        grid=(ni, nf),
        in_specs=[
            pl.BlockSpec((tm, d), lambda i, f: (i, 0)),
            pl.BlockSpec((1, d), const),
            pl.BlockSpec((1, d), const),
            pl.BlockSpec((d, tf), lambda i, f: (0, f)),
            pl.BlockSpec((d, tf), lambda i, f: (0, f + nf)),
            pl.BlockSpec((tf, d), lambda i, f: (f, 0)),
        ] + side_specs + cast_specs,
        out_specs=[pl.BlockSpec((tm, d), lambda i, f: (i, 0))] + side_specs + cast_specs,
        out_shape=[jax.ShapeDtypeStruct((rows, d), F32)]
        + [jax.ShapeDtypeStruct(v.shape, F32) for v in sides]
        + [jax.ShapeDtypeStruct(c.shape, BF16) for c in casts],
        scratch_shapes=[pltpu.VMEM((tm, d), BF16)]
        + [pltpu.VMEM(v.shape, BF16) for v in sides],
        compiler_params=_params(("arbitrary", "arbitrary"), FFN_VMEM_LIMIT),
        name="ffn",
    )(x, g_pre, g_post, w_gu, w_gu, w_down, *sides, *casts)


def _inproj_kernel(*refs, n_casts):
    h_ref, gmix_ref, w_ref, gq_ref, gkv_ref, cos_ref, sin_ref = refs[:7]
    cast_in = refs[7:7 + n_casts]
    u_ref, cq_ref, ckv_ref, kpe_ref = refs[7 + n_casts:11 + n_casts]
    cast_out = refs[11 + n_casts:]
    c0, c1, c2, c3 = POOL_WIDTH, POOL_WIDTH + Q_LORA, POOL_WIDTH + Q_LORA + KV_LORA, w_ref.shape[1]
    for n, rows in enumerate(_row_chunks(h_ref.shape[0], ROW_CHUNK)):
        hn = (_rms(h_ref[rows, :]) * gmix_ref[...]).astype(BF16)
        u_ref[rows, :] = jnp.dot(hn, w_ref[:, :c0], preferred_element_type=F32)
        if n == 0:
            for src, dst in zip(cast_in, cast_out):
                dst[...] = src[...].astype(BF16)
        cq = jnp.dot(hn, w_ref[:, c0:c1], preferred_element_type=F32)
        cq_ref[rows, :] = (_rms(cq) * gq_ref[...]).astype(BF16)
        ckv = jnp.dot(hn, w_ref[:, c1:c2], preferred_element_type=F32)
        ckv_ref[rows, :] = (_rms(ckv) * gkv_ref[...]).astype(BF16)
        kr = jnp.dot(hn, w_ref[:, c2:c3], preferred_element_type=F32)
        kr = jnp.where(lax.broadcasted_iota(jnp.int32, kr.shape, 1) < QK_ROPE, kr, 0.0)
        kpe_ref[rows, :] = _rope(kr, cos_ref[rows, :], sin_ref[rows, :]).astype(BF16)


def _inproj(h, g_mix, w_in, g_q, g_kv, cos, sin, *, tm, casts=()):
    rows, d = h.shape
    n_pos = cos.shape[0] // tm
    wcols = POOL_WIDTH + Q_LORA + KV_LORA + LANES
    steps = rows // tm
    row = lambda i: (i, 0)
    const = lambda i: (0, 0)
    pos = lambda i: (i % n_pos, 0)
    cast_specs = [pl.BlockSpec((c.shape[0] // steps, c.shape[1]), row) for c in casts]
    assert all(c.shape[0] % (steps * BF16_ROWS) == 0 for c in casts)
    return pl.pallas_call(
        functools.partial(_inproj_kernel, n_casts=len(casts)),
        grid=(steps,),
        in_specs=[
            pl.BlockSpec((tm, d), row),
            pl.BlockSpec((1, d), const),
            pl.BlockSpec((d, wcols), const),
            pl.BlockSpec((1, Q_LORA), const),
            pl.BlockSpec((1, KV_LORA), const),
            pl.BlockSpec((tm, LANES), pos),
            pl.BlockSpec((tm, LANES), pos),
        ] + cast_specs,
        out_specs=[
            pl.BlockSpec((tm, POOL_WIDTH), row),
            pl.BlockSpec((tm, Q_LORA), row),
            pl.BlockSpec((tm, KV_LORA), row),
            pl.BlockSpec((tm, LANES), row),
        ] + cast_specs,
        out_shape=[
            jax.ShapeDtypeStruct((rows, POOL_WIDTH), F32),
            jax.ShapeDtypeStruct((rows, Q_LORA), BF16),
            jax.ShapeDtypeStruct((rows, KV_LORA), BF16),
            jax.ShapeDtypeStruct((rows, LANES), BF16),
        ] + [jax.ShapeDtypeStruct(c.shape, BF16) for c in casts],
        compiler_params=_params(("parallel",)),
        name="inproj",
    )(h, g_mix, w_in, g_q, g_kv, cos, sin, *casts)


def _qkv_kernel(cq_ref, ckv_ref, wqt_ref, wk_ref, wvt_ref, cos_ref, sin_ref,
                qt_ref, k_ref, vt_ref):
    cq = cq_ref[...]
    ckv = ckv_ref[...]
    nt = (((1,), (1,)), ((), ()))
    tk = vt_ref.shape[-1]
    chunks = [slice(c * tk, (c + 1) * tk) for c in range(vt_ref.shape[1])]
    cos = cos_ref[...] * Q_SCALE
    sin = sin_ref[...] * Q_SCALE
    half = QK_ROPE // 2
    qt = lax.dot_general(wqt_ref[...], cq, nt, preferred_element_type=F32)
    zeros = jnp.zeros((HEAD_PAD - QK_DIM, tk), BF16)
    for h in range(MLA_HEADS):
        lo = h * HEAD_PAD
        nope = qt[lo:lo + QK_NOPE] * Q_SCALE
        x1 = qt[lo + QK_NOPE:lo + QK_NOPE + half]
        x2 = qt[lo + QK_NOPE + half:lo + QK_DIM]
        r1 = x1 * cos - x2 * sin
        r2 = x2 * cos + x1 * sin
        for c, cols in enumerate(chunks):
            qt_ref[0, c, lo:lo + QK_NOPE, :] = nope[:, cols].astype(BF16)
            qt_ref[0, c, lo + QK_NOPE:lo + QK_NOPE + half, :] = r1[:, cols].astype(BF16)
            qt_ref[0, c, lo + QK_NOPE + half:lo + QK_DIM, :] = r2[:, cols].astype(BF16)
            qt_ref[0, c, lo + QK_DIM:lo + HEAD_PAD, :] = zeros
    k_ref[...] = jnp.dot(ckv, wk_ref[...], preferred_element_type=F32).astype(BF16)
    vt = lax.dot_general(wvt_ref[...], ckv, nt, preferred_element_type=F32)
    for c, cols in enumerate(chunks):
        vt_ref[0, c] = vt[:, cols].astype(BF16)


def _qkv(cq, ckv, w_qt, w_k, w_vt, cos_t, sin_t, *, tm, tk, seq):
    rows = cq.shape[0]
    n_pos = seq // tm
    nh = MLA_HEADS
    row = lambda i: (i, 0)
    const = lambda i: (0, 0)
    tiled = lambda i: (i // n_pos, i % n_pos, 0, 0)
    return pl.pallas_call(
        _qkv_kernel,
        grid=(rows // tm,),
        in_specs=[
            pl.BlockSpec((tm, Q_LORA), row),
            pl.BlockSpec((tm, KV_LORA), row),
            pl.BlockSpec((nh * HEAD_PAD, Q_LORA), const),
            pl.BlockSpec((KV_LORA, nh * QK_NOPE), const),
            pl.BlockSpec((nh * V_DIM, KV_LORA), const),
            pl.BlockSpec((QK_ROPE // 2, tm), lambda i: (0, i % n_pos)),
            pl.BlockSpec((QK_ROPE // 2, tm), lambda i: (0, i % n_pos)),
        ],
        out_specs=[
            pl.BlockSpec((1, tm // tk, nh * HEAD_PAD, tk), tiled),
            pl.BlockSpec((tm, nh * QK_NOPE), row),
            pl.BlockSpec((1, tm // tk, nh * V_DIM, tk), tiled),
        ],
        out_shape=[
            jax.ShapeDtypeStruct((rows // seq, seq // tk, nh * HEAD_PAD, tk), BF16),
            jax.ShapeDtypeStruct((rows, nh * QK_NOPE), BF16),
            jax.ShapeDtypeStruct((rows // seq, seq // tk, nh * V_DIM, tk), BF16),
        ],
        compiler_params=_params(("parallel",)),
        name="qkv",
    )(cq, ckv, w_qt, w_k, w_vt, cos_t, sin_t)


def _attn_kernel(qi_ref, kj_ref, new_ref, q_ref, kn_ref, kp_ref, vt_ref, knm_ref, kpm_ref,
                 vtm_ref, o_ref, acc_ref, sa_ref, sb_ref, st_ref):
    tk = vt_ref.shape[-1]
    tq = tk
    hp = acc_ref.shape[0]
    n_pairs = qi_ref.shape[0]
    bufs = ((sa_ref, 2), (sb_ref, 3))

    def q_of(h, qi):
        return q_ref[0, qi, h * HEAD_PAD:(h + 1) * HEAD_PAD, :]

    def init_tile(qi, heads):
        kpm = kpm_ref[...]
        for h in heads:
            km = jnp.concatenate([knm_ref[:, h * QK_NOPE:(h + 1) * QK_NOPE], kpm], axis=1)
            s = jnp.dot(km, q_of(h, qi), preferred_element_type=F32)
            m0 = jnp.max(s, axis=0, keepdims=True)
            p = jnp.exp2(s - m0)
            acc_ref[h] = jnp.dot(vtm_ref[h * V_DIM:(h + 1) * V_DIM, :], p.astype(BF16),
                                 preferred_element_type=F32)
            st_ref[h, 0] = m0
            st_ref[h, 1] = jnp.sum(p, axis=0, keepdims=True)

    def scores(t, dst, masked, heads):
        s_ref, row = dst
        qi = qi_ref[t]
        start = pl.multiple_of(kj_ref[t] * tk, tk)
        kp = kp_ref[0, pl.ds(start, tk), :]
        for h in heads:
            k = jnp.concatenate(
                [kn_ref[0, pl.ds(start, tk), h * QK_NOPE:(h + 1) * QK_NOPE], kp], axis=1)
            s = jnp.dot(k, q_of(h, qi), preferred_element_type=F32)
            if masked:
                kpos = lax.broadcasted_iota(jnp.int32, s.shape, 0)
                qpos = lax.broadcasted_iota(jnp.int32, s.shape, 1)
                s = jnp.where(kpos <= qpos, s, MASK_VALUE)
            s_ref[h] = s
            st_ref[h, row] = jnp.max(s, axis=0, keepdims=True)

    def absorb(t, src, heads):
        s_ref, row = src
        kj = kj_ref[t]
        ones = jnp.ones((BF16_ROWS, tk), BF16)
        for h in heads:
            m, l = st_ref[h, 0], st_ref[h, 1]
            m_new = jnp.maximum(m, st_ref[h, row])
            alpha = jnp.exp2(m - m_new)
            p = jnp.exp2(s_ref[h] - m_new)
            vt1 = jnp.concatenate([vt_ref[0, kj, h * V_DIM:(h + 1) * V_DIM, :], ones], axis=0)
            pv = jnp.dot(vt1, p.astype(BF16), preferred_element_type=F32)
            acc_ref[h] = alpha * acc_ref[h] + pv[:V_DIM]
            st_ref[h, 0] = m_new
            st_ref[h, 1] = alpha * l + pv[V_DIM:V_DIM + 1]

    def finalize(qi, heads):
        rows = pl.ds(pl.multiple_of(qi * tq, tq), tq)
        for h in heads:
            o_ref[0, rows, h * V_DIM:(h + 1) * V_DIM] = (
                acc_ref[h] / st_ref[h, 1]).T.astype(BF16)

    def run(stages):
        items = [(st, h) for st in stages for h in range(hp)]

        def put_scores(item):
            (t, _, dst, is_fresh), h = item
            scores(t, dst, is_fresh, [h])

        def put_rest(item):
            (t, src, _, is_fresh), h = item
            absorb(t - 1, src, [h])
            if is_fresh:
                finalize(qi_ref[t - 1], [h])
                init_tile(qi_ref[t], [h])

        for item in items[:SCORE_LEAD + 1]:
            put_scores(item)
        for n, item in enumerate(items):
            put_rest(item)
            if n + SCORE_LEAD + 1 < len(items):
                put_scores(items[n + SCORE_LEAD + 1])

    heads = list(range(hp))
    init_tile(qi_ref[0], heads)
    scores(0, bufs[0], True, heads)

    def body(r, carry):
        t = 2 * r + 1
        a, b = bufs
        for fresh0, fresh1 in ((True, False), (False, True), (False, False)):
            cond = jnp.logical_and(new_ref[t] == int(fresh0), new_ref[t + 1] == int(fresh1))
            pl.when(cond)(functools.partial(
                run, [(t, a, b, fresh0), (t + 1, b, a, fresh1)]))
        return carry

    lax.fori_loop(0, (n_pairs - 1) // 2, body, 0)
    last = bufs[0]
    if (n_pairs - 1) % 2:
        t = n_pairs - 1
        for is_fresh in (True, False):
            pl.when(new_ref[t] == int(is_fresh))(functools.partial(
                run, [(t, bufs[0], bufs[1], is_fresh)]))
        last = bufs[1]
    absorb(n_pairs - 1, last, heads)
    finalize(qi_ref[n_pairs - 1], heads)


def _attn(qt, k_nope, k_pe, vt, knm, kpm, vtm, *, hp):
    b, s, _ = k_nope.shape
    tk = vt.shape[-1]
    nq = s // tk
    nh = MLA_HEADS
    qi, kj, new = [], [], []
    for i in range(nq):
        for n, j in enumerate([i] + list(range(i))):
            qi.append(i)
            kj.append(j)
            new.append(int(n == 0))
    assert not any(new[t] and new[t + 1] for t in range(1, len(new) - 1, 2))
    tables = [jnp.asarray(v, jnp.int32) for v in (qi, kj, new)]
    grid_spec = pltpu.PrefetchScalarGridSpec(
        num_scalar_prefetch=len(tables),
        grid=(b, nh // hp),
        in_specs=[
            pl.BlockSpec((1, nq, hp * HEAD_PAD, tk), lambda bi, h, *_: (bi, 0, h, 0)),
            pl.BlockSpec((1, s, hp * QK_NOPE), lambda bi, h, *_: (bi, 0, h)),
            pl.BlockSpec((1, s, LANES), lambda bi, h, *_: (bi, 0, 0)),
            pl.BlockSpec((1, nq, hp * V_DIM, tk), lambda bi, h, *_: (bi, 0, h, 0)),
            pl.BlockSpec((N_META, hp * QK_NOPE), lambda bi, h, *_: (0, h)),
            pl.BlockSpec((N_META, LANES), lambda bi, h, *_: (0, 0)),
            pl.BlockSpec((hp * V_DIM, N_META), lambda bi, h, *_: (h, 0)),
        ],
        out_specs=pl.BlockSpec((1, s, hp * V_DIM), lambda bi, h, *_: (bi, 0, h)),
        scratch_shapes=[pltpu.VMEM((hp, V_DIM, tk), F32), pltpu.VMEM((hp, tk, tk), F32),
                        pltpu.VMEM((hp, tk, tk), F32), pltpu.VMEM((hp, 4, 1, tk), F32)],
    )
    return pl.pallas_call(
        _attn_kernel,
        grid_spec=grid_spec,
        out_shape=jax.ShapeDtypeStruct((b, s, nh * V_DIM), BF16),
        compiler_params=_params(("parallel", "parallel")),
        name="attn",
    )(*tables, qt, k_nope, k_pe, vt, knm, kpm, vtm)


def _mixout_kernel(h_ref, gmix_ref, gpost_ref, u_ref, uprev_ref, umeta_ref, pw_ref, pscale_ref,
                   attn_ref, wgp_ref, wgm_ref, wpo_ref, wmo_ref, wout_ref, o_ref,
                   hn_ref, pool_ref, buf_ref, *, tiles_per_seq):
    c = pl.program_id(1)
    last = pl.num_programs(1) - 1
    chunks = _row_chunks(h_ref.shape[0], ROW_CHUNK)

    def pool(rows, halo):
        for g, win in enumerate(POOL_WINDOWS):
            cols = slice(g * POOL_GROUP, (g + 1) * POOL_GROUP)
            buf_ref[:N_META, :] = halo[:, cols]
            buf_ref[N_META:, :] = u_ref[rows, cols]
            tot = buf_ref[...]
            shift = 1
            while shift < win:
                tot = tot + pltpu.roll(tot, shift, 0)
                shift *= 2
            dlt = (tot[N_META:, :] * (1.0 / win) - buf_ref[N_META:, :]).astype(BF16)
            y = jnp.dot(dlt, pw_ref[g], preferred_element_type=F32)
            pool_ref[rows, cols] = (y * pscale_ref[:, cols]).astype(BF16)

    def mix(hn, rows):
        g_pool = jnp.dot(hn, wgp_ref[...], preferred_element_type=F32)
        g_mla = jnp.dot(hn, wgm_ref[...], preferred_element_type=F32)
        y_pool = jnp.dot(pool_ref[rows, :], wpo_ref[...], preferred_element_type=F32)
        y_mla = jnp.dot(attn_ref[rows, :], wmo_ref[...], preferred_element_type=F32)
        y = (jax.nn.sigmoid(g_pool) * y_pool + jax.nn.sigmoid(g_mla) * y_mla).astype(BF16)
        return jnp.dot(y, wout_ref[...], preferred_element_type=F32)

    @pl.when(c == 0)
    def _():
        seq_start = pl.program_id(0) % tiles_per_seq == 0
        for n, rows in enumerate(chunks):
            if n == 0:
                halo = jnp.where(seq_start, umeta_ref[...], uprev_ref[...])
            else:
                halo = u_ref[rows.start - N_META:rows.start, :]
            pool(rows, halo)
            hn = (_rms(h_ref[rows, :]) * gmix_ref[...]).astype(BF16)
            hn_ref[rows, :] = hn
            o_ref[rows, :] = mix(hn, rows)

    @pl.when(jnp.logical_and(c > 0, c < last))
    def _():
        o_ref[...] += mix(hn_ref[...], slice(None))

    @pl.when(c == last)
    def _():
        for rows in chunks:
            y = o_ref[rows, :] + mix(hn_ref[rows, :], rows)
            o_ref[rows, :] = h_ref[rows, :] + _rms(y) * gpost_ref[...]


def _mixout(h, g_mix, g_post, u, u_meta, pool_w, pool_scale, attn,
            w_gp, w_gm, w_po, w_mo, w_out, *, tm, tn, seq):
    rows, d = h.shape
    pw = u.shape[1]
    hb = tm // N_META
    chunk = min(ROW_CHUNK, tm)
    row = lambda i, c: (i, 0)
    const = lambda i, c: (0, 0)
    col = lambda i, c: (0, c)
    return pl.pallas_call(
        functools.partial(_mixout_kernel, tiles_per_seq=seq // tm),
        grid=(rows // tm, d // tn),
        in_specs=[
            pl.BlockSpec((tm, d), row),
            pl.BlockSpec((1, d), const),
            pl.BlockSpec((1, d), const),
            pl.BlockSpec((tm, pw), row),
            pl.BlockSpec((N_META, pw), lambda i, c: (jnp.maximum(i * hb - 1, 0), 0)),
            pl.BlockSpec((N_META, pw), const),
            pl.BlockSpec((len(POOL_WINDOWS), POOL_GROUP, POOL_GROUP), lambda i, c: (0, 0, 0)),
            pl.BlockSpec((1, pw), const),
            pl.BlockSpec((tm, attn.shape[1]), row),
            pl.BlockSpec((d, tn), col),
            pl.BlockSpec((d, tn), col),
            pl.BlockSpec((pw, tn), col),
            pl.BlockSpec((attn.shape[1], tn), col),
            pl.BlockSpec((tn, d), lambda i, c: (c, 0)),
        ],
        out_specs=pl.BlockSpec((tm, d), row),
        out_shape=jax.ShapeDtypeStruct((rows, d), F32),
        scratch_shapes=[pltpu.VMEM((tm, d), BF16), pltpu.VMEM((tm, pw), BF16),
                        pltpu.VMEM((chunk + N_META, POOL_GROUP), F32)],
        compiler_params=_params(("parallel", "arbitrary")),
        name="mixout",
    )(h, g_mix, g_post, u, u, u_meta, pool_w, pool_scale, attn, w_gp, w_gm, w_po, w_mo, w_out)


def _rope_tables(n_pos):
    pos = jnp.arange(n_pos, dtype=F32)
    inv = ROPE_THETA ** (-jnp.arange(0, QK_ROPE, 2, dtype=F32) / QK_ROPE)
    ang = pos[:, None] * inv[None, :]
    ang_t = inv[:, None] * pos[None, :]
    ang2 = jnp.concatenate([ang, ang], axis=-1)
    pad = jnp.zeros((n_pos, LANES - QK_ROPE), F32)
    return (jnp.concatenate([jnp.cos(ang2), pad], axis=-1),
            jnp.concatenate([jnp.sin(ang2), pad], axis=-1), jnp.cos(ang_t), jnp.sin(ang_t))


def kernel(x, meta_tokens, norm_ffn1_pre, norm_ffn1_post, ffn1_w_gu, ffn1_w_down, norm_mix_pre, norm_mix_post, w_in, pool_w, pool_scale, w_pool_o, q_a_norm, w_q_b, kv_a_norm, w_kv_b, w_mla_o, w_out, norm_ffn2_pre, norm_ffn2_post, ffn2_w_gu, ffn2_w_down):
    bsz, seq, d = x.shape
    depth = w_in.shape[0]
    nh = MLA_HEADS
    tm, tf, tn, tq = 512, 512, 512, 512
    tm_ffn = 1024

    cos, sin, cos_t, sin_t = _rope_tables(N_META + seq)
    cos_m, sin_m, cos_r, sin_r = cos[:N_META], sin[:N_META], cos[N_META:], sin[N_META:]
    cos_tm, sin_tm, cos_tr, sin_tr = (cos_t[:, :N_META], sin_t[:, :N_META],
                                      cos_t[:, N_META:], sin_t[:, N_META:])

    h = x.reshape(bsz * seq, d)
    hm = meta_tokens.astype(x.dtype)
    row = lambda v: v.reshape(1, -1)

    for i in range(depth):
        w_gu1, w_dn1 = ffn1_w_gu[i].astype(BF16), ffn1_w_down[i].astype(BF16)
        w_q = jnp.pad(w_q_b[i].reshape(Q_LORA, nh, QK_DIM),
                      ((0, 0), (0, 0), (0, HEAD_PAD - QK_DIM))).reshape(Q_LORA, nh * HEAD_PAD)
        w_qt = w_q.T.astype(BF16)
        w_kv = w_kv_b[i].reshape(KV_LORA, nh, QK_NOPE + V_DIM)
        w_k = w_kv[:, :, :QK_NOPE].reshape(KV_LORA, nh * QK_NOPE).astype(BF16)
        w_vt = w_kv[:, :, QK_NOPE:].reshape(KV_LORA, nh * V_DIM).T.astype(BF16)
        p_w = pool_w[i].astype(BF16)

        h1, h1m, w_gu2, w_dn2 = _ffn(
            h, row(norm_ffn1_pre[i]), row(norm_ffn1_post[i]), w_gu1, w_dn1, tm=tm_ffn, tf=tf,
            side=hm, casts=(ffn2_w_gu[i], ffn2_w_down[i]))
        w_in_bf = w_in[i].astype(BF16)
        n_lat = POOL_WIDTH + Q_LORA + KV_LORA + QK_ROPE
        w_gp, w_gm = w_in_bf[:, n_lat:n_lat + d], w_in_bf[:, n_lat + d:]

        u, cq, ckv, kpe, w_po, w_mo, w_o = _inproj(
            h1, row(norm_mix_pre[i]), w_in_bf, row(q_a_norm[i]), row(kv_a_norm[i]), cos_r, sin_r,
            tm=tm, casts=(w_pool_o[i], w_mla_o[i], w_out[i]))
        um, cqm, ckvm, kpem = _inproj(h1m, row(norm_mix_pre[i]), w_in_bf, row(q_a_norm[i]),
                                      row(kv_a_norm[i]), cos_m, sin_m, tm=N_META)
        qt, kn, vt = _qkv(cq, ckv, w_qt, w_k, w_vt, cos_tr, sin_tr, tm=tm, tk=tq, seq=seq)
        _, knm, vtm = _qkv(cqm, ckvm, w_qt, w_k, w_vt, cos_tm, sin_tm,
                           tm=N_META, tk=N_META, seq=N_META)

        attn = _attn(qt, kn.reshape(bsz, seq, nh * QK_NOPE),
                     kpe.reshape(bsz, seq, LANES), vt, knm, kpem,
                     vtm.reshape(nh * V_DIM, N_META), hp=4)

        h2 = _mixout(h1, row(norm_mix_pre[i]), row(norm_mix_post[i]), u, um, p_w,
                     row(pool_scale[i]), attn.reshape(bsz * seq, nh * V_DIM),
                     w_gp, w_gm, w_po, w_mo, w_o, tm=tm, tn=tn, seq=seq)

        h, = _ffn(h2, row(norm_ffn2_pre[i]), row(norm_ffn2_post[i]), w_gu2, w_dn2,
                  tm=tm_ffn, tf=tf)
        if i + 1 < depth:
            raise NotImplementedError("only DEPTH == 1 is supported")

    return h.reshape(bsz, seq, d)
```

```python
import functools
import math

import jax
import jax.numpy as jnp
from jax import lax
from jax.experimental import pallas as pl
from jax.experimental.pallas import tpu as pltpu

F32 = jnp.float32
BF16 = jnp.bfloat16

N_META = 16
POOL_WINDOWS = (2, 4, 8, 16)
POOL_GROUP = 256
POOL_WIDTH = POOL_GROUP * len(POOL_WINDOWS)
MLA_HEADS = 16
Q_LORA = 512
KV_LORA = 512
QK_NOPE = 128
QK_ROPE = 64
V_DIM = 128
QK_DIM = QK_NOPE + QK_ROPE
ROPE_THETA = 10000.0
EPS = 1e-6
LANES = 128
BF16_ROWS = 16
HEAD_PAD = 2 * LANES
Q_SCALE = (QK_DIM ** -0.5) * math.log2(math.e)
MASK_VALUE = -1e30

VMEM_LIMIT = 56 * 1024 * 1024
FFN_VMEM_LIMIT = 62 * 1024 * 1024
ROW_CHUNK = 256
SCORE_LEAD = 0


def _rms(x):
    return x * lax.rsqrt(jnp.mean(x * x, axis=-1, keepdims=True) + EPS)


def _rope(x, cos, sin):
    rot = pltpu.roll(x, 32, 1) - pltpu.roll(x, 96, 1)
    return x * cos + rot * sin


def _params(sem, vmem_limit=VMEM_LIMIT):
    return pltpu.CompilerParams(dimension_semantics=sem, vmem_limit_bytes=vmem_limit)


def _row_chunks(rows, chunk):
    chunk = min(chunk, rows)
    return [slice(r, r + chunk) for r in range(0, rows, chunk)]


def _ffn_kernel(*refs, n_casts, has_side):
    ns = int(has_side)
    x_ref, gpre_ref, gpost_ref, wg_ref, wu_ref, wd_ref = refs[:6]
    n_in = 6 + ns + n_casts
    cast_in = refs[6 + ns:n_in]
    o_ref = refs[n_in]
    cast_out = refs[n_in + 1 + ns:n_in + 1 + ns + n_casts]
    xn_ref = refs[n_in + 1 + ns + n_casts]
    f = pl.program_id(1)
    last = pl.num_programs(1) - 1
    chunks = _row_chunks(x_ref.shape[0], ROW_CHUNK)

    def mlp(xn):
        g = jnp.dot(xn, wg_ref[...], preferred_element_type=F32)
        u = jnp.dot(xn, wu_ref[...], preferred_element_type=F32)
        a = ((g * jax.nn.sigmoid(g)) * u).astype(BF16)
        return jnp.dot(a, wd_ref[...], preferred_element_type=F32)

    def casts():
        for src, dst in zip(cast_in, cast_out):
            dst[...] = src[...].astype(BF16)

    def side(step):
        if not has_side:
            return
        xs_ref, os_ref, xsn_ref = refs[6], refs[n_in + 1], refs[-1]

        @pl.when(pl.program_id(0) == 0)
        def _():
            if step == "first":
                xsn_ref[...] = (_rms(xs_ref[...]) * gpre_ref[...]).astype(BF16)
                os_ref[...] = mlp(xsn_ref[...])
            elif step == "middle":
                os_ref[...] += mlp(xsn_ref[...])
            else:
                y = os_ref[...] + mlp(xsn_ref[...])
                os_ref[...] = xs_ref[...] + 0.5 * (_rms(y) * gpost_ref[...])

    @pl.when(f == 0)
    def _():
        for n, rows in enumerate(chunks):
            xn = (_rms(x_ref[rows, :]) * gpre_ref[...]).astype(BF16)
            xn_ref[rows, :] = xn
            o_ref[rows, :] = mlp(xn)
            if n == 0:
                casts()
        side("first")

    @pl.when(jnp.logical_and(f > 0, f < last))
    def _():
        o_ref[...] += mlp(xn_ref[...])
        casts()
        side("middle")

    @pl.when(f == last)
    def _():
        for n, rows in enumerate(chunks):
            y = o_ref[rows, :] + mlp(xn_ref[rows, :])
            o_ref[rows, :] = x_ref[rows, :] + 0.5 * (_rms(y) * gpost_ref[...])
            if n == 0:
                casts()
        side("last")


def _ffn(x, g_pre, g_post, w_gu, w_down, *, tm, tf, side=None, casts=()):
    rows, d = x.shape
    d_ff = w_down.shape[0]
    nf = d_ff // tf
    ni = rows // tm
    const = lambda i, f: (0, 0)
    cast_specs = []
    for c in casts:
        r, k = c.shape
        if r % (ni * BF16_ROWS) == 0 and k % (nf * LANES) == 0:
            cast_specs.append(pl.BlockSpec((r // ni, k // nf), lambda i, f: (i, f)))
        else:
            s = max(v for v in range(1, nf + 1) if r % (ni * v * BF16_ROWS) == 0)
            cast_specs.append(pl.BlockSpec(
                (r // (ni * s), k), lambda i, f, s=s: (i * s + (f * s) // nf, 0)))
    sides = [] if side is None else [side]
    side_specs = [pl.BlockSpec(v.shape, const) for v in sides]
    return pl.pallas_call(
        functools.partial(_ffn_kernel, n_casts=len(casts), has_side=bool(sides)),
        grid=(ni, nf),
        in_specs=[
            pl.BlockSpec((tm, d), lambda i, f: (i, 0)),
            pl.BlockSpec((1, d), const),
            pl.BlockSpec((1, d), const),
            pl.BlockSpec((d, tf), lambda i, f: (0, f)),
            pl.BlockSpec((d, tf), lambda i, f: (0, f + nf)),
            pl.BlockSpec((tf, d), lambda i, f: (f, 0)),
        ] + side_specs + cast_specs,
        out_specs=[pl.BlockSpec((tm, d), lambda i, f: (i, 0))] + side_specs + cast_specs,
        out_shape=[jax.ShapeDtypeStruct((rows, d), F32)]
        + [jax.ShapeDtypeStruct(v.shape, F32) for v in sides]
        + [jax.ShapeDtypeStruct(c.shape, BF16) for c in casts],
        scratch_shapes=[pltpu.VMEM((tm, d), BF16)]
        + [pltpu.VMEM(v.shape, BF16) for v in sides],
        compiler_params=_params(("arbitrary", "arbitrary"), FFN_VMEM_LIMIT),
        name="ffn",
    )(x, g_pre, g_post, w_gu, w_gu, w_down, *sides, *casts)


def _inproj_kernel(*refs, n_casts):
    h_ref, gmix_ref, w_ref, gq_ref, gkv_ref, cos_ref, sin_ref = refs[:7]
    cast_in = refs[7:7 + n_casts]
    u_ref, cq_ref, ckv_ref, kpe_ref = refs[7 + n_casts:11 + n_casts]
    cast_out = refs[11 + n_casts:]
    c0, c1, c2, c3 = POOL_WIDTH, POOL_WIDTH + Q_LORA, POOL_WIDTH + Q_LORA + KV_LORA, w_ref.shape[1]
    for n, rows in enumerate(_row_chunks(h_ref.shape[0], ROW_CHUNK)):
        hn = (_rms(h_ref[rows, :]) * gmix_ref[...]).astype(BF16)
        u_ref[rows, :] = jnp.dot(hn, w_ref[:, :c0], preferred_element_type=F32)
        if n == 0:
            for src, dst in zip(cast_in, cast_out):
                dst[...] = src[...].astype(BF16)
        cq = jnp.dot(hn, w_ref[:, c0:c1], preferred_element_type=F32)
        cq_ref[rows, :] = (_rms(cq) * gq_ref[...]).astype(BF16)
        ckv = jnp.dot(hn, w_ref[:, c1:c2], preferred_element_type=F32)
        ckv_ref[rows, :] = (_rms(ckv) * gkv_ref[...]).astype(BF16)
        kr = jnp.dot(hn, w_ref[:, c2:c3], preferred_element_type=F32)
        kr = jnp.where(lax.broadcasted_iota(jnp.int32, kr.shape, 1) < QK_ROPE, kr, 0.0)
        kpe_ref[rows, :] = _rope(kr, cos_ref[rows, :], sin_ref[rows, :]).astype(BF16)


def _inproj(h, g_mix, w_in, g_q, g_kv, cos, sin, *, tm, casts=()):
    rows, d = h.shape
    n_pos = cos.shape[0] // tm
    wcols = POOL_WIDTH + Q_LORA + KV_LORA + LANES
    steps = rows // tm
    row = lambda i: (i, 0)
    const = lambda i: (0, 0)
    pos = lambda i: (i % n_pos, 0)
    cast_specs = [pl.BlockSpec((c.shape[0] // steps, c.shape[1]), row) for c in casts]
    assert all(c.shape[0] % (steps * BF16_ROWS) == 0 for c in casts)
    return pl.pallas_call(
        functools.partial(_inproj_kernel, n_casts=len(casts)),
        grid=(steps,),
        in_specs=[
            pl.BlockSpec((tm, d), row),
            pl.BlockSpec((1, d), const),
            pl.BlockSpec((d, wcols), const),
            pl.BlockSpec((1, Q_LORA), const),
            pl.BlockSpec((1, KV_LORA), const),
            pl.BlockSpec((tm, LANES), pos),
            pl.BlockSpec((tm, LANES), pos),
        ] + cast_specs,
        out_specs=[
            pl.BlockSpec((tm, POOL_WIDTH), row),
            pl.BlockSpec((tm, Q_LORA), row),
            pl.BlockSpec((tm, KV_LORA), row),
            pl.BlockSpec((tm, LANES), row),
        ] + cast_specs,
        out_shape=[
            jax.ShapeDtypeStruct((rows, POOL_WIDTH), F32),
            jax.ShapeDtypeStruct((rows, Q_LORA), BF16),
            jax.ShapeDtypeStruct((rows, KV_LORA), BF16),
            jax.ShapeDtypeStruct((rows, LANES), BF16),
        ] + [jax.ShapeDtypeStruct(c.shape, BF16) for c in casts],
        compiler_params=_params(("parallel",)),
        name="inproj",
    )(h, g_mix, w_in, g_q, g_kv, cos, sin, *casts)


def _qkv_kernel(cq_ref, ckv_ref, wqt_ref, wk_ref, wvt_ref, cos_ref, sin_ref,
                qt_ref, k_ref, vt_ref):
    cq = cq_ref[...]
    ckv = ckv_ref[...]
    nt = (((1,), (1,)), ((), ()))
    tk = vt_ref.shape[-1]
    chunks = [slice(c * tk, (c + 1) * tk) for c in range(vt_ref.shape[1])]
    cos = cos_ref[...] * Q_SCALE
    sin = sin_ref[...] * Q_SCALE
    half = QK_ROPE // 2
    qt = lax.dot_general(wqt_ref[...], cq, nt, preferred_element_type=F32)
    zeros = jnp.zeros((HEAD_PAD - QK_DIM, tk), BF16)
    for h in range(MLA_HEADS):
        lo = h * HEAD_PAD
        nope = qt[lo:lo + QK_NOPE] * Q_SCALE
        x1 = qt[lo + QK_NOPE:lo + QK_NOPE + half]
        x2 = qt[lo + QK_NOPE + half:lo + QK_DIM]
        r1 = x1 * cos - x2 * sin
        r2 = x2 * cos + x1 * sin
        for c, cols in enumerate(chunks):
            qt_ref[0, c, lo:lo + QK_NOPE, :] = nope[:, cols].astype(BF16)
            qt_ref[0, c, lo + QK_NOPE:lo + QK_NOPE + half, :] = r1[:, cols].astype(BF16)
            qt_ref[0, c, lo + QK_NOPE + half:lo + QK_DIM, :] = r2[:, cols].astype(BF16)
            qt_ref[0, c, lo + QK_DIM:lo + HEAD_PAD, :] = zeros
    k_ref[...] = jnp.dot(ckv, wk_ref[...], preferred_element_type=F32).astype(BF16)
    vt = lax.dot_general(wvt_ref[...], ckv, nt, preferred_element_type=F32)
    for c, cols in enumerate(chunks):
        vt_ref[0, c] = vt[:, cols].astype(BF16)


def _qkv(cq, ckv, w_qt, w_k, w_vt, cos_t, sin_t, *, tm, tk, seq):
    rows = cq.shape[0]
    n_pos = seq // tm
    nh = MLA_HEADS
    row = lambda i: (i, 0)
    const = lambda i: (0, 0)
    tiled = lambda i: (i // n_pos, i % n_pos, 0, 0)
    return pl.pallas_call(
        _qkv_kernel,
        grid=(rows // tm,),
        in_specs=[
            pl.BlockSpec((tm, Q_LORA), row),
            pl.BlockSpec((tm, KV_LORA), row),
            pl.BlockSpec((nh * HEAD_PAD, Q_LORA), const),
            pl.BlockSpec((KV_LORA, nh * QK_NOPE), const),
            pl.BlockSpec((nh * V_DIM, KV_LORA), const),
            pl.BlockSpec((QK_ROPE // 2, tm), lambda i: (0, i % n_pos)),
            pl.BlockSpec((QK_ROPE // 2, tm), lambda i: (0, i % n_pos)),
        ],
        out_specs=[
            pl.BlockSpec((1, tm // tk, nh * HEAD_PAD, tk), tiled),
            pl.BlockSpec((tm, nh * QK_NOPE), row),
            pl.BlockSpec((1, tm // tk, nh * V_DIM, tk), tiled),
        ],
        out_shape=[
            jax.ShapeDtypeStruct((rows // seq, seq // tk, nh * HEAD_PAD, tk), BF16),
            jax.ShapeDtypeStruct((rows, nh * QK_NOPE), BF16),
            jax.ShapeDtypeStruct((rows // seq, seq // tk, nh * V_DIM, tk), BF16),
        ],
        compiler_params=_params(("parallel",)),
        name="qkv",
    )(cq, ckv, w_qt, w_k, w_vt, cos_t, sin_t)


def _attn_kernel(qi_ref, kj_ref, new_ref, q_ref, kn_ref, kp_ref, vt_ref, knm_ref, kpm_ref,
                 vtm_ref, o_ref, acc_ref, sa_ref, sb_ref, st_ref, pm_ref):
    tk = vt_ref.shape[-1]
    tq = tk
    hp = acc_ref.shape[0]
    n_pairs = qi_ref.shape[0]
    bufs = ((sa_ref, 2), (sb_ref, 3))

    def q_of(h, qi):
        return q_ref[0, qi, h * HEAD_PAD:(h + 1) * HEAD_PAD, :]

    def init_scores(qi, heads):
        kpm = kpm_ref[...]
        for h in heads:
            km = jnp.concatenate([knm_ref[:, h * QK_NOPE:(h + 1) * QK_NOPE], kpm], axis=1)
            s = jnp.dot(km, q_of(h, qi), preferred_element_type=F32)
            m0 = jnp.max(s, axis=0, keepdims=True)
            p = jnp.exp2(s - m0)
            pm_ref[h] = p.astype(BF16)
            st_ref[h, 0] = m0
            st_ref[h, 1] = jnp.sum(p, axis=0, keepdims=True)

    def init_values(heads):
        for h in heads:
            acc_ref[h] = jnp.dot(vtm_ref[h * V_DIM:(h + 1) * V_DIM, :], pm_ref[h],
                                 preferred_element_type=F32)

    def scores(t, dst, masked, heads):
        s_ref, row = dst
        qi = qi_ref[t]
        start = pl.multiple_of(kj_ref[t] * tk, tk)
        kp = kp_ref[0, pl.ds(start, tk), :]
        for h in heads:
            k = jnp.concatenate(
                [kn_ref[0, pl.ds(start, tk), h * QK_NOPE:(h + 1) * QK_NOPE], kp], axis=1)
            s = jnp.dot(k, q_of(h, qi), preferred_element_type=F32)
            if masked:
                kpos = lax.broadcasted_iota(jnp.int32, s.shape, 0)
                qpos = lax.broadcasted_iota(jnp.int32, s.shape, 1)
                s = jnp.where(kpos <= qpos, s, MASK_VALUE)
            s_ref[h] = s
            st_ref[h, row] = jnp.max(s, axis=0, keepdims=True)

    def absorb(t, src, heads):
        s_ref, row = src
        kj = kj_ref[t]
        ones = jnp.ones((BF16_ROWS, tk), BF16)
        for h in heads:
            m, l = st_ref[h, 0], st_ref[h, 1]
            m_new = jnp.maximum(m, st_ref[h, row])
            alpha = jnp.exp2(m - m_new)
            p = jnp.exp2(s_ref[h] - m_new)
            vt1 = jnp.concatenate([vt_ref[0, kj, h * V_DIM:(h + 1) * V_DIM, :], ones], axis=0)
            pv = jnp.dot(vt1, p.astype(BF16), preferred_element_type=F32)
            acc_ref[h] = alpha * acc_ref[h] + pv[:V_DIM]
            st_ref[h, 0] = m_new
            st_ref[h, 1] = alpha * l + pv[V_DIM:V_DIM + 1]

    def finalize(qi, heads):
        rows = pl.ds(pl.multiple_of(qi * tq, tq), tq)
        for h in heads:
            o_ref[0, rows, h * V_DIM:(h + 1) * V_DIM] = (
                acc_ref[h] / st_ref[h, 1]).T.astype(BF16)

    def run(stages):
        items = [(st, h) for st in stages for h in range(hp)]

        def put_scores(item):
            (t, _, dst, is_fresh), h = item
            scores(t, dst, is_fresh, [h])

        pending = []

        def put_rest(item):
            (t, src, _, is_fresh), h = item
            absorb(t - 1, src, [h])
            if is_fresh:
                finalize(qi_ref[t - 1], [h])
                init_scores(qi_ref[t], [h])
            init_values(pending)
            pending.clear()
            if is_fresh:
                pending.append(h)

        for item in items[:SCORE_LEAD + 1]:
            put_scores(item)
        for n, item in enumerate(items):
            put_rest(item)
            if n + SCORE_LEAD + 1 < len(items):
                put_scores(items[n + SCORE_LEAD + 1])
        init_values(pending)

    heads = list(range(hp))
    init_scores(qi_ref[0], heads)
    init_values(heads)
    scores(0, bufs[0], True, heads)

    def body(r, carry):
        t = 2 * r + 1
        a, b = bufs
        for fresh0, fresh1 in ((True, False), (False, True), (False, False)):
            cond = jnp.logical_and(new_ref[t] == int(fresh0), new_ref[t + 1] == int(fresh1))
            pl.when(cond)(functools.partial(
                run, [(t, a, b, fresh0), (t + 1, b, a, fresh1)]))
        return carry

    lax.fori_loop(0, (n_pairs - 1) // 2, body, 0)
    last = bufs[0]
    if (n_pairs - 1) % 2:
        t = n_pairs - 1
        for is_fresh in (True, False):
            pl.when(new_ref[t] == int(is_fresh))(functools.partial(
                run, [(t, bufs[0], bufs[1], is_fresh)]))
        last = bufs[1]
    absorb(n_pairs - 1, last, heads)
    finalize(qi_ref[n_pairs - 1], heads)


def _attn(qt, k_nope, k_pe, vt, knm, kpm, vtm, *, hp):
    b, s, _ = k_nope.shape
    tk = vt.shape[-1]
    nq = s // tk
    nh = MLA_HEADS
    qi, kj, new = [], [], []
    for i in range(nq):
        for n, j in enumerate([i] + list(range(i))):
            qi.append(i)
            kj.append(j)
            new.append(int(n == 0))
    assert not any(new[t] and new[t + 1] for t in range(1, len(new) - 1, 2))
    tables = [jnp.asarray(v, jnp.int32) for v in (qi, kj, new)]
    grid_spec = pltpu.PrefetchScalarGridSpec(
        num_scalar_prefetch=len(tables),
        grid=(b, nh // hp),
        in_specs=[
            pl.BlockSpec((1, nq, hp * HEAD_PAD, tk), lambda bi, h, *_: (bi, 0, h, 0)),
            pl.BlockSpec((1, s, hp * QK_NOPE), lambda bi, h, *_: (bi, 0, h)),
            pl.BlockSpec((1, s, LANES), lambda bi, h, *_: (bi, 0, 0)),
            pl.BlockSpec((1, nq, hp * V_DIM, tk), lambda bi, h, *_: (bi, 0, h, 0)),
            pl.BlockSpec((N_META, hp * QK_NOPE), lambda bi, h, *_: (0, h)),
            pl.BlockSpec((N_META, LANES), lambda bi, h, *_: (0, 0)),
            pl.BlockSpec((hp * V_DIM, N_META), lambda bi, h, *_: (h, 0)),
        ],
        out_specs=pl.BlockSpec((1, s, hp * V_DIM), lambda bi, h, *_: (bi, 0, h)),
        scratch_shapes=[pltpu.VMEM((hp, V_DIM, tk), F32), pltpu.VMEM((hp, tk, tk), F32),
                        pltpu.VMEM((hp, tk, tk), F32), pltpu.VMEM((hp, 4, 1, tk), F32),
                        pltpu.VMEM((hp, N_META, tk), BF16)],
    )
    return pl.pallas_call(
        _attn_kernel,
        grid_spec=grid_spec,
        out_shape=jax.ShapeDtypeStruct((b, s, nh * V_DIM), BF16),
        compiler_params=_params(("parallel", "parallel")),
        name="attn",
    )(*tables, qt, k_nope, k_pe, vt, knm, kpm, vtm)


def _mixout_kernel(h_ref, gmix_ref, gpost_ref, u_ref, uprev_ref, umeta_ref, pw_ref, pscale_ref,
                   attn_ref, wgp_ref, wgm_ref, wpo_ref, wmo_ref, wout_ref, o_ref,
                   hn_ref, pool_ref, buf_ref, *, tiles_per_seq):
    c = pl.program_id(1)
    last = pl.num_programs(1) - 1
    chunks = _row_chunks(h_ref.shape[0], ROW_CHUNK)

    def pool(rows, halo):
        for g, win in enumerate(POOL_WINDOWS):
            cols = slice(g * POOL_GROUP, (g + 1) * POOL_GROUP)
            buf_ref[:N_META, :] = halo[:, cols]
            buf_ref[N_META:, :] = u_ref[rows, cols]
            tot = buf_ref[...]
            shift = 1
            while shift < win:
                tot = tot + pltpu.roll(tot, shift, 0)
                shift *= 2
            dlt = (tot[N_META:, :] * (1.0 / win) - buf_ref[N_META:, :]).astype(BF16)
            y = jnp.dot(dlt, pw_ref[g], preferred_element_type=F32)
            pool_ref[rows, cols] = (y * pscale_ref[:, cols]).astype(BF16)

    def mix(hn, rows):
        g_pool = jnp.dot(hn, wgp_ref[...], preferred_element_type=F32)
        g_mla = jnp.dot(hn, wgm_ref[...], preferred_element_type=F32)
        y_pool = jnp.dot(pool_ref[rows, :], wpo_ref[...], preferred_element_type=F32)
        y_mla = jnp.dot(attn_ref[rows, :], wmo_ref[...], preferred_element_type=F32)
        y = (jax.nn.sigmoid(g_pool) * y_pool + jax.nn.sigmoid(g_mla) * y_mla).astype(BF16)
        return jnp.dot(y, wout_ref[...], preferred_element_type=F32)

    @pl.when(c == 0)
    def _():
        seq_start = pl.program_id(0) % tiles_per_seq == 0
        for n, rows in enumerate(chunks):
            if n == 0:
                halo = jnp.where(seq_start, umeta_ref[...], uprev_ref[...])
            else:
                halo = u_ref[rows.start - N_META:rows.start, :]
            pool(rows, halo)
            hn = (_rms(h_ref[rows, :]) * gmix_ref[...]).astype(BF16)
            hn_ref[rows, :] = hn
            o_ref[rows, :] = mix(hn, rows)

    @pl.when(jnp.logical_and(c > 0, c < last))
    def _():
        o_ref[...] += mix(hn_ref[...], slice(None))

    @pl.when(c == last)
    def _():
        for rows in chunks:
            y = o_ref[rows, :] + mix(hn_ref[rows, :], rows)
            o_ref[rows, :] = h_ref[rows, :] + _rms(y) * gpost_ref[...]


def _mixout(h, g_mix, g_post, u, u_meta, pool_w, pool_scale, attn,
            w_gp, w_gm, w_po, w_mo, w_out, *, tm, tn, seq):
    rows, d = h.shape
    pw = u.shape[1]
    hb = tm // N_META
    chunk = min(ROW_CHUNK, tm)
    row = lambda i, c: (i, 0)
    const = lambda i, c: (0, 0)
    col = lambda i, c: (0, c)
    return pl.pallas_call(
        functools.partial(_mixout_kernel, tiles_per_seq=seq // tm),
        grid=(rows // tm, d // tn),
        in_specs=[
            pl.BlockSpec((tm, d), row),
            pl.BlockSpec((1, d), const),
            pl.BlockSpec((1, d), const),
            pl.BlockSpec((tm, pw), row),
            pl.BlockSpec((N_META, pw), lambda i, c: (jnp.maximum(i * hb - 1, 0), 0)),
            pl.BlockSpec((N_META, pw), const),
            pl.BlockSpec((len(POOL_WINDOWS), POOL_GROUP, POOL_GROUP), lambda i, c: (0, 0, 0)),
            pl.BlockSpec((1, pw), const),
            pl.BlockSpec((tm, attn.shape[1]), row),
            pl.BlockSpec((d, tn), col),
            pl.BlockSpec((d, tn), col),
            pl.BlockSpec((pw, tn), col),
            pl.BlockSpec((attn.shape[1], tn), col),
            pl.BlockSpec((tn, d), lambda i, c: (c, 0)),
        ],
        out_specs=pl.BlockSpec((tm, d), row),
        out_shape=jax.ShapeDtypeStruct((rows, d), F32),
        scratch_shapes=[pltpu.VMEM((tm, d), BF16), pltpu.VMEM((tm, pw), BF16),
                        pltpu.VMEM((chunk + N_META, POOL_GROUP), F32)],
        compiler_params=_params(("parallel", "arbitrary")),
        name="mixout",
    )(h, g_mix, g_post, u, u, u_meta, pool_w, pool_scale, attn, w_gp, w_gm, w_po, w_mo, w_out)


def _rope_tables(n_pos):
    pos = jnp.arange(n_pos, dtype=F32)
    inv = ROPE_THETA ** (-jnp.arange(0, QK_ROPE, 2, dtype=F32) / QK_ROPE)
    ang = pos[:, None] * inv[None, :]
    ang_t = inv[:, None] * pos[None, :]
    ang2 = jnp.concatenate([ang, ang], axis=-1)
    pad = jnp.zeros((n_pos, LANES - QK_ROPE), F32)
    return (jnp.concatenate([jnp.cos(ang2), pad], axis=-1),
            jnp.concatenate([jnp.sin(ang2), pad], axis=-1), jnp.cos(ang_t), jnp.sin(ang_t))


def kernel(x, meta_tokens, norm_ffn1_pre, norm_ffn1_post, ffn1_w_gu, ffn1_w_down, norm_mix_pre, norm_mix_post, w_in, pool_w, pool_scale, w_pool_o, q_a_norm, w_q_b, kv_a_norm, w_kv_b, w_mla_o, w_out, norm_ffn2_pre, norm_ffn2_post, ffn2_w_gu, ffn2_w_down):
    bsz, seq, d = x.shape
    depth = w_in.shape[0]
    nh = MLA_HEADS
    tm, tf, tn, tq = 512, 512, 512, 512
    tm_ffn = 1024

    cos, sin, cos_t, sin_t = _rope_tables(N_META + seq)
    cos_m, sin_m, cos_r, sin_r = cos[:N_META], sin[:N_META], cos[N_META:], sin[N_META:]
    cos_tm, sin_tm, cos_tr, sin_tr = (cos_t[:, :N_META], sin_t[:, :N_META],
                                      cos_t[:, N_META:], sin_t[:, N_META:])

    h = x.reshape(bsz * seq, d)
    hm = meta_tokens.astype(x.dtype)
    row = lambda v: v.reshape(1, -1)

    for i in range(depth):
        w_gu1, w_dn1 = ffn1_w_gu[i].astype(BF16), ffn1_w_down[i].astype(BF16)
        w_q = jnp.pad(w_q_b[i].reshape(Q_LORA, nh, QK_DIM),
                      ((0, 0), (0, 0), (0, HEAD_PAD - QK_DIM))).reshape(Q_LORA, nh * HEAD_PAD)
        w_qt = w_q.T.astype(BF16)
        w_kv = w_kv_b[i].reshape(KV_LORA, nh, QK_NOPE + V_DIM)
        w_k = w_kv[:, :, :QK_NOPE].reshape(KV_LORA, nh * QK_NOPE).astype(BF16)
        w_vt = w_kv[:, :, QK_NOPE:].reshape(KV_LORA, nh * V_DIM).T.astype(BF16)
        p_w = pool_w[i].astype(BF16)

        h1, h1m, w_gu2, w_dn2 = _ffn(
            h, row(norm_ffn1_pre[i]), row(norm_ffn1_post[i]), w_gu1, w_dn1, tm=tm_ffn, tf=tf,
            side=hm, casts=(ffn2_w_gu[i], ffn2_w_down[i]))
        w_in_bf = w_in[i].astype(BF16)
        n_lat = POOL_WIDTH + Q_LORA + KV_LORA + QK_ROPE
        w_gp, w_gm = w_in_bf[:, n_lat:n_lat + d], w_in_bf[:, n_lat + d:]

        u, cq, ckv, kpe, w_po, w_mo, w_o = _inproj(
            h1, row(norm_mix_pre[i]), w_in_bf, row(q_a_norm[i]), row(kv_a_norm[i]), cos_r, sin_r,
            tm=tm, casts=(w_pool_o[i], w_mla_o[i], w_out[i]))
        um, cqm, ckvm, kpem = _inproj(h1m, row(norm_mix_pre[i]), w_in_bf, row(q_a_norm[i]),
                                      row(kv_a_norm[i]), cos_m, sin_m, tm=N_META)
        qt, kn, vt = _qkv(cq, ckv, w_qt, w_k, w_vt, cos_tr, sin_tr, tm=tm, tk=tq, seq=seq)
        _, knm, vtm = _qkv(cqm, ckvm, w_qt, w_k, w_vt, cos_tm, sin_tm,
                           tm=N_META, tk=N_META, seq=N_META)

        attn = _attn(qt, kn.reshape(bsz, seq, nh * QK_NOPE),
                     kpe.reshape(bsz, seq, LANES), vt, knm, kpem,
                     vtm.reshape(nh * V_DIM, N_META), hp=4)

        h2 = _mixout(h1, row(norm_mix_pre[i]), row(norm_mix_post[i]), u, um, p_w,
                     row(pool_scale[i]), attn.reshape(bsz * seq, nh * V_DIM),
                     w_gp, w_gm, w_po, w_mo, w_o, tm=tm, tn=tn, seq=seq)

        h, = _ffn(h2, row(norm_ffn2_pre[i]), row(norm_ffn2_post[i]), w_gu2, w_dn2,
                  tm=tm_ffn, tf=tf)
        if i + 1 < depth:
            raise NotImplementedError("only DEPTH == 1 is supported")

    return h.reshape(bsz, seq, d)
```

```python
import functools
import math

import jax
import jax.numpy as jnp
from jax import lax
from jax.experimental import pallas as pl
from jax.experimental.pallas import tpu as pltpu

F32 = jnp.float32
BF16 = jnp.bfloat16

N_META = 16
POOL_WINDOWS = (2, 4, 8, 16)
POOL_GROUP = 256
POOL_WIDTH = POOL_GROUP * len(POOL_WINDOWS)
MLA_HEADS = 16
Q_LORA = 512
KV_LORA = 512
QK_NOPE = 128
QK_ROPE = 64
V_DIM = 128
QK_DIM = QK_NOPE + QK_ROPE
ROPE_THETA = 10000.0
EPS = 1e-6
LANES = 128
BF16_ROWS = 16
HEAD_PAD = 2 * LANES
Q_SCALE = (QK_DIM ** -0.5) * math.log2(math.e)
MASK_VALUE = -1e30

VMEM_LIMIT = 56 * 1024 * 1024
FFN_VMEM_LIMIT = 62 * 1024 * 1024
ROW_CHUNK = 256
MID_CHUNK = 512
SCORE_LEAD = 0


def _rms(x):
    return x * lax.rsqrt(jnp.mean(x * x, axis=-1, keepdims=True) + EPS)


def _rope(x, cos, sin):
    rot = pltpu.roll(x, 32, 1) - pltpu.roll(x, 96, 1)
    return x * cos + rot * sin


def _params(sem, vmem_limit=VMEM_LIMIT):
    return pltpu.CompilerParams(dimension_semantics=sem, vmem_limit_bytes=vmem_limit)


def _row_chunks(rows, chunk):
    chunk = min(chunk, rows)
    return [slice(r, r + chunk) for r in range(0, rows, chunk)]


def _ffn_kernel(*refs, n_casts, has_side):
    ns = int(has_side)
    x_ref, gpre_ref, gpost_ref, wg_ref, wu_ref, wd_ref = refs[:6]
    n_in = 6 + ns + n_casts
    cast_in = refs[6 + ns:n_in]
    o_ref = refs[n_in]
    cast_out = refs[n_in + 1 + ns:n_in + 1 + ns + n_casts]
    xn_ref = refs[n_in + 1 + ns + n_casts]
    f = pl.program_id(1)
    last = pl.num_programs(1) - 1
    chunks = _row_chunks(x_ref.shape[0], ROW_CHUNK)

    def gate_up(xn):
        return (jnp.dot(xn, wg_ref[...], preferred_element_type=F32),
                jnp.dot(xn, wu_ref[...], preferred_element_type=F32))

    def down(gu):
        g, u = gu
        a = ((g * jax.nn.sigmoid(g)) * u).astype(BF16)
        return jnp.dot(a, wd_ref[...], preferred_element_type=F32)

    def mlp_chunks(chunks, get_xn, put):
        ahead = gate_up(get_xn(chunks[0]))
        for n, rows in enumerate(chunks):
            gu = ahead
            if n + 1 < len(chunks):
                ahead = gate_up(get_xn(chunks[n + 1]))
            put(rows, down(gu))
            if n == 0:
                casts()

    def casts():
        for src, dst in zip(cast_in, cast_out):
            dst[...] = src[...].astype(BF16)

    def side(step):
        if not has_side:
            return
        xs_ref, os_ref, xsn_ref = refs[6], refs[n_in + 1], refs[-1]

        @pl.when(pl.program_id(0) == 0)
        def _():
            if step == "first":
                xsn_ref[...] = (_rms(xs_ref[...]) * gpre_ref[...]).astype(BF16)
                os_ref[...] = down(gate_up(xsn_ref[...]))
            elif step == "middle":
                os_ref[...] += down(gate_up(xsn_ref[...]))
            else:
                y = os_ref[...] + down(gate_up(xsn_ref[...]))
                os_ref[...] = xs_ref[...] + 0.5 * (_rms(y) * gpost_ref[...])

    @pl.when(f == 0)
    def _():
        def normed(rows):
            xn = (_rms(x_ref[rows, :]) * gpre_ref[...]).astype(BF16)
            xn_ref[rows, :] = xn
            return xn

        def put(rows, d):
            o_ref[rows, :] = d

        mlp_chunks(chunks, normed, put)
        side("first")

    @pl.when(jnp.logical_and(f > 0, f < last))
    def _():
        def put(rows, d):
            o_ref[rows, :] += d

        mlp_chunks(_row_chunks(x_ref.shape[0], MID_CHUNK), lambda rows: xn_ref[rows, :], put)
        side("middle")

    @pl.when(f == last)
    def _():
        def put(rows, d):
            y = o_ref[rows, :] + d
            o_ref[rows, :] = x_ref[rows, :] + 0.5 * (_rms(y) * gpost_ref[...])

        mlp_chunks(chunks, lambda rows: xn_ref[rows, :], put)
        side("last")


def _ffn(x, g_pre, g_post, w_gu, w_down, *, tm, tf, side=None, casts=()):
    rows, d = x.shape
    d_ff = w_down.shape[0]
    nf = d_ff // tf
    ni = rows // tm
    const = lambda i, f: (0, 0)
    cast_specs = []
    for c in casts:
        r, k = c.shape
        if r % (ni * BF16_ROWS) == 0 and k % (nf * LANES) == 0:
            cast_specs.append(pl.BlockSpec((r // ni, k // nf), lambda i, f: (i, f)))
        else:
            s = max(v for v in range(1, nf + 1) if r % (ni * v * BF16_ROWS) == 0)
            cast_specs.append(pl.BlockSpec(
                (r // (ni * s), k), lambda i, f, s=s: (i * s + (f * s) // nf, 0)))
    sides = [] if side is None else [side]
    side_specs = [pl.BlockSpec(v.shape, const) for v in sides]
    return pl.pallas_call(
        functools.partial(_ffn_kernel, n_casts=len(casts), has_side=bool(sides)),
        grid=(ni, nf),
        in_specs=[
            pl.BlockSpec((tm, d), lambda i, f: (i, 0)),
            pl.BlockSpec((1, d), const),
            pl.BlockSpec((1, d), const),
            pl.BlockSpec((d, tf), lambda i, f: (0, f)),
            pl.BlockSpec((d, tf), lambda i, f: (0, f + nf)),
            pl.BlockSpec((tf, d), lambda i, f: (f, 0)),
        ] + side_specs + cast_specs,
        out_specs=[pl.BlockSpec((tm, d), lambda i, f: (i, 0))] + side_specs + cast_specs,
        out_shape=[jax.ShapeDtypeStruct((rows, d), F32)]
        + [jax.ShapeDtypeStruct(v.shape, F32) for v in sides]
        + [jax.ShapeDtypeStruct(c.shape, BF16) for c in casts],
        scratch_shapes=[pltpu.VMEM((tm, d), BF16)]
        + [pltpu.VMEM(v.shape, BF16) for v in sides],
        compiler_params=_params(("arbitrary", "arbitrary"), FFN_VMEM_LIMIT),
        name="ffn",
    )(x, g_pre, g_post, w_gu, w_gu, w_down, *sides, *casts)


def _inproj_kernel(*refs, n_casts):
    h_ref, gmix_ref, w_ref, gq_ref, gkv_ref, cos_ref, sin_ref = refs[:7]
    cast_in = refs[7:7 + n_casts]
    u_ref, cq_ref, ckv_ref, kpe_ref = refs[7 + n_casts:11 + n_casts]
    cast_out = refs[11 + n_casts:]
    c0, c1, c2, c3 = POOL_WIDTH, POOL_WIDTH + Q_LORA, POOL_WIDTH + Q_LORA + KV_LORA, w_ref.shape[1]
    for n, rows in enumerate(_row_chunks(h_ref.shape[0], ROW_CHUNK)):
        hn = (_rms(h_ref[rows, :]) * gmix_ref[...]).astype(BF16)
        u_ref[rows, :] = jnp.dot(hn, w_ref[:, :c0], preferred_element_type=F32)
        if n == 0:
            for src, dst in zip(cast_in, cast_out):
                dst[...] = src[...].astype(BF16)
        cq = jnp.dot(hn, w_ref[:, c0:c1], preferred_element_type=F32)
        cq_ref[rows, :] = (_rms(cq) * gq_ref[...]).astype(BF16)
        ckv = jnp.dot(hn, w_ref[:, c1:c2], preferred_element_type=F32)
        ckv_ref[rows, :] = (_rms(ckv) * gkv_ref[...]).astype(BF16)
        kr = jnp.dot(hn, w_ref[:, c2:c3], preferred_element_type=F32)
        kr = jnp.where(lax.broadcasted_iota(jnp.int32, kr.shape, 1) < QK_ROPE, kr, 0.0)
        kpe_ref[rows, :] = _rope(kr, cos_ref[rows, :], sin_ref[rows, :]).astype(BF16)


def _inproj(h, g_mix, w_in, g_q, g_kv, cos, sin, *, tm, casts=()):
    rows, d = h.shape
    n_pos = cos.shape[0] // tm
    wcols = POOL_WIDTH + Q_LORA + KV_LORA + LANES
    steps = rows // tm
    row = lambda i: (i, 0)
    const = lambda i: (0, 0)
    pos = lambda i: (i % n_pos, 0)
    cast_specs = [pl.BlockSpec((c.shape[0] // steps, c.shape[1]), row) for c in casts]
    assert all(c.shape[0] % (steps * BF16_ROWS) == 0 for c in casts)
    return pl.pallas_call(
        functools.partial(_inproj_kernel, n_casts=len(casts)),
        grid=(steps,),
        in_specs=[
            pl.BlockSpec((tm, d), row),
            pl.BlockSpec((1, d), const),
            pl.BlockSpec((d, wcols), const),
            pl.BlockSpec((1, Q_LORA), const),
            pl.BlockSpec((1, KV_LORA), const),
            pl.BlockSpec((tm, LANES), pos),
            pl.BlockSpec((tm, LANES), pos),
        ] + cast_specs,
        out_specs=[
            pl.BlockSpec((tm, POOL_WIDTH), row),
            pl.BlockSpec((tm, Q_LORA), row),
            pl.BlockSpec((tm, KV_LORA), row),
            pl.BlockSpec((tm, LANES), row),
        ] + cast_specs,
        out_shape=[
            jax.ShapeDtypeStruct((rows, POOL_WIDTH), F32),
            jax.ShapeDtypeStruct((rows, Q_LORA), BF16),
            jax.ShapeDtypeStruct((rows, KV_LORA), BF16),
            jax.ShapeDtypeStruct((rows, LANES), BF16),
        ] + [jax.ShapeDtypeStruct(c.shape, BF16) for c in casts],
        compiler_params=_params(("parallel",)),
        name="inproj",
    )(h, g_mix, w_in, g_q, g_kv, cos, sin, *casts)


def _qkv_kernel(cq_ref, ckv_ref, wqt_ref, wk_ref, wvt_ref, cos_ref, sin_ref,
                qt_ref, k_ref, vt_ref):
    cq = cq_ref[...]
    ckv = ckv_ref[...]
    nt = (((1,), (1,)), ((), ()))
    tk = vt_ref.shape[-1]
    chunks = [slice(c * tk, (c + 1) * tk) for c in range(vt_ref.shape[1])]
    cos = cos_ref[...] * Q_SCALE
    sin = sin_ref[...] * Q_SCALE
    half = QK_ROPE // 2
    qt = lax.dot_general(wqt_ref[...], cq, nt, preferred_element_type=F32)
    zeros = jnp.zeros((HEAD_PAD - QK_DIM, tk), BF16)
    for h in range(MLA_HEADS):
        lo = h * HEAD_PAD
        nope = qt[lo:lo + QK_NOPE] * Q_SCALE
        x1 = qt[lo + QK_NOPE:lo + QK_NOPE + half]
        x2 = qt[lo + QK_NOPE + half:lo + QK_DIM]
        r1 = x1 * cos - x2 * sin
        r2 = x2 * cos + x1 * sin
        for c, cols in enumerate(chunks):
            qt_ref[0, c, lo:lo + QK_NOPE, :] = nope[:, cols].astype(BF16)
            qt_ref[0, c, lo + QK_NOPE:lo + QK_NOPE + half, :] = r1[:, cols].astype(BF16)
            qt_ref[0, c, lo + QK_NOPE + half:lo + QK_DIM, :] = r2[:, cols].astype(BF16)
            qt_ref[0, c, lo + QK_DIM:lo + HEAD_PAD, :] = zeros
    k_ref[...] = jnp.dot(ckv, wk_ref[...], preferred_element_type=F32).astype(BF16)
    vt = lax.dot_general(wvt_ref[...], ckv, nt, preferred_element_type=F32)
    for c, cols in enumerate(chunks):
        vt_ref[0, c] = vt[:, cols].astype(BF16)


def _qkv(cq, ckv, w_qt, w_k, w_vt, cos_t, sin_t, *, tm, tk, seq):
    rows = cq.shape[0]
    n_pos = seq // tm
    nh = MLA_HEADS
    row = lambda i: (i, 0)
    const = lambda i: (0, 0)
    tiled = lambda i: (i // n_pos, i % n_pos, 0, 0)
    return pl.pallas_call(
        _qkv_kernel,
        grid=(rows // tm,),
        in_specs=[
            pl.BlockSpec((tm, Q_LORA), row),
            pl.BlockSpec((tm, KV_LORA), row),
            pl.BlockSpec((nh * HEAD_PAD, Q_LORA), const),
            pl.BlockSpec((KV_LORA, nh * QK_NOPE), const),
            pl.BlockSpec((nh * V_DIM, KV_LORA), const),
            pl.BlockSpec((QK_ROPE // 2, tm), lambda i: (0, i % n_pos)),
            pl.BlockSpec((QK_ROPE // 2, tm), lambda i: (0, i % n_pos)),
        ],
        out_specs=[
            pl.BlockSpec((1, tm // tk, nh * HEAD_PAD, tk), tiled),
            pl.BlockSpec((tm, nh * QK_NOPE), row),
            pl.BlockSpec((1, tm // tk, nh * V_DIM, tk), tiled),
        ],
        out_shape=[
            jax.ShapeDtypeStruct((rows // seq, seq // tk, nh * HEAD_PAD, tk), BF16),
            jax.ShapeDtypeStruct((rows, nh * QK_NOPE), BF16),
            jax.ShapeDtypeStruct((rows // seq, seq // tk, nh * V_DIM, tk), BF16),
        ],
        compiler_params=_params(("parallel",)),
        name="qkv",
    )(cq, ckv, w_qt, w_k, w_vt, cos_t, sin_t)


def _attn_kernel(qi_ref, kj_ref, new_ref, q_ref, kn_ref, kp_ref, vt_ref, knm_ref, kpm_ref,
                 vtm_ref, o_ref, acc_ref, sa_ref, sb_ref, st_ref, pm_ref):
    tk = vt_ref.shape[-1]
    tq = tk
    hp = acc_ref.shape[0]
    n_pairs = qi_ref.shape[0]
    bufs = ((sa_ref, 2), (sb_ref, 3))

    def q_of(h, qi):
        return q_ref[0, qi, h * HEAD_PAD:(h + 1) * HEAD_PAD, :]

    def init_scores(qi, heads):
        kpm = kpm_ref[...]
        for h in heads:
            km = jnp.concatenate([knm_ref[:, h * QK_NOPE:(h + 1) * QK_NOPE], kpm], axis=1)
            s = jnp.dot(km, q_of(h, qi), preferred_element_type=F32)
            m0 = jnp.max(s, axis=0, keepdims=True)
            p = jnp.exp2(s - m0)
            pm_ref[h] = p.astype(BF16)
            st_ref[h, 0] = m0
            st_ref[h, 1] = jnp.sum(p, axis=0, keepdims=True)

    def init_values(heads):
        for h in heads:
            acc_ref[h] = jnp.dot(vtm_ref[h * V_DIM:(h + 1) * V_DIM, :], pm_ref[h],
                                 preferred_element_type=F32)

    def scores(t, dst, masked, heads):
        s_ref, row = dst
        qi = qi_ref[t]
        start = pl.multiple_of(kj_ref[t] * tk, tk)
        kp = kp_ref[0, pl.ds(start, tk), :]
        for h in heads:
            k = jnp.concatenate(
                [kn_ref[0, pl.ds(start, tk), h * QK_NOPE:(h + 1) * QK_NOPE], kp], axis=1)
            s = jnp.dot(k, q_of(h, qi), preferred_element_type=F32)
            if masked:
                kpos = lax.broadcasted_iota(jnp.int32, s.shape, 0)
                qpos = lax.broadcasted_iota(jnp.int32, s.shape, 1)
                s = jnp.where(kpos <= qpos, s, MASK_VALUE)
            s_ref[h] = s
            st_ref[h, row] = jnp.max(s, axis=0, keepdims=True)

    def absorb(t, src, heads):
        s_ref, row = src
        kj = kj_ref[t]
        ones = jnp.ones((BF16_ROWS, tk), BF16)
        for h in heads:
            m, l = st_ref[h, 0], st_ref[h, 1]
            m_new = jnp.maximum(m, st_ref[h, row])
            alpha = jnp.exp2(m - m_new)
            p = jnp.exp2(s_ref[h] - m_new)
            vt1 = jnp.concatenate([vt_ref[0, kj, h * V_DIM:(h + 1) * V_DIM, :], ones], axis=0)
            pv = jnp.dot(vt1, p.astype(BF16), preferred_element_type=F32)
            acc_ref[h] = alpha * acc_ref[h] + pv[:V_DIM]
            st_ref[h, 0] = m_new
            st_ref[h, 1] = alpha * l + pv[V_DIM:V_DIM + 1]

    def finalize(qi, heads):
        rows = pl.ds(pl.multiple_of(qi * tq, tq), tq)
        for h in heads:
            o_ref[0, rows, h * V_DIM:(h + 1) * V_DIM] = (
                acc_ref[h] / st_ref[h, 1]).T.astype(BF16)

    def run(stages):
        items = [(st, h) for st in stages for h in range(hp)]

        def put_scores(item):
            (t, _, dst, is_fresh), h = item
            scores(t, dst, is_fresh, [h])

        pending = []

        def put_rest(item):
            (t, src, _, is_fresh), h = item
            absorb(t - 1, src, [h])
            if is_fresh:
                finalize(qi_ref[t - 1], [h])
                init_scores(qi_ref[t], [h])
            init_values(pending)
            pending.clear()
            if is_fresh:
                pending.append(h)

        for item in items[:SCORE_LEAD + 1]:
            put_scores(item)
        for n, item in enumerate(items):
            put_rest(item)
            if n + SCORE_LEAD + 1 < len(items):
                put_scores(items[n + SCORE_LEAD + 1])
        init_values(pending)

    heads = list(range(hp))
    init_scores(qi_ref[0], heads)
    init_values(heads)
    scores(0, bufs[0], True, heads)

    def body(r, carry):
        t = 2 * r + 1
        a, b = bufs
        for fresh0, fresh1 in ((True, False), (False, True), (False, False)):
            cond = jnp.logical_and(new_ref[t] == int(fresh0), new_ref[t + 1] == int(fresh1))
            pl.when(cond)(functools.partial(
                run, [(t, a, b, fresh0), (t + 1, b, a, fresh1)]))
        return carry

    lax.fori_loop(0, (n_pairs - 1) // 2, body, 0)
    last = bufs[0]
    if (n_pairs - 1) % 2:
        t = n_pairs - 1
        for is_fresh in (True, False):
            pl.when(new_ref[t] == int(is_fresh))(functools.partial(
                run, [(t, bufs[0], bufs[1], is_fresh)]))
        last = bufs[1]
    absorb(n_pairs - 1, last, heads)
    finalize(qi_ref[n_pairs - 1], heads)


def _attn(qt, k_nope, k_pe, vt, knm, kpm, vtm, *, hp):
    b, s, _ = k_nope.shape
    tk = vt.shape[-1]
    nq = s // tk
    nh = MLA_HEADS
    qi, kj, new = [], [], []
    for i in range(nq):
        for n, j in enumerate([i] + list(range(i))):
            qi.append(i)
            kj.append(j)
            new.append(int(n == 0))
    assert not any(new[t] and new[t + 1] for t in range(1, len(new) - 1, 2))
    tables = [jnp.asarray(v, jnp.int32) for v in (qi, kj, new)]
    grid_spec = pltpu.PrefetchScalarGridSpec(
        num_scalar_prefetch=len(tables),
        grid=(b, nh // hp),
        in_specs=[
            pl.BlockSpec((1, nq, hp * HEAD_PAD, tk), lambda bi, h, *_: (bi, 0, h, 0)),
            pl.BlockSpec((1, s, hp * QK_NOPE), lambda bi, h, *_: (bi, 0, h)),
            pl.BlockSpec((1, s, LANES), lambda bi, h, *_: (bi, 0, 0)),
            pl.BlockSpec((1, nq, hp * V_DIM, tk), lambda bi, h, *_: (bi, 0, h, 0)),
            pl.BlockSpec((N_META, hp * QK_NOPE), lambda bi, h, *_: (0, h)),
            pl.BlockSpec((N_META, LANES), lambda bi, h, *_: (0, 0)),
            pl.BlockSpec((hp * V_DIM, N_META), lambda bi, h, *_: (h, 0)),
        ],
        out_specs=pl.BlockSpec((1, s, hp * V_DIM), lambda bi, h, *_: (bi, 0, h)),
        scratch_shapes=[pltpu.VMEM((hp, V_DIM, tk), F32), pltpu.VMEM((hp, tk, tk), F32),
                        pltpu.VMEM((hp, tk, tk), F32), pltpu.VMEM((hp, 4, 1, tk), F32),
                        pltpu.VMEM((hp, N_META, tk), BF16)],
    )
    return pl.pallas_call(
        _attn_kernel,
        grid_spec=grid_spec,
        out_shape=jax.ShapeDtypeStruct((b, s, nh * V_DIM), BF16),
        compiler_params=_params(("parallel", "parallel")),
        name="attn",
    )(*tables, qt, k_nope, k_pe, vt, knm, kpm, vtm)


def _mixout_kernel(h_ref, gmix_ref, gpost_ref, u_ref, uprev_ref, umeta_ref, pw_ref, pscale_ref,
                   attn_ref, wgp_ref, wgm_ref, wpo_ref, wmo_ref, wout_ref, o_ref,
                   hn_ref, pool_ref, buf_ref, *, tiles_per_seq):
    c = pl.program_id(1)
    last = pl.num_programs(1) - 1
    chunks = _row_chunks(h_ref.shape[0], ROW_CHUNK)

    def pool(rows, halo):
        for g, win in enumerate(POOL_WINDOWS):
            cols = slice(g * POOL_GROUP, (g + 1) * POOL_GROUP)
            buf_ref[:N_META, :] = halo[:, cols]
            buf_ref[N_META:, :] = u_ref[rows, cols]
            tot = buf_ref[...]
            shift = 1
            while shift < win:
                tot = tot + pltpu.roll(tot, shift, 0)
                shift *= 2
            dlt = (tot[N_META:, :] * (1.0 / win) - buf_ref[N_META:, :]).astype(BF16)
            y = jnp.dot(dlt, pw_ref[g], preferred_element_type=F32)
            pool_ref[rows, cols] = (y * pscale_ref[:, cols]).astype(BF16)

    def mix(hn, rows):
        g_pool = jnp.dot(hn, wgp_ref[...], preferred_element_type=F32)
        g_mla = jnp.dot(hn, wgm_ref[...], preferred_element_type=F32)
        y_pool = jnp.dot(pool_ref[rows, :], wpo_ref[...], preferred_element_type=F32)
        y_mla = jnp.dot(attn_ref[rows, :], wmo_ref[...], preferred_element_type=F32)
        y = (jax.nn.sigmoid(g_pool) * y_pool + jax.nn.sigmoid(g_mla) * y_mla).astype(BF16)
        return jnp.dot(y, wout_ref[...], preferred_element_type=F32)

    @pl.when(c == 0)
    def _():
        seq_start = pl.program_id(0) % tiles_per_seq == 0
        for n, rows in enumerate(chunks):
            if n == 0:
                halo = jnp.where(seq_start, umeta_ref[...], uprev_ref[...])
            else:
                halo = u_ref[rows.start - N_META:rows.start, :]
            pool(rows, halo)
            hn = (_rms(h_ref[rows, :]) * gmix_ref[...]).astype(BF16)
            hn_ref[rows, :] = hn
            o_ref[rows, :] = mix(hn, rows)

    @pl.when(jnp.logical_and(c > 0, c < last))
    def _():
        o_ref[...] += mix(hn_ref[...], slice(None))

    @pl.when(c == last)
    def _():
        for rows in chunks:
            y = o_ref[rows, :] + mix(hn_ref[rows, :], rows)
            o_ref[rows, :] = h_ref[rows, :] + _rms(y) * gpost_ref[...]


def _mixout(h, g_mix, g_post, u, u_meta, pool_w, pool_scale, attn,
            w_gp, w_gm, w_po, w_mo, w_out, *, tm, tn, seq):
    rows, d = h.shape
    pw = u.shape[1]
    hb = tm // N_META
    chunk = min(ROW_CHUNK, tm)
    row = lambda i, c: (i, 0)
    const = lambda i, c: (0, 0)
    col = lambda i, c: (0, c)
    return pl.pallas_call(
        functools.partial(_mixout_kernel, tiles_per_seq=seq // tm),
        grid=(rows // tm, d // tn),
        in_specs=[
            pl.BlockSpec((tm, d), row),
            pl.BlockSpec((1, d), const),
            pl.BlockSpec((1, d), const),
            pl.BlockSpec((tm, pw), row),
            pl.BlockSpec((N_META, pw), lambda i, c: (jnp.maximum(i * hb - 1, 0), 0)),
            pl.BlockSpec((N_META, pw), const),
            pl.BlockSpec((len(POOL_WINDOWS), POOL_GROUP, POOL_GROUP), lambda i, c: (0, 0, 0)),
            pl.BlockSpec((1, pw), const),
            pl.BlockSpec((tm, attn.shape[1]), row),
            pl.BlockSpec((d, tn), col),
            pl.BlockSpec((d, tn), col),
            pl.BlockSpec((pw, tn), col),
            pl.BlockSpec((attn.shape[1], tn), col),
            pl.BlockSpec((tn, d), lambda i, c: (c, 0)),
        ],
        out_specs=pl.BlockSpec((tm, d), row),
        out_shape=jax.ShapeDtypeStruct((rows, d), F32),
        scratch_shapes=[pltpu.VMEM((tm, d), BF16), pltpu.VMEM((tm, pw), BF16),
                        pltpu.VMEM((chunk + N_META, POOL_GROUP), F32)],
        compiler_params=_params(("parallel", "arbitrary")),
        name="mixout",
    )(h, g_mix, g_post, u, u, u_meta, pool_w, pool_scale, attn, w_gp, w_gm, w_po, w_mo, w_out)


def _rope_tables(n_pos):
    pos = jnp.arange(n_pos, dtype=F32)
    inv = ROPE_THETA ** (-jnp.arange(0, QK_ROPE, 2, dtype=F32) / QK_ROPE)
    ang = pos[:, None] * inv[None, :]
    ang_t = inv[:, None] * pos[None, :]
    ang2 = jnp.concatenate([ang, ang], axis=-1)
    pad = jnp.zeros((n_pos, LANES - QK_ROPE), F32)
    return (jnp.concatenate([jnp.cos(ang2), pad], axis=-1),
            jnp.concatenate([jnp.sin(ang2), pad], axis=-1), jnp.cos(ang_t), jnp.sin(ang_t))


def kernel(x, meta_tokens, norm_ffn1_pre, norm_ffn1_post, ffn1_w_gu, ffn1_w_down, norm_mix_pre, norm_mix_post, w_in, pool_w, pool_scale, w_pool_o, q_a_norm, w_q_b, kv_a_norm, w_kv_b, w_mla_o, w_out, norm_ffn2_pre, norm_ffn2_post, ffn2_w_gu, ffn2_w_down):
    bsz, seq, d = x.shape
    depth = w_in.shape[0]
    nh = MLA_HEADS
    tm, tf, tn, tq = 512, 512, 512, 512
    tm_ffn = 1024

    cos, sin, cos_t, sin_t = _rope_tables(N_META + seq)
    cos_m, sin_m, cos_r, sin_r = cos[:N_META], sin[:N_META], cos[N_META:], sin[N_META:]
    cos_tm, sin_tm, cos_tr, sin_tr = (cos_t[:, :N_META], sin_t[:, :N_META],
                                      cos_t[:, N_META:], sin_t[:, N_META:])

    h = x.reshape(bsz * seq, d)
    hm = meta_tokens.astype(x.dtype)
    row = lambda v: v.reshape(1, -1)

    for i in range(depth):
        w_gu1, w_dn1 = ffn1_w_gu[i].astype(BF16), ffn1_w_down[i].astype(BF16)
        w_q = jnp.pad(w_q_b[i].reshape(Q_LORA, nh, QK_DIM),
                      ((0, 0), (0, 0), (0, HEAD_PAD - QK_DIM))).reshape(Q_LORA, nh * HEAD_PAD)
        w_qt = w_q.T.astype(BF16)
        w_kv = w_kv_b[i].reshape(KV_LORA, nh, QK_NOPE + V_DIM)
        w_k = w_kv[:, :, :QK_NOPE].reshape(KV_LORA, nh * QK_NOPE).astype(BF16)
        w_vt = w_kv[:, :, QK_NOPE:].reshape(KV_LORA, nh * V_DIM).T.astype(BF16)
        p_w = pool_w[i].astype(BF16)

        h1, h1m, w_gu2, w_dn2 = _ffn(
            h, row(norm_ffn1_pre[i]), row(norm_ffn1_post[i]), w_gu1, w_dn1, tm=tm_ffn, tf=tf,
            side=hm, casts=(ffn2_w_gu[i], ffn2_w_down[i]))
        w_in_bf = w_in[i].astype(BF16)
        n_lat = POOL_WIDTH + Q_LORA + KV_LORA + QK_ROPE
        w_gp, w_gm = w_in_bf[:, n_lat:n_lat + d], w_in_bf[:, n_lat + d:]

        u, cq, ckv, kpe, w_po, w_mo, w_o = _inproj(
            h1, row(norm_mix_pre[i]), w_in_bf, row(q_a_norm[i]), row(kv_a_norm[i]), cos_r, sin_r,
            tm=tm, casts=(w_pool_o[i], w_mla_o[i], w_out[i]))
        um, cqm, ckvm, kpem = _inproj(h1m, row(norm_mix_pre[i]), w_in_bf, row(q_a_norm[i]),
                                      row(kv_a_norm[i]), cos_m, sin_m, tm=N_META)
        qt, kn, vt = _qkv(cq, ckv, w_qt, w_k, w_vt, cos_tr, sin_tr, tm=tm, tk=tq, seq=seq)
        _, knm, vtm = _qkv(cqm, ckvm, w_qt, w_k, w_vt, cos_tm, sin_tm,
                           tm=N_META, tk=N_META, seq=N_META)

        attn = _attn(qt, kn.reshape(bsz, seq, nh * QK_NOPE),
                     kpe.reshape(bsz, seq, LANES), vt, knm, kpem,
                     vtm.reshape(nh * V_DIM, N_META), hp=4)

        h2 = _mixout(h1, row(norm_mix_pre[i]), row(norm_mix_post[i]), u, um, p_w,
                     row(pool_scale[i]), attn.reshape(bsz * seq, nh * V_DIM),
                     w_gp, w_gm, w_po, w_mo, w_o, tm=tm, tn=tn, seq=seq)

        h, = _ffn(h2, row(norm_ffn2_pre[i]), row(norm_ffn2_post[i]), w_gu2, w_dn2,
                  tm=tm_ffn, tf=tf)
        if i + 1 < depth:
            raise NotImplementedError("only DEPTH == 1 is supported")

    return h.reshape(bsz, seq, d)
```

```python
import functools
import math

import jax
import jax.numpy as jnp
from jax import lax
from jax.experimental import pallas as pl
from jax.experimental.pallas import tpu as pltpu

F32 = jnp.float32
BF16 = jnp.bfloat16

N_META = 16
POOL_WINDOWS = (2, 4, 8, 16)
POOL_GROUP = 256
POOL_WIDTH = POOL_GROUP * len(POOL_WINDOWS)
MLA_HEADS = 16
Q_LORA = 512
KV_LORA = 512
QK_NOPE = 128
QK_ROPE = 64
V_DIM = 128
QK_DIM = QK_NOPE + QK_ROPE
ROPE_THETA = 10000.0
EPS = 1e-6
LANES = 128
BF16_ROWS = 16
HEAD_PAD = 2 * LANES
Q_SCALE = (QK_DIM ** -0.5) * math.log2(math.e)
MASK_VALUE = -1e30

VMEM_LIMIT = 56 * 1024 * 1024
FFN_VMEM_LIMIT = 62 * 1024 * 1024
ROW_CHUNK = 256
SCORE_LEAD = 0


def _rms(x):
    return x * lax.rsqrt(jnp.mean(x * x, axis=-1, keepdims=True) + EPS)


def _rope(x, cos, sin):
    rot = pltpu.roll(x, 32, 1) - pltpu.roll(x, 96, 1)
    return x * cos + rot * sin


def _params(sem, vmem_limit=VMEM_LIMIT):
    return pltpu.CompilerParams(dimension_semantics=sem, vmem_limit_bytes=vmem_limit)


def _row_chunks(rows, chunk):
    chunk = min(chunk, rows)
    return [slice(r, r + chunk) for r in range(0, rows, chunk)]


def _ffn_kernel(*refs, n_casts, has_side):
    ns = int(has_side)
    x_ref, gpre_ref, gpost_ref, wg_ref, wu_ref, wd_ref = refs[:6]
    n_in = 6 + ns + n_casts
    cast_in = refs[6 + ns:n_in]
    o_ref = refs[n_in]
    cast_out = refs[n_in + 1 + ns:n_in + 1 + ns + n_casts]
    xn_ref = refs[n_in + 1 + ns + n_casts]
    f = pl.program_id(1)
    last = pl.num_programs(1) - 1
    chunks = _row_chunks(x_ref.shape[0], ROW_CHUNK)

    def gate_up(xn):
        return (jnp.dot(xn, wg_ref[...], preferred_element_type=F32),
                jnp.dot(xn, wu_ref[...], preferred_element_type=F32))

    def down(gu):
        g, u = gu
        a = ((g * jax.nn.sigmoid(g)) * u).astype(BF16)
        return jnp.dot(a, wd_ref[...], preferred_element_type=F32)

    def mlp_chunks(chunks, get_xn, put):
        ahead = gate_up(get_xn(chunks[0]))
        for n, rows in enumerate(chunks):
            gu = ahead
            if n + 1 < len(chunks):
                ahead = gate_up(get_xn(chunks[n + 1]))
            put(rows, down(gu))
            if n == 0:
                casts()

    def casts():
        for src, dst in zip(cast_in, cast_out):
            dst[...] = src[...].astype(BF16)

    def side(step):
        if not has_side:
            return
        xs_ref, os_ref, xsn_ref = refs[6], refs[n_in + 1], refs[-1]

        @pl.when(pl.program_id(0) == 0)
        def _():
            if step == "first":
                xsn_ref[...] = (_rms(xs_ref[...]) * gpre_ref[...]).astype(BF16)
                os_ref[...] = down(gate_up(xsn_ref[...]))
            elif step == "middle":
                os_ref[...] += down(gate_up(xsn_ref[...]))
            else:
                y = os_ref[...] + down(gate_up(xsn_ref[...]))
                os_ref[...] = xs_ref[...] + 0.5 * (_rms(y) * gpost_ref[...])

    @pl.when(f == 0)
    def _():
        def normed(rows):
            xn = (_rms(x_ref[rows, :]) * gpre_ref[...]).astype(BF16)
            xn_ref[rows, :] = xn
            return xn

        def put(rows, d):
            o_ref[rows, :] = d

        mlp_chunks(chunks, normed, put)
        side("first")

    @pl.when(jnp.logical_and(f > 0, f < last))
    def _():
        def put(rows, d):
            o_ref[rows, :] += d

        mlp_chunks(chunks, lambda rows: xn_ref[rows, :], put)
        side("middle")

    @pl.when(f == last)
    def _():
        def put(rows, d):
            y = o_ref[rows, :] + d
            o_ref[rows, :] = x_ref[rows, :] + 0.5 * (_rms(y) * gpost_ref[...])

        mlp_chunks(chunks, lambda rows: xn_ref[rows, :], put)
        side("last")


def _ffn(x, g_pre, g_post, w_gu, w_down, *, tm, tf, side=None, casts=()):
    rows, d = x.shape
    d_ff = w_down.shape[0]
    nf = d_ff // tf
    ni = rows // tm
    const = lambda i, f: (0, 0)
    cast_specs = []
    for c in casts:
        r, k = c.shape
        if r % (ni * BF16_ROWS) == 0 and k % (nf * LANES) == 0:
            cast_specs.append(pl.BlockSpec((r // ni, k // nf), lambda i, f: (i, f)))
        else:
            s = max(v for v in range(1, nf + 1) if r % (ni * v * BF16_ROWS) == 0)
            cast_specs.append(pl.BlockSpec(
                (r // (ni * s), k), lambda i, f, s=s: (i * s + (f * s) // nf, 0)))
    sides = [] if side is None else [side]
    side_specs = [pl.BlockSpec(v.shape, const) for v in sides]
    return pl.pallas_call(
        functools.partial(_ffn_kernel, n_casts=len(casts), has_side=bool(sides)),
        grid=(ni, nf),
        in_specs=[
            pl.BlockSpec((tm, d), lambda i, f: (i, 0)),
            pl.BlockSpec((1, d), const),
            pl.BlockSpec((1, d), const),
            pl.BlockSpec((d, tf), lambda i, f: (0, f)),
            pl.BlockSpec((d, tf), lambda i, f: (0, f + nf)),
            pl.BlockSpec((tf, d), lambda i, f: (f, 0)),
        ] + side_specs + cast_specs,
        out_specs=[pl.BlockSpec((tm, d), lambda i, f: (i, 0))] + side_specs + cast_specs,
        out_shape=[jax.ShapeDtypeStruct((rows, d), F32)]
        + [jax.ShapeDtypeStruct(v.shape, F32) for v in sides]
        + [jax.ShapeDtypeStruct(c.shape, BF16) for c in casts],
        scratch_shapes=[pltpu.VMEM((tm, d), BF16)]
        + [pltpu.VMEM(v.shape, BF16) for v in sides],
        compiler_params=_params(("arbitrary", "arbitrary"), FFN_VMEM_LIMIT),
        name="ffn",
    )(x, g_pre, g_post, w_gu, w_gu, w_down, *sides, *casts)


def _inproj_kernel(*refs, n_casts):
    h_ref, gmix_ref, w_ref, gq_ref, gkv_ref, cos_ref, sin_ref = refs[:7]
    cast_in = refs[7:7 + n_casts]
    u_ref, cq_ref, ckv_ref, kpe_ref = refs[7 + n_casts:11 + n_casts]
    cast_out = refs[11 + n_casts:]
    c0, c1, c2, c3 = POOL_WIDTH, POOL_WIDTH + Q_LORA, POOL_WIDTH + Q_LORA + KV_LORA, w_ref.shape[1]
    for n, rows in enumerate(_row_chunks(h_ref.shape[0], ROW_CHUNK)):
        hn = (_rms(h_ref[rows, :]) * gmix_ref[...]).astype(BF16)
        u_ref[rows, :] = jnp.dot(hn, w_ref[:, :c0], preferred_element_type=F32)
        if n == 0:
            for src, dst in zip(cast_in, cast_out):
                dst[...] = src[...].astype(BF16)
        cq = jnp.dot(hn, w_ref[:, c0:c1], preferred_element_type=F32)
        cq_ref[rows, :] = (_rms(cq) * gq_ref[...]).astype(BF16)
        ckv = jnp.dot(hn, w_ref[:, c1:c2], preferred_element_type=F32)
        ckv_ref[rows, :] = (_rms(ckv) * gkv_ref[...]).astype(BF16)
        kr = jnp.dot(hn, w_ref[:, c2:c3], preferred_element_type=F32)
        kr = jnp.where(lax.broadcasted_iota(jnp.int32, kr.shape, 1) < QK_ROPE, kr, 0.0)
        kpe_ref[rows, :] = _rope(kr, cos_ref[rows, :], sin_ref[rows, :]).astype(BF16)


def _inproj(h, g_mix, w_in, g_q, g_kv, cos, sin, *, tm, casts=()):
    rows, d = h.shape
    n_pos = cos.shape[0] // tm
    wcols = POOL_WIDTH + Q_LORA + KV_LORA + LANES
    steps = rows // tm
    row = lambda i: (i, 0)
    const = lambda i: (0, 0)
    pos = lambda i: (i % n_pos, 0)
    cast_specs = [pl.BlockSpec((c.shape[0] // steps, c.shape[1]), row) for c in casts]
    assert all(c.shape[0] % (steps * BF16_ROWS) == 0 for c in casts)
    return pl.pallas_call(
        functools.partial(_inproj_kernel, n_casts=len(casts)),
        grid=(steps,),
        in_specs=[
            pl.BlockSpec((tm, d), row),
            pl.BlockSpec((1, d), const),
            pl.BlockSpec((d, wcols), const),
            pl.BlockSpec((1, Q_LORA), const),
            pl.BlockSpec((1, KV_LORA), const),
            pl.BlockSpec((tm, LANES), pos),
            pl.BlockSpec((tm, LANES), pos),
        ] + cast_specs,
        out_specs=[
            pl.BlockSpec((tm, POOL_WIDTH), row),
            pl.BlockSpec((tm, Q_LORA), row),
            pl.BlockSpec((tm, KV_LORA), row),
            pl.BlockSpec((tm, LANES), row),
        ] + cast_specs,
        out_shape=[
            jax.ShapeDtypeStruct((rows, POOL_WIDTH), F32),
            jax.ShapeDtypeStruct((rows, Q_LORA), BF16),
            jax.ShapeDtypeStruct((rows, KV_LORA), BF16),
            jax.ShapeDtypeStruct((rows, LANES), BF16),
        ] + [jax.ShapeDtypeStruct(c.shape, BF16) for c in casts],
        compiler_params=_params(("parallel",)),
        name="inproj",
    )(h, g_mix, w_in, g_q, g_kv, cos, sin, *casts)


def _qkv_kernel(cq_ref, ckv_ref, wqt_ref, wk_ref, wvt_ref, cos_ref, sin_ref,
                qt_ref, k_ref, vt_ref):
    cq = cq_ref[...]
    ckv = ckv_ref[...]
    nt = (((1,), (1,)), ((), ()))
    tk = vt_ref.shape[-1]
    chunks = [slice(c * tk, (c + 1) * tk) for c in range(vt_ref.shape[1])]
    cos = cos_ref[...] * Q_SCALE
    sin = sin_ref[...] * Q_SCALE
    half = QK_ROPE // 2
    qt = lax.dot_general(wqt_ref[...], cq, nt, preferred_element_type=F32)
    zeros = jnp.zeros((HEAD_PAD - QK_DIM, tk), BF16)
    for h in range(MLA_HEADS):
        lo = h * HEAD_PAD
        nope = qt[lo:lo + QK_NOPE] * Q_SCALE
        x1 = qt[lo + QK_NOPE:lo + QK_NOPE + half]
        x2 = qt[lo + QK_NOPE + half:lo + QK_DIM]
        r1 = x1 * cos - x2 * sin
        r2 = x2 * cos + x1 * sin
        for c, cols in enumerate(chunks):
            qt_ref[0, c, lo:lo + QK_NOPE, :] = nope[:, cols].astype(BF16)
            qt_ref[0, c, lo + QK_NOPE:lo + QK_NOPE + half, :] = r1[:, cols].astype(BF16)
            qt_ref[0, c, lo + QK_NOPE + half:lo + QK_DIM, :] = r2[:, cols].astype(BF16)
            qt_ref[0, c, lo + QK_DIM:lo + HEAD_PAD, :] = zeros
    k_ref[...] = jnp.dot(ckv, wk_ref[...], preferred_element_type=F32).astype(BF16)
    vt = lax.dot_general(wvt_ref[...], ckv, nt, preferred_element_type=F32)
    for c, cols in enumerate(chunks):
        vt_ref[0, c] = vt[:, cols].astype(BF16)


def _qkv(cq, ckv, w_qt, w_k, w_vt, cos_t, sin_t, *, tm, tk, seq):
    rows = cq.shape[0]
    n_pos = seq // tm
    nh = MLA_HEADS
    row = lambda i: (i, 0)
    const = lambda i: (0, 0)
    tiled = lambda i: (i // n_pos, i % n_pos, 0, 0)
    return pl.pallas_call(
        _qkv_kernel,
        grid=(rows // tm,),
        in_specs=[
            pl.BlockSpec((tm, Q_LORA), row),
            pl.BlockSpec((tm, KV_LORA), row),
            pl.BlockSpec((nh * HEAD_PAD, Q_LORA), const),
            pl.BlockSpec((KV_LORA, nh * QK_NOPE), const),
            pl.BlockSpec((nh * V_DIM, KV_LORA), const),
            pl.BlockSpec((QK_ROPE // 2, tm), lambda i: (0, i % n_pos)),
            pl.BlockSpec((QK_ROPE // 2, tm), lambda i: (0, i % n_pos)),
        ],
        out_specs=[
            pl.BlockSpec((1, tm // tk, nh * HEAD_PAD, tk), tiled),
            pl.BlockSpec((tm, nh * QK_NOPE), row),
            pl.BlockSpec((1, tm // tk, nh * V_DIM, tk), tiled),
        ],
        out_shape=[
            jax.ShapeDtypeStruct((rows // seq, seq // tk, nh * HEAD_PAD, tk), BF16),
            jax.ShapeDtypeStruct((rows, nh * QK_NOPE), BF16),
            jax.ShapeDtypeStruct((rows // seq, seq // tk, nh * V_DIM, tk), BF16),
        ],
        compiler_params=_params(("parallel",)),
        name="qkv",
    )(cq, ckv, w_qt, w_k, w_vt, cos_t, sin_t)


def _attn_kernel(qi_ref, kj_ref, new_ref, q_ref, kn_ref, kp_ref, vt_ref, knm_ref, kpm_ref,
                 vtm_ref, o_ref, acc_ref, sa_ref, sb_ref, st_ref, pm_ref):
    tk = vt_ref.shape[-1]
    tq = tk
    hp = acc_ref.shape[0]
    n_pairs = qi_ref.shape[0]
    bufs = ((sa_ref, 2), (sb_ref, 3))

    def q_of(h, qi):
        return q_ref[0, qi, h * HEAD_PAD:(h + 1) * HEAD_PAD, :]

    def init_scores(qi, heads):
        kpm = kpm_ref[...]
        for h in heads:
            km = jnp.concatenate([knm_ref[:, h * QK_NOPE:(h + 1) * QK_NOPE], kpm], axis=1)
            s = jnp.dot(km, q_of(h, qi), preferred_element_type=F32)
            m0 = jnp.max(s, axis=0, keepdims=True)
            p = jnp.exp2(s - m0)
            pm_ref[h] = p.astype(BF16)
            st_ref[h, 0] = m0
            st_ref[h, 1] = jnp.sum(p, axis=0, keepdims=True)

    def init_values(heads):
        for h in heads:
            acc_ref[h] = jnp.dot(vtm_ref[h * V_DIM:(h + 1) * V_DIM, :], pm_ref[h],
                                 preferred_element_type=F32)

    def scores(t, dst, masked, heads):
        s_ref, row = dst
        qi = qi_ref[t]
        start = pl.multiple_of(kj_ref[t] * tk, tk)
        kp = kp_ref[0, pl.ds(start, tk), :]
        for h in heads:
            k = jnp.concatenate(
                [kn_ref[0, pl.ds(start, tk), h * QK_NOPE:(h + 1) * QK_NOPE], kp], axis=1)
            s = jnp.dot(k, q_of(h, qi), preferred_element_type=F32)
            if masked:
                kpos = lax.broadcasted_iota(jnp.int32, s.shape, 0)
                qpos = lax.broadcasted_iota(jnp.int32, s.shape, 1)
                s = jnp.where(kpos <= qpos, s, MASK_VALUE)
            s_ref[h] = s
            st_ref[h, row] = jnp.max(s, axis=0, keepdims=True)

    def absorb(t, src, heads):
        s_ref, row = src
        kj = kj_ref[t]
        ones = jnp.ones((BF16_ROWS, tk), BF16)
        for h in heads:
            m, l = st_ref[h, 0], st_ref[h, 1]
            m_new = jnp.maximum(m, st_ref[h, row])
            alpha = jnp.exp2(m - m_new)
            p = jnp.exp2(s_ref[h] - m_new)
            vt1 = jnp.concatenate([vt_ref[0, kj, h * V_DIM:(h + 1) * V_DIM, :], ones], axis=0)
            pv = jnp.dot(vt1, p.astype(BF16), preferred_element_type=F32)
            acc_ref[h] = alpha * acc_ref[h] + pv[:V_DIM]
            st_ref[h, 0] = m_new
            st_ref[h, 1] = alpha * l + pv[V_DIM:V_DIM + 1]

    def finalize(qi, heads):
        rows = pl.ds(pl.multiple_of(qi * tq, tq), tq)
        for h in heads:
            o_ref[0, rows, h * V_DIM:(h + 1) * V_DIM] = (
                acc_ref[h] / st_ref[h, 1]).T.astype(BF16)

    def run(stages):
        items = [(st, h) for st in stages for h in range(hp)]

        def put_scores(item):
            (t, _, dst, is_fresh), h = item
            scores(t, dst, is_fresh, [h])

        pending = []

        def put_rest(item):
            (t, src, _, is_fresh), h = item
            absorb(t - 1, src, [h])
            if is_fresh:
                finalize(qi_ref[t - 1], [h])
                init_scores(qi_ref[t], [h])
            init_values(pending)
            pending.clear()
            if is_fresh:
                pending.append(h)

        for item in items[:SCORE_LEAD + 1]:
            put_scores(item)
        for n, item in enumerate(items):
            put_rest(item)
            if n + SCORE_LEAD + 1 < len(items):
                put_scores(items[n + SCORE_LEAD + 1])
        init_values(pending)

    heads = list(range(hp))
    init_scores(qi_ref[0], heads)
    init_values(heads)
    scores(0, bufs[0], True, heads)

    def body(r, carry):
        t = 2 * r + 1
        a, b = bufs
        for fresh0, fresh1 in ((True, False), (False, True), (False, False)):
            cond = jnp.logical_and(new_ref[t] == int(fresh0), new_ref[t + 1] == int(fresh1))
            pl.when(cond)(functools.partial(
                run, [(t, a, b, fresh0), (t + 1, b, a, fresh1)]))
        return carry

    lax.fori_loop(0, (n_pairs - 1) // 2, body, 0)
    last = bufs[0]
    if (n_pairs - 1) % 2:
        t = n_pairs - 1
        for is_fresh in (True, False):
            pl.when(new_ref[t] == int(is_fresh))(functools.partial(
                run, [(t, bufs[0], bufs[1], is_fresh)]))
        last = bufs[1]
    absorb(n_pairs - 1, last, heads)
    finalize(qi_ref[n_pairs - 1], heads)


def _attn(qt, k_nope, k_pe, vt, knm, kpm, vtm, *, hp):
    b, s, _ = k_nope.shape
    tk = vt.shape[-1]
    nq = s // tk
    nh = MLA_HEADS
    qi, kj, new = [], [], []
    for i in range(nq):
        for n, j in enumerate([i] + list(range(i))):
            qi.append(i)
            kj.append(j)
            new.append(int(n == 0))
    assert not any(new[t] and new[t + 1] for t in range(1, len(new) - 1, 2))
    tables = [jnp.asarray(v, jnp.int32) for v in (qi, kj, new)]
    grid_spec = pltpu.PrefetchScalarGridSpec(
        num_scalar_prefetch=len(tables),
        grid=(b, nh // hp),
        in_specs=[
            pl.BlockSpec((1, nq, hp * HEAD_PAD, tk), lambda bi, h, *_: (bi, 0, h, 0)),
            pl.BlockSpec((1, s, hp * QK_NOPE), lambda bi, h, *_: (bi, 0, h)),
            pl.BlockSpec((1, s, LANES), lambda bi, h, *_: (bi, 0, 0)),
            pl.BlockSpec((1, nq, hp * V_DIM, tk), lambda bi, h, *_: (bi, 0, h, 0)),
            pl.BlockSpec((N_META, hp * QK_NOPE), lambda bi, h, *_: (0, h)),
            pl.BlockSpec((N_META, LANES), lambda bi, h, *_: (0, 0)),
            pl.BlockSpec((hp * V_DIM, N_META), lambda bi, h, *_: (h, 0)),
        ],
        out_specs=pl.BlockSpec((1, s, hp * V_DIM), lambda bi, h, *_: (bi, 0, h)),
        scratch_shapes=[pltpu.VMEM((hp, V_DIM, tk), F32), pltpu.VMEM((hp, tk, tk), F32),
                        pltpu.VMEM((hp, tk, tk), F32), pltpu.VMEM((hp, 4, 1, tk), F32),
                        pltpu.VMEM((hp, N_META, tk), BF16)],
    )
    return pl.pallas_call(
        _attn_kernel,
        grid_spec=grid_spec,
        out_shape=jax.ShapeDtypeStruct((b, s, nh * V_DIM), BF16),
        compiler_params=_params(("parallel", "parallel")),
        name="attn",
    )(*tables, qt, k_nope, k_pe, vt, knm, kpm, vtm)


def _mixout_kernel(h_ref, gmix_ref, gpost_ref, u_ref, uprev_ref, umeta_ref, pw_ref, pscale_ref,
                   attn_ref, wgp_ref, wgm_ref, wpo_ref, wmo_ref, wout_ref, o_ref,
                   hn_ref, pool_ref, buf_ref, *, tiles_per_seq):
    c = pl.program_id(1)
    last = pl.num_programs(1) - 1
    chunks = _row_chunks(h_ref.shape[0], ROW_CHUNK)

    def pool(rows, halo):
        for g, win in enumerate(POOL_WINDOWS):
            cols = slice(g * POOL_GROUP, (g + 1) * POOL_GROUP)
            buf_ref[:N_META, :] = halo[:, cols]
            buf_ref[N_META:, :] = u_ref[rows, cols]
            tot = buf_ref[...]
            shift = 1
            while shift < win:
                tot = tot + pltpu.roll(tot, shift, 0)
                shift *= 2
            dlt = (tot[N_META:, :] * (1.0 / win) - buf_ref[N_META:, :]).astype(BF16)
            y = jnp.dot(dlt, pw_ref[g], preferred_element_type=F32)
            pool_ref[rows, cols] = (y * pscale_ref[:, cols]).astype(BF16)

    def front(hn, rows):
        return (jnp.dot(hn, wgp_ref[...], preferred_element_type=F32),
                jnp.dot(hn, wgm_ref[...], preferred_element_type=F32),
                jnp.dot(pool_ref[rows, :], wpo_ref[...], preferred_element_type=F32),
                jnp.dot(attn_ref[rows, :], wmo_ref[...], preferred_element_type=F32))

    def back(parts):
        g_pool, g_mla, y_pool, y_mla = parts
        y = (jax.nn.sigmoid(g_pool) * y_pool + jax.nn.sigmoid(g_mla) * y_mla).astype(BF16)
        return jnp.dot(y, wout_ref[...], preferred_element_type=F32)

    def mix_chunks(get_hn, put):
        ahead = front(get_hn(chunks[0]), chunks[0])
        for n, rows in enumerate(chunks):
            parts = ahead
            if n + 1 < len(chunks):
                ahead = front(get_hn(chunks[n + 1]), chunks[n + 1])
            put(rows, back(parts))

    @pl.when(c == 0)
    def _():
        seq_start = pl.program_id(0) % tiles_per_seq == 0

        def prepared(rows):
            if rows.start == 0:
                halo = jnp.where(seq_start, umeta_ref[...], uprev_ref[...])
            else:
                halo = u_ref[rows.start - N_META:rows.start, :]
            pool(rows, halo)
            hn = (_rms(h_ref[rows, :]) * gmix_ref[...]).astype(BF16)
            hn_ref[rows, :] = hn
            return hn

        def put(rows, d):
            o_ref[rows, :] = d

        mix_chunks(prepared, put)

    @pl.when(jnp.logical_and(c > 0, c < last))
    def _():
        def put(rows, d):
            o_ref[rows, :] += d

        mix_chunks(lambda rows: hn_ref[rows, :], put)

    @pl.when(c == last)
    def _():
        def put(rows, d):
            y = o_ref[rows, :] + d
            o_ref[rows, :] = h_ref[rows, :] + _rms(y) * gpost_ref[...]

        mix_chunks(lambda rows: hn_ref[rows, :], put)


def _mixout(h, g_mix, g_post, u, u_meta, pool_w, pool_scale, attn,
            w_gp, w_gm, w_po, w_mo, w_out, *, tm, tn, seq):
    rows, d = h.shape
    pw = u.shape[1]
    hb = tm // N_META
    chunk = min(ROW_CHUNK, tm)
    row = lambda i, c: (i, 0)
    const = lambda i, c: (0, 0)
    col = lambda i, c: (0, c)
    return pl.pallas_call(
        functools.partial(_mixout_kernel, tiles_per_seq=seq // tm),
        grid=(rows // tm, d // tn),
        in_specs=[
            pl.BlockSpec((tm, d), row),
            pl.BlockSpec((1, d), const),
            pl.BlockSpec((1, d), const),
            pl.BlockSpec((tm, pw), row),
            pl.BlockSpec((N_META, pw), lambda i, c: (jnp.maximum(i * hb - 1, 0), 0)),
            pl.BlockSpec((N_META, pw), const),
            pl.BlockSpec((len(POOL_WINDOWS), POOL_GROUP, POOL_GROUP), lambda i, c: (0, 0, 0)),
            pl.BlockSpec((1, pw), const),
            pl.BlockSpec((tm, attn.shape[1]), row),
            pl.BlockSpec((d, tn), col),
            pl.BlockSpec((d, tn), col),
            pl.BlockSpec((pw, tn), col),
            pl.BlockSpec((attn.shape[1], tn), col),
            pl.BlockSpec((tn, d), lambda i, c: (c, 0)),
        ],
        out_specs=pl.BlockSpec((tm, d), row),
        out_shape=jax.ShapeDtypeStruct((rows, d), F32),
        scratch_shapes=[pltpu.VMEM((tm, d), BF16), pltpu.VMEM((tm, pw), BF16),
                        pltpu.VMEM((chunk + N_META, POOL_GROUP), F32)],
        compiler_params=_params(("parallel", "arbitrary")),
        name="mixout",
    )(h, g_mix, g_post, u, u, u_meta, pool_w, pool_scale, attn, w_gp, w_gm, w_po, w_mo, w_out)


def _rope_tables(n_pos):
    pos = jnp.arange(n_pos, dtype=F32)
    inv = ROPE_THETA ** (-jnp.arange(0, QK_ROPE, 2, dtype=F32) / QK_ROPE)
    ang = pos[:, None] * inv[None, :]
    cos, sin = jnp.cos(ang), jnp.sin(ang)
    pad = jnp.zeros((n_pos, LANES - QK_ROPE), F32)
    return (jnp.concatenate([cos, cos, pad], axis=-1), jnp.concatenate([sin, sin, pad], axis=-1),
            cos.T, sin.T)


def kernel(x, meta_tokens, norm_ffn1_pre, norm_ffn1_post, ffn1_w_gu, ffn1_w_down, norm_mix_pre, norm_mix_post, w_in, pool_w, pool_scale, w_pool_o, q_a_norm, w_q_b, kv_a_norm, w_kv_b, w_mla_o, w_out, norm_ffn2_pre, norm_ffn2_post, ffn2_w_gu, ffn2_w_down):
    bsz, seq, d = x.shape
    depth = w_in.shape[0]
    nh = MLA_HEADS
    tm, tf, tn, tq = 512, 512, 512, 512
    tm_ffn = 1024

    cos, sin, cos_t, sin_t = _rope_tables(N_META + seq)
    cos_m, sin_m, cos_r, sin_r = cos[:N_META], sin[:N_META], cos[N_META:], sin[N_META:]
    cos_tm, sin_tm, cos_tr, sin_tr = (cos_t[:, :N_META], sin_t[:, :N_META],
                                      cos_t[:, N_META:], sin_t[:, N_META:])

    h = x.reshape(bsz * seq, d)
    hm = meta_tokens.astype(x.dtype)
    row = lambda v: v.reshape(1, -1)

    for i in range(depth):
        w_gu1, w_dn1 = ffn1_w_gu[i].astype(BF16), ffn1_w_down[i].astype(BF16)
        w_q = jnp.pad(w_q_b[i].reshape(Q_LORA, nh, QK_DIM),
                      ((0, 0), (0, 0), (0, HEAD_PAD - QK_DIM))).reshape(Q_LORA, nh * HEAD_PAD)
        w_qt = w_q.T.astype(BF16)
        w_kv = w_kv_b[i].reshape(KV_LORA, nh, QK_NOPE + V_DIM)
        w_k = w_kv[:, :, :QK_NOPE].reshape(KV_LORA, nh * QK_NOPE).astype(BF16)
        w_vt = w_kv[:, :, QK_NOPE:].reshape(KV_LORA, nh * V_DIM).T.astype(BF16)
        p_w = pool_w[i].astype(BF16)

        h1, h1m, w_gu2, w_dn2 = _ffn(
            h, row(norm_ffn1_pre[i]), row(norm_ffn1_post[i]), w_gu1, w_dn1, tm=tm_ffn, tf=tf,
            side=hm, casts=(ffn2_w_gu[i], ffn2_w_down[i]))
        w_in_bf = w_in[i].astype(BF16)
        n_lat = POOL_WIDTH + Q_LORA + KV_LORA + QK_ROPE
        w_gp, w_gm = w_in_bf[:, n_lat:n_lat + d], w_in_bf[:, n_lat + d:]

        u, cq, ckv, kpe, w_po, w_mo, w_o = _inproj(
            h1, row(norm_mix_pre[i]), w_in_bf, row(q_a_norm[i]), row(kv_a_norm[i]), cos_r, sin_r,
            tm=tm, casts=(w_pool_o[i], w_mla_o[i], w_out[i]))
        um, cqm, ckvm, kpem = _inproj(h1m, row(norm_mix_pre[i]), w_in_bf, row(q_a_norm[i]),
                                      row(kv_a_norm[i]), cos_m, sin_m, tm=N_META)
        qt, kn, vt = _qkv(cq, ckv, w_qt, w_k, w_vt, cos_tr, sin_tr, tm=tm, tk=tq, seq=seq)
        _, knm, vtm = _qkv(cqm, ckvm, w_qt, w_k, w_vt, cos_tm, sin_tm,
                           tm=N_META, tk=N_META, seq=N_META)

        attn = _attn(qt, kn.reshape(bsz, seq, nh * QK_NOPE),
                     kpe.reshape(bsz, seq, LANES), vt, knm, kpem,
                     vtm.reshape(nh * V_DIM, N_META), hp=4)

        h2 = _mixout(h1, row(norm_mix_pre[i]), row(norm_mix_post[i]), u, um, p_w,
                     row(pool_scale[i]), attn.reshape(bsz * seq, nh * V_DIM),
                     w_gp, w_gm, w_po, w_mo, w_o, tm=tm, tn=tn, seq=seq)

        h, = _ffn(h2, row(norm_ffn2_pre[i]), row(norm_ffn2_post[i]), w_gu2, w_dn2,
                  tm=tm_ffn, tf=tf)
        if i + 1 < depth:
            raise NotImplementedError("only DEPTH == 1 is supported")

    return h.reshape(bsz, seq, d)
```

```python
import functools
import math

import jax
import jax.numpy as jnp
from jax import lax
from jax.experimental import pallas as pl
from jax.experimental.pallas import tpu as pltpu

F32 = jnp.float32
BF16 = jnp.bfloat16

N_META = 16
POOL_WINDOWS = (2, 4, 8, 16)
POOL_GROUP = 256
POOL_WIDTH = POOL_GROUP * len(POOL_WINDOWS)
MLA_HEADS = 16
Q_LORA = 512
KV_LORA = 512
QK_NOPE = 128
QK_ROPE = 64
V_DIM = 128
QK_DIM = QK_NOPE + QK_ROPE
ROPE_THETA = 10000.0
EPS = 1e-6
LANES = 128
BF16_ROWS = 16
HEAD_PAD = 2 * LANES
Q_SCALE = (QK_DIM ** -0.5) * math.log2(math.e)
MASK_VALUE = -1e30

VMEM_LIMIT = 56 * 1024 * 1024
FFN_VMEM_LIMIT = 62 * 1024 * 1024
ROW_CHUNK = 256
SCORE_LEAD = 0


def _rms(x):
    return x * lax.rsqrt(jnp.mean(x * x, axis=-1, keepdims=True) + EPS)


def _rope(x, cos, sin):
    rot = pltpu.roll(x, 32, 1) - pltpu.roll(x, 96, 1)
    return x * cos + rot * sin


def _params(sem, vmem_limit=VMEM_LIMIT):
    return pltpu.CompilerParams(dimension_semantics=sem, vmem_limit_bytes=vmem_limit)


def _row_chunks(rows, chunk):
    chunk = min(chunk, rows)
    return [slice(r, r + chunk) for r in range(0, rows, chunk)]


def _ffn_kernel(*refs, n_casts, has_side):
    ns = int(has_side)
    x_ref, gpre_ref, gpost_ref, wg_ref, wu_ref, wd_ref = refs[:6]
    n_in = 6 + ns + n_casts
    cast_in = refs[6 + ns:n_in]
    o_ref = refs[n_in]
    cast_out = refs[n_in + 1 + ns:n_in + 1 + ns + n_casts]
    xn_ref = refs[n_in + 1 + ns + n_casts]
    f = pl.program_id(1)
    last = pl.num_programs(1) - 1
    chunks = _row_chunks(x_ref.shape[0], ROW_CHUNK)

    def gate_up(xn):
        return (jnp.dot(xn, wg_ref[...], preferred_element_type=F32),
                jnp.dot(xn, wu_ref[...], preferred_element_type=F32))

    def down(gu):
        g, u = gu
        a = ((g * jax.nn.sigmoid(g)) * u).astype(BF16)
        return jnp.dot(a, wd_ref[...], preferred_element_type=F32)

    def mlp_chunks(chunks, get_xn, put):
        ahead = gate_up(get_xn(chunks[0]))
        for n, rows in enumerate(chunks):
            gu = ahead
            if n + 1 < len(chunks):
                ahead = gate_up(get_xn(chunks[n + 1]))
            put(rows, down(gu))
            if n == 0:
                casts()

    def casts():
        for src, dst in zip(cast_in, cast_out):
            dst[...] = src[...].astype(BF16)

    def side(step):
        if not has_side:
            return
        xs_ref, os_ref, xsn_ref = refs[6], refs[n_in + 1], refs[-1]

        @pl.when(pl.program_id(0) == 0)
        def _():
            if step == "first":
                xsn_ref[...] = (_rms(xs_ref[...]) * gpre_ref[...]).astype(BF16)
                os_ref[...] = down(gate_up(xsn_ref[...]))
            elif step == "middle":
                os_ref[...] += down(gate_up(xsn_ref[...]))
            else:
                y = os_ref[...] + down(gate_up(xsn_ref[...]))
                os_ref[...] = xs_ref[...] + 0.5 * (_rms(y) * gpost_ref[...])

    @pl.when(f == 0)
    def _():
        def normed(rows):
            xn = (_rms(x_ref[rows, :]) * gpre_ref[...]).astype(BF16)
            xn_ref[rows, :] = xn
            return xn

        def put(rows, d):
            o_ref[rows, :] = d

        mlp_chunks(chunks, normed, put)
        side("first")

    @pl.when(jnp.logical_and(f > 0, f < last))
    def _():
        def put(rows, d):
            o_ref[rows, :] += d

        mlp_chunks(chunks, lambda rows: xn_ref[rows, :], put)
        side("middle")

    @pl.when(f == last)
    def _():
        def put(rows, d):
            y = o_ref[rows, :] + d
            o_ref[rows, :] = x_ref[rows, :] + 0.5 * (_rms(y) * gpost_ref[...])

        mlp_chunks(chunks, lambda rows: xn_ref[rows, :], put)
        side("last")


def _ffn(x, g_pre, g_post, w_gu, w_down, *, tm, tf, side=None, casts=()):
    rows, d = x.shape
    d_ff = w_down.shape[0]
    nf = d_ff // tf
    ni = rows // tm
    const = lambda i, f: (0, 0)
    cast_specs = []
    for c in casts:
        r, k = c.shape
        if r % (ni * BF16_ROWS) == 0 and k % (nf * LANES) == 0:
            cast_specs.append(pl.BlockSpec((r // ni, k // nf), lambda i, f: (i, f)))
        else:
            s = max(v for v in range(1, nf + 1) if r % (ni * v * BF16_ROWS) == 0)
            cast_specs.append(pl.BlockSpec(
                (r // (ni * s), k), lambda i, f, s=s: (i * s + (f * s) // nf, 0)))
    sides = [] if side is None else [side]
    side_specs = [pl.BlockSpec(v.shape, const) for v in sides]
    return pl.pallas_call(
        functools.partial(_ffn_kernel, n_casts=len(casts), has_side=bool(sides)),
        grid=(ni, nf),
        in_specs=[
            pl.BlockSpec((tm, d), lambda i, f: (i, 0)),
            pl.BlockSpec((1, d), const),
            pl.BlockSpec((1, d), const),
            pl.BlockSpec((d, tf), lambda i, f: (0, f)),
            pl.BlockSpec((d, tf), lambda i, f: (0, f + nf)),
            pl.BlockSpec((tf, d), lambda i, f: (f, 0)),
        ] + side_specs + cast_specs,
        out_specs=[pl.BlockSpec((tm, d), lambda i, f: (i, 0))] + side_specs + cast_specs,
        out_shape=[jax.ShapeDtypeStruct((rows, d), F32)]
        + [jax.ShapeDtypeStruct(v.shape, F32) for v in sides]
        + [jax.ShapeDtypeStruct(c.shape, BF16) for c in casts],
        scratch_shapes=[pltpu.VMEM((tm, d), BF16)]
        + [pltpu.VMEM(v.shape, BF16) for v in sides],
        compiler_params=_params(("arbitrary", "arbitrary"), FFN_VMEM_LIMIT),
        name="ffn",
    )(x, g_pre, g_post, w_gu, w_gu, w_down, *sides, *casts)


def _inproj_kernel(*refs, n_casts):
    h_ref, gmix_ref, w_ref, gq_ref, gkv_ref, cos_ref, sin_ref = refs[:7]
    cast_in = refs[7:7 + n_casts]
    u_ref, cq_ref, ckv_ref, kpe_ref = refs[7 + n_casts:11 + n_casts]
    cast_out = refs[11 + n_casts:]
    c0, c1, c2, c3 = POOL_WIDTH, POOL_WIDTH + Q_LORA, POOL_WIDTH + Q_LORA + KV_LORA, w_ref.shape[1]
    for n, rows in enumerate(_row_chunks(h_ref.shape[0], ROW_CHUNK)):
        hn = (_rms(h_ref[rows, :]) * gmix_ref[...]).astype(BF16)
        u_ref[rows, :] = jnp.dot(hn, w_ref[:, :c0], preferred_element_type=F32)
        if n == 0:
            for src, dst in zip(cast_in, cast_out):
                dst[...] = src[...].astype(BF16)
        cq = jnp.dot(hn, w_ref[:, c0:c1], preferred_element_type=F32)
        cq_ref[rows, :] = (_rms(cq) * gq_ref[...]).astype(BF16)
        ckv = jnp.dot(hn, w_ref[:, c1:c2], preferred_element_type=F32)
        ckv_ref[rows, :] = (_rms(ckv) * gkv_ref[...]).astype(BF16)
        kr = jnp.dot(hn, w_ref[:, c2:c3], preferred_element_type=F32)
        kr = jnp.where(lax.broadcasted_iota(jnp.int32, kr.shape, 1) < QK_ROPE, kr, 0.0)
        kpe_ref[rows, :] = _rope(kr, cos_ref[rows, :], sin_ref[rows, :]).astype(BF16)


def _inproj(h, g_mix, w_in, g_q, g_kv, cos, sin, *, tm, casts=()):
    rows, d = h.shape
    n_pos = cos.shape[0] // tm
    wcols = POOL_WIDTH + Q_LORA + KV_LORA + LANES
    steps = rows // tm
    row = lambda i: (i, 0)
    const = lambda i: (0, 0)
    pos = lambda i: (i % n_pos, 0)
    cast_specs = [pl.BlockSpec((c.shape[0] // steps, c.shape[1]), row) for c in casts]
    assert all(c.shape[0] % (steps * BF16_ROWS) == 0 for c in casts)
    return pl.pallas_call(
        functools.partial(_inproj_kernel, n_casts=len(casts)),
        grid=(steps,),
        in_specs=[
            pl.BlockSpec((tm, d), row),
            pl.BlockSpec((1, d), const),
            pl.BlockSpec((d, wcols), const),
            pl.BlockSpec((1, Q_LORA), const),
            pl.BlockSpec((1, KV_LORA), const),
            pl.BlockSpec((tm, LANES), pos),
            pl.BlockSpec((tm, LANES), pos),
        ] + cast_specs,
        out_specs=[
            pl.BlockSpec((tm, POOL_WIDTH), row),
            pl.BlockSpec((tm, Q_LORA), row),
            pl.BlockSpec((tm, KV_LORA), row),
            pl.BlockSpec((tm, LANES), row),
        ] + cast_specs,
        out_shape=[
            jax.ShapeDtypeStruct((rows, POOL_WIDTH), F32),
            jax.ShapeDtypeStruct((rows, Q_LORA), BF16),
            jax.ShapeDtypeStruct((rows, KV_LORA), BF16),
            jax.ShapeDtypeStruct((rows, LANES), BF16),
        ] + [jax.ShapeDtypeStruct(c.shape, BF16) for c in casts],
        compiler_params=_params(("parallel",)),
        name="inproj",
    )(h, g_mix, w_in, g_q, g_kv, cos, sin, *casts)


def _qkv_kernel(cq_ref, ckv_ref, wqt_ref, wk_ref, wvt_ref, cos_ref, sin_ref,
                qt_ref, k_ref, vt_ref):
    cq = cq_ref[...]
    ckv = ckv_ref[...]
    nt = (((1,), (1,)), ((), ()))
    tk = vt_ref.shape[-1]
    chunks = [slice(c * tk, (c + 1) * tk) for c in range(vt_ref.shape[1])]
    cos = cos_ref[...] * Q_SCALE
    sin = sin_ref[...] * Q_SCALE
    half = QK_ROPE // 2
    qt = lax.dot_general(wqt_ref[...], cq, nt, preferred_element_type=F32)
    zeros = jnp.zeros((HEAD_PAD - QK_DIM, tk), BF16)
    for h in range(MLA_HEADS):
        lo = h * HEAD_PAD
        nope = qt[lo:lo + QK_NOPE] * Q_SCALE
        x1 = qt[lo + QK_NOPE:lo + QK_NOPE + half]
        x2 = qt[lo + QK_NOPE + half:lo + QK_DIM]
        r1 = x1 * cos - x2 * sin
        r2 = x2 * cos + x1 * sin
        for c, cols in enumerate(chunks):
            qt_ref[0, c, lo:lo + QK_NOPE, :] = nope[:, cols].astype(BF16)
            qt_ref[0, c, lo + QK_NOPE:lo + QK_NOPE + half, :] = r1[:, cols].astype(BF16)
            qt_ref[0, c, lo + QK_NOPE + half:lo + QK_DIM, :] = r2[:, cols].astype(BF16)
            qt_ref[0, c, lo + QK_DIM:lo + HEAD_PAD, :] = zeros
    k_ref[...] = jnp.dot(ckv, wk_ref[...], preferred_element_type=F32).astype(BF16)
    vt = lax.dot_general(wvt_ref[...], ckv, nt, preferred_element_type=F32)
    for c, cols in enumerate(chunks):
        vt_ref[0, c] = vt[:, cols].astype(BF16)


def _qkv(cq, ckv, w_qt, w_k, w_vt, cos_t, sin_t, *, tm, tk, seq):
    rows = cq.shape[0]
    n_pos = seq // tm
    nh = MLA_HEADS
    row = lambda i: (i, 0)
    const = lambda i: (0, 0)
    tiled = lambda i: (i // n_pos, i % n_pos, 0, 0)
    return pl.pallas_call(
        _qkv_kernel,
        grid=(rows // tm,),
        in_specs=[
            pl.BlockSpec((tm, Q_LORA), row),
            pl.BlockSpec((tm, KV_LORA), row),
            pl.BlockSpec((nh * HEAD_PAD, Q_LORA), const),
            pl.BlockSpec((KV_LORA, nh * QK_NOPE), const),
            pl.BlockSpec((nh * V_DIM, KV_LORA), const),
            pl.BlockSpec((QK_ROPE // 2, tm), lambda i: (0, i % n_pos)),
            pl.BlockSpec((QK_ROPE // 2, tm), lambda i: (0, i % n_pos)),
        ],
        out_specs=[
            pl.BlockSpec((1, tm // tk, nh * HEAD_PAD, tk), tiled),
            pl.BlockSpec((tm, nh * QK_NOPE), row),
            pl.BlockSpec((1, tm // tk, nh * V_DIM, tk), tiled),
        ],
        out_shape=[
            jax.ShapeDtypeStruct((rows // seq, seq // tk, nh * HEAD_PAD, tk), BF16),
            jax.ShapeDtypeStruct((rows, nh * QK_NOPE), BF16),
            jax.ShapeDtypeStruct((rows // seq, seq // tk, nh * V_DIM, tk), BF16),
        ],
        compiler_params=_params(("parallel",)),
        name="qkv",
    )(cq, ckv, w_qt, w_k, w_vt, cos_t, sin_t)


def _attn_kernel(qi_ref, kj_ref, new_ref, q_ref, kn_ref, kp_ref, vt_ref, knm_ref, kpm_ref,
                 vtm_ref, o_ref, acc_ref, sa_ref, sb_ref, st_ref, pm_ref):
    tk = vt_ref.shape[-1]
    tq = tk
    hp = acc_ref.shape[0]
    n_pairs = qi_ref.shape[0]
    bufs = ((sa_ref, 2), (sb_ref, 3))

    def q_of(h, qi):
        return q_ref[0, qi, h * HEAD_PAD:(h + 1) * HEAD_PAD, :]

    def init_scores(qi, heads):
        kpm = kpm_ref[...]
        for h in heads:
            km = jnp.concatenate([knm_ref[:, h * QK_NOPE:(h + 1) * QK_NOPE], kpm], axis=1)
            s = jnp.dot(km, q_of(h, qi), preferred_element_type=F32)
            m0 = jnp.max(s, axis=0, keepdims=True)
            p = jnp.exp2(s - m0)
            pm_ref[h] = p.astype(BF16)
            st_ref[h, 0] = m0
            st_ref[h, 1] = jnp.sum(p, axis=0, keepdims=True)

    def init_values(heads):
        for h in heads:
            acc_ref[h] = jnp.dot(vtm_ref[h * V_DIM:(h + 1) * V_DIM, :], pm_ref[h],
                                 preferred_element_type=F32)

    def scores(t, dst, masked, heads):
        s_ref, row = dst
        qi = qi_ref[t]
        start = pl.multiple_of(kj_ref[t] * tk, tk)
        kp = kp_ref[0, pl.ds(start, tk), :]
        for h in heads:
            k = jnp.concatenate(
                [kn_ref[0, pl.ds(start, tk), h * QK_NOPE:(h + 1) * QK_NOPE], kp], axis=1)
            s = jnp.dot(k, q_of(h, qi), preferred_element_type=F32)
            if masked:
                kpos = lax.broadcasted_iota(jnp.int32, s.shape, 0)
                qpos = lax.broadcasted_iota(jnp.int32, s.shape, 1)
                s = jnp.where(kpos <= qpos, s, MASK_VALUE)
            s_ref[h] = s
            st_ref[h, row] = jnp.max(s, axis=0, keepdims=True)

    def absorb(t, src, heads):
        s_ref, row = src
        kj = kj_ref[t]
        ones = jnp.ones((BF16_ROWS, tk), BF16)
        for h in heads:
            m, l = st_ref[h, 0], st_ref[h, 1]
            m_new = jnp.maximum(m, st_ref[h, row])
            alpha = jnp.exp2(m - m_new)
            p = jnp.exp2(s_ref[h] - m_new)
            vt1 = jnp.concatenate([vt_ref[0, kj, h * V_DIM:(h + 1) * V_DIM, :], ones], axis=0)
            pv = jnp.dot(vt1, p.astype(BF16), preferred_element_type=F32)
            acc_ref[h] = alpha * acc_ref[h] + pv[:V_DIM]
            st_ref[h, 0] = m_new
            st_ref[h, 1] = alpha * l + pv[V_DIM:V_DIM + 1]

    def finalize(qi, heads):
        rows = pl.ds(pl.multiple_of(qi * tq, tq), tq)
        for h in heads:
            o_ref[0, rows, h * V_DIM:(h + 1) * V_DIM] = (
                acc_ref[h] / st_ref[h, 1]).T.astype(BF16)

    def run(stages):
        items = [(st, h) for st in stages for h in range(hp)]

        def put_scores(item):
            (t, _, dst, is_fresh), h = item
            scores(t, dst, is_fresh, [h])

        pending = []

        def put_rest(item):
            (t, src, _, is_fresh), h = item
            absorb(t - 1, src, [h])
            if is_fresh:
                finalize(qi_ref[t - 1], [h])
                init_scores(qi_ref[t], [h])
            init_values(pending)
            pending.clear()
            if is_fresh:
                pending.append(h)

        for item in items[:SCORE_LEAD + 1]:
            put_scores(item)
        for n, item in enumerate(items):
            put_rest(item)
            if n + SCORE_LEAD + 1 < len(items):
                put_scores(items[n + SCORE_LEAD + 1])
        init_values(pending)

    heads = list(range(hp))
    init_scores(qi_ref[0], heads)
    init_values(heads)
    scores(0, bufs[0], True, heads)

    def body(r, carry):
        t = 2 * r + 1
        a, b = bufs
        for fresh0, fresh1 in ((True, False), (False, True), (False, False)):
            cond = jnp.logical_and(new_ref[t] == int(fresh0), new_ref[t + 1] == int(fresh1))
            pl.when(cond)(functools.partial(
                run, [(t, a, b, fresh0), (t + 1, b, a, fresh1)]))
        return carry

    lax.fori_loop(0, (n_pairs - 1) // 2, body, 0)
    last = bufs[0]
    if (n_pairs - 1) % 2:
        t = n_pairs - 1
        for is_fresh in (True, False):
            pl.when(new_ref[t] == int(is_fresh))(functools.partial(
                run, [(t, bufs[0], bufs[1], is_fresh)]))
        last = bufs[1]
    absorb(n_pairs - 1, last, heads)
    finalize(qi_ref[n_pairs - 1], heads)


def _attn(qt, k_nope, k_pe, vt, knm, kpm, vtm, *, hp):
    b, s, _ = k_nope.shape
    tk = vt.shape[-1]
    nq = s // tk
    nh = MLA_HEADS
    qi, kj, new = [], [], []
    for i in range(nq):
        for n, j in enumerate([i] + list(range(i))):
            qi.append(i)
            kj.append(j)
            new.append(int(n == 0))
    assert not any(new[t] and new[t + 1] for t in range(1, len(new) - 1, 2))
    tables = [jnp.asarray(v, jnp.int32) for v in (qi, kj, new)]
    grid_spec = pltpu.PrefetchScalarGridSpec(
        num_scalar_prefetch=len(tables),
        grid=(b, nh // hp),
        in_specs=[
            pl.BlockSpec((1, nq, hp * HEAD_PAD, tk), lambda bi, h, *_: (bi, 0, h, 0)),
            pl.BlockSpec((1, s, hp * QK_NOPE), lambda bi, h, *_: (bi, 0, h)),
            pl.BlockSpec((1, s, LANES), lambda bi, h, *_: (bi, 0, 0)),
            pl.BlockSpec((1, nq, hp * V_DIM, tk), lambda bi, h, *_: (bi, 0, h, 0)),
            pl.BlockSpec((N_META, hp * QK_NOPE), lambda bi, h, *_: (0, h)),
            pl.BlockSpec((N_META, LANES), lambda bi, h, *_: (0, 0)),
            pl.BlockSpec((hp * V_DIM, N_META), lambda bi, h, *_: (h, 0)),
        ],
        out_specs=pl.BlockSpec((1, s, hp * V_DIM), lambda bi, h, *_: (bi, 0, h)),
        scratch_shapes=[pltpu.VMEM((hp, V_DIM, tk), F32), pltpu.VMEM((hp, tk, tk), F32),
                        pltpu.VMEM((hp, tk, tk), F32), pltpu.VMEM((hp, 4, 1, tk), F32),
                        pltpu.VMEM((hp, N_META, tk), BF16)],
    )
    return pl.pallas_call(
        _attn_kernel,
        grid_spec=grid_spec,
        out_shape=jax.ShapeDtypeStruct((b, s, nh * V_DIM), BF16),
        compiler_params=_params(("parallel", "parallel")),
        name="attn",
    )(*tables, qt, k_nope, k_pe, vt, knm, kpm, vtm)


def _mixout_kernel(h_ref, gmix_ref, gpost_ref, u_ref, uprev_ref, umeta_ref, pw_ref, pscale_ref,
                   attn_ref, wgp_ref, wgm_ref, wpo_ref, wmo_ref, wout_ref, o_ref,
                   hn_ref, pool_ref, buf_ref, *, tiles_per_seq):
    c = pl.program_id(1)
    last = pl.num_programs(1) - 1
    chunks = _row_chunks(h_ref.shape[0], ROW_CHUNK)

    def pool_steps(rows, halo):
        def group(g, win):
            cols = slice(g * POOL_GROUP, (g + 1) * POOL_GROUP)
            buf_ref[:N_META, :] = halo[:, cols]
            buf_ref[N_META:, :] = u_ref[rows, cols]
            tot = buf_ref[...]
            shift = 1
            while shift < win:
                tot = tot + pltpu.roll(tot, shift, 0)
                shift *= 2
            dlt = (tot[N_META:, :] * (1.0 / win) - buf_ref[N_META:, :]).astype(BF16)
            y = jnp.dot(dlt, pw_ref[g], preferred_element_type=F32)
            pool_ref[rows, cols] = (y * pscale_ref[:, cols]).astype(BF16)

        return [functools.partial(group, g, win) for g, win in enumerate(POOL_WINDOWS)]

    def front(hn, rows, between=()):
        between = list(between) + [None] * 4
        parts = []
        for n, (lhs, w_ref) in enumerate(((hn, wgp_ref), (hn, wgm_ref), (pool_ref[rows, :], wpo_ref),
                                         (attn_ref[rows, :], wmo_ref))):
            parts.append(jnp.dot(lhs, w_ref[...], preferred_element_type=F32))
            if between[n] is not None:
                between[n]()
        return parts

    def back(parts):
        g_pool, g_mla, y_pool, y_mla = parts
        y = (jax.nn.sigmoid(g_pool) * y_pool + jax.nn.sigmoid(g_mla) * y_mla).astype(BF16)
        return jnp.dot(y, wout_ref[...], preferred_element_type=F32)

    def mix_chunks(get_hn, put):
        ahead = front(get_hn(chunks[0]), chunks[0])
        for n, rows in enumerate(chunks):
            parts = ahead
            if n + 1 < len(chunks):
                ahead = front(get_hn(chunks[n + 1]), chunks[n + 1])
            put(rows, back(parts))

    @pl.when(c == 0)
    def _():
        seq_start = pl.program_id(0) % tiles_per_seq == 0

        def pooling(rows):
            if rows.start == 0:
                halo = jnp.where(seq_start, umeta_ref[...], uprev_ref[...])
            else:
                halo = u_ref[rows.start - N_META:rows.start, :]
            return pool_steps(rows, halo)

        def normed(rows):
            hn = (_rms(h_ref[rows, :]) * gmix_ref[...]).astype(BF16)
            hn_ref[rows, :] = hn
            return hn

        for step in pooling(chunks[0]):
            step()
        hn = normed(chunks[0])
        waiting = None
        for n, rows in enumerate(chunks):
            later = pooling(chunks[n + 1]) if n + 1 < len(chunks) else ()
            parts = front(hn, rows, later)
            if n + 1 < len(chunks):
                hn = normed(chunks[n + 1])
            if waiting is not None:
                o_ref[waiting[0], :] = back(waiting[1])
            waiting = (rows, parts)
        o_ref[waiting[0], :] = back(waiting[1])

    @pl.when(jnp.logical_and(c > 0, c < last))
    def _():
        def put(rows, d):
            o_ref[rows, :] += d

        mix_chunks(lambda rows: hn_ref[rows, :], put)

    @pl.when(c == last)
    def _():
        def put(rows, d):
            y = o_ref[rows, :] + d
            o_ref[rows, :] = h_ref[rows, :] + _rms(y) * gpost_ref[...]

        mix_chunks(lambda rows: hn_ref[rows, :], put)


def _mixout(h, g_mix, g_post, u, u_meta, pool_w, pool_scale, attn,
            w_gp, w_gm, w_po, w_mo, w_out, *, tm, tn, seq):
    rows, d = h.shape
    pw = u.shape[1]
    hb = tm // N_META
    chunk = min(ROW_CHUNK, tm)
    row = lambda i, c: (i, 0)
    const = lambda i, c: (0, 0)
    col = lambda i, c: (0, c)
    return pl.pallas_call(
        functools.partial(_mixout_kernel, tiles_per_seq=seq // tm),
        grid=(rows // tm, d // tn),
        in_specs=[
            pl.BlockSpec((tm, d), row),
            pl.BlockSpec((1, d), const),
            pl.BlockSpec((1, d), const),
            pl.BlockSpec((tm, pw), row),
            pl.BlockSpec((N_META, pw), lambda i, c: (jnp.maximum(i * hb - 1, 0), 0)),
            pl.BlockSpec((N_META, pw), const),
            pl.BlockSpec((len(POOL_WINDOWS), POOL_GROUP, POOL_GROUP), lambda i, c: (0, 0, 0)),
            pl.BlockSpec((1, pw), const),
            pl.BlockSpec((tm, attn.shape[1]), row),
            pl.BlockSpec((d, tn), col),
            pl.BlockSpec((d, tn), col),
            pl.BlockSpec((pw, tn), col),
            pl.BlockSpec((attn.shape[1], tn), col),
            pl.BlockSpec((tn, d), lambda i, c: (c, 0)),
        ],
        out_specs=pl.BlockSpec((tm, d), row),
        out_shape=jax.ShapeDtypeStruct((rows, d), F32),
        scratch_shapes=[pltpu.VMEM((tm, d), BF16), pltpu.VMEM((tm, pw), BF16),
                        pltpu.VMEM((chunk + N_META, POOL_GROUP), F32)],
        compiler_params=_params(("parallel", "arbitrary")),
        name="mixout",
    )(h, g_mix, g_post, u, u, u_meta, pool_w, pool_scale, attn, w_gp, w_gm, w_po, w_mo, w_out)


def _rope_tables(n_pos):
    pos = jnp.arange(n_pos, dtype=F32)
    inv = ROPE_THETA ** (-jnp.arange(0, QK_ROPE, 2, dtype=F32) / QK_ROPE)
    ang = pos[:, None] * inv[None, :]
    cos, sin = jnp.cos(ang), jnp.sin(ang)
    pad = jnp.zeros((n_pos, LANES - QK_ROPE), F32)
    return (jnp.concatenate([cos, cos, pad], axis=-1), jnp.concatenate([sin, sin, pad], axis=-1),
            cos.T, sin.T)


def kernel(x, meta_tokens, norm_ffn1_pre, norm_ffn1_post, ffn1_w_gu, ffn1_w_down, norm_mix_pre, norm_mix_post, w_in, pool_w, pool_scale, w_pool_o, q_a_norm, w_q_b, kv_a_norm, w_kv_b, w_mla_o, w_out, norm_ffn2_pre, norm_ffn2_post, ffn2_w_gu, ffn2_w_down):
    bsz, seq, d = x.shape
    depth = w_in.shape[0]
    nh = MLA_HEADS
    tm, tf, tn, tq = 512, 512, 512, 512
    tm_ffn = 1024

    cos, sin, cos_t, sin_t = _rope_tables(N_META + seq)
    cos_m, sin_m, cos_r, sin_r = cos[:N_META], sin[:N_META], cos[N_META:], sin[N_META:]
    cos_tm, sin_tm, cos_tr, sin_tr = (cos_t[:, :N_META], sin_t[:, :N_META],
                                      cos_t[:, N_META:], sin_t[:, N_META:])

    h = x.reshape(bsz * seq, d)
    hm = meta_tokens.astype(x.dtype)
    row = lambda v: v.reshape(1, -1)

    for i in range(depth):
        w_gu1, w_dn1 = ffn1_w_gu[i].astype(BF16), ffn1_w_down[i].astype(BF16)
        w_q = jnp.pad(w_q_b[i].reshape(Q_LORA, nh, QK_DIM),
                      ((0, 0), (0, 0), (0, HEAD_PAD - QK_DIM))).reshape(Q_LORA, nh * HEAD_PAD)
        w_qt = w_q.T.astype(BF16)
        w_kv = w_kv_b[i].reshape(KV_LORA, nh, QK_NOPE + V_DIM)
        w_k = w_kv[:, :, :QK_NOPE].reshape(KV_LORA, nh * QK_NOPE).astype(BF16)
        w_vt = w_kv[:, :, QK_NOPE:].reshape(KV_LORA, nh * V_DIM).T.astype(BF16)
        p_w = pool_w[i].astype(BF16)

        h1, h1m, w_gu2, w_dn2 = _ffn(
            h, row(norm_ffn1_pre[i]), row(norm_ffn1_post[i]), w_gu1, w_dn1, tm=tm_ffn, tf=tf,
            side=hm, casts=(ffn2_w_gu[i], ffn2_w_down[i]))
        w_in_bf = w_in[i].astype(BF16)
        n_lat = POOL_WIDTH + Q_LORA + KV_LORA + QK_ROPE
        w_gp, w_gm = w_in_bf[:, n_lat:n_lat + d], w_in_bf[:, n_lat + d:]

        u, cq, ckv, kpe, w_po, w_mo, w_o = _inproj(
            h1, row(norm_mix_pre[i]), w_in_bf, row(q_a_norm[i]), row(kv_a_norm[i]), cos_r, sin_r,
            tm=tm, casts=(w_pool_o[i], w_mla_o[i], w_out[i]))
        um, cqm, ckvm, kpem = _inproj(h1m, row(norm_mix_pre[i]), w_in_bf, row(q_a_norm[i]),
                                      row(kv_a_norm[i]), cos_m, sin_m, tm=N_META)
        qt, kn, vt = _qkv(cq, ckv, w_qt, w_k, w_vt, cos_tr, sin_tr, tm=tm, tk=tq, seq=seq)
        _, knm, vtm = _qkv(cqm, ckvm, w_qt, w_k, w_vt, cos_tm, sin_tm,
                           tm=N_META, tk=N_META, seq=N_META)

        attn = _attn(qt, kn.reshape(bsz, seq, nh * QK_NOPE),
                     kpe.reshape(bsz, seq, LANES), vt, knm, kpem,
                     vtm.reshape(nh * V_DIM, N_META), hp=4)

        h2 = _mixout(h1, row(norm_mix_pre[i]), row(norm_mix_post[i]), u, um, p_w,
                     row(pool_scale[i]), attn.reshape(bsz * seq, nh * V_DIM),
                     w_gp, w_gm, w_po, w_mo, w_o, tm=tm, tn=tn, seq=seq)

        h, = _ffn(h2, row(norm_ffn2_pre[i]), row(norm_ffn2_post[i]), w_gu2, w_dn2,
                  tm=tm_ffn, tf=tf)
        if i + 1 < depth:
            raise NotImplementedError("only DEPTH == 1 is supported")

    return h.reshape(bsz, seq, d)
```

```python
import functools
import math

import jax
import jax.numpy as jnp
from jax import lax
from jax.experimental import pallas as pl
from jax.experimental.pallas import tpu as pltpu

F32 = jnp.float32
BF16 = jnp.bfloat16

N_META = 16
POOL_WINDOWS = (2, 4, 8, 16)
POOL_GROUP = 256
POOL_WIDTH = POOL_GROUP * len(POOL_WINDOWS)
MLA_HEADS = 16
Q_LORA = 512
KV_LORA = 512
QK_NOPE = 128
QK_ROPE = 64
V_DIM = 128
QK_DIM = QK_NOPE + QK_ROPE
ROPE_THETA = 10000.0
EPS = 1e-6
LANES = 128
BF16_ROWS = 16
HEAD_PAD = 2 * LANES
Q_SCALE = (QK_DIM ** -0.5) * math.log2(math.e)
MASK_VALUE = -1e30

VMEM_LIMIT = 56 * 1024 * 1024
FFN_VMEM_LIMIT = 62 * 1024 * 1024
ROW_CHUNK = 256
SCORE_LEAD = 0


def _rms(x):
    return x * lax.rsqrt(jnp.mean(x * x, axis=-1, keepdims=True) + EPS)


def _rope(x, cos, sin):
    rot = pltpu.roll(x, 32, 1) - pltpu.roll(x, 96, 1)
    return x * cos + rot * sin


def _params(sem, vmem_limit=VMEM_LIMIT):
    return pltpu.CompilerParams(dimension_semantics=sem, vmem_limit_bytes=vmem_limit)


def _row_chunks(rows, chunk):
    chunk = min(chunk, rows)
    return [slice(r, r + chunk) for r in range(0, rows, chunk)]


def _ffn_kernel(*refs, n_casts, has_side):
    ns = int(has_side)
    x_ref, gpre_ref, gpost_ref, wg_ref, wu_ref, wd_ref = refs[:6]
    n_in = 6 + ns + n_casts
    cast_in = refs[6 + ns:n_in]
    o_ref = refs[n_in]
    cast_out = refs[n_in + 1 + ns:n_in + 1 + ns + n_casts]
    xn_ref = refs[n_in + 1 + ns + n_casts]
    f = pl.program_id(1)
    last = pl.num_programs(1) - 1
    chunks = _row_chunks(x_ref.shape[0], ROW_CHUNK)

    def gate_up(xn):
        return (jnp.dot(xn, wg_ref[...], preferred_element_type=F32),
                jnp.dot(xn, wu_ref[...], preferred_element_type=F32))

    def down(gu):
        g, u = gu
        a = ((g * jax.nn.sigmoid(g)) * u).astype(BF16)
        return jnp.dot(a, wd_ref[...].astype(BF16), preferred_element_type=F32)

    def mlp_chunks(chunks, get_xn, put):
        ahead = gate_up(get_xn(chunks[0]))
        for n, rows in enumerate(chunks):
            gu = ahead
            if n + 1 < len(chunks):
                ahead = gate_up(get_xn(chunks[n + 1]))
            put(rows, down(gu))
            if n == 0:
                casts()

    def casts():
        for src, dst in zip(cast_in, cast_out):
            dst[...] = src[...].astype(BF16)

    def side(step):
        if not has_side:
            return
        xs_ref, os_ref, xsn_ref = refs[6], refs[n_in + 1], refs[-1]

        @pl.when(pl.program_id(0) == 0)
        def _():
            if step == "first":
                xsn_ref[...] = (_rms(xs_ref[...]) * gpre_ref[...]).astype(BF16)
                os_ref[...] = down(gate_up(xsn_ref[...]))
            elif step == "middle":
                os_ref[...] += down(gate_up(xsn_ref[...]))
            else:
                y = os_ref[...] + down(gate_up(xsn_ref[...]))
                os_ref[...] = xs_ref[...] + 0.5 * (_rms(y) * gpost_ref[...])

    @pl.when(f == 0)
    def _():
        def normed(rows):
            xn = (_rms(x_ref[rows, :]) * gpre_ref[...]).astype(BF16)
            xn_ref[rows, :] = xn
            return xn

        def put(rows, d):
            o_ref[rows, :] = d

        mlp_chunks(chunks, normed, put)
        side("first")

    @pl.when(jnp.logical_and(f > 0, f < last))
    def _():
        def put(rows, d):
            o_ref[rows, :] += d

        mlp_chunks(chunks, lambda rows: xn_ref[rows, :], put)
        side("middle")

    @pl.when(f == last)
    def _():
        def put(rows, d):
            y = o_ref[rows, :] + d
            o_ref[rows, :] = x_ref[rows, :] + 0.5 * (_rms(y) * gpost_ref[...])

        mlp_chunks(chunks, lambda rows: xn_ref[rows, :], put)
        side("last")


def _ffn(x, g_pre, g_post, w_gu, w_down, *, tm, tf, side=None, casts=()):
    rows, d = x.shape
    d_ff = w_down.shape[0]
    nf = d_ff // tf
    ni = rows // tm
    const = lambda i, f: (0, 0)
    cast_specs = []
    for c in casts:
        r, k = c.shape
        if r % (ni * BF16_ROWS) == 0 and k % (nf * LANES) == 0:
            cast_specs.append(pl.BlockSpec((r // ni, k // nf), lambda i, f: (i, f)))
        else:
            s = max(v for v in range(1, nf + 1) if r % (ni * v * BF16_ROWS) == 0)
            cast_specs.append(pl.BlockSpec(
                (r // (ni * s), k), lambda i, f, s=s: (i * s + (f * s) // nf, 0)))
    sides = [] if side is None else [side]
    side_specs = [pl.BlockSpec(v.shape, const) for v in sides]
    return pl.pallas_call(
        functools.partial(_ffn_kernel, n_casts=len(casts), has_side=bool(sides)),
        grid=(ni, nf),
        in_specs=[
            pl.BlockSpec((tm, d), lambda i, f: (i, 0)),
            pl.BlockSpec((1, d), const),
            pl.BlockSpec((1, d), const),
            pl.BlockSpec((d, tf), lambda i, f: (0, f)),
            pl.BlockSpec((d, tf), lambda i, f: (0, f + nf)),
            pl.BlockSpec((tf, d), lambda i, f: (f, 0)),
        ] + side_specs + cast_specs,
        out_specs=[pl.BlockSpec((tm, d), lambda i, f: (i, 0))] + side_specs + cast_specs,
        out_shape=[jax.ShapeDtypeStruct((rows, d), F32)]
        + [jax.ShapeDtypeStruct(v.shape, F32) for v in sides]
        + [jax.ShapeDtypeStruct(c.shape, BF16) for c in casts],
        scratch_shapes=[pltpu.VMEM((tm, d), BF16)]
        + [pltpu.VMEM(v.shape, BF16) for v in sides],
        compiler_params=_params(("arbitrary", "arbitrary"), FFN_VMEM_LIMIT),
        name="ffn",
    )(x, g_pre, g_post, w_gu, w_gu, w_down, *sides, *casts)


def _inproj_kernel(*refs, n_casts):
    h_ref, gmix_ref, w_ref, gq_ref, gkv_ref, cos_ref, sin_ref = refs[:7]
    cast_in = refs[7:7 + n_casts]
    u_ref, cq_ref, ckv_ref, kpe_ref = refs[7 + n_casts:11 + n_casts]
    cast_out = refs[11 + n_casts:]
    c0, c1, c2, c3 = POOL_WIDTH, POOL_WIDTH + Q_LORA, POOL_WIDTH + Q_LORA + KV_LORA, w_ref.shape[1]
    for n, rows in enumerate(_row_chunks(h_ref.shape[0], ROW_CHUNK)):
        hn = (_rms(h_ref[rows, :]) * gmix_ref[...]).astype(BF16)
        u_ref[rows, :] = jnp.dot(hn, w_ref[:, :c0], preferred_element_type=F32)
        if n == 0:
            for src, dst in zip(cast_in, cast_out):
                dst[...] = src[...].astype(BF16)
        cq = jnp.dot(hn, w_ref[:, c0:c1], preferred_element_type=F32)
        cq_ref[rows, :] = (_rms(cq) * gq_ref[...]).astype(BF16)
        ckv = jnp.dot(hn, w_ref[:, c1:c2], preferred_element_type=F32)
        ckv_ref[rows, :] = (_rms(ckv) * gkv_ref[...]).astype(BF16)
        kr = jnp.dot(hn, w_ref[:, c2:c3], preferred_element_type=F32)
        kr = jnp.where(lax.broadcasted_iota(jnp.int32, kr.shape, 1) < QK_ROPE, kr, 0.0)
        kpe_ref[rows, :] = _rope(kr, cos_ref[rows, :], sin_ref[rows, :]).astype(BF16)


def _inproj(h, g_mix, w_in, g_q, g_kv, cos, sin, *, tm, casts=()):
    rows, d = h.shape
    n_pos = cos.shape[0] // tm
    wcols = POOL_WIDTH + Q_LORA + KV_LORA + LANES
    steps = rows // tm
    row = lambda i: (i, 0)
    const = lambda i: (0, 0)
    pos = lambda i: (i % n_pos, 0)
    cast_specs = [pl.BlockSpec((c.shape[0] // steps, c.shape[1]), row) for c in casts]
    assert all(c.shape[0] % (steps * BF16_ROWS) == 0 for c in casts)
    return pl.pallas_call(
        functools.partial(_inproj_kernel, n_casts=len(casts)),
        grid=(steps,),
        in_specs=[
            pl.BlockSpec((tm, d), row),
            pl.BlockSpec((1, d), const),
            pl.BlockSpec((d, wcols), const),
            pl.BlockSpec((1, Q_LORA), const),
            pl.BlockSpec((1, KV_LORA), const),
            pl.BlockSpec((tm, LANES), pos),
            pl.BlockSpec((tm, LANES), pos),
        ] + cast_specs,
        out_specs=[
            pl.BlockSpec((tm, POOL_WIDTH), row),
            pl.BlockSpec((tm, Q_LORA), row),
            pl.BlockSpec((tm, KV_LORA), row),
            pl.BlockSpec((tm, LANES), row),
        ] + cast_specs,
        out_shape=[
            jax.ShapeDtypeStruct((rows, POOL_WIDTH), F32),
            jax.ShapeDtypeStruct((rows, Q_LORA), BF16),
            jax.ShapeDtypeStruct((rows, KV_LORA), BF16),
            jax.ShapeDtypeStruct((rows, LANES), BF16),
        ] + [jax.ShapeDtypeStruct(c.shape, BF16) for c in casts],
        compiler_params=_params(("parallel",)),
        name="inproj",
    )(h, g_mix, w_in, g_q, g_kv, cos, sin, *casts)


def _qkv_kernel(cq_ref, ckv_ref, wqt_ref, wk_ref, wvt_ref, cos_ref, sin_ref,
                qt_ref, k_ref, vt_ref):
    cq = cq_ref[...]
    ckv = ckv_ref[...]
    nt = (((1,), (1,)), ((), ()))
    tk = vt_ref.shape[-1]
    chunks = [slice(c * tk, (c + 1) * tk) for c in range(vt_ref.shape[1])]
    cos = cos_ref[...] * Q_SCALE
    sin = sin_ref[...] * Q_SCALE
    half = QK_ROPE // 2
    qt = lax.dot_general(wqt_ref[...], cq, nt, preferred_element_type=F32)
    zeros = jnp.zeros((HEAD_PAD - QK_DIM, tk), BF16)
    for h in range(MLA_HEADS):
        lo = h * HEAD_PAD
        nope = qt[lo:lo + QK_NOPE] * Q_SCALE
        x1 = qt[lo + QK_NOPE:lo + QK_NOPE + half]
        x2 = qt[lo + QK_NOPE + half:lo + QK_DIM]
        r1 = x1 * cos - x2 * sin
        r2 = x2 * cos + x1 * sin
        for c, cols in enumerate(chunks):
            qt_ref[0, c, lo:lo + QK_NOPE, :] = nope[:, cols].astype(BF16)
            qt_ref[0, c, lo + QK_NOPE:lo + QK_NOPE + half, :] = r1[:, cols].astype(BF16)
            qt_ref[0, c, lo + QK_NOPE + half:lo + QK_DIM, :] = r2[:, cols].astype(BF16)
            qt_ref[0, c, lo + QK_DIM:lo + HEAD_PAD, :] = zeros
    k_ref[...] = jnp.dot(ckv, wk_ref[...], preferred_element_type=F32).astype(BF16)
    vt = lax.dot_general(wvt_ref[...], ckv, nt, preferred_element_type=F32)
    for c, cols in enumerate(chunks):
        vt_ref[0, c] = vt[:, cols].astype(BF16)


def _qkv(cq, ckv, w_qt, w_k, w_vt, cos_t, sin_t, *, tm, tk, seq):
    rows = cq.shape[0]
    n_pos = seq // tm
    nh = MLA_HEADS
    row = lambda i: (i, 0)
    const = lambda i: (0, 0)
    tiled = lambda i: (i // n_pos, i % n_pos, 0, 0)
    return pl.pallas_call(
        _qkv_kernel,
        grid=(rows // tm,),
        in_specs=[
            pl.BlockSpec((tm, Q_LORA), row),
            pl.BlockSpec((tm, KV_LORA), row),
            pl.BlockSpec((nh * HEAD_PAD, Q_LORA), const),
            pl.BlockSpec((KV_LORA, nh * QK_NOPE), const),
            pl.BlockSpec((nh * V_DIM, KV_LORA), const),
            pl.BlockSpec((QK_ROPE // 2, tm), lambda i: (0, i % n_pos)),
            pl.BlockSpec((QK_ROPE // 2, tm), lambda i: (0, i % n_pos)),
        ],
        out_specs=[
            pl.BlockSpec((1, tm // tk, nh * HEAD_PAD, tk), tiled),
            pl.BlockSpec((tm, nh * QK_NOPE), row),
            pl.BlockSpec((1, tm // tk, nh * V_DIM, tk), tiled),
        ],
        out_shape=[
            jax.ShapeDtypeStruct((rows // seq, seq // tk, nh * HEAD_PAD, tk), BF16),
            jax.ShapeDtypeStruct((rows, nh * QK_NOPE), BF16),
            jax.ShapeDtypeStruct((rows // seq, seq // tk, nh * V_DIM, tk), BF16),
        ],
        compiler_params=_params(("parallel",)),
        name="qkv",
    )(cq, ckv, w_qt, w_k, w_vt, cos_t, sin_t)


def _attn_kernel(qi_ref, kj_ref, new_ref, q_ref, kn_ref, kp_ref, vt_ref, knm_ref, kpm_ref,
                 vtm_ref, o_ref, acc_ref, sa_ref, sb_ref, st_ref, pm_ref):
    tk = vt_ref.shape[-1]
    tq = tk
    hp = acc_ref.shape[0]
    n_pairs = qi_ref.shape[0]
    bufs = ((sa_ref, 2), (sb_ref, 3))

    def q_of(h, qi):
        return q_ref[0, qi, h * HEAD_PAD:(h + 1) * HEAD_PAD, :]

    def init_scores(qi, heads):
        kpm = kpm_ref[...]
        for h in heads:
            km = jnp.concatenate([knm_ref[:, h * QK_NOPE:(h + 1) * QK_NOPE], kpm], axis=1)
            s = jnp.dot(km, q_of(h, qi), preferred_element_type=F32)
            m0 = jnp.max(s, axis=0, keepdims=True)
            p = jnp.exp2(s - m0)
            pm_ref[h] = p.astype(BF16)
            st_ref[h, 0] = m0
            st_ref[h, 1] = jnp.sum(p, axis=0, keepdims=True)

    def init_values(heads):
        for h in heads:
            acc_ref[h] = jnp.dot(vtm_ref[h * V_DIM:(h + 1) * V_DIM, :], pm_ref[h],
                                 preferred_element_type=F32)

    def scores(t, dst, masked, heads):
        s_ref, row = dst
        qi = qi_ref[t]
        start = pl.multiple_of(kj_ref[t] * tk, tk)
        kp = kp_ref[0, pl.ds(start, tk), :]
        for h in heads:
            k = jnp.concatenate(
                [kn_ref[0, pl.ds(start, tk), h * QK_NOPE:(h + 1) * QK_NOPE], kp], axis=1)
            s = jnp.dot(k, q_of(h, qi), preferred_element_type=F32)
            if masked:
                kpos = lax.broadcasted_iota(jnp.int32, s.shape, 0)
                qpos = lax.broadcasted_iota(jnp.int32, s.shape, 1)
                s = jnp.where(kpos <= qpos, s, MASK_VALUE)
            s_ref[h] = s
            st_ref[h, row] = jnp.max(s, axis=0, keepdims=True)

    def absorb(t, src, heads):
        s_ref, row = src
        kj = kj_ref[t]
        ones = jnp.ones((BF16_ROWS, tk), BF16)
        for h in heads:
            m, l = st_ref[h, 0], st_ref[h, 1]
            m_new = jnp.maximum(m, st_ref[h, row])
            alpha = jnp.exp2(m - m_new)
            p = jnp.exp2(s_ref[h] - m_new)
            vt1 = jnp.concatenate([vt_ref[0, kj, h * V_DIM:(h + 1) * V_DIM, :], ones], axis=0)
            pv = jnp.dot(vt1, p.astype(BF16), preferred_element_type=F32)
            acc_ref[h] = alpha * acc_ref[h] + pv[:V_DIM]
            st_ref[h, 0] = m_new
            st_ref[h, 1] = alpha * l + pv[V_DIM:V_DIM + 1]

    def finalize(qi, heads):
        rows = pl.ds(pl.multiple_of(qi * tq, tq), tq)
        for h in heads:
            o_ref[0, rows, h * V_DIM:(h + 1) * V_DIM] = (
                acc_ref[h] / st_ref[h, 1]).T.astype(BF16)

    def run(stages):
        items = [(st, h) for st in stages for h in range(hp)]

        def put_scores(item):
            (t, _, dst, is_fresh), h = item
            scores(t, dst, is_fresh, [h])

        pending = []

        def put_rest(item):
            (t, src, _, is_fresh), h = item
            absorb(t - 1, src, [h])
            if is_fresh:
                finalize(qi_ref[t - 1], [h])
                init_scores(qi_ref[t], [h])
            init_values(pending)
            pending.clear()
            if is_fresh:
                pending.append(h)

        for item in items[:SCORE_LEAD + 1]:
            put_scores(item)
        for n, item in enumerate(items):
            put_rest(item)
            if n + SCORE_LEAD + 1 < len(items):
                put_scores(items[n + SCORE_LEAD + 1])
        init_values(pending)

    heads = list(range(hp))
    init_scores(qi_ref[0], heads)
    init_values(heads)
    scores(0, bufs[0], True, heads)

    def body(r, carry):
        t = 2 * r + 1
        a, b = bufs
        for fresh0, fresh1 in ((True, False), (False, True), (False, False)):
            cond = jnp.logical_and(new_ref[t] == int(fresh0), new_ref[t + 1] == int(fresh1))
            pl.when(cond)(functools.partial(
                run, [(t, a, b, fresh0), (t + 1, b, a, fresh1)]))
        return carry

    lax.fori_loop(0, (n_pairs - 1) // 2, body, 0)
    last = bufs[0]
    if (n_pairs - 1) % 2:
        t = n_pairs - 1
        for is_fresh in (True, False):
            pl.when(new_ref[t] == int(is_fresh))(functools.partial(
                run, [(t, bufs[0], bufs[1], is_fresh)]))
        last = bufs[1]
    absorb(n_pairs - 1, last, heads)
    finalize(qi_ref[n_pairs - 1], heads)


def _attn(qt, k_nope, k_pe, vt, knm, kpm, vtm, *, hp):
    b, s, _ = k_nope.shape
    tk = vt.shape[-1]
    nq = s // tk
    nh = MLA_HEADS
    qi, kj, new = [], [], []
    for i in range(nq):
        for n, j in enumerate([i] + list(range(i))):
            qi.append(i)
            kj.append(j)
            new.append(int(n == 0))
    assert not any(new[t] and new[t + 1] for t in range(1, len(new) - 1, 2))
    tables = [jnp.asarray(v, jnp.int32) for v in (qi, kj, new)]
    grid_spec = pltpu.PrefetchScalarGridSpec(
        num_scalar_prefetch=len(tables),
        grid=(b, nh // hp),
        in_specs=[
            pl.BlockSpec((1, nq, hp * HEAD_PAD, tk), lambda bi, h, *_: (bi, 0, h, 0)),
            pl.BlockSpec((1, s, hp * QK_NOPE), lambda bi, h, *_: (bi, 0, h)),
            pl.BlockSpec((1, s, LANES), lambda bi, h, *_: (bi, 0, 0)),
            pl.BlockSpec((1, nq, hp * V_DIM, tk), lambda bi, h, *_: (bi, 0, h, 0)),
            pl.BlockSpec((N_META, hp * QK_NOPE), lambda bi, h, *_: (0, h)),
            pl.BlockSpec((N_META, LANES), lambda bi, h, *_: (0, 0)),
            pl.BlockSpec((hp * V_DIM, N_META), lambda bi, h, *_: (h, 0)),
        ],
        out_specs=pl.BlockSpec((1, s, hp * V_DIM), lambda bi, h, *_: (bi, 0, h)),
        scratch_shapes=[pltpu.VMEM((hp, V_DIM, tk), F32), pltpu.VMEM((hp, tk, tk), F32),
                        pltpu.VMEM((hp, tk, tk), F32), pltpu.VMEM((hp, 4, 1, tk), F32),
                        pltpu.VMEM((hp, N_META, tk), BF16)],
    )
    return pl.pallas_call(
        _attn_kernel,
        grid_spec=grid_spec,
        out_shape=jax.ShapeDtypeStruct((b, s, nh * V_DIM), BF16),
        compiler_params=_params(("parallel", "parallel")),
        name="attn",
    )(*tables, qt, k_nope, k_pe, vt, knm, kpm, vtm)


def _mixout_kernel(h_ref, gmix_ref, gpost_ref, u_ref, uprev_ref, umeta_ref, pw_ref, pscale_ref,
                   attn_ref, wgp_ref, wgm_ref, wpo_ref, wmo_ref, wout_ref, o_ref,
                   hn_ref, pool_ref, buf_ref, *, tiles_per_seq):
    c = pl.program_id(1)
    last = pl.num_programs(1) - 1
    chunks = _row_chunks(h_ref.shape[0], ROW_CHUNK)

    def pool(rows, halo):
        for g, win in enumerate(POOL_WINDOWS):
            cols = slice(g * POOL_GROUP, (g + 1) * POOL_GROUP)
            buf_ref[:N_META, :] = halo[:, cols]
            buf_ref[N_META:, :] = u_ref[rows, cols]
            tot = buf_ref[...]
            shift = 1
            while shift < win:
                tot = tot + pltpu.roll(tot, shift, 0)
                shift *= 2
            dlt = (tot[N_META:, :] * (1.0 / win) - buf_ref[N_META:, :]).astype(BF16)
            y = jnp.dot(dlt, pw_ref[g], preferred_element_type=F32)
            pool_ref[rows, cols] = (y * pscale_ref[:, cols]).astype(BF16)

    def front(hn, rows):
        return (jnp.dot(hn, wgp_ref[...], preferred_element_type=F32),
                jnp.dot(hn, wgm_ref[...], preferred_element_type=F32),
                jnp.dot(pool_ref[rows, :], wpo_ref[...], preferred_element_type=F32),
                jnp.dot(attn_ref[rows, :], wmo_ref[...], preferred_element_type=F32))

    def back(parts):
        g_pool, g_mla, y_pool, y_mla = parts
        y = (jax.nn.sigmoid(g_pool) * y_pool + jax.nn.sigmoid(g_mla) * y_mla).astype(BF16)
        return jnp.dot(y, wout_ref[...], preferred_element_type=F32)

    def mix_chunks(get_hn, put):
        ahead = front(get_hn(chunks[0]), chunks[0])
        for n, rows in enumerate(chunks):
            parts = ahead
            if n + 1 < len(chunks):
                ahead = front(get_hn(chunks[n + 1]), chunks[n + 1])
            put(rows, back(parts))

    @pl.when(c == 0)
    def _():
        seq_start = pl.program_id(0) % tiles_per_seq == 0

        def prepared(rows):
            if rows.start == 0:
                halo = jnp.where(seq_start, umeta_ref[...], uprev_ref[...])
            else:
                halo = u_ref[rows.start - N_META:rows.start, :]
            pool(rows, halo)
            hn = (_rms(h_ref[rows, :]) * gmix_ref[...]).astype(BF16)
            hn_ref[rows, :] = hn
            return hn

        def put(rows, d):
            o_ref[rows, :] = d

        mix_chunks(prepared, put)

    @pl.when(jnp.logical_and(c > 0, c < last))
    def _():
        def put(rows, d):
            o_ref[rows, :] += d

        mix_chunks(lambda rows: hn_ref[rows, :], put)

    @pl.when(c == last)
    def _():
        def put(rows, d):
            y = o_ref[rows, :] + d
            o_ref[rows, :] = h_ref[rows, :] + _rms(y) * gpost_ref[...]

        mix_chunks(lambda rows: hn_ref[rows, :], put)


def _mixout(h, g_mix, g_post, u, u_meta, pool_w, pool_scale, attn,
            w_gp, w_gm, w_po, w_mo, w_out, *, tm, tn, seq):
    rows, d = h.shape
    pw = u.shape[1]
    hb = tm // N_META
    chunk = min(ROW_CHUNK, tm)
    row = lambda i, c: (i, 0)
    const = lambda i, c: (0, 0)
    col = lambda i, c: (0, c)
    return pl.pallas_call(
        functools.partial(_mixout_kernel, tiles_per_seq=seq // tm),
        grid=(rows // tm, d // tn),
        in_specs=[
            pl.BlockSpec((tm, d), row),
            pl.BlockSpec((1, d), const),
            pl.BlockSpec((1, d), const),
            pl.BlockSpec((tm, pw), row),
            pl.BlockSpec((N_META, pw), lambda i, c: (jnp.maximum(i * hb - 1, 0), 0)),
            pl.BlockSpec((N_META, pw), const),
            pl.BlockSpec((len(POOL_WINDOWS), POOL_GROUP, POOL_GROUP), lambda i, c: (0, 0, 0)),
            pl.BlockSpec((1, pw), const),
            pl.BlockSpec((tm, attn.shape[1]), row),
            pl.BlockSpec((d, tn), col),
            pl.BlockSpec((d, tn), col),
            pl.BlockSpec((pw, tn), col),
            pl.BlockSpec((attn.shape[1], tn), col),
            pl.BlockSpec((tn, d), lambda i, c: (c, 0)),
        ],
        out_specs=pl.BlockSpec((tm, d), row),
        out_shape=jax.ShapeDtypeStruct((rows, d), F32),
        scratch_shapes=[pltpu.VMEM((tm, d), BF16), pltpu.VMEM((tm, pw), BF16),
                        pltpu.VMEM((chunk + N_META, POOL_GROUP), F32)],
        compiler_params=_params(("parallel", "arbitrary")),
        name="mixout",
    )(h, g_mix, g_post, u, u, u_meta, pool_w, pool_scale, attn, w_gp, w_gm, w_po, w_mo, w_out)


def _rope_tables(n_pos):
    pos = jnp.arange(n_pos, dtype=F32)
    inv = ROPE_THETA ** (-jnp.arange(0, QK_ROPE, 2, dtype=F32) / QK_ROPE)
    ang = pos[:, None] * inv[None, :]
    cos, sin = jnp.cos(ang), jnp.sin(ang)
    pad = jnp.zeros((n_pos, LANES - QK_ROPE), F32)
    return (jnp.concatenate([cos, cos, pad], axis=-1), jnp.concatenate([sin, sin, pad], axis=-1),
            cos.T, sin.T)


def kernel(x, meta_tokens, norm_ffn1_pre, norm_ffn1_post, ffn1_w_gu, ffn1_w_down, norm_mix_pre, norm_mix_post, w_in, pool_w, pool_scale, w_pool_o, q_a_norm, w_q_b, kv_a_norm, w_kv_b, w_mla_o, w_out, norm_ffn2_pre, norm_ffn2_post, ffn2_w_gu, ffn2_w_down):
    bsz, seq, d = x.shape
    depth = w_in.shape[0]
    nh = MLA_HEADS
    tm, tf, tn, tq = 512, 512, 512, 512
    tm_ffn = 1024

    cos, sin, cos_t, sin_t = _rope_tables(N_META + seq)
    cos_m, sin_m, cos_r, sin_r = cos[:N_META], sin[:N_META], cos[N_META:], sin[N_META:]
    cos_tm, sin_tm, cos_tr, sin_tr = (cos_t[:, :N_META], sin_t[:, :N_META],
                                      cos_t[:, N_META:], sin_t[:, N_META:])

    h = x.reshape(bsz * seq, d)
    hm = meta_tokens.astype(x.dtype)
    row = lambda v: v.reshape(1, -1)

    for i in range(depth):
        w_gu1, w_dn1 = ffn1_w_gu[i].astype(BF16), ffn1_w_down[i]
        w_q = jnp.pad(w_q_b[i].reshape(Q_LORA, nh, QK_DIM),
                      ((0, 0), (0, 0), (0, HEAD_PAD - QK_DIM))).reshape(Q_LORA, nh * HEAD_PAD)
        w_qt = w_q.T.astype(BF16)
        w_kv = w_kv_b[i].reshape(KV_LORA, nh, QK_NOPE + V_DIM)
        w_k = w_kv[:, :, :QK_NOPE].reshape(KV_LORA, nh * QK_NOPE).astype(BF16)
        w_vt = w_kv[:, :, QK_NOPE:].reshape(KV_LORA, nh * V_DIM).T.astype(BF16)
        p_w = pool_w[i].astype(BF16)

        h1, h1m, w_gu2, w_dn2 = _ffn(
            h, row(norm_ffn1_pre[i]), row(norm_ffn1_post[i]), w_gu1, w_dn1, tm=tm_ffn, tf=tf,
            side=hm, casts=(ffn2_w_gu[i], ffn2_w_down[i]))
        w_in_bf = w_in[i].astype(BF16)
        n_lat = POOL_WIDTH + Q_LORA + KV_LORA + QK_ROPE
        w_gp, w_gm = w_in_bf[:, n_lat:n_lat + d], w_in_bf[:, n_lat + d:]

        u, cq, ckv, kpe, w_po, w_mo, w_o = _inproj(
            h1, row(norm_mix_pre[i]), w_in_bf, row(q_a_norm[i]), row(kv_a_norm[i]), cos_r, sin_r,
            tm=tm, casts=(w_pool_o[i], w_mla_o[i], w_out[i]))
        um, cqm, ckvm, kpem = _inproj(h1m, row(norm_mix_pre[i]), w_in_bf, row(q_a_norm[i]),
                                      row(kv_a_norm[i]), cos_m, sin_m, tm=N_META)
        qt, kn, vt = _qkv(cq, ckv, w_qt, w_k, w_vt, cos_tr, sin_tr, tm=tm, tk=tq, seq=seq)
        _, knm, vtm = _qkv(cqm, ckvm, w_qt, w_k, w_vt, cos_tm, sin_tm,
                           tm=N_META, tk=N_META, seq=N_META)

        attn = _attn(qt, kn.reshape(bsz, seq, nh * QK_NOPE),
                     kpe.reshape(bsz, seq, LANES), vt, knm, kpem,
                     vtm.reshape(nh * V_DIM, N_META), hp=4)

        h2 = _mixout(h1, row(norm_mix_pre[i]), row(norm_mix_post[i]), u, um, p_w,
                     row(pool_scale[i]), attn.reshape(bsz * seq, nh * V_DIM),
                     w_gp, w_gm, w_po, w_mo, w_o, tm=tm, tn=tn, seq=seq)

        h, = _ffn(h2, row(norm_ffn2_pre[i]), row(norm_ffn2_post[i]), w_gu2, w_dn2,
                  tm=tm_ffn, tf=tf)
        if i + 1 < depth:
            raise NotImplementedError("only DEPTH == 1 is supported")

    return h.reshape(bsz, seq, d)
```

```python
import functools
import math

import jax
import jax.numpy as jnp
from jax import lax
from jax.experimental import pallas as pl
from jax.experimental.pallas import tpu as pltpu

F32 = jnp.float32
BF16 = jnp.bfloat16

N_META = 16
POOL_WINDOWS = (2, 4, 8, 16)
POOL_GROUP = 256
POOL_WIDTH = POOL_GROUP * len(POOL_WINDOWS)
MLA_HEADS = 16
Q_LORA = 512
KV_LORA = 512
QK_NOPE = 128
QK_ROPE = 64
V_DIM = 128
QK_DIM = QK_NOPE + QK_ROPE
ROPE_THETA = 10000.0
EPS = 1e-6
LANES = 128
BF16_ROWS = 16
HEAD_PAD = 2 * LANES
Q_SCALE = (QK_DIM ** -0.5) * math.log2(math.e)
MASK_VALUE = -1e30

VMEM_LIMIT = 56 * 1024 * 1024
FFN_VMEM_LIMIT = 62 * 1024 * 1024
ROW_CHUNK = 256
SCORE_LEAD = 1


def _rms(x):
    return x * lax.rsqrt(jnp.mean(x * x, axis=-1, keepdims=True) + EPS)


def _rope(x, cos, sin):
    rot = pltpu.roll(x, 32, 1) - pltpu.roll(x, 96, 1)
    return x * cos + rot * sin


def _params(sem, vmem_limit=VMEM_LIMIT):
    return pltpu.CompilerParams(dimension_semantics=sem, vmem_limit_bytes=vmem_limit)


def _row_chunks(rows, chunk):
    chunk = min(chunk, rows)
    return [slice(r, r + chunk) for r in range(0, rows, chunk)]


def _ffn_kernel(*refs, n_casts, has_side):
    ns = int(has_side)
    x_ref, gpre_ref, gpost_ref, wg_ref, wu_ref, wd_ref = refs[:6]
    n_in = 6 + ns + n_casts
    cast_in = refs[6 + ns:n_in]
    o_ref = refs[n_in]
    cast_out = refs[n_in + 1 + ns:n_in + 1 + ns + n_casts]
    xn_ref = refs[n_in + 1 + ns + n_casts]
    f = pl.program_id(1)
    last = pl.num_programs(1) - 1
    chunks = _row_chunks(x_ref.shape[0], ROW_CHUNK)

    def gate_up(xn):
        return (jnp.dot(xn, wg_ref[...], preferred_element_type=F32),
                jnp.dot(xn, wu_ref[...], preferred_element_type=F32))

    def down(gu):
        g, u = gu
        a = ((g * jax.nn.sigmoid(g)) * u).astype(BF16)
        return jnp.dot(a, wd_ref[...], preferred_element_type=F32)

    def mlp_chunks(chunks, get_xn, put):
        ahead = gate_up(get_xn(chunks[0]))
        for n, rows in enumerate(chunks):
            gu = ahead
            if n + 1 < len(chunks):
                ahead = gate_up(get_xn(chunks[n + 1]))
            put(rows, down(gu))
            if n == 0:
                casts()

    def casts():
        for src, dst in zip(cast_in, cast_out):
            dst[...] = src[...].astype(BF16)

    def side(step):
        if not has_side:
            return
        xs_ref, os_ref, xsn_ref = refs[6], refs[n_in + 1], refs[-1]

        @pl.when(pl.program_id(0) == 0)
        def _():
            if step == "first":
                xsn_ref[...] = (_rms(xs_ref[...]) * gpre_ref[...]).astype(BF16)
                os_ref[...] = down(gate_up(xsn_ref[...]))
            elif step == "middle":
                os_ref[...] += down(gate_up(xsn_ref[...]))
            else:
                y = os_ref[...] + down(gate_up(xsn_ref[...]))
                os_ref[...] = xs_ref[...] + 0.5 * (_rms(y) * gpost_ref[...])

    @pl.when(f == 0)
    def _():
        def normed(rows):
            xn = (_rms(x_ref[rows, :]) * gpre_ref[...]).astype(BF16)
            xn_ref[rows, :] = xn
            return xn

        def put(rows, d):
            o_ref[rows, :] = d

        mlp_chunks(chunks, normed, put)
        side("first")

    @pl.when(jnp.logical_and(f > 0, f < last))
    def _():
        def put(rows, d):
            o_ref[rows, :] += d

        mlp_chunks(chunks, lambda rows: xn_ref[rows, :], put)
        side("middle")

    @pl.when(f == last)
    def _():
        def put(rows, d):
            y = o_ref[rows, :] + d
            o_ref[rows, :] = x_ref[rows, :] + 0.5 * (_rms(y) * gpost_ref[...])

        mlp_chunks(chunks, lambda rows: xn_ref[rows, :], put)
        side("last")


def _ffn(x, g_pre, g_post, w_gu, w_down, *, tm, tf, side=None, casts=()):
    rows, d = x.shape
    d_ff = w_down.shape[0]
    nf = d_ff // tf
    ni = rows // tm
    const = lambda i, f: (0, 0)
    cast_specs = []
    for c in casts:
        r, k = c.shape
        if r % (ni * BF16_ROWS) == 0 and k % (nf * LANES) == 0:
            cast_specs.append(pl.BlockSpec((r // ni, k // nf), lambda i, f: (i, f)))
        else:
            s = max(v for v in range(1, nf + 1) if r % (ni * v * BF16_ROWS) == 0)
            cast_specs.append(pl.BlockSpec(
                (r // (ni * s), k), lambda i, f, s=s: (i * s + (f * s) // nf, 0)))
    sides = [] if side is None else [side]
    side_specs = [pl.BlockSpec(v.shape, const) for v in sides]
    return pl.pallas_call(
        functools.partial(_ffn_kernel, n_casts=len(casts), has_side=bool(sides)),
        grid=(ni, nf),
        in_specs=[
            pl.BlockSpec((tm, d), lambda i, f: (i, 0)),
            pl.BlockSpec((1, d), const),
            pl.BlockSpec((1, d), const),
            pl.BlockSpec((d, tf), lambda i, f: (0, f)),
            pl.BlockSpec((d, tf), lambda i, f: (0, f + nf)),
            pl.BlockSpec((tf, d), lambda i, f: (f, 0)),
        ] + side_specs + cast_specs,
        out_specs=[pl.BlockSpec((tm, d), lambda i, f: (i, 0))] + side_specs + cast_specs,
        out_shape=[jax.ShapeDtypeStruct((rows, d), F32)]
        + [jax.ShapeDtypeStruct(v.shape, F32) for v in sides]
        + [jax.ShapeDtypeStruct(c.shape, BF16) for c in casts],
        scratch_shapes=[pltpu.VMEM((tm, d), BF16)]
        + [pltpu.VMEM(v.shape, BF16) for v in sides],
        compiler_params=_params(("arbitrary", "arbitrary"), FFN_VMEM_LIMIT),
        name="ffn",
    )(x, g_pre, g_post, w_gu, w_gu, w_down, *sides, *casts)


def _inproj_kernel(*refs, n_casts):
    h_ref, gmix_ref, w_ref, gq_ref, gkv_ref, cos_ref, sin_ref = refs[:7]
    cast_in = refs[7:7 + n_casts]
    u_ref, cq_ref, ckv_ref, kpe_ref = refs[7 + n_casts:11 + n_casts]
    cast_out = refs[11 + n_casts:]
    c0, c1, c2, c3 = POOL_WIDTH, POOL_WIDTH + Q_LORA, POOL_WIDTH + Q_LORA + KV_LORA, w_ref.shape[1]
    for n, rows in enumerate(_row_chunks(h_ref.shape[0], ROW_CHUNK)):
        hn = (_rms(h_ref[rows, :]) * gmix_ref[...]).astype(BF16)
        u_ref[rows, :] = jnp.dot(hn, w_ref[:, :c0], preferred_element_type=F32)
        if n == 0:
            for src, dst in zip(cast_in, cast_out):
                dst[...] = src[...].astype(BF16)
        cq = jnp.dot(hn, w_ref[:, c0:c1], preferred_element_type=F32)
        cq_ref[rows, :] = (_rms(cq) * gq_ref[...]).astype(BF16)
        ckv = jnp.dot(hn, w_ref[:, c1:c2], preferred_element_type=F32)
        ckv_ref[rows, :] = (_rms(ckv) * gkv_ref[...]).astype(BF16)
        kr = jnp.dot(hn, w_ref[:, c2:c3], preferred_element_type=F32)
        kr = jnp.where(lax.broadcasted_iota(jnp.int32, kr.shape, 1) < QK_ROPE, kr, 0.0)
        kpe_ref[rows, :] = _rope(kr, cos_ref[rows, :], sin_ref[rows, :]).astype(BF16)


def _inproj(h, g_mix, w_in, g_q, g_kv, cos, sin, *, tm, casts=()):
    rows, d = h.shape
    n_pos = cos.shape[0] // tm
    wcols = POOL_WIDTH + Q_LORA + KV_LORA + LANES
    steps = rows // tm
    row = lambda i: (i, 0)
    const = lambda i: (0, 0)
    pos = lambda i: (i % n_pos, 0)
    cast_specs = [pl.BlockSpec((c.shape[0] // steps, c.shape[1]), row) for c in casts]
    assert all(c.shape[0] % (steps * BF16_ROWS) == 0 for c in casts)
    return pl.pallas_call(
        functools.partial(_inproj_kernel, n_casts=len(casts)),
        grid=(steps,),
        in_specs=[
            pl.BlockSpec((tm, d), row),
            pl.BlockSpec((1, d), const),
            pl.BlockSpec((d, wcols), const),
            pl.BlockSpec((1, Q_LORA), const),
            pl.BlockSpec((1, KV_LORA), const),
            pl.BlockSpec((tm, LANES), pos),
            pl.BlockSpec((tm, LANES), pos),
        ] + cast_specs,
        out_specs=[
            pl.BlockSpec((tm, POOL_WIDTH), row),
            pl.BlockSpec((tm, Q_LORA), row),
            pl.BlockSpec((tm, KV_LORA), row),
            pl.BlockSpec((tm, LANES), row),
        ] + cast_specs,
        out_shape=[
            jax.ShapeDtypeStruct((rows, POOL_WIDTH), F32),
            jax.ShapeDtypeStruct((rows, Q_LORA), BF16),
            jax.ShapeDtypeStruct((rows, KV_LORA), BF16),
            jax.ShapeDtypeStruct((rows, LANES), BF16),
        ] + [jax.ShapeDtypeStruct(c.shape, BF16) for c in casts],
        compiler_params=_params(("parallel",)),
        name="inproj",
    )(h, g_mix, w_in, g_q, g_kv, cos, sin, *casts)


def _qkv_kernel(cq_ref, ckv_ref, wqt_ref, wk_ref, wvt_ref, cos_ref, sin_ref,
                qt_ref, k_ref, vt_ref):
    cq = cq_ref[...]
    ckv = ckv_ref[...]
    nt = (((1,), (1,)), ((), ()))
    tk = vt_ref.shape[-1]
    chunks = [slice(c * tk, (c + 1) * tk) for c in range(vt_ref.shape[1])]
    cos = cos_ref[...] * Q_SCALE
    sin = sin_ref[...] * Q_SCALE
    half = QK_ROPE // 2
    qt = lax.dot_general(wqt_ref[...], cq, nt, preferred_element_type=F32)
    zeros = jnp.zeros((HEAD_PAD - QK_DIM, tk), BF16)
    for h in range(MLA_HEADS):
        lo = h * HEAD_PAD
        nope = qt[lo:lo + QK_NOPE] * Q_SCALE
        x1 = qt[lo + QK_NOPE:lo + QK_NOPE + half]
        x2 = qt[lo + QK_NOPE + half:lo + QK_DIM]
        r1 = x1 * cos - x2 * sin
        r2 = x2 * cos + x1 * sin
        for c, cols in enumerate(chunks):
            qt_ref[0, c, lo:lo + QK_NOPE, :] = nope[:, cols].astype(BF16)
            qt_ref[0, c, lo + QK_NOPE:lo + QK_NOPE + half, :] = r1[:, cols].astype(BF16)
            qt_ref[0, c, lo + QK_NOPE + half:lo + QK_DIM, :] = r2[:, cols].astype(BF16)
            qt_ref[0, c, lo + QK_DIM:lo + HEAD_PAD, :] = zeros
    k_ref[...] = jnp.dot(ckv, wk_ref[...], preferred_element_type=F32).astype(BF16)
    vt = lax.dot_general(wvt_ref[...], ckv, nt, preferred_element_type=F32)
    for c, cols in enumerate(chunks):
        vt_ref[0, c] = vt[:, cols].astype(BF16)


def _qkv(cq, ckv, w_qt, w_k, w_vt, cos_t, sin_t, *, tm, tk, seq):
    rows = cq.shape[0]
    n_pos = seq // tm
    nh = MLA_HEADS
    row = lambda i: (i, 0)
    const = lambda i: (0, 0)
    tiled = lambda i: (i // n_pos, i % n_pos, 0, 0)
    return pl.pallas_call(
        _qkv_kernel,
        grid=(rows // tm,),
        in_specs=[
            pl.BlockSpec((tm, Q_LORA), row),
            pl.BlockSpec((tm, KV_LORA), row),
            pl.BlockSpec((nh * HEAD_PAD, Q_LORA), const),
            pl.BlockSpec((KV_LORA, nh * QK_NOPE), const),
            pl.BlockSpec((nh * V_DIM, KV_LORA), const),
            pl.BlockSpec((QK_ROPE // 2, tm), lambda i: (0, i % n_pos)),
            pl.BlockSpec((QK_ROPE // 2, tm), lambda i: (0, i % n_pos)),
        ],
        out_specs=[
            pl.BlockSpec((1, tm // tk, nh * HEAD_PAD, tk), tiled),
            pl.BlockSpec((tm, nh * QK_NOPE), row),
            pl.BlockSpec((1, tm // tk, nh * V_DIM, tk), tiled),
        ],
        out_shape=[
            jax.ShapeDtypeStruct((rows // seq, seq // tk, nh * HEAD_PAD, tk), BF16),
            jax.ShapeDtypeStruct((rows, nh * QK_NOPE), BF16),
            jax.ShapeDtypeStruct((rows // seq, seq // tk, nh * V_DIM, tk), BF16),
        ],
        compiler_params=_params(("parallel",)),
        name="qkv",
    )(cq, ckv, w_qt, w_k, w_vt, cos_t, sin_t)


def _attn_kernel(qi_ref, kj_ref, new_ref, q_ref, kn_ref, kp_ref, vt_ref, knm_ref, kpm_ref,
                 vtm_ref, o_ref, acc_ref, sa_ref, sb_ref, st_ref, pm_ref):
    tk = vt_ref.shape[-1]
    tq = tk
    hp = acc_ref.shape[0]
    n_pairs = qi_ref.shape[0]
    bufs = ((sa_ref, 2), (sb_ref, 3))

    def q_of(h, qi):
        return q_ref[0, qi, h * HEAD_PAD:(h + 1) * HEAD_PAD, :]

    def init_scores(qi, heads):
        kpm = kpm_ref[...]
        for h in heads:
            km = jnp.concatenate([knm_ref[:, h * QK_NOPE:(h + 1) * QK_NOPE], kpm], axis=1)
            s = jnp.dot(km, q_of(h, qi), preferred_element_type=F32)
            m0 = jnp.max(s, axis=0, keepdims=True)
            p = jnp.exp2(s - m0)
            pm_ref[h] = p.astype(BF16)
            st_ref[h, 0] = m0
            st_ref[h, 1] = jnp.sum(p, axis=0, keepdims=True)

    def init_values(heads):
        for h in heads:
            acc_ref[h] = jnp.dot(vtm_ref[h * V_DIM:(h + 1) * V_DIM, :], pm_ref[h],
                                 preferred_element_type=F32)

    def scores(t, dst, masked, heads):
        s_ref, row = dst
        qi = qi_ref[t]
        start = pl.multiple_of(kj_ref[t] * tk, tk)
        kp = kp_ref[0, pl.ds(start, tk), :]
        for h in heads:
            k = jnp.concatenate(
                [kn_ref[0, pl.ds(start, tk), h * QK_NOPE:(h + 1) * QK_NOPE], kp], axis=1)
            s = jnp.dot(k, q_of(h, qi), preferred_element_type=F32)
            if masked:
                kpos = lax.broadcasted_iota(jnp.int32, s.shape, 0)
                qpos = lax.broadcasted_iota(jnp.int32, s.shape, 1)
                s = jnp.where(kpos <= qpos, s, MASK_VALUE)
            s_ref[h] = s
            st_ref[h, row] = jnp.max(s, axis=0, keepdims=True)

    def absorb(t, src, heads):
        s_ref, row = src
        kj = kj_ref[t]
        ones = jnp.ones((BF16_ROWS, tk), BF16)
        for h in heads:
            m, l = st_ref[h, 0], st_ref[h, 1]
            m_new = jnp.maximum(m, st_ref[h, row])
            alpha = jnp.exp2(m - m_new)
            p = jnp.exp2(s_ref[h] - m_new)
            vt1 = jnp.concatenate([vt_ref[0, kj, h * V_DIM:(h + 1) * V_DIM, :], ones], axis=0)
            pv = jnp.dot(vt1, p.astype(BF16), preferred_element_type=F32)
            acc_ref[h] = alpha * acc_ref[h] + pv[:V_DIM]
            st_ref[h, 0] = m_new
            st_ref[h, 1] = alpha * l + pv[V_DIM:V_DIM + 1]

    def finalize(qi, heads):
        rows = pl.ds(pl.multiple_of(qi * tq, tq), tq)
        for h in heads:
            o_ref[0, rows, h * V_DIM:(h + 1) * V_DIM] = (
                acc_ref[h] / st_ref[h, 1]).T.astype(BF16)

    def run(stages):
        items = [(st, h) for st in stages for h in range(hp)]

        def put_scores(item):
            (t, _, dst, is_fresh), h = item
            scores(t, dst, is_fresh, [h])

        pending = []

        def put_rest(item):
            (t, src, _, is_fresh), h = item
            absorb(t - 1, src, [h])
            if is_fresh:
                finalize(qi_ref[t - 1], [h])
                init_scores(qi_ref[t], [h])
            init_values(pending)
            pending.clear()
            if is_fresh:
                pending.append(h)

        for item in items[:SCORE_LEAD + 1]:
            put_scores(item)
        for n, item in enumerate(items):
            put_rest(item)
            if n + SCORE_LEAD + 1 < len(items):
                put_scores(items[n + SCORE_LEAD + 1])
        init_values(pending)

    heads = list(range(hp))
    init_scores(qi_ref[0], heads)
    init_values(heads)
    scores(0, bufs[0], True, heads)

    def body(r, carry):
        t = 2 * r + 1
        a, b = bufs
        for fresh0, fresh1 in ((True, False), (False, True), (False, False)):
            cond = jnp.logical_and(new_ref[t] == int(fresh0), new_ref[t + 1] == int(fresh1))
            pl.when(cond)(functools.partial(
                run, [(t, a, b, fresh0), (t + 1, b, a, fresh1)]))
        return carry

    lax.fori_loop(0, (n_pairs - 1) // 2, body, 0)
    last = bufs[0]
    if (n_pairs - 1) % 2:
        t = n_pairs - 1
        for is_fresh in (True, False):
            pl.when(new_ref[t] == int(is_fresh))(functools.partial(
                run, [(t, bufs[0], bufs[1], is_fresh)]))
        last = bufs[1]
    absorb(n_pairs - 1, last, heads)
    finalize(qi_ref[n_pairs - 1], heads)


def _attn(qt, k_nope, k_pe, vt, knm, kpm, vtm, *, hp):
    b, s, _ = k_nope.shape
    tk = vt.shape[-1]
    nq = s // tk
    nh = MLA_HEADS
    qi, kj, new = [], [], []
    for i in range(nq):
        for n, j in enumerate([i] + list(range(i))):
            qi.append(i)
            kj.append(j)
            new.append(int(n == 0))
    assert not any(new[t] and new[t + 1] for t in range(1, len(new) - 1, 2))
    tables = [jnp.asarray(v, jnp.int32) for v in (qi, kj, new)]
    grid_spec = pltpu.PrefetchScalarGridSpec(
        num_scalar_prefetch=len(tables),
        grid=(b, nh // hp),
        in_specs=[
            pl.BlockSpec((1, nq, hp * HEAD_PAD, tk), lambda bi, h, *_: (bi, 0, h, 0)),
            pl.BlockSpec((1, s, hp * QK_NOPE), lambda bi, h, *_: (bi, 0, h)),
            pl.BlockSpec((1, s, LANES), lambda bi, h, *_: (bi, 0, 0)),
            pl.BlockSpec((1, nq, hp * V_DIM, tk), lambda bi, h, *_: (bi, 0, h, 0)),
            pl.BlockSpec((N_META, hp * QK_NOPE), lambda bi, h, *_: (0, h)),
            pl.BlockSpec((N_META, LANES), lambda bi, h, *_: (0, 0)),
            pl.BlockSpec((hp * V_DIM, N_META), lambda bi, h, *_: (h, 0)),
        ],
        out_specs=pl.BlockSpec((1, s, hp * V_DIM), lambda bi, h, *_: (bi, 0, h)),
        scratch_shapes=[pltpu.VMEM((hp, V_DIM, tk), F32), pltpu.VMEM((hp, tk, tk), F32),
                        pltpu.VMEM((hp, tk, tk), F32), pltpu.VMEM((hp, 4, 1, tk), F32),
                        pltpu.VMEM((hp, N_META, tk), BF16)],
    )
    return pl.pallas_call(
        _attn_kernel,
        grid_spec=grid_spec,
        out_shape=jax.ShapeDtypeStruct((b, s, nh * V_DIM), BF16),
        compiler_params=_params(("parallel", "parallel")),
        name="attn",
    )(*tables, qt, k_nope, k_pe, vt, knm, kpm, vtm)


def _mixout_kernel(h_ref, gmix_ref, gpost_ref, u_ref, uprev_ref, umeta_ref, pw_ref, pscale_ref,
                   attn_ref, wgp_ref, wgm_ref, wpo_ref, wmo_ref, wout_ref, o_ref,
                   hn_ref, pool_ref, buf_ref, *, tiles_per_seq):
    c = pl.program_id(1)
    last = pl.num_programs(1) - 1
    chunks = _row_chunks(h_ref.shape[0], ROW_CHUNK)

    def pool(rows, halo):
        for g, win in enumerate(POOL_WINDOWS):
            cols = slice(g * POOL_GROUP, (g + 1) * POOL_GROUP)
            buf_ref[:N_META, :] = halo[:, cols]
            buf_ref[N_META:, :] = u_ref[rows, cols]
            tot = buf_ref[...]
            shift = 1
            while shift < win:
                tot = tot + pltpu.roll(tot, shift, 0)
                shift *= 2
            dlt = (tot[N_META:, :] * (1.0 / win) - buf_ref[N_META:, :]).astype(BF16)
            y = jnp.dot(dlt, pw_ref[g], preferred_element_type=F32)
            pool_ref[rows, cols] = (y * pscale_ref[:, cols]).astype(BF16)

    def front(hn, rows):
        return (jnp.dot(hn, wgp_ref[...], preferred_element_type=F32),
                jnp.dot(hn, wgm_ref[...], preferred_element_type=F32),
                jnp.dot(pool_ref[rows, :], wpo_ref[...], preferred_element_type=F32),
                jnp.dot(attn_ref[rows, :], wmo_ref[...], preferred_element_type=F32))

    def back(parts):
        g_pool, g_mla, y_pool, y_mla = parts
        y = (jax.nn.sigmoid(g_pool) * y_pool + jax.nn.sigmoid(g_mla) * y_mla).astype(BF16)
        return jnp.dot(y, wout_ref[...], preferred_element_type=F32)

    def mix_chunks(get_hn, put):
        ahead = front(get_hn(chunks[0]), chunks[0])
        for n, rows in enumerate(chunks):
            parts = ahead
            if n + 1 < len(chunks):
                ahead = front(get_hn(chunks[n + 1]), chunks[n + 1])
            put(rows, back(parts))

    @pl.when(c == 0)
    def _():
        seq_start = pl.program_id(0) % tiles_per_seq == 0

        def prepared(rows):
            if rows.start == 0:
                halo = jnp.where(seq_start, umeta_ref[...], uprev_ref[...])
            else:
                halo = u_ref[rows.start - N_META:rows.start, :]
            pool(rows, halo)
            hn = (_rms(h_ref[rows, :]) * gmix_ref[...]).astype(BF16)
            hn_ref[rows, :] = hn
            return hn

        def put(rows, d):
            o_ref[rows, :] = d

        mix_chunks(prepared, put)

    @pl.when(jnp.logical_and(c > 0, c < last))
    def _():
        def put(rows, d):
            o_ref[rows, :] += d

        mix_chunks(lambda rows: hn_ref[rows, :], put)

    @pl.when(c == last)
    def _():
        def put(rows, d):
            y = o_ref[rows, :] + d
            o_ref[rows, :] = h_ref[rows, :] + _rms(y) * gpost_ref[...]

        mix_chunks(lambda rows: hn_ref[rows, :], put)


def _mixout(h, g_mix, g_post, u, u_meta, pool_w, pool_scale, attn,
            w_gp, w_gm, w_po, w_mo, w_out, *, tm, tn, seq):
    rows, d = h.shape
    pw = u.shape[1]
    hb = tm // N_META
    chunk = min(ROW_CHUNK, tm)
    row = lambda i, c: (i, 0)
    const = lambda i, c: (0, 0)
    col = lambda i, c: (0, c)
    return pl.pallas_call(
        functools.partial(_mixout_kernel, tiles_per_seq=seq // tm),
        grid=(rows // tm, d // tn),
        in_specs=[
            pl.BlockSpec((tm, d), row),
            pl.BlockSpec((1, d), const),
            pl.BlockSpec((1, d), const),
            pl.BlockSpec((tm, pw), row),
            pl.BlockSpec((N_META, pw), lambda i, c: (jnp.maximum(i * hb - 1, 0), 0)),
            pl.BlockSpec((N_META, pw), const),
            pl.BlockSpec((len(POOL_WINDOWS), POOL_GROUP, POOL_GROUP), lambda i, c: (0, 0, 0)),
            pl.BlockSpec((1, pw), const),
            pl.BlockSpec((tm, attn.shape[1]), row),
            pl.BlockSpec((d, tn), col),
            pl.BlockSpec((d, tn), col),
            pl.BlockSpec((pw, tn), col),
            pl.BlockSpec((attn.shape[1], tn), col),
            pl.BlockSpec((tn, d), lambda i, c: (c, 0)),
        ],
        out_specs=pl.BlockSpec((tm, d), row),
        out_shape=jax.ShapeDtypeStruct((rows, d), F32),
        scratch_shapes=[pltpu.VMEM((tm, d), BF16), pltpu.VMEM((tm, pw), BF16),
                        pltpu.VMEM((chunk + N_META, POOL_GROUP), F32)],
        compiler_params=_params(("parallel", "arbitrary")),
        name="mixout",
    )(h, g_mix, g_post, u, u, u_meta, pool_w, pool_scale, attn, w_gp, w_gm, w_po, w_mo, w_out)


def _rope_tables(n_pos):
    pos = jnp.arange(n_pos, dtype=F32)
    inv = ROPE_THETA ** (-jnp.arange(0, QK_ROPE, 2, dtype=F32) / QK_ROPE)
    ang = pos[:, None] * inv[None, :]
    cos, sin = jnp.cos(ang), jnp.sin(ang)
    pad = jnp.zeros((n_pos, LANES - QK_ROPE), F32)
    return (jnp.concatenate([cos, cos, pad], axis=-1), jnp.concatenate([sin, sin, pad], axis=-1),
            cos.T, sin.T)


def kernel(x, meta_tokens, norm_ffn1_pre, norm_ffn1_post, ffn1_w_gu, ffn1_w_down, norm_mix_pre, norm_mix_post, w_in, pool_w, pool_scale, w_pool_o, q_a_norm, w_q_b, kv_a_norm, w_kv_b, w_mla_o, w_out, norm_ffn2_pre, norm_ffn2_post, ffn2_w_gu, ffn2_w_down):
    bsz, seq, d = x.shape
    depth = w_in.shape[0]
    nh = MLA_HEADS
    tm, tf, tn, tq = 512, 512, 512, 512
    tm_ffn = 1024

    cos, sin, cos_t, sin_t = _rope_tables(N_META + seq)
    cos_m, sin_m, cos_r, sin_r = cos[:N_META], sin[:N_META], cos[N_META:], sin[N_META:]
    cos_tm, sin_tm, cos_tr, sin_tr = (cos_t[:, :N_META], sin_t[:, :N_META],
                                      cos_t[:, N_META:], sin_t[:, N_META:])

    h = x.reshape(bsz * seq, d)
    hm = meta_tokens.astype(x.dtype)
    row = lambda v: v.reshape(1, -1)

    for i in range(depth):
        w_gu1, w_dn1 = ffn1_w_gu[i].astype(BF16), ffn1_w_down[i].astype(BF16)
        w_q = jnp.pad(w_q_b[i].reshape(Q_LORA, nh, QK_DIM),
                      ((0, 0), (0, 0), (0, HEAD_PAD - QK_DIM))).reshape(Q_LORA, nh * HEAD_PAD)
        w_qt = w_q.T.astype(BF16)
        w_kv = w_kv_b[i].reshape(KV_LORA, nh, QK_NOPE + V_DIM)
        w_k = w_kv[:, :, :QK_NOPE].reshape(KV_LORA, nh * QK_NOPE).astype(BF16)
        w_vt = w_kv[:, :, QK_NOPE:].reshape(KV_LORA, nh * V_DIM).T.astype(BF16)
        p_w = pool_w[i].astype(BF16)

        h1, h1m, w_gu2, w_dn2 = _ffn(
            h, row(norm_ffn1_pre[i]), row(norm_ffn1_post[i]), w_gu1, w_dn1, tm=tm_ffn, tf=tf,
            side=hm, casts=(ffn2_w_gu[i], ffn2_w_down[i]))
        w_in_bf = w_in[i].astype(BF16)
        n_lat = POOL_WIDTH + Q_LORA + KV_LORA + QK_ROPE
        w_gp, w_gm = w_in_bf[:, n_lat:n_lat + d], w_in_bf[:, n_lat + d:]

        u, cq, ckv, kpe, w_po, w_mo, w_o = _inproj(
            h1, row(norm_mix_pre[i]), w_in_bf, row(q_a_norm[i]), row(kv_a_norm[i]), cos_r, sin_r,
            tm=tm, casts=(w_pool_o[i], w_mla_o[i], w_out[i]))
        um, cqm, ckvm, kpem = _inproj(h1m, row(norm_mix_pre[i]), w_in_bf, row(q_a_norm[i]),
                                      row(kv_a_norm[i]), cos_m, sin_m, tm=N_META)
        qt, kn, vt = _qkv(cq, ckv, w_qt, w_k, w_vt, cos_tr, sin_tr, tm=tm, tk=tq, seq=seq)
        _, knm, vtm = _qkv(cqm, ckvm, w_qt, w_k, w_vt, cos_tm, sin_tm,
                           tm=N_META, tk=N_META, seq=N_META)

        attn = _attn(qt, kn.reshape(bsz, seq, nh * QK_NOPE),
                     kpe.reshape(bsz, seq, LANES), vt, knm, kpem,
                     vtm.reshape(nh * V_DIM, N_META), hp=4)

        h2 = _mixout(h1, row(norm_mix_pre[i]), row(norm_mix_post[i]), u, um, p_w,
                     row(pool_scale[i]), attn.reshape(bsz * seq, nh * V_DIM),
                     w_gp, w_gm, w_po, w_mo, w_o, tm=tm, tn=tn, seq=seq)

        h, = _ffn(h2, row(norm_ffn2_pre[i]), row(norm_ffn2_post[i]), w_gu2, w_dn2,
                  tm=tm_ffn, tf=tf)
        if i + 1 < depth:
            raise NotImplementedError("only DEPTH == 1 is supported")

    return h.reshape(bsz, seq, d)
```

```python
import functools
import math

import jax
import jax.numpy as jnp
from jax import lax
from jax.experimental import pallas as pl
from jax.experimental.pallas import tpu as pltpu

F32 = jnp.float32
BF16 = jnp.bfloat16

N_META = 16
POOL_WINDOWS = (2, 4, 8, 16)
POOL_GROUP = 256
POOL_WIDTH = POOL_GROUP * len(POOL_WINDOWS)
MLA_HEADS = 16
Q_LORA = 512
KV_LORA = 512
QK_NOPE = 128
QK_ROPE = 64
V_DIM = 128
QK_DIM = QK_NOPE + QK_ROPE
ROPE_THETA = 10000.0
EPS = 1e-6
LANES = 128
BF16_ROWS = 16
HEAD_PAD = 2 * LANES
Q_SCALE = (QK_DIM ** -0.5) * math.log2(math.e)
MASK_VALUE = -1e30

VMEM_LIMIT = 56 * 1024 * 1024
FFN_VMEM_LIMIT = 62 * 1024 * 1024
ROW_CHUNK = 256
SCORE_LEAD = 0


def _rms(x):
    return x * lax.rsqrt(jnp.mean(x * x, axis=-1, keepdims=True) + EPS)


def _rope(x, cos, sin):
    rot = pltpu.roll(x, 32, 1) - pltpu.roll(x, 96, 1)
    return x * cos + rot * sin


def _params(sem, vmem_limit=VMEM_LIMIT):
    return pltpu.CompilerParams(dimension_semantics=sem, vmem_limit_bytes=vmem_limit)


def _row_chunks(rows, chunk):
    chunk = min(chunk, rows)
    return [slice(r, r + chunk) for r in range(0, rows, chunk)]


def _ffn_kernel(*refs, n_casts, has_side):
    ns = int(has_side)
    x_ref, gpre_ref, gpost_ref, wg_ref, wu_ref, wd_ref = refs[:6]
    n_in = 6 + ns + n_casts
    cast_in = refs[6 + ns:n_in]
    o_ref = refs[n_in]
    cast_out = refs[n_in + 1 + ns:n_in + 1 + ns + n_casts]
    xn_ref = refs[n_in + 1 + ns + n_casts]
    f = pl.program_id(1)
    last = pl.num_programs(1) - 1
    chunks = _row_chunks(x_ref.shape[0], ROW_CHUNK)

    def gate_up(xn):
        return (jnp.dot(xn, wg_ref[...], preferred_element_type=F32),
                jnp.dot(xn, wu_ref[...], preferred_element_type=F32))

    def down(gu):
        g, u = gu
        a = ((g * jax.nn.sigmoid(g)) * u).astype(BF16)
        return jnp.dot(a, wd_ref[...], preferred_element_type=F32)

    def mlp_chunks(chunks, get_xn, put):
        ahead = gate_up(get_xn(chunks[0]))
        for n, rows in enumerate(chunks):
            gu = ahead
            if n + 1 < len(chunks):
                ahead = gate_up(get_xn(chunks[n + 1]))
            put(rows, down(gu))
            if n == 0:
                casts()

    def casts():
        for src, dst in zip(cast_in, cast_out):
            dst[...] = src[...].astype(BF16)

    def side(step):
        if not has_side:
            return
        xs_ref, os_ref, xsn_ref = refs[6], refs[n_in + 1], refs[-1]

        @pl.when(pl.program_id(0) == 0)
        def _():
            if step == "first":
                xsn_ref[...] = (_rms(xs_ref[...]) * gpre_ref[...]).astype(BF16)
                os_ref[...] = down(gate_up(xsn_ref[...]))
            elif step == "middle":
                os_ref[...] += down(gate_up(xsn_ref[...]))
            else:
                y = os_ref[...] + down(gate_up(xsn_ref[...]))
                os_ref[...] = xs_ref[...] + 0.5 * (_rms(y) * gpost_ref[...])

    @pl.when(f == 0)
    def _():
        def normed(rows):
            xn = (_rms(x_ref[rows, :]) * gpre_ref[...]).astype(BF16)
            xn_ref[rows, :] = xn
            return xn

        def put(rows, d):
            o_ref[rows, :] = d

        mlp_chunks(chunks, normed, put)
        side("first")

    @pl.when(jnp.logical_and(f > 0, f < last))
    def _():
        def put(rows, d):
            o_ref[rows, :] += d

        mlp_chunks(chunks, lambda rows: xn_ref[rows, :], put)
        side("middle")

    @pl.when(f == last)
    def _():
        def put(rows, d):
            y = o_ref[rows, :] + d
            o_ref[rows, :] = x_ref[rows, :] + 0.5 * (_rms(y) * gpost_ref[...])

        mlp_chunks(chunks, lambda rows: xn_ref[rows, :], put)
        side("last")


def _ffn(x, g_pre, g_post, w_gu, w_down, *, tm, tf, side=None, casts=()):
    rows, d = x.shape
    d_ff = w_down.shape[0]
    nf = d_ff // tf
    ni = rows // tm
    const = lambda i, f: (0, 0)
    cast_specs = []
    for c in casts:
        r, k = c.shape
        if r % (ni * BF16_ROWS) == 0 and k % (nf * LANES) == 0:
            cast_specs.append(pl.BlockSpec((r // ni, k // nf), lambda i, f: (i, f)))
        else:
            s = max(v for v in range(1, nf + 1) if r % (ni * v * BF16_ROWS) == 0)
            cast_specs.append(pl.BlockSpec(
                (r // (ni * s), k), lambda i, f, s=s: (i * s + (f * s) // nf, 0)))
    sides = [] if side is None else [side]
    side_specs = [pl.BlockSpec(v.shape, const) for v in sides]
    return pl.pallas_call(
        functools.partial(_ffn_kernel, n_casts=len(casts), has_side=bool(sides)),
        grid=(ni, nf),
        in_specs=[
            pl.BlockSpec((tm, d), lambda i, f: (i, 0)),
            pl.BlockSpec((1, d), const),
            pl.BlockSpec((1, d), const),
            pl.BlockSpec((d, tf), lambda i, f: (0, f)),
            pl.BlockSpec((d, tf), lambda i, f: (0, f + nf)),
            pl.BlockSpec((tf, d), lambda i, f: (f, 0)),
        ] + side_specs + cast_specs,
        out_specs=[pl.BlockSpec((tm, d), lambda i, f: (i, 0))] + side_specs + cast_specs,
        out_shape=[jax.ShapeDtypeStruct((rows, d), F32)]
        + [jax.ShapeDtypeStruct(v.shape, F32) for v in sides]
        + [jax.ShapeDtypeStruct(c.shape, BF16) for c in casts],
        scratch_shapes=[pltpu.VMEM((tm, d), BF16)]
        + [pltpu.VMEM(v.shape, BF16) for v in sides],
        compiler_params=_params(("arbitrary", "arbitrary"), FFN_VMEM_LIMIT),
        name="ffn",
    )(x, g_pre, g_post, w_gu, w_gu, w_down, *sides, *casts)


def _inproj_kernel(*refs, n_casts):
    h_ref, gmix_ref, w_ref, gq_ref, gkv_ref, cos_ref, sin_ref = refs[:7]
    cast_in = refs[7:7 + n_casts]
    u_ref, cq_ref, ckv_ref, kpe_ref = refs[7 + n_casts:11 + n_casts]
    cast_out = refs[11 + n_casts:]
    c0, c1, c2, c3 = POOL_WIDTH, POOL_WIDTH + Q_LORA, POOL_WIDTH + Q_LORA + KV_LORA, w_ref.shape[1]
    for n, rows in enumerate(_row_chunks(h_ref.shape[0], ROW_CHUNK)):
        hn = (_rms(h_ref[rows, :]) * gmix_ref[...]).astype(BF16)
        u_ref[rows, :] = jnp.dot(hn, w_ref[:, :c0], preferred_element_type=F32)
        if n == 0:
            for src, dst in zip(cast_in, cast_out):
                dst[...] = src[...].astype(BF16)
        cq = jnp.dot(hn, w_ref[:, c0:c1], preferred_element_type=F32)
        cq_ref[rows, :] = (_rms(cq) * gq_ref[...]).astype(BF16)
        ckv = jnp.dot(hn, w_ref[:, c1:c2], preferred_element_type=F32)
        ckv_ref[rows, :] = (_rms(ckv) * gkv_ref[...]).astype(BF16)
        kr = jnp.dot(hn, w_ref[:, c2:c3], preferred_element_type=F32)
        kr = jnp.where(lax.broadcasted_iota(jnp.int32, kr.shape, 1) < QK_ROPE, kr, 0.0)
        kpe_ref[rows, :] = _rope(kr, cos_ref[rows, :], sin_ref[rows, :]).astype(BF16)


def _inproj(h, g_mix, w_in, g_q, g_kv, cos, sin, *, tm, casts=()):
    rows, d = h.shape
    n_pos = cos.shape[0] // tm
    wcols = POOL_WIDTH + Q_LORA + KV_LORA + LANES
    steps = rows // tm
    row = lambda i: (i, 0)
    const = lambda i: (0, 0)
    pos = lambda i: (i % n_pos, 0)
    cast_specs = [pl.BlockSpec((c.shape[0] // steps, c.shape[1]), row) for c in casts]
    assert all(c.shape[0] % (steps * BF16_ROWS) == 0 for c in casts)
    return pl.pallas_call(
        functools.partial(_inproj_kernel, n_casts=len(casts)),
        grid=(steps,),
        in_specs=[
            pl.BlockSpec((tm, d), row),
            pl.BlockSpec((1, d), const),
            pl.BlockSpec((d, wcols), const),
            pl.BlockSpec((1, Q_LORA), const),
            pl.BlockSpec((1, KV_LORA), const),
            pl.BlockSpec((tm, LANES), pos),
            pl.BlockSpec((tm, LANES), pos),
        ] + cast_specs,
        out_specs=[
            pl.BlockSpec((tm, POOL_WIDTH), row),
            pl.BlockSpec((tm, Q_LORA), row),
            pl.BlockSpec((tm, KV_LORA), row),
            pl.BlockSpec((tm, LANES), row),
        ] + cast_specs,
        out_shape=[
            jax.ShapeDtypeStruct((rows, POOL_WIDTH), F32),
            jax.ShapeDtypeStruct((rows, Q_LORA), BF16),
            jax.ShapeDtypeStruct((rows, KV_LORA), BF16),
            jax.ShapeDtypeStruct((rows, LANES), BF16),
        ] + [jax.ShapeDtypeStruct(c.shape, BF16) for c in casts],
        compiler_params=_params(("parallel",)),
        name="inproj",
    )(h, g_mix, w_in, g_q, g_kv, cos, sin, *casts)


def _qkv_kernel(cq_ref, ckv_ref, wqt_ref, wk_ref, wvt_ref, cos_ref, sin_ref,
                qt_ref, k_ref, vt_ref):
    cq = cq_ref[...]
    ckv = ckv_ref[...]
    nt = (((1,), (1,)), ((), ()))
    tk = vt_ref.shape[-1]
    chunks = [slice(c * tk, (c + 1) * tk) for c in range(vt_ref.shape[1])]
    cos = cos_ref[...] * Q_SCALE
    sin = sin_ref[...] * Q_SCALE
    half = QK_ROPE // 2
    qt = lax.dot_general(wqt_ref[...], cq, nt, preferred_element_type=F32)
    zeros = jnp.zeros((HEAD_PAD - QK_DIM, tk), BF16)
    for h in range(MLA_HEADS):
        lo = h * HEAD_PAD
        nope = qt[lo:lo + QK_NOPE] * Q_SCALE
        x1 = qt[lo + QK_NOPE:lo + QK_NOPE + half]
        x2 = qt[lo + QK_NOPE + half:lo + QK_DIM]
        r1 = x1 * cos - x2 * sin
        r2 = x2 * cos + x1 * sin
        for c, cols in enumerate(chunks):
            qt_ref[0, c, lo:lo + QK_NOPE, :] = nope[:, cols].astype(BF16)
            qt_ref[0, c, lo + QK_NOPE:lo + QK_NOPE + half, :] = r1[:, cols].astype(BF16)
            qt_ref[0, c, lo + QK_NOPE + half:lo + QK_DIM, :] = r2[:, cols].astype(BF16)
            qt_ref[0, c, lo + QK_DIM:lo + HEAD_PAD, :] = zeros
    k_ref[...] = jnp.dot(ckv, wk_ref[...], preferred_element_type=F32).astype(BF16)
    vt = lax.dot_general(wvt_ref[...], ckv, nt, preferred_element_type=F32)
    for c, cols in enumerate(chunks):
        vt_ref[0, c] = vt[:, cols].astype(BF16)


def _qkv(cq, ckv, w_qt, w_k, w_vt, cos_t, sin_t, *, tm, tk, seq):
    rows = cq.shape[0]
    n_pos = seq // tm
    nh = MLA_HEADS
    row = lambda i: (i, 0)
    const = lambda i: (0, 0)
    tiled = lambda i: (i // n_pos, i % n_pos, 0, 0)
    return pl.pallas_call(
        _qkv_kernel,
        grid=(rows // tm,),
        in_specs=[
            pl.BlockSpec((tm, Q_LORA), row),
            pl.BlockSpec((tm, KV_LORA), row),
            pl.BlockSpec((nh * HEAD_PAD, Q_LORA), const),
            pl.BlockSpec((KV_LORA, nh * QK_NOPE), const),
            pl.BlockSpec((nh * V_DIM, KV_LORA), const),
            pl.BlockSpec((QK_ROPE // 2, tm), lambda i: (0, i % n_pos)),
            pl.BlockSpec((QK_ROPE // 2, tm), lambda i: (0, i % n_pos)),
        ],
        out_specs=[
            pl.BlockSpec((1, tm // tk, nh * HEAD_PAD, tk), tiled),
            pl.BlockSpec((tm, nh * QK_NOPE), row),
            pl.BlockSpec((1, tm // tk, nh * V_DIM, tk), tiled),
        ],
        out_shape=[
            jax.ShapeDtypeStruct((rows // seq, seq // tk, nh * HEAD_PAD, tk), BF16),
            jax.ShapeDtypeStruct((rows, nh * QK_NOPE), BF16),
            jax.ShapeDtypeStruct((rows // seq, seq // tk, nh * V_DIM, tk), BF16),
        ],
        compiler_params=_params(("parallel",)),
        name="qkv",
    )(cq, ckv, w_qt, w_k, w_vt, cos_t, sin_t)


def _attn_kernel(qi_ref, kj_ref, new_ref, q_ref, kn_ref, kp_ref, vt_ref, knm_ref, kpm_ref,
                 vtm_ref, o_ref, acc_ref, sa_ref, sb_ref, st_ref, pm_ref, *, tail_fresh):
    tk = vt_ref.shape[-1]
    tq = tk
    hp = acc_ref.shape[0]
    n_pairs = qi_ref.shape[0]
    bufs = ((sa_ref, 2), (sb_ref, 3))

    def q_of(h, qi):
        return q_ref[0, qi, h * HEAD_PAD:(h + 1) * HEAD_PAD, :]

    def init_scores(qi, heads):
        kpm = kpm_ref[...]
        for h in heads:
            km = jnp.concatenate([knm_ref[:, h * QK_NOPE:(h + 1) * QK_NOPE], kpm], axis=1)
            s = jnp.dot(km, q_of(h, qi), preferred_element_type=F32)
            m0 = jnp.max(s, axis=0, keepdims=True)
            p = jnp.exp2(s - m0)
            pm_ref[h] = p.astype(BF16)
            st_ref[h, 0] = m0
            st_ref[h, 1] = jnp.sum(p, axis=0, keepdims=True)

    def init_values(heads):
        for h in heads:
            acc_ref[h] = jnp.dot(vtm_ref[h * V_DIM:(h + 1) * V_DIM, :], pm_ref[h],
                                 preferred_element_type=F32)

    def scores(t, dst, masked, heads):
        s_ref, row = dst
        qi = qi_ref[t]
        start = pl.multiple_of(kj_ref[t] * tk, tk)
        kp = kp_ref[0, pl.ds(start, tk), :]
        for h in heads:
            k = jnp.concatenate(
                [kn_ref[0, pl.ds(start, tk), h * QK_NOPE:(h + 1) * QK_NOPE], kp], axis=1)
            s = jnp.dot(k, q_of(h, qi), preferred_element_type=F32)
            if masked:
                kpos = lax.broadcasted_iota(jnp.int32, s.shape, 0)
                qpos = lax.broadcasted_iota(jnp.int32, s.shape, 1)
                s = jnp.where(kpos <= qpos, s, MASK_VALUE)
            s_ref[h] = s
            st_ref[h, row] = jnp.max(s, axis=0, keepdims=True)

    def absorb(t, src, heads):
        s_ref, row = src
        kj = kj_ref[t]
        ones = jnp.ones((BF16_ROWS, tk), BF16)
        for h in heads:
            m, l = st_ref[h, 0], st_ref[h, 1]
            m_new = jnp.maximum(m, st_ref[h, row])
            alpha = jnp.exp2(m - m_new)
            p = jnp.exp2(s_ref[h] - m_new)
            vt1 = jnp.concatenate([vt_ref[0, kj, h * V_DIM:(h + 1) * V_DIM, :], ones], axis=0)
            pv = jnp.dot(vt1, p.astype(BF16), preferred_element_type=F32)
            acc_ref[h] = alpha * acc_ref[h] + pv[:V_DIM]
            st_ref[h, 0] = m_new
            st_ref[h, 1] = alpha * l + pv[V_DIM:V_DIM + 1]

    def finalize(qi, heads):
        rows = pl.ds(pl.multiple_of(qi * tq, tq), tq)
        for h in heads:
            o_ref[0, rows, h * V_DIM:(h + 1) * V_DIM] = (
                acc_ref[h] / st_ref[h, 1]).T.astype(BF16)

    def run(stages):
        items = [(st, h) for st in stages for h in range(hp)]

        def put_scores(item):
            (t, _, dst, is_fresh), h = item
            scores(t, dst, is_fresh, [h])

        pending = []

        def put_rest(item):
            (t, src, _, is_fresh), h = item
            absorb(t - 1, src, [h])
            if is_fresh:
                finalize(qi_ref[t - 1], [h])
                init_scores(qi_ref[t], [h])
            init_values(pending)
            pending.clear()
            if is_fresh:
                pending.append(h)

        for item in items[:SCORE_LEAD + 1]:
            put_scores(item)
        for n, item in enumerate(items):
            put_rest(item)
            if n + SCORE_LEAD + 1 < len(items):
                put_scores(items[n + SCORE_LEAD + 1])
        init_values(pending)

    heads = list(range(hp))
    init_scores(qi_ref[0], heads)
    init_values(heads)
    scores(0, bufs[0], True, heads)

    def body(r, carry):
        t = 2 * r + 1
        a, b = bufs
        for fresh0, fresh1 in ((True, False), (False, True), (False, False)):
            cond = jnp.logical_and(new_ref[t] == int(fresh0), new_ref[t + 1] == int(fresh1))
            pl.when(cond)(functools.partial(
                run, [(t, a, b, fresh0), (t + 1, b, a, fresh1)]))
        return carry

    lax.fori_loop(0, (n_pairs - 1) // 2, body, 0)
    last = bufs[0]
    if (n_pairs - 1) % 2:
        run([(n_pairs - 1, bufs[0], bufs[1], tail_fresh)])
        last = bufs[1]
    for h in heads:
        absorb(n_pairs - 1, last, [h])
        finalize(qi_ref[n_pairs - 1], [h])


def _attn(qt, k_nope, k_pe, vt, knm, kpm, vtm, *, hp):
    b, s, _ = k_nope.shape
    tk = vt.shape[-1]
    nq = s // tk
    nh = MLA_HEADS
    qi, kj, new = [], [], []
    for i in range(nq):
        for n, j in enumerate([i] + list(range(i))):
            qi.append(i)
            kj.append(j)
            new.append(int(n == 0))
    assert not any(new[t] and new[t + 1] for t in range(1, len(new) - 1, 2))
    tables = [jnp.asarray(v, jnp.int32) for v in (qi, kj, new)]
    grid_spec = pltpu.PrefetchScalarGridSpec(
        num_scalar_prefetch=len(tables),
        grid=(b, nh // hp),
        in_specs=[
            pl.BlockSpec((1, nq, hp * HEAD_PAD, tk), lambda bi, h, *_: (bi, 0, h, 0)),
            pl.BlockSpec((1, s, hp * QK_NOPE), lambda bi, h, *_: (bi, 0, h)),
            pl.BlockSpec((1, s, LANES), lambda bi, h, *_: (bi, 0, 0)),
            pl.BlockSpec((1, nq, hp * V_DIM, tk), lambda bi, h, *_: (bi, 0, h, 0)),
            pl.BlockSpec((N_META, hp * QK_NOPE), lambda bi, h, *_: (0, h)),
            pl.BlockSpec((N_META, LANES), lambda bi, h, *_: (0, 0)),
            pl.BlockSpec((hp * V_DIM, N_META), lambda bi, h, *_: (h, 0)),
        ],
        out_specs=pl.BlockSpec((1, s, hp * V_DIM), lambda bi, h, *_: (bi, 0, h)),
        scratch_shapes=[pltpu.VMEM((hp, V_DIM, tk), F32), pltpu.VMEM((hp, tk, tk), F32),
                        pltpu.VMEM((hp, tk, tk), F32), pltpu.VMEM((hp, 4, 1, tk), F32),
                        pltpu.VMEM((hp, N_META, tk), BF16)],
    )
    return pl.pallas_call(
        functools.partial(_attn_kernel, tail_fresh=bool(new[-1])),
        grid_spec=grid_spec,
        out_shape=jax.ShapeDtypeStruct((b, s, nh * V_DIM), BF16),
        compiler_params=_params(("parallel", "parallel")),
        name="attn",
    )(*tables, qt, k_nope, k_pe, vt, knm, kpm, vtm)


def _mixout_kernel(h_ref, gmix_ref, gpost_ref, u_ref, uprev_ref, umeta_ref, pw_ref, pscale_ref,
                   attn_ref, wgp_ref, wgm_ref, wpo_ref, wmo_ref, wout_ref, o_ref,
                   hn_ref, pool_ref, buf_ref, *, tiles_per_seq):
    c = pl.program_id(1)
    last = pl.num_programs(1) - 1
    chunks = _row_chunks(h_ref.shape[0], ROW_CHUNK)

    def pool(rows, halo):
        for g, win in enumerate(POOL_WINDOWS):
            cols = slice(g * POOL_GROUP, (g + 1) * POOL_GROUP)
            buf_ref[:N_META, :] = halo[:, cols]
            buf_ref[N_META:, :] = u_ref[rows, cols]
            tot = buf_ref[...]
            shift = 1
            while shift < win:
                tot = tot + pltpu.roll(tot, shift, 0)
                shift *= 2
            dlt = (tot[N_META:, :] * (1.0 / win) - buf_ref[N_META:, :]).astype(BF16)
            y = jnp.dot(dlt, pw_ref[g], preferred_element_type=F32)
            pool_ref[rows, cols] = (y * pscale_ref[:, cols]).astype(BF16)

    def front(hn, rows):
        return (jnp.dot(hn, wgp_ref[...], preferred_element_type=F32),
                jnp.dot(hn, wgm_ref[...], preferred_element_type=F32),
                jnp.dot(pool_ref[rows, :], wpo_ref[...], preferred_element_type=F32),
                jnp.dot(attn_ref[rows, :], wmo_ref[...], preferred_element_type=F32))

    def back(parts):
        g_pool, g_mla, y_pool, y_mla = parts
        y = (jax.nn.sigmoid(g_pool) * y_pool + jax.nn.sigmoid(g_mla) * y_mla).astype(BF16)
        return jnp.dot(y, wout_ref[...], preferred_element_type=F32)

    def mix_chunks(get_hn, put):
        ahead = front(get_hn(chunks[0]), chunks[0])
        for n, rows in enumerate(chunks):
            parts = ahead
            if n + 1 < len(chunks):
                ahead = front(get_hn(chunks[n + 1]), chunks[n + 1])
            put(rows, back(parts))

    @pl.when(c == 0)
    def _():
        seq_start = pl.program_id(0) % tiles_per_seq == 0

        def prepared(rows):
            if rows.start == 0:
                halo = jnp.where(seq_start, umeta_ref[...], uprev_ref[...])
            else:
                halo = u_ref[rows.start - N_META:rows.start, :]
            pool(rows, halo)
            hn = (_rms(h_ref[rows, :]) * gmix_ref[...]).astype(BF16)
            hn_ref[rows, :] = hn
            return hn

        def put(rows, d):
            o_ref[rows, :] = d

        mix_chunks(prepared, put)

    @pl.when(jnp.logical_and(c > 0, c < last))
    def _():
        def put(rows, d):
            o_ref[rows, :] += d

        mix_chunks(lambda rows: hn_ref[rows, :], put)

    @pl.when(c == last)
    def _():
        def put(rows, d):
            y = o_ref[rows, :] + d
            o_ref[rows, :] = h_ref[rows, :] + _rms(y) * gpost_ref[...]

        mix_chunks(lambda rows: hn_ref[rows, :], put)


def _mixout(h, g_mix, g_post, u, u_meta, pool_w, pool_scale, attn,
            w_gp, w_gm, w_po, w_mo, w_out, *, tm, tn, seq):
    rows, d = h.shape
    pw = u.shape[1]
    hb = tm // N_META
    chunk = min(ROW_CHUNK, tm)
    row = lambda i, c: (i, 0)
    const = lambda i, c: (0, 0)
    col = lambda i, c: (0, c)
    return pl.pallas_call(
        functools.partial(_mixout_kernel, tiles_per_seq=seq // tm),
        grid=(rows // tm, d // tn),
        in_specs=[
            pl.BlockSpec((tm, d), row),
            pl.BlockSpec((1, d), const),
            pl.BlockSpec((1, d), const),
            pl.BlockSpec((tm, pw), row),
            pl.BlockSpec((N_META, pw), lambda i, c: (jnp.maximum(i * hb - 1, 0), 0)),
            pl.BlockSpec((N_META, pw), const),
            pl.BlockSpec((len(POOL_WINDOWS), POOL_GROUP, POOL_GROUP), lambda i, c: (0, 0, 0)),
            pl.BlockSpec((1, pw), const),
            pl.BlockSpec((tm, attn.shape[1]), row),
            pl.BlockSpec((d, tn), col),
            pl.BlockSpec((d, tn), col),
            pl.BlockSpec((pw, tn), col),
            pl.BlockSpec((attn.shape[1], tn), col),
            pl.BlockSpec((tn, d), lambda i, c: (c, 0)),
        ],
        out_specs=pl.BlockSpec((tm, d), row),
        out_shape=jax.ShapeDtypeStruct((rows, d), F32),
        scratch_shapes=[pltpu.VMEM((tm, d), BF16), pltpu.VMEM((tm, pw), BF16),
                        pltpu.VMEM((chunk + N_META, POOL_GROUP), F32)],
        compiler_params=_params(("parallel", "arbitrary")),
        name="mixout",
    )(h, g_mix, g_post, u, u, u_meta, pool_w, pool_scale, attn, w_gp, w_gm, w_po, w_mo, w_out)


def _rope_tables(n_pos):
    pos = jnp.arange(n_pos, dtype=F32)
    inv = ROPE_THETA ** (-jnp.arange(0, QK_ROPE, 2, dtype=F32) / QK_ROPE)
    ang = pos[:, None] * inv[None, :]
    cos, sin = jnp.cos(ang), jnp.sin(ang)
    pad = jnp.zeros((n_pos, LANES - QK_ROPE), F32)
    return (jnp.concatenate([cos, cos, pad], axis=-1), jnp.concatenate([sin, sin, pad], axis=-1),
            cos.T, sin.T)


def kernel(x, meta_tokens, norm_ffn1_pre, norm_ffn1_post, ffn1_w_gu, ffn1_w_down, norm_mix_pre, norm_mix_post, w_in, pool_w, pool_scale, w_pool_o, q_a_norm, w_q_b, kv_a_norm, w_kv_b, w_mla_o, w_out, norm_ffn2_pre, norm_ffn2_post, ffn2_w_gu, ffn2_w_down):
    bsz, seq, d = x.shape
    depth = w_in.shape[0]
    nh = MLA_HEADS
    tm, tf, tn, tq = 512, 512, 512, 512
    tm_ffn = 1024

    cos, sin, cos_t, sin_t = _rope_tables(N_META + seq)
    cos_m, sin_m, cos_r, sin_r = cos[:N_META], sin[:N_META], cos[N_META:], sin[N_META:]
    cos_tm, sin_tm, cos_tr, sin_tr = (cos_t[:, :N_META], sin_t[:, :N_META],
                                      cos_t[:, N_META:], sin_t[:, N_META:])

    h = x.reshape(bsz * seq, d)
    hm = meta_tokens.astype(x.dtype)
    row = lambda v: v.reshape(1, -1)

    for i in range(depth):
        w_gu1, w_dn1 = ffn1_w_gu[i].astype(BF16), ffn1_w_down[i].astype(BF16)
        w_q = jnp.pad(w_q_b[i].reshape(Q_LORA, nh, QK_DIM),
                      ((0, 0), (0, 0), (0, HEAD_PAD - QK_DIM))).reshape(Q_LORA, nh * HEAD_PAD)
        w_qt = w_q.T.astype(BF16)
        w_kv = w_kv_b[i].reshape(KV_LORA, nh, QK_NOPE + V_DIM)
        w_k = w_kv[:, :, :QK_NOPE].reshape(KV_LORA, nh * QK_NOPE).astype(BF16)
        w_vt = w_kv[:, :, QK_NOPE:].reshape(KV_LORA, nh * V_DIM).T.astype(BF16)
        p_w = pool_w[i].astype(BF16)

        h1, h1m, w_gu2, w_dn2 = _ffn(
            h, row(norm_ffn1_pre[i]), row(norm_ffn1_post[i]), w_gu1, w_dn1, tm=tm_ffn, tf=tf,
            side=hm, casts=(ffn2_w_gu[i], ffn2_w_down[i]))
        w_in_bf = w_in[i].astype(BF16)
        n_lat = POOL_WIDTH + Q_LORA + KV_LORA + QK_ROPE
        w_gp, w_gm = w_in_bf[:, n_lat:n_lat + d], w_in_bf[:, n_lat + d:]

        u, cq, ckv, kpe, w_po, w_mo, w_o = _inproj(
            h1, row(norm_mix_pre[i]), w_in_bf, row(q_a_norm[i]), row(kv_a_norm[i]), cos_r, sin_r,
            tm=tm, casts=(w_pool_o[i], w_mla_o[i], w_out[i]))
        um, cqm, ckvm, kpem = _inproj(h1m, row(norm_mix_pre[i]), w_in_bf, row(q_a_norm[i]),
                                      row(kv_a_norm[i]), cos_m, sin_m, tm=N_META)
        qt, kn, vt = _qkv(cq, ckv, w_qt, w_k, w_vt, cos_tr, sin_tr, tm=tm, tk=tq, seq=seq)
        _, knm, vtm = _qkv(cqm, ckvm, w_qt, w_k, w_vt, cos_tm, sin_tm,
                           tm=N_META, tk=N_META, seq=N_META)

        attn = _attn(qt, kn.reshape(bsz, seq, nh * QK_NOPE),
                     kpe.reshape(bsz, seq, LANES), vt, knm, kpem,
                     vtm.reshape(nh * V_DIM, N_META), hp=4)

        h2 = _mixout(h1, row(norm_mix_pre[i]), row(norm_mix_post[i]), u, um, p_w,
                     row(pool_scale[i]), attn.reshape(bsz * seq, nh * V_DIM),
                     w_gp, w_gm, w_po, w_mo, w_o, tm=tm, tn=tn, seq=seq)

        h, = _ffn(h2, row(norm_ffn2_pre[i]), row(norm_ffn2_post[i]), w_gu2, w_dn2,
                  tm=tm_ffn, tf=tf)
        if i + 1 < depth:
            raise NotImplementedError("only DEPTH == 1 is supported")

    return h.reshape(bsz, seq, d)
```
